```python
import math
import jax, jax.numpy as jnp
from jax import lax
import numpy as np

D_MODEL = 1024
BATCH = 32
SEQ = 256
DEPTH = 2
DEC_BATCH = 2
DEC_SEQ = 2048
PAST_LEN = 256

GRID_W = 64
N_DIRS = 2
ATT_HEADS = 8
ATT_KV_HEADS = 2
HEAD_DIM = 64
ATT_WIDTH = ATT_HEADS * HEAD_DIM
KV_WIDTH = ATT_KV_HEADS * HEAD_DIM
ROPE_THETA = 10000.0
Q_BLOCK = 128
RWKV_HEADS = 4
RWKV_HEAD = 64
RWKV_WIDTH = RWKV_HEADS * RWKV_HEAD
RWKV_DECAY_LORA = 32
RWKV_AAA_LORA = 32
RWKV_GATE_LORA = 64
RWKV_COLS = 3 * RWKV_WIDTH + RWKV_DECAY_LORA + RWKV_AAA_LORA + RWKV_GATE_LORA
GN_EPS = 64e-5
LRU_BLOCKS = 4
LRU_BLOCK = 64
LRU_WIDTH = LRU_BLOCKS * LRU_BLOCK
LRU_CONV_W = 4
LRU_C = 8.0
MIX_WIDTH = ATT_WIDTH + RWKV_WIDTH + LRU_WIDTH
IN_COLS = ATT_WIDTH + 2 * KV_WIDTH + RWKV_COLS + 2 * LRU_WIDTH
N_EXPERTS = 16
EC_FACTOR = 2
EXPERT_FF = 1024
RMS_EPS = 1e-6

kernel_name = 'hybrid_diffusion_prefix_trunk_step'


def rms_norm(x, gain):
    xf = x.astype(jnp.float32)
    y = xf * lax.rsqrt(jnp.mean(xf * xf, axis=-1, keepdims=True) + RMS_EPS)
    return (y * gain.astype(jnp.float32)).astype(x.dtype)


def axial_rope_angles(n_tokens):
    rows = n_tokens // GRID_W
    row = jnp.repeat(jnp.arange(rows, dtype=jnp.float32), GRID_W)
    col = jnp.tile(jnp.arange(GRID_W, dtype=jnp.float32), rows)
    half = HEAD_DIM // 2
    inv = ROPE_THETA ** (-jnp.arange(0, half, 2, dtype=jnp.float32) / half)
    return row[:, None] * inv, col[:, None] * inv


def _rotate_pair(x, ang):
    cos = jnp.cos(ang)[None, :, None, :]
    sin = jnp.sin(ang)[None, :, None, :]
    x1, x2 = jnp.split(x, 2, axis=-1)
    return jnp.concatenate([x1 * cos - x2 * sin, x2 * cos + x1 * sin], axis=-1)


def apply_axial_rope(x, ang_row, ang_col):
    half = HEAD_DIM // 2
    xf = x.astype(jnp.float32)
    out = jnp.concatenate([_rotate_pair(xf[..., :half], ang_row),
                           _rotate_pair(xf[..., half:], ang_col)], axis=-1)
    return out.astype(x.dtype)


def blocked_attention(q, k, v):
    B, T = q.shape[0], q.shape[1]
    rep = ATT_HEADS // ATT_KV_HEADS
    nb = T // Q_BLOCK
    qb = q.reshape(B, nb, Q_BLOCK, ATT_KV_HEADS, rep, HEAD_DIM).transpose(1, 0, 2, 3, 4, 5)
    scale = HEAD_DIM ** -0.5

    def one_block(qblk):
        s = jnp.einsum('bqgrd,bkgd->bgrqk', qblk, k).astype(jnp.float32) * scale
        p = jax.nn.softmax(s, axis=-1).astype(v.dtype)
        return jnp.einsum('bgrqk,bkgd->bqgrd', p, v)

    o = lax.map(one_block, qb)
    return o.transpose(1, 0, 2, 3, 4, 5).reshape(B, T, ATT_WIDTH)


def centred_shift(p):
    pad = jnp.pad(p, ((0, 0), (1, 1), (0, 0)))
    return 0.5 * (pad[:, :-2] + pad[:, 2:])


def rwkv7_scan(r, w, k, v, a, b, s0, reverse):
    def step(S, inp):
        r_t, w_t, k_t, v_t, a_t, b_t = inp
        sa = jnp.einsum('bhij,bhj->bhi', S, a_t)
        S = (S * w_t[:, :, None, :] + sa[..., None] * b_t[:, :, None, :]
             + v_t[..., None] * k_t[:, :, None, :])
        y = jnp.einsum('bhij,bhj->bhi', S, r_t)
        return S, y
    xs = tuple(jnp.moveaxis(t, 1, 0) for t in (r, w, k, v, a, b))
    S, ys = lax.scan(step, s0.astype(jnp.float32), xs, reverse=reverse)
    return jnp.moveaxis(ys, 0, 1), S


def rwkv7_mix(p, s0, mu, w0, w_up, a0, a_up, g_up, k_k, k_a, r_k, ln_w, ln_b):
    p = p + mu * (centred_shift(p) - p)
    B, T, _ = p.shape
    pf = p.astype(jnp.float32)
    W = RWKV_WIDTH
    r, k, v = pf[..., :W], pf[..., W:2 * W], pf[..., 2 * W:3 * W]
    o = 3 * W
    wd = pf[..., o:o + RWKV_DECAY_LORA]
    o += RWKV_DECAY_LORA
    ad = pf[..., o:o + RWKV_AAA_LORA]
    o += RWKV_AAA_LORA
    gd = pf[..., o:o + RWKV_GATE_LORA]

    def hv(t):
        return t.reshape(B, T, RWKV_HEADS, RWKV_HEAD)

    kk = hv(k * k_k)
    kk = kk / jnp.maximum(jnp.sqrt(jnp.sum(kk * kk, axis=-1, keepdims=True)), 1e-12)
    g = jnp.dot(jax.nn.sigmoid(gd), g_up)
    rh, vh = hv(r), hv(v)
    y_sum = 0.0
    bonus = 0.0
    finals = []
    for d in range(N_DIRS):
        w_log = -jax.nn.softplus(-(w0[d] + jnp.dot(jnp.tanh(wd), w_up[d]))) - 0.5
        decay = jnp.exp(-jnp.exp(w_log))
        a_rate = jax.nn.sigmoid(a0[d] + jnp.dot(ad, a_up[d]))
        kd = hv(k * (1.0 + (a_rate - 1.0) * k_a))
        y, s_fin = rwkv7_scan(rh, hv(decay), kd, vh, -kk, kk * hv(a_rate), s0[:, d], d == 1)
        y_sum = y_sum + y
        bonus = bonus + jnp.sum(rh * kd * r_k, axis=-1, keepdims=True) * vh
        finals.append(s_fin)
    mean = jnp.mean(y_sum, axis=-1, keepdims=True)
    var = jnp.mean(jnp.square(y_sum - mean), axis=-1, keepdims=True)
    yn = ((y_sum - mean) * lax.rsqrt(var + GN_EPS)).reshape(B, T, W) * ln_w + ln_b
    out = (yn + bonus.reshape(B, T, W)) * g
    return out.astype(p.dtype), jnp.stack(finals, axis=1)


def centred_dwconv(x, w, b):
    T = x.shape[1]
    left = LRU_CONV_W // 2
    pad = jnp.pad(x, ((0, 0), (left, LRU_CONV_W - 1 - left), (0, 0)))
    return sum(pad[:, j:j + T] * w[j] for j in range(LRU_CONV_W)) + b


def _linear_combine(left, right):
    a1, b1 = left
    a2, b2 = right
    return a1 * a2, a2 * b1 + b2


def rglru_mix(xb, gb, h0, conv_w, conv_b, wa, ba, wx, bx, lam):
    x = centred_dwconv(xb, conv_w, conv_b).astype(jnp.float32)
    B, T, C = x.shape
    xblk = x.reshape(B, T, LRU_BLOCKS, LRU_BLOCK)
    h0f = h0.astype(jnp.float32)
    h_sum = 0.0
    finals = []
    for d in range(N_DIRS):
        r_gate = jax.nn.sigmoid(jnp.einsum('btkc,kcd->btkd', xblk, wa[d]).reshape(B, T, C) + ba[d])
        i_gate = jax.nn.sigmoid(jnp.einsum('btkc,kcd->btkd', xblk, wx[d]).reshape(B, T, C) + bx[d])
        log_a = -LRU_C * r_gate * jax.nn.softplus(-lam[d])
        a = jnp.exp(log_a)
        u = jnp.sqrt(-jnp.expm1(2.0 * log_a)) * (i_gate * x)
        a_cum, b_cum = lax.associative_scan(_linear_combine, (a, u), axis=1, reverse=(d == 1))
        h = a_cum * h0f[:, d, None, :] + b_cum
        h_sum = h_sum + h
        finals.append(h[:, 0] if d == 1 else h[:, -1])
    out = h_sum * jax.nn.gelu(gb.astype(jnp.float32))
    return out.astype(xb.dtype), jnp.stack(finals, axis=1)


def expert_choice_ffn(h, router, w_gate, w_up, w_down):
    B, T, _ = h.shape
    cap = EC_FACTOR * T // N_EXPERTS
    logits = jnp.einsum('btd,de->bte', h, router).astype(jnp.float32)
    aff = jax.nn.softmax(logits, axis=-1)
    gates, idx = lax.top_k(jnp.swapaxes(aff, 1, 2), cap)
    bidx = jnp.arange(B)[:, None, None]
    xs = h[bidx, idx]
    a = jnp.einsum('becd,edf->becf', xs, w_gate)
    u = jnp.einsum('becd,edf->becf', xs, w_up)
    y = jnp.einsum('becf,efd->becd', jax.nn.silu(a) * u, w_down)
    y = y * gates[..., None].astype(y.dtype)
    return jnp.zeros_like(h).at[bidx, idx].add(y.astype(h.dtype))


def trunk_layer(x, cond, lp, cache):
    B, T, _ = x.shape
    mod = jnp.dot(jax.nn.silu(cond), lp['w_ada']) + lp['b_ada']
    sh1, sc1, gt1, sh2, sc2, gt2 = jnp.split(mod[:, None, :], 6, axis=-1)
    h = rms_norm(x, lp['norm1']) * (1.0 + sc1) + sh1
    p = jnp.dot(h, lp['w_in'])
    o1 = ATT_WIDTH
    o2 = o1 + KV_WIDTH
    o3 = o2 + KV_WIDTH
    o4 = o3 + RWKV_COLS
    o5 = o4 + LRU_WIDTH
    q = rms_norm(p[..., :o1].reshape(B, T, ATT_HEADS, HEAD_DIM), lp['q_norm'])
    k = rms_norm(p[..., o1:o2].reshape(B, T, ATT_KV_HEADS, HEAD_DIM), lp['k_norm'])
    v = p[..., o2:o3].reshape(B, T, ATT_KV_HEADS, HEAD_DIM)
    if cache is None:
        k_att, v_att = k, v
        s0_rwkv = jnp.zeros((B, N_DIRS, RWKV_HEADS, RWKV_HEAD, RWKV_HEAD), jnp.float32)
        h0_lru = jnp.zeros((B, N_DIRS, LRU_WIDTH), jnp.float32)
    else:
        ctx_k, ctx_v, s0_rwkv, h0_lru = cache
        ang_row, ang_col = axial_rope_angles(T)
        q = apply_axial_rope(q, ang_row, ang_col)
        k_att = jnp.concatenate([ctx_k.astype(k.dtype), apply_axial_rope(k, ang_row, ang_col)], axis=1)
        v_att = jnp.concatenate([ctx_v.astype(v.dtype), v], axis=1)
    att_out = blocked_attention(q, k_att, v_att)
    rwkv_out, s_rwkv = rwkv7_mix(p[..., o3:o4], s0_rwkv, lp['rwkv_mu'], lp['rwkv_w0'], lp['rwkv_w_up'],
                                 lp['rwkv_a0'], lp['rwkv_a_up'], lp['rwkv_g_up'], lp['rwkv_k_k'],
                                 lp['rwkv_k_a'], lp['rwkv_r_k'], lp['rwkv_ln_w'], lp['rwkv_ln_b'])
    lru_out, h_lru = rglru_mix(p[..., o4:o5], p[..., o5:], h0_lru, lp['lru_conv_w'], lp['lru_conv_b'],
                               lp['lru_wa'], lp['lru_ba'], lp['lru_wx'], lp['lru_bx'], lp['lru_lambda'])
    mixed = jnp.dot(jnp.concatenate([att_out, rwkv_out, lru_out], axis=-1), lp['w_out'])
    x = x + gt1 * mixed
    h2 = rms_norm(x, lp['norm2']) * (1.0 + sc2) + sh2
    x = x + gt2 * expert_choice_ffn(h2, lp['router'], lp['exp_w_gate'], lp['exp_w_up'], lp['exp_w_down'])
    return x, (k, v, s_rwkv.astype(x.dtype), h_lru.astype(x.dtype))


def setup_inputs(seed: int = 0) -> dict:
    key = jax.random.key(seed)
    keys = iter(jax.random.split(key, 64))

    def nrm(shape, scale):
        return jax.random.normal(next(keys), shape, jnp.float32) * scale

    def unif(shape, lo, hi):
        return jax.random.uniform(next(keys), shape, jnp.float32, lo, hi)

    L = DEPTH
    u = unif((L, N_DIRS, LRU_WIDTH), 0.9, 0.999)
    s = u ** (1.0 / LRU_C)
    lru_lambda = jnp.log(s) - jnp.log1p(-s)
    return {
        'x_prompt': nrm((BATCH, SEQ, D_MODEL), 1.0),
        'x_sample': nrm((DEC_BATCH, DEC_SEQ, D_MODEL), 1.0),
        'cache_k': nrm((DEC_BATCH, L, PAST_LEN, ATT_KV_HEADS, HEAD_DIM), 1.0),
        'cache_v': nrm((DEC_BATCH, L, PAST_LEN, ATT_KV_HEADS, HEAD_DIM), 1.0),
        'state_rwkv': nrm((DEC_BATCH, L, N_DIRS, RWKV_HEADS, RWKV_HEAD, RWKV_HEAD), 0.3),
        'state_lru': nrm((DEC_BATCH, L, N_DIRS, LRU_WIDTH), 0.5),
        'c': nrm((DEC_BATCH, D_MODEL), 1.0),
        'c_ctx': nrm((D_MODEL,), 1.0),
        'w_ada': nrm((L, D_MODEL, 6 * D_MODEL), 0.5 * D_MODEL ** -0.5),
        'b_ada': nrm((L, 6 * D_MODEL), 0.01),
        'norm1': 1.0 + nrm((L, D_MODEL), 0.02),
        'norm2': 1.0 + nrm((L, D_MODEL), 0.02),
        'w_in': nrm((L, D_MODEL, IN_COLS), D_MODEL ** -0.5),
        'w_out': nrm((L, MIX_WIDTH, D_MODEL), MIX_WIDTH ** -0.5),
        'q_norm': 1.0 + nrm((L, HEAD_DIM), 0.02),
        'k_norm': 1.0 + nrm((L, HEAD_DIM), 0.02),
        'rwkv_mu': unif((L, RWKV_COLS), 0.0, 1.0),
        'rwkv_w0': unif((L, N_DIRS, RWKV_WIDTH), -6.0, 1.0),
        'rwkv_w_up': nrm((L, N_DIRS, RWKV_DECAY_LORA, RWKV_WIDTH), 0.1),
        'rwkv_a0': nrm((L, N_DIRS, RWKV_WIDTH), 0.1),
        'rwkv_a_up': nrm((L, N_DIRS, RWKV_AAA_LORA, RWKV_WIDTH), 0.1),
        'rwkv_g_up': nrm((L, RWKV_GATE_LORA, RWKV_WIDTH), RWKV_GATE_LORA ** -0.5),
        'rwkv_k_k': 0.85 + nrm((L, RWKV_WIDTH), 0.05),
        'rwkv_k_a': 1.0 + nrm((L, RWKV_WIDTH), 0.05),
        'rwkv_r_k': nrm((L, RWKV_HEADS, RWKV_HEAD), 0.1),
        'rwkv_ln_w': 1.0 + nrm((L, RWKV_WIDTH), 0.02),
        'rwkv_ln_b': nrm((L, RWKV_WIDTH), 0.01),
        'lru_conv_w': nrm((L, LRU_CONV_W, LRU_WIDTH), LRU_CONV_W ** -0.5),
        'lru_conv_b': nrm((L, LRU_WIDTH), 0.01),
        'lru_wa': nrm((L, N_DIRS, LRU_BLOCKS, LRU_BLOCK, LRU_BLOCK), LRU_BLOCK ** -0.5),
        'lru_ba': nrm((L, N_DIRS, LRU_WIDTH), 0.01),
        'lru_wx': nrm((L, N_DIRS, LRU_BLOCKS, LRU_BLOCK, LRU_BLOCK), LRU_BLOCK ** -0.5),
        'lru_bx': nrm((L, N_DIRS, LRU_WIDTH), 0.01),
        'lru_lambda': lru_lambda,
        'router': nrm((L, D_MODEL, N_EXPERTS), D_MODEL ** -0.5),
        'exp_w_gate': nrm((L, N_EXPERTS, D_MODEL, EXPERT_FF), D_MODEL ** -0.5),
        'exp_w_up': nrm((L, N_EXPERTS, D_MODEL, EXPERT_FF), D_MODEL ** -0.5),
        'exp_w_down': nrm((L, N_EXPERTS, EXPERT_FF, D_MODEL), EXPERT_FF ** -0.5),
    }


def reference(x_prompt, x_sample, cache_k, cache_v, state_rwkv, state_lru, c, c_ctx,
              w_ada, b_ada, norm1, norm2, w_in, w_out, q_norm, k_norm,
              rwkv_mu, rwkv_w0, rwkv_w_up, rwkv_a0, rwkv_a_up, rwkv_g_up, rwkv_k_k, rwkv_k_a,
              rwkv_r_k, rwkv_ln_w, rwkv_ln_b,
              lru_conv_w, lru_conv_b, lru_wa, lru_ba, lru_wx, lru_bx, lru_lambda,
              router, exp_w_gate, exp_w_up, exp_w_down):
    ctx_cond = jnp.broadcast_to(c_ctx, (x_prompt.shape[0], c_ctx.shape[0]))
    y_prompt = x_prompt
    y_sample = x_sample
    ks, vs, srs, sls = [], [], [], []
    for l in range(DEPTH):
        lp = {
            'w_ada': w_ada[l], 'b_ada': b_ada[l], 'norm1': norm1[l], 'norm2': norm2[l],
            'w_in': w_in[l], 'w_out': w_out[l], 'q_norm': q_norm[l], 'k_norm': k_norm[l],
            'rwkv_mu': rwkv_mu[l], 'rwkv_w0': rwkv_w0[l], 'rwkv_w_up': rwkv_w_up[l],
            'rwkv_a0': rwkv_a0[l], 'rwkv_a_up': rwkv_a_up[l], 'rwkv_g_up': rwkv_g_up[l],
            'rwkv_k_k': rwkv_k_k[l], 'rwkv_k_a': rwkv_k_a[l], 'rwkv_r_k': rwkv_r_k[l],
            'rwkv_ln_w': rwkv_ln_w[l], 'rwkv_ln_b': rwkv_ln_b[l],
            'lru_conv_w': lru_conv_w[l], 'lru_conv_b': lru_conv_b[l], 'lru_wa': lru_wa[l],
            'lru_ba': lru_ba[l], 'lru_wx': lru_wx[l], 'lru_bx': lru_bx[l], 'lru_lambda': lru_lambda[l],
            'router': router[l], 'exp_w_gate': exp_w_gate[l], 'exp_w_up': exp_w_up[l],
            'exp_w_down': exp_w_down[l],
        }
        y_prompt, (k_l, v_l, sr_l, sl_l) = trunk_layer(y_prompt, ctx_cond, lp, None)
        ks.append(k_l)
        vs.append(v_l)
        srs.append(sr_l)
        sls.append(sl_l)
        y_sample, _ = trunk_layer(y_sample, c, lp,
                                  (cache_k[:, l], cache_v[:, l], state_rwkv[:, l], state_lru[:, l]))
    new_cache_k = jnp.stack(ks, axis=1)
    new_cache_v = jnp.stack(vs, axis=1)
    new_state_rwkv = jnp.stack(srs, axis=1)
    new_state_lru = jnp.stack(sls, axis=1)
    return (y_prompt, y_sample, new_cache_k, new_cache_v, new_state_rwkv, new_state_lru)
```

```python
import functools

import jax
import jax.numpy as jnp
from jax import lax
from jax.experimental import pallas as pl
from jax.experimental.pallas import tpu as pltpu

F32 = jnp.float32
BF16 = jnp.bfloat16
HIGHEST = lax.Precision.HIGHEST

D_MODEL = 1024
DEPTH = 2
GRID_W = 64
ATT_HEADS = 8
ATT_KV_HEADS = 2
HEAD_DIM = 64
ATT_WIDTH = ATT_HEADS * HEAD_DIM
KV_WIDTH = ATT_KV_HEADS * HEAD_DIM
ROPE_THETA = 10000.0
RWKV_HEADS = 4
RWKV_HEAD = 64
RWKV_WIDTH = RWKV_HEADS * RWKV_HEAD
RWKV_DECAY_LORA = 32
RWKV_AAA_LORA = 32
RWKV_GATE_LORA = 64
RWKV_COLS = 3 * RWKV_WIDTH + RWKV_DECAY_LORA + RWKV_AAA_LORA + RWKV_GATE_LORA
GN_EPS = 64e-5
LRU_BLOCKS = 4
LRU_BLOCK = 64
LRU_WIDTH = LRU_BLOCKS * LRU_BLOCK
LRU_CONV_W = 4
LRU_C = 8.0
IN_COLS = ATT_WIDTH + 2 * KV_WIDTH + RWKV_COLS + 2 * LRU_WIDTH
N_EXPERTS = 16
EC_FACTOR = 2
EXPERT_FF = 1024
RMS_EPS = 1e-6

LANES = 128
SUBLANES = 8
VMEM_LIMIT = 56 * 1024 * 1024
N_MOD = 6
COND_ROWS = 8
ONE_BITS = 0x3F800000


def _params(*sem):
    return pltpu.CompilerParams(dimension_semantics=sem, vmem_limit_bytes=VMEM_LIMIT)


def _dot(a, b, precision=None):
    return jnp.dot(a, b, preferred_element_type=F32, precision=precision)


def _sigmoid(x):
    return 1.0 / (1.0 + jnp.exp(-x))


def _softplus(x):
    return jnp.maximum(x, 0.0) + jnp.log1p(jnp.exp(-jnp.abs(x)))


def _seg_sum(x, ones_bd):
    hi = x.astype(BF16)
    lo = (x - hi.astype(F32)).astype(BF16)
    return _dot(hi, ones_bd) + _dot(lo, ones_bd)


def _ada_kernel(c_ref, w_ref, b_ref, o_ref):
    c = c_ref[...]
    s = c * _sigmoid(c)
    o_ref[...] = _dot(s, w_ref[...], HIGHEST) + b_ref[...]


def _ada(cond, w_ada, b_ada):
    n_l = w_ada.shape[0]
    tn = 1536
    n_t = N_MOD * D_MODEL // tn
    return pl.pallas_call(
        _ada_kernel,
        grid=(n_l, n_t),
        in_specs=[
            pl.BlockSpec((COND_ROWS, D_MODEL), lambda l, j: (0, 0)),
            pl.BlockSpec((None, D_MODEL, tn), lambda l, j: (l, 0, j)),
            pl.BlockSpec((None, 1, tn), lambda l, j: (l, 0, j)),
        ],
        out_specs=pl.BlockSpec((None, COND_ROWS, tn), lambda l, j: (l, 0, j)),
        out_shape=jax.ShapeDtypeStruct((n_l, COND_ROWS, N_MOD * D_MODEL), F32),
        compiler_params=_params("arbitrary", "arbitrary"),
        name="ada",
    )(cond, w_ada, b_ada)


def _head_rms(x, gain, ones_bd):
    ms = _seg_sum(x * x, ones_bd) * (1.0 / HEAD_DIM)
    return x * lax.rsqrt(ms + RMS_EPS) * gain


def _rope(x, cos, sin_up, sin_dn):
    return x * cos + pltpu.roll(x, LANES - 16, 1) * sin_up + pltpu.roll(x, 16, 1) * sin_dn


def _inproj_kernel(*refs, rope):
    if rope:
        (x_ref, mod_ref, n1_ref, w_ref, qn_ref, kn_ref, bd_ref, cos_ref, su_ref, sd_ref,
         q_ref, ko_ref, ka_ref, v_ref, pr_ref, lx_ref, lg_ref) = refs
    else:
        (x_ref, mod_ref, n1_ref, w_ref, qn_ref, kn_ref, bd_ref,
         q_ref, ko_ref, v_ref, pr_ref, lx_ref, lg_ref) = refs
    x = x_ref[...]
    rs = lax.rsqrt(jnp.mean(x * x, axis=-1, keepdims=True) + RMS_EPS)
    h = (x * rs * n1_ref[...]) * (1.0 + mod_ref[1:2, :]) + mod_ref[0:1, :]
    p = _dot(h.astype(BF16), w_ref[...])
    bd = bd_ref[...]
    for c in range(ATT_WIDTH // LANES):
        qc = _head_rms(p[:, c * LANES:(c + 1) * LANES], qn_ref[...], bd)
        if rope:
            qc = _rope(qc, cos_ref[...], su_ref[...], sd_ref[...])
        q_ref[:, c * LANES:(c + 1) * LANES] = qc
    o = ATT_WIDTH
    kc = _head_rms(p[:, o:o + KV_WIDTH], kn_ref[...], bd)
    ko_ref[...] = kc
    if rope:
        ka_ref[...] = _rope(kc, cos_ref[...], su_ref[...], sd_ref[...])
    o += KV_WIDTH
    v_ref[...] = p[:, o:o + KV_WIDTH]
    o += KV_WIDTH
    pr_ref[...] = p[:, o:o + RWKV_COLS]
    o += RWKV_COLS
    lx_ref[...] = p[:, o:o + LRU_WIDTH]
    o += LRU_WIDTH
    lg_ref[...] = p[:, o:o + LRU_WIDTH]


def _inproj(x, mod_l, norm1, w_in_bf, layer, qn, kn, bd, rope_tabs, seq, cond_row0):
    n_tok = x.shape[0]
    tm = 512 if seq >= 512 else seq * (512 // seq)
    per_seq = max(seq // tm, 1)
    n_t = n_tok // tm
    rope = rope_tabs is not None
    if cond_row0 == 0:
        row = lambda i: 0
    else:
        row = lambda i: cond_row0 + i // per_seq
    tok = lambda w: pl.BlockSpec((tm, w), lambda i: (i, 0))
    const = lambda shape: pl.BlockSpec(shape, lambda i: tuple(0 for _ in shape))
    in_specs = [
        tok(D_MODEL),
        pl.BlockSpec((None, N_MOD, D_MODEL), lambda i: (row(i), 0, 0)),
        const((1, D_MODEL)),
        pl.BlockSpec((None, D_MODEL, IN_COLS), lambda i: (layer, 0, 0)),
        const((1, LANES)), const((1, LANES)), const((LANES, LANES)),
    ]
    args = [x, mod_l, norm1, w_in_bf, qn, kn, bd]
    outs = [ATT_WIDTH, KV_WIDTH]
    if rope:
        in_specs += [pl.BlockSpec((tm, LANES), lambda i: (i % per_seq, 0))] * 3
        args += list(rope_tabs)
        outs.append(KV_WIDTH)
    outs += [KV_WIDTH, RWKV_COLS, LRU_WIDTH, LRU_WIDTH]
    return pl.pallas_call(
        functools.partial(_inproj_kernel, rope=rope),
        grid=(n_t,),
        in_specs=in_specs,
        out_specs=[tok(w) for w in outs],
        out_shape=[jax.ShapeDtypeStruct((n_tok, w), F32) for w in outs],
        compiler_params=_params("arbitrary"),
        name="inproj_rope" if rope else "inproj",
    )(*args)


def _attn_kernel(*refs, has_cache):
    if has_cache:
        q_ref, k_ref, v_ref, ck_ref, cv_ref, o_ref, kn_s, ks_s, vn_s, vs_s = refs
    else:
        q_ref, k_ref, v_ref, o_ref, kn_s, ks_s, vn_s, vs_s = refs

    @pl.when(pl.program_id(1) == 0)
    def _():
        k = k_ref[...]
        v = v_ref[...]
        if has_cache:
            k = jnp.concatenate([ck_ref[...], k], axis=0)
            v = jnp.concatenate([cv_ref[...], v], axis=0)
        kn_s[...] = k.astype(BF16)
        ks_s[...] = pltpu.roll(k, HEAD_DIM, 1).astype(BF16)
        vn_s[...] = v.astype(BF16)
        vs_s[...] = pltpu.roll(v, HEAD_DIM, 1).astype(BF16)

    tq = q_ref.shape[0]
    lo = lax.broadcasted_iota(jnp.int32, (tq, LANES), 1) < HEAD_DIM
    rep = ATT_HEADS // ATT_KV_HEADS
    for c in range(ATT_WIDTH // LANES):
        qc = q_ref[:, c * LANES:(c + 1) * LANES] * (HEAD_DIM ** -0.5)
        halves = []
        for half in range(2):
            g = (2 * c + half) // rep
            qm = jnp.where(lo if half == 0 else jnp.logical_not(lo), qc, 0.0).astype(BF16)
            k_s, v_s = (kn_s, vn_s) if half == g else (ks_s, vs_s)
            s = lax.dot_general(qm, k_s[...], (((1,), (1,)), ((), ())), preferred_element_type=F32)
            m = jnp.max(s, axis=-1, keepdims=True)
            e = jnp.exp(s - m)
            l = jnp.sum(e, axis=-1, keepdims=True)
            halves.append(_dot(e.astype(BF16), v_s[...]) / l)
        o_ref[:, c * LANES:(c + 1) * LANES] = jnp.where(lo, halves[0], halves[1])


def _attention(q, k, v, n_b, seq, cache=None):
    tq = 256
    n_q = seq // tq
    has_cache = cache is not None
    past = cache[0].shape[1] if has_cache else 0
    in_specs = [
        pl.BlockSpec((tq, ATT_WIDTH), lambda b, i: (b * n_q + i, 0)),
        pl.BlockSpec((seq, KV_WIDTH), lambda b, i: (b, 0)),
        pl.BlockSpec((seq, KV_WIDTH), lambda b, i: (b, 0)),
    ]
    args = [q, k, v]
    if has_cache:
        in_specs += [pl.BlockSpec((None, past, KV_WIDTH), lambda b, i: (b, 0, 0))] * 2
        args += list(cache)
    return pl.pallas_call(
        functools.partial(_attn_kernel, has_cache=has_cache),
        grid=(n_b, n_q),
        in_specs=in_specs,
        out_specs=pl.BlockSpec((tq, ATT_WIDTH), lambda b, i: (b * n_q + i, 0)),
        out_shape=jax.ShapeDtypeStruct((n_b * seq, ATT_WIDTH), F32),
        scratch_shapes=[pltpu.VMEM((past + seq, KV_WIDTH), BF16)] * 4,
        compiler_params=_params("arbitrary", "arbitrary"),
        name="attn_cache" if has_cache else "attn",
    )(*args)


def _rwkv_prep_kernel(p_ref, mu_ref, kk_ref, ka_ref, rk_ref, w0_ref, a0_ref, wup_ref, aup_ref,
                      gup_ref, bd_ref,
                      r_ref, v_ref, na_ref, w_ref, k_ref, b_ref, g_ref, bonus_ref):
    tm = r_ref.shape[0]
    seq = p_ref.shape[0]
    i = pl.program_id(1)
    n_t = pl.num_programs(1)
    start = pl.multiple_of(i * tm, tm)
    cur = p_ref[pl.ds(start, tm), :]
    prev_base = pl.multiple_of(jnp.maximum(start - SUBLANES, 0), SUBLANES)
    next_base = pl.multiple_of(jnp.minimum(start + tm, seq - SUBLANES), SUBLANES)
    prev_row = p_ref[pl.ds(prev_base, SUBLANES), :][SUBLANES - 1:SUBLANES]
    next_row = p_ref[pl.ds(next_base, SUBLANES), :][0:1]
    prev_row = jnp.where(i > 0, prev_row, 0.0)
    next_row = jnp.where(i < n_t - 1, next_row, 0.0)
    row = lax.broadcasted_iota(jnp.int32, cur.shape, 0)
    prev = jnp.where(row == 0, prev_row, pltpu.roll(cur, 1, 0))
    nxt = jnp.where(row == tm - 1, next_row, pltpu.roll(cur, tm - 1, 0))
    ps = cur + mu_ref[...] * (0.5 * (prev + nxt) - cur)

    W = RWKV_WIDTH
    r = ps[:, :W]
    k = ps[:, W:2 * W]
    v = ps[:, 2 * W:3 * W]
    lora = ps[:, 3 * W:]
    r_ref[...] = r
    v_ref[...] = v
    g_ref[...] = _dot(_sigmoid(lora), gup_ref[...], HIGHEST)
    lora_t = jnp.tanh(lora)
    kk = k * kk_ref[...]
    kk_parts = []
    for c in range(W // LANES):
        kc = kk[:, c * LANES:(c + 1) * LANES]
        nrm = jnp.sqrt(_seg_sum(kc * kc, bd_ref[...]))
        kk_parts.append(kc / jnp.maximum(nrm, 1e-12))
    kk = jnp.concatenate(kk_parts, axis=-1)
    na_ref[...] = -kk
    bonus = None
    for d in range(2):
        w_log = -_softplus(-(w0_ref[d:d + 1, :] + _dot(lora_t, wup_ref[d], HIGHEST))) - 0.5
        w_ref[d] = jnp.exp(-jnp.exp(w_log))
        a_rate = _sigmoid(a0_ref[d:d + 1, :] + _dot(lora, aup_ref[d], HIGHEST))
        kd = k * (1.0 + (a_rate - 1.0) * ka_ref[...])
        k_ref[d] = kd
        b_ref[d] = kk * a_rate
        rkr = r * kd * rk_ref[...]
        parts = [_seg_sum(rkr[:, c * LANES:(c + 1) * LANES], bd_ref[...]) for c in range(W // LANES)]
        bd_term = jnp.concatenate(parts, axis=-1) * v
        bonus = bd_term if bonus is None else bonus + bd_term
    bonus_ref[...] = bonus


def _rwkv_prep(p, n_b, seq, wts, bd):
    mu, k_k, k_a, r_k, w0, a0, wup, aup, gup = wts
    tm = min(seq, 512)
    n_t = seq // tm
    W = RWKV_WIDTH
    const = lambda shape: pl.BlockSpec(shape, lambda b, i: tuple(0 for _ in shape))
    tok = pl.BlockSpec((tm, W), lambda b, i: (b * n_t + i, 0))
    tok2 = pl.BlockSpec((2, tm, W), lambda b, i: (0, b * n_t + i, 0))
    one = jax.ShapeDtypeStruct((n_b * seq, W), F32)
    two = jax.ShapeDtypeStruct((2, n_b * seq, W), F32)
    return pl.pallas_call(
        _rwkv_prep_kernel,
        grid=(n_b, n_t),
        in_specs=[
            pl.BlockSpec((None, seq, RWKV_COLS), lambda b, i: (b, 0, 0)),
            const((1, RWKV_COLS)), const((1, W)), const((1, W)), const((1, W)),
            const((2, W)), const((2, W)),
            const((2, LANES, W)), const((2, LANES, W)), const((LANES, W)),
            const((LANES, LANES)),
        ],
        out_specs=[tok, tok, tok, tok2, tok2, tok2, tok, tok],
        out_shape=[one, one, one, two, two, two, one, one],
        compiler_params=_params("arbitrary", "arbitrary"),
        name="rwkv_prep",
    )(p, mu, k_k, k_a, r_k, w0, a0, wup, aup, gup, bd)


RWKV_BB = 2
RWKV_PAIRS = RWKV_HEADS * RWKV_HEAD // LANES
RWKV_CHUNK = 256


def _rwkv_scan_kernel(rf_ref, vf_ref, af_ref, wf_ref, kf_ref, bf_ref,
                      rb_ref, vb_ref, ab_ref, wb_ref, kb_ref, bb_ref, s0_ref,
                      yf_ref, yb_ref, sfin_ref, st_s):
    j = pl.program_id(1)
    tc = rf_ref.shape[1]

    @pl.when(j == 0)
    def _():
        st_s[...] = s0_ref[...]

    lane = lax.broadcasted_iota(jnp.int32, (RWKV_HEAD, LANES), 1)
    rowi = lax.broadcasted_iota(jnp.int32, (RWKV_HEAD, LANES), 0)
    lo = lane < RWKV_HEAD
    diag = (lane % RWKV_HEAD) == rowi

    def seg_bcast(x):
        s_lo = jnp.sum(jnp.where(lo, x, 0.0), axis=-1, keepdims=True)
        s_hi = jnp.sum(jnp.where(lo, 0.0, x), axis=-1, keepdims=True)
        return jnp.where(lo, s_lo, s_hi)

    dirs = ((rf_ref, vf_ref, af_ref, wf_ref, kf_ref, bf_ref, yf_ref),
            (rb_ref, vb_ref, ab_ref, wb_ref, kb_ref, bb_ref, yb_ref))

    def tile_step(i, carry):
        for bi in range(RWKV_BB):
            for d in range(2):
                r_r, v_r, a_r, w_r, k_r, b_r, y_r = dirs[d]
                base = i * SUBLANES if d == 0 else tc - SUBLANES - i * SUBLANES
                base = pl.multiple_of(base, SUBLANES)
                order = range(SUBLANES) if d == 0 else range(SUBLANES - 1, -1, -1)
                for pr in range(RWKV_PAIRS):
                    ls = slice(pr * LANES, (pr + 1) * LANES)
                    r8, v8, a8, w8, k8, b8 = (ref[bi, pl.ds(base, SUBLANES), ls]
                                              for ref in (r_r, v_r, a_r, w_r, k_r, b_r))
                    S = st_s[bi, d, pr]
                    rows = [None] * SUBLANES
                    for s in order:
                        sa = seg_bcast(S * a8[s:s + 1])
                        vcol = seg_bcast(jnp.where(diag, v8[s:s + 1], 0.0))
                        S = S * w8[s:s + 1] + sa * b8[s:s + 1] + vcol * k8[s:s + 1]
                        ycol = seg_bcast(S * r8[s:s + 1])
                        rows[s] = jnp.sum(jnp.where(diag, ycol, 0.0), axis=0, keepdims=True)
                    st_s[bi, d, pr] = S
                    y_r[bi, pl.ds(base, SUBLANES), ls] = jnp.concatenate(rows, axis=0)
        return carry

    lax.fori_loop(0, tc // SUBLANES, tile_step, 0)

    @pl.when(j == pl.num_programs(1) - 1)
    def _():
        sfin_ref[...] = st_s[...]


def _rwkv_scan(r, v, na, w, k, b, s0, n_b, seq):
    tc = min(seq, RWKV_CHUNK)
    n_c = seq // tc
    W = RWKV_WIDTH
    blk = (RWKV_BB, tc, W)
    fwd = pl.BlockSpec(blk, lambda g, j: (g, j, 0))
    bwd = pl.BlockSpec(blk, lambda g, j: (g, n_c - 1 - j, 0))
    fwd_d = pl.BlockSpec((None,) + blk, lambda g, j: (0, g, j, 0))
    bwd_d = pl.BlockSpec((None,) + blk, lambda g, j: (1, g, n_c - 1 - j, 0))
    st_blk = (RWKV_BB, 2, RWKV_PAIRS, RWKV_HEAD, LANES)
    st_spec = pl.BlockSpec(st_blk, lambda g, j: (g, 0, 0, 0, 0))
    y_shape = jax.ShapeDtypeStruct((n_b, seq, W), F32)
    return pl.pallas_call(
        _rwkv_scan_kernel,
        grid=(n_b // RWKV_BB, n_c),
        in_specs=[fwd, fwd, fwd, fwd_d, fwd_d, fwd_d, bwd, bwd, bwd, bwd_d, bwd_d, bwd_d, st_spec],
        out_specs=[fwd, bwd, st_spec],
        out_shape=[y_shape, y_shape, jax.ShapeDtypeStruct(s0.shape, F32)],
        scratch_shapes=[pltpu.VMEM(st_blk, F32)],
        compiler_params=_params("arbitrary", "arbitrary"),
        name="rwkv_scan",
    )(r, v, na, w, k, b, r, v, na, w, k, b, s0)


def _lru_kernel(x_ref, g_ref, h0_ref, cw_ref, cb_ref, wa_ref, ba_ref, wx_ref, bx_ref, lam_ref,
                o_ref, hfin_ref, a_s, u_s, h_s):
    seq = x_ref.shape[0]
    x = x_ref[...]
    row = lax.broadcasted_iota(jnp.int32, x.shape, 0)
    xm2 = jnp.where(row >= 2, pltpu.roll(x, 2, 0), 0.0)
    xm1 = jnp.where(row >= 1, pltpu.roll(x, 1, 0), 0.0)
    xp1 = jnp.where(row < seq - 1, pltpu.roll(x, seq - 1, 0), 0.0)
    xc = (xm2 * cw_ref[0:1, :] + xm1 * cw_ref[1:2, :] + x * cw_ref[2:3, :] + xp1 * cw_ref[3:4, :]
          + cb_ref[...])
    xb = xc.astype(BF16)
    for d in range(2):
        r_gate = _sigmoid(_dot(xb, wa_ref[d]) + ba_ref[d:d + 1, :])
        i_gate = _sigmoid(_dot(xb, wx_ref[d]) + bx_ref[d:d + 1, :])
        log_a = -LRU_C * r_gate * _softplus(-lam_ref[d:d + 1, :])
        a_s[d] = jnp.exp(log_a)
        u_s[d] = jnp.sqrt(1.0 - jnp.exp(2.0 * log_a)) * (i_gate * xc)

    n_t = seq // SUBLANES

    def body(i, carry):
        hf, hb = carry
        base = pl.multiple_of(i * SUBLANES, SUBLANES)
        a = a_s[0, pl.ds(base, SUBLANES), :]
        u = u_s[0, pl.ds(base, SUBLANES), :]
        rows = []
        for s in range(SUBLANES):
            hf = a[s:s + 1] * hf + u[s:s + 1]
            rows.append(hf)
        h_s[0, pl.ds(base, SUBLANES), :] = jnp.concatenate(rows, axis=0)
        base = pl.multiple_of(seq - SUBLANES - i * SUBLANES, SUBLANES)
        a = a_s[1, pl.ds(base, SUBLANES), :]
        u = u_s[1, pl.ds(base, SUBLANES), :]
        rows = [None] * SUBLANES
        for s in range(SUBLANES - 1, -1, -1):
            hb = a[s:s + 1] * hb + u[s:s + 1]
            rows[s] = hb
        h_s[1, pl.ds(base, SUBLANES), :] = jnp.concatenate(rows, axis=0)
        return hf, hb

    hf, hb = lax.fori_loop(0, n_t, body, (h0_ref[0:1, :], h0_ref[1:2, :]))
    hfin_ref[...] = jnp.concatenate([hf, hb], axis=0)
    g = g_ref[...]
    gelu = 0.5 * g * (1.0 + jnp.tanh(0.7978845608028654 * (g + 0.044715 * (g * g * g))))
    o_ref[...] = (h_s[0] + h_s[1]) * gelu


def _lru(xb, gb, h0, wts, n_b, seq):
    cw, cb, wa, ba, wx, bx, lam = wts
    C = LRU_WIDTH
    const = lambda shape: pl.BlockSpec(shape, lambda b: tuple(0 for _ in shape))
    tok = pl.BlockSpec((seq, C), lambda b: (b, 0))
    st = pl.BlockSpec((None, 2, C), lambda b: (b, 0, 0))
    return pl.pallas_call(
        _lru_kernel,
        grid=(n_b,),
        in_specs=[tok, tok, st, const((LRU_CONV_W, C)), const((1, C)), const((2, C, C)), const((2, C)),
                  const((2, C, C)), const((2, C)), const((2, C))],
        out_specs=[tok, st],
        out_shape=[jax.ShapeDtypeStruct((n_b * seq, C), F32), jax.ShapeDtypeStruct((n_b, 2, C), F32)],
        scratch_shapes=[pltpu.VMEM((2, seq, C), F32)] * 3,
        compiler_params=_params("arbitrary"),
        name="lru",
    )(xb, gb, h0, cw, cb, wa, ba, wx, bx, lam)


def _outproj_kernel(x_ref, att_ref, yf_ref, yb_ref, bonus_ref, g_ref, lru_ref, mod_ref, n2_ref,
                    lnw_ref, lnb_ref, w_ref, rt_ref, bd_ref,
                    x1_ref, h2_ref, lg_ref):
    y = yf_ref[...] + yb_ref[...]
    parts = []
    for c in range(RWKV_WIDTH // LANES):
        yc = y[:, c * LANES:(c + 1) * LANES]
        mean = _seg_sum(yc, bd_ref[...]) * (1.0 / RWKV_HEAD)
        dev = yc - mean
        var = _seg_sum(dev * dev, bd_ref[...]) * (1.0 / RWKV_HEAD)
        parts.append(dev * lax.rsqrt(var + GN_EPS))
    yn = jnp.concatenate(parts, axis=-1) * lnw_ref[...] + lnb_ref[...]
    rwkv = (yn + bonus_ref[...]) * g_ref[...]
    o1 = ATT_WIDTH
    o2 = o1 + RWKV_WIDTH
    mixed = (_dot(att_ref[...].astype(BF16), w_ref[:o1, :])
             + _dot(rwkv.astype(BF16), w_ref[o1:o2, :])
             + _dot(lru_ref[...].astype(BF16), w_ref[o2:, :]))
    x1 = x_ref[...] + mod_ref[2:3, :] * mixed
    x1_ref[...] = x1
    rs = lax.rsqrt(jnp.mean(x1 * x1, axis=-1, keepdims=True) + RMS_EPS)
    h2 = (x1 * rs * n2_ref[...]) * (1.0 + mod_ref[4:5, :]) + mod_ref[3:4, :]
    h2_ref[...] = h2.astype(BF16)
    lg_ref[...] = _dot(h2, rt_ref[...], HIGHEST)


def _outproj(x, att, yf, yb, bonus, g, lru, mod_l, norm2, lnw, lnb, w_out_bf, layer, router_pad, bd,
             seq, cond_row0):
    n_tok = x.shape[0]
    tm = 512 if seq >= 512 else seq * (512 // seq)
    per_seq = max(seq // tm, 1)
    n_t = n_tok // tm
    if cond_row0 == 0:
        row = lambda i: 0
    else:
        row = lambda i: cond_row0 + i // per_seq
    tok = lambda w: pl.BlockSpec((tm, w), lambda i: (i, 0))
    const = lambda shape: pl.BlockSpec(shape, lambda i: tuple(0 for _ in shape))
    W = RWKV_WIDTH
    return pl.pallas_call(
        _outproj_kernel,
        grid=(n_t,),
        in_specs=[
            tok(D_MODEL), tok(ATT_WIDTH), tok(W), tok(W), tok(W), tok(W), tok(LRU_WIDTH),
            pl.BlockSpec((None, N_MOD, D_MODEL), lambda i: (row(i), 0, 0)),
            const((1, D_MODEL)), const((1, W)), const((1, W)),
            pl.BlockSpec((None, D_MODEL, D_MODEL), lambda i: (layer, 0, 0)),
            const((D_MODEL, LANES)), const((LANES, LANES)),
        ],
        out_specs=[tok(D_MODEL), tok(D_MODEL), tok(LANES)],
        out_shape=[jax.ShapeDtypeStruct((n_tok, D_MODEL), F32),
                   jax.ShapeDtypeStruct((n_tok, D_MODEL), BF16),
                   jax.ShapeDtypeStruct((n_tok, LANES), F32)],
        compiler_params=_params("arbitrary"),
        name="outproj",
    )(x, att, yf, yb, bonus, g, lru, mod_l, norm2, lnw, lnb, w_out_bf, router_pad, bd)


PREFIX_BLOCK = 256


def _prefix_count(mask_f, tri):
    seq = mask_f.shape[0]
    outs = []
    carry = jnp.zeros((1, LANES), F32)
    for blk in range(seq // PREFIX_BLOCK):
        m = mask_f[blk * PREFIX_BLOCK:(blk + 1) * PREFIX_BLOCK]
        outs.append(_dot(tri, m.astype(BF16)) + carry)
        carry = carry + jnp.sum(m, axis=0, keepdims=True)
    return jnp.concatenate(outs, axis=0) if len(outs) > 1 else outs[0]


def _route_kernel(lg_ref, h2_ref, tri_ref, xs_ref, slot_ref, aff_ref, *, cap):
    seq = lg_ref.shape[0]
    lane = lax.broadcasted_iota(jnp.int32, (seq, LANES), 1)
    real = lane < N_EXPERTS
    lg = jnp.where(real, lg_ref[...], -jnp.inf)
    m = jnp.max(lg, axis=-1, keepdims=True)
    e = jnp.exp(lg - m)
    aff = e / jnp.sum(e, axis=-1, keepdims=True)
    aff_ref[...] = aff
    bits = lax.bitcast_convert_type(aff, jnp.int32)

    def bisect(_, carry):
        lo_b, hi_b = carry
        mid = lo_b + ((hi_b - lo_b) >> 1)
        cnt = jnp.sum(jnp.where(bits >= mid, 1.0, 0.0), axis=0, keepdims=True)
        ok = cnt >= cap
        return jnp.where(ok, mid, lo_b), jnp.where(ok, hi_b, mid)

    lo_b = jnp.zeros((1, LANES), jnp.int32)
    hi_b = jnp.full((1, LANES), ONE_BITS + 1, jnp.int32)
    thr, _ = lax.fori_loop(0, 32, bisect, (lo_b, hi_b))
    gt = jnp.where(bits > thr, 1.0, 0.0)
    eq = jnp.where(bits == thr, 1.0, 0.0)
    need = cap - jnp.sum(gt, axis=0, keepdims=True)
    tri = tri_ref[...]
    sel = jnp.where(real, gt + eq * jnp.where(_prefix_count(eq, tri) < need, 1.0, 0.0), 0.0)
    slot = jnp.where(sel > 0.0, _prefix_count(sel, tri), -1.0)
    slot_ref[...] = slot

    slot_t = slot.T
    h2 = h2_ref[...]
    c_iota = lax.broadcasted_iota(jnp.int32, (cap, seq), 0).astype(F32)
    for ex in range(N_EXPERTS):
        onehot = jnp.where(c_iota == slot_t[ex:ex + 1, :], 1.0, 0.0).astype(BF16)
        xs_ref[ex] = _dot(onehot, h2).astype(BF16)


def _route(logits, h2, tri, n_b, seq):
    cap = EC_FACTOR * seq // N_EXPERTS
    tok = lambda w: pl.BlockSpec((seq, w), lambda b: (b, 0))
    return pl.pallas_call(
        functools.partial(_route_kernel, cap=cap),
        grid=(n_b,),
        in_specs=[tok(LANES), tok(D_MODEL), pl.BlockSpec((PREFIX_BLOCK, PREFIX_BLOCK), lambda b: (0, 0))],
        out_specs=[pl.BlockSpec((N_EXPERTS, cap, D_MODEL), lambda b: (0, b, 0)), tok(LANES), tok(LANES)],
        out_shape=[jax.ShapeDtypeStruct((N_EXPERTS, n_b * cap, D_MODEL), BF16),
                   jax.ShapeDtypeStruct((n_b * seq, LANES), F32),
                   jax.ShapeDtypeStruct((n_b * seq, LANES), F32)],
        compiler_params=_params("arbitrary"),
        name="route",
    )(logits, h2, tri)


EXPERT_FT = 512
EXPERT_RB = 512


def _expert_kernel(xa_ref, xb_ref, wg_ref, wu_ref, wd_ref, ya_ref, yb_ref, acc_a, acc_b):
    f = pl.program_id(1)
    last = pl.num_programs(1) - 1
    wg = wg_ref[...].astype(BF16)
    wu = wu_ref[...].astype(BF16)
    wd = wd_ref[...].astype(BF16)
    for x_ref, y_ref, acc in ((xa_ref, ya_ref, acc_a), (xb_ref, yb_ref, acc_b)):
        for rb in range(x_ref.shape[0] // EXPERT_RB):
            rows = slice(rb * EXPERT_RB, (rb + 1) * EXPERT_RB)
            x = x_ref[rows, :]
            a = _dot(x, wg)
            u = _dot(x, wu)
            hid = (a * _sigmoid(a) * u).astype(BF16)
            y = _dot(hid, wd)

            @pl.when(f == 0)
            def _():
                acc[rows, :] = y

            @pl.when(jnp.logical_and(f > 0, f < last))
            def _():
                acc[rows, :] += y

            @pl.when(f == last)
            def _():
                y_ref[rows, :] = (acc[rows, :] + y).astype(BF16)


def _experts(xs_a, xs_b, w_gate, w_up, w_down, layer):
    n_f = EXPERT_FF // EXPERT_FT
    ma, mb = xs_a.shape[1], xs_b.shape[1]
    xspec = lambda m: pl.BlockSpec((None, m, D_MODEL), lambda e, f: (e, 0, 0))
    return pl.pallas_call(
        _expert_kernel,
        grid=(N_EXPERTS, n_f),
        in_specs=[
            xspec(ma), xspec(mb),
            pl.BlockSpec((None, None, D_MODEL, EXPERT_FT), lambda e, f: (layer, e, 0, f)),
            pl.BlockSpec((None, None, D_MODEL, EXPERT_FT), lambda e, f: (layer, e, 0, f)),
            pl.BlockSpec((None, None, EXPERT_FT, D_MODEL), lambda e, f: (layer, e, f, 0)),
        ],
        out_specs=[xspec(ma), xspec(mb)],
        out_shape=[jax.ShapeDtypeStruct(xs_a.shape, BF16), jax.ShapeDtypeStruct(xs_b.shape, BF16)],
        scratch_shapes=[pltpu.VMEM((ma, D_MODEL), F32), pltpu.VMEM((mb, D_MODEL), F32)],
        compiler_params=_params("arbitrary", "arbitrary"),
        name="experts",
    )(xs_a, xs_b, w_gate, w_up, w_down)


def _combine_kernel(y_ref, slot_ref, aff_ref, x1_ref, mod_ref, o_ref, *, cap):
    seq = x1_ref.shape[0]
    c_iota = lax.broadcasted_iota(jnp.int32, (seq, cap), 1).astype(F32)
    slot = slot_ref[...]
    aff = aff_ref[...]
    acc = jnp.zeros((seq, D_MODEL), F32)
    for ex in range(N_EXPERTS):
        onehot = jnp.where(slot[:, ex:ex + 1] == c_iota, 1.0, 0.0).astype(BF16)
        acc = acc + aff[:, ex:ex + 1] * _dot(onehot, y_ref[ex])
    o_ref[...] = x1_ref[...] + mod_ref[5:6, :] * acc


def _combine(y, slot, aff, x1, mod_l, n_b, seq, cond_row0):
    cap = EC_FACTOR * seq // N_EXPERTS
    if cond_row0 == 0:
        row = lambda b: 0
    else:
        row = lambda b: cond_row0 + b
    tm = min(seq, 512)
    n_t = seq // tm
    tok = lambda w: pl.BlockSpec((tm, w), lambda b, i: (b * n_t + i, 0))
    return pl.pallas_call(
        functools.partial(_combine_kernel, cap=cap),
        grid=(n_b, n_t),
        in_specs=[
            pl.BlockSpec((N_EXPERTS, cap, D_MODEL), lambda b, i: (0, b, 0)),
            tok(LANES), tok(LANES), tok(D_MODEL),
            pl.BlockSpec((None, N_MOD, D_MODEL), lambda b, i: (row(b), 0, 0)),
        ],
        out_specs=tok(D_MODEL),
        out_shape=jax.ShapeDtypeStruct((n_b * seq, D_MODEL), F32),
        compiler_params=_params("arbitrary", "arbitrary"),
        name="combine",
    )(y, slot, aff, x1, mod_l)


def _rope_tables(seq):
    t = jnp.arange(seq, dtype=jnp.int32)
    row = (t // GRID_W).astype(F32)
    col = (t % GRID_W).astype(F32)
    half = HEAD_DIM // 2
    inv = ROPE_THETA ** (-jnp.arange(0, half, 2, dtype=F32) / half)
    lane = jnp.arange(LANES)
    u = lane % HEAD_DIM
    pos = jnp.where((u // half)[None, :] == 0, row[:, None], col[:, None])
    ang = pos * inv[(u % half) % (half // 2)][None, :]
    first = ((u % half) < half // 2)[None, :]
    sin = jnp.sin(ang)
    return jnp.cos(ang), jnp.where(first, -sin, 0.0), jnp.where(first, 0.0, sin)


def _block_diag(w):
    n, k, _ = w.shape
    eye = jnp.eye(n, dtype=w.dtype)
    return (eye[:, None, :, None] * w[:, :, None, :]).reshape(n * k, n * k)


def _pad_rows(w, offset, total=LANES):
    return jnp.zeros((total, w.shape[1]), w.dtype).at[offset:offset + w.shape[0]].set(w)


def _pack_state(s):
    b = s.shape[0]
    s = s.reshape(b, 2, RWKV_PAIRS, 2, RWKV_HEAD, RWKV_HEAD)
    return s.transpose(0, 1, 2, 4, 3, 5).reshape(b, 2, RWKV_PAIRS, RWKV_HEAD, LANES)


def _unpack_state(s):
    b = s.shape[0]
    s = s.reshape(b, 2, RWKV_PAIRS, RWKV_HEAD, 2, RWKV_HEAD)
    return s.transpose(0, 1, 2, 4, 3, 5).reshape(b, 2, RWKV_HEADS, RWKV_HEAD, RWKV_HEAD)


def kernel(x_prompt, x_sample, cache_k, cache_v, state_rwkv, state_lru, c, c_ctx, w_ada, b_ada, norm1, norm2, w_in, w_out, q_norm, k_norm, rwkv_mu, rwkv_w0, rwkv_w_up, rwkv_a0, rwkv_a_up, rwkv_g_up, rwkv_k_k, rwkv_k_a, rwkv_r_k, rwkv_ln_w, rwkv_ln_b, lru_conv_w, lru_conv_b, lru_wa, lru_ba, lru_wx, lru_bx, lru_lambda, router, exp_w_gate, exp_w_up, exp_w_down):
    n_ctx, seq_ctx, _ = x_prompt.shape
    n_lat, seq_lat, _ = x_sample.shape
    past = cache_k.shape[2]
    assert n_lat + 1 <= COND_ROWS

    cond = jnp.zeros((COND_ROWS, D_MODEL), F32).at[0].set(c_ctx).at[1:1 + n_lat].set(c)
    mod = _ada(cond, w_ada, b_ada.reshape(DEPTH, 1, N_MOD * D_MODEL))
    mod = mod.reshape(DEPTH, COND_ROWS, N_MOD, D_MODEL)

    w_in_bf = w_in.astype(BF16)
    w_out_bf = w_out.astype(BF16)
    lane = jnp.arange(LANES)
    bd = (lane[:, None] // HEAD_DIM == lane[None, :] // HEAD_DIM).astype(BF16)
    pidx = jnp.arange(PREFIX_BLOCK)
    tri = (pidx[None, :] < pidx[:, None]).astype(BF16)
    rope_tabs = _rope_tables(seq_lat)

    paths = [
        dict(x=x_prompt.reshape(n_ctx * seq_ctx, D_MODEL), n_b=n_ctx, seq=seq_ctx, row0=0, rope=None),
        dict(x=x_sample.reshape(n_lat * seq_lat, D_MODEL), n_b=n_lat, seq=seq_lat, row0=1, rope=rope_tabs),
    ]
    new_k, new_v, new_sr, new_sl = [], [], [], []
    for l in range(DEPTH):
        mod_l = mod[l]
        qn = jnp.tile(q_norm[l], LANES // HEAD_DIM)[None, :]
        kn = jnp.tile(k_norm[l], LANES // HEAD_DIM)[None, :]
        prep_w = (
            rwkv_mu[l][None, :], rwkv_k_k[l][None, :], rwkv_k_a[l][None, :],
            rwkv_r_k[l].reshape(1, RWKV_WIDTH), rwkv_w0[l], rwkv_a0[l],
            jnp.stack([_pad_rows(rwkv_w_up[l, d], 0) for d in range(2)]),
            jnp.stack([_pad_rows(rwkv_a_up[l, d], RWKV_DECAY_LORA) for d in range(2)]),
            _pad_rows(rwkv_g_up[l], RWKV_DECAY_LORA + RWKV_AAA_LORA),
        )
        lru_w = (
            lru_conv_w[l], lru_conv_b[l][None, :],
            jnp.stack([_block_diag(lru_wa[l, d]) for d in range(2)]).astype(BF16), lru_ba[l],
            jnp.stack([_block_diag(lru_wx[l, d]) for d in range(2)]).astype(BF16), lru_bx[l],
            lru_lambda[l],
        )
        router_pad = jnp.zeros((D_MODEL, LANES), F32).at[:, :N_EXPERTS].set(router[l])
        mids = []
        for pi, pth in enumerate(paths):
            n_b, seq, row0 = pth["n_b"], pth["seq"], pth["row0"]
            latent = pth["rope"] is not None
            outs = _inproj(pth["x"], mod_l, norm1[l][None, :], w_in_bf, l, qn, kn, bd, pth["rope"], seq, row0)
            if latent:
                q, k_n, k_att, v, p_rwkv, lru_x, lru_g = outs
                cache = (cache_k[:, l].reshape(n_b, past, KV_WIDTH), cache_v[:, l].reshape(n_b, past, KV_WIDTH))
                s0 = _pack_state(state_rwkv[:, l])
                h0 = state_lru[:, l]
            else:
                q, k_n, v, p_rwkv, lru_x, lru_g = outs
                k_att, cache = k_n, None
                s0 = jnp.zeros((n_b, 2, RWKV_PAIRS, RWKV_HEAD, LANES), F32)
                h0 = jnp.zeros((n_b, 2, LRU_WIDTH), F32)
                new_k.append(k_n.reshape(n_b, seq, ATT_KV_HEADS, HEAD_DIM))
                new_v.append(v.reshape(n_b, seq, ATT_KV_HEADS, HEAD_DIM))
            att = _attention(q, k_att, v, n_b, seq, cache)
            r, vv, na, w, kd, b, g, bonus = _rwkv_prep(p_rwkv.reshape(n_b, seq, RWKV_COLS), n_b, seq, prep_w, bd)
            W = RWKV_WIDTH
            yf, yb, s_fin = _rwkv_scan(r.reshape(n_b, seq, W), vv.reshape(n_b, seq, W), na.reshape(n_b, seq, W),
                                       w.reshape(2, n_b, seq, W), kd.reshape(2, n_b, seq, W),
                                       b.reshape(2, n_b, seq, W), s0, n_b, seq)
            lru_out, h_fin = _lru(lru_x, lru_g, h0, lru_w, n_b, seq)
            if not latent:
                new_sr.append(_unpack_state(s_fin))
                new_sl.append(h_fin)
            x1, h2, logits = _outproj(pth["x"], att, yf.reshape(n_b * seq, W), yb.reshape(n_b * seq, W), bonus, g,
                                      lru_out, mod_l, norm2[l][None, :], rwkv_ln_w[l][None, :],
                                      rwkv_ln_b[l][None, :], w_out_bf, l, router_pad, bd, seq, row0)
            xs, slot, aff = _route(logits, h2, tri, n_b, seq)
            mids.append((xs, slot, aff, x1))
        y_a, y_b = _experts(mids[0][0], mids[1][0], exp_w_gate, exp_w_up, exp_w_down, l)
        for pth, (xs, slot, aff, x1), y in zip(paths, mids, (y_a, y_b)):
            pth["x"] = _combine(y, slot, aff, x1, mod_l, pth["n_b"], pth["seq"], pth["row0"])

    y_prompt = paths[0]["x"].reshape(n_ctx, seq_ctx, D_MODEL)
    y_sample = paths[1]["x"].reshape(n_lat, seq_lat, D_MODEL)
    return (y_prompt, y_sample, jnp.stack(new_k, axis=1), jnp.stack(new_v, axis=1),
            jnp.stack(new_sr, axis=1), jnp.stack(new_sl, axis=1))
```

```python
import functools

import jax
import jax.numpy as jnp
from jax import lax
from jax.experimental import pallas as pl
from jax.experimental.pallas import tpu as pltpu

F32 = jnp.float32
BF16 = jnp.bfloat16
HIGHEST = lax.Precision.HIGHEST

D_MODEL = 1024
DEPTH = 2
GRID_W = 64
ATT_HEADS = 8
ATT_KV_HEADS = 2
HEAD_DIM = 64
ATT_WIDTH = ATT_HEADS * HEAD_DIM
KV_WIDTH = ATT_KV_HEADS * HEAD_DIM
ROPE_THETA = 10000.0
RWKV_HEADS = 4
RWKV_HEAD = 64
RWKV_WIDTH = RWKV_HEADS * RWKV_HEAD
RWKV_DECAY_LORA = 32
RWKV_AAA_LORA = 32
RWKV_GATE_LORA = 64
RWKV_COLS = 3 * RWKV_WIDTH + RWKV_DECAY_LORA + RWKV_AAA_LORA + RWKV_GATE_LORA
GN_EPS = 64e-5
LRU_BLOCKS = 4
LRU_BLOCK = 64
LRU_WIDTH = LRU_BLOCKS * LRU_BLOCK
LRU_CONV_W = 4
LRU_C = 8.0
IN_COLS = ATT_WIDTH + 2 * KV_WIDTH + RWKV_COLS + 2 * LRU_WIDTH
N_EXPERTS = 16
EC_FACTOR = 2
EXPERT_FF = 1024
RMS_EPS = 1e-6

LANES = 128
SUBLANES = 8
VMEM_LIMIT = 56 * 1024 * 1024
N_MOD = 6
COND_ROWS = 8
ONE_BITS = 0x3F800000


def _params(*sem):
    return pltpu.CompilerParams(dimension_semantics=sem, vmem_limit_bytes=VMEM_LIMIT)


def _dot(a, b, precision=None):
    return jnp.dot(a, b, preferred_element_type=F32, precision=precision)


def _sigmoid(x):
    return 1.0 / (1.0 + jnp.exp(-x))


def _softplus(x):
    return jnp.maximum(x, 0.0) + jnp.log1p(jnp.exp(-jnp.abs(x)))


def _seg_sum(x, ones_bd):
    hi = x.astype(BF16)
    lo = (x - hi.astype(F32)).astype(BF16)
    return _dot(hi, ones_bd) + _dot(lo, ones_bd)


def _ada_kernel(c_ref, w_ref, b_ref, o_ref):
    c = c_ref[...]
    s = c * _sigmoid(c)
    o_ref[...] = _dot(s, w_ref[...], HIGHEST) + b_ref[...]


def _ada(cond, w_ada, b_ada):
    n_l = w_ada.shape[0]
    tn = 1536
    n_t = N_MOD * D_MODEL // tn
    return pl.pallas_call(
        _ada_kernel,
        grid=(n_l, n_t),
        in_specs=[
            pl.BlockSpec((COND_ROWS, D_MODEL), lambda l, j: (0, 0)),
            pl.BlockSpec((None, D_MODEL, tn), lambda l, j: (l, 0, j)),
            pl.BlockSpec((None, 1, tn), lambda l, j: (l, 0, j)),
        ],
        out_specs=pl.BlockSpec((None, COND_ROWS, tn), lambda l, j: (l, 0, j)),
        out_shape=jax.ShapeDtypeStruct((n_l, COND_ROWS, N_MOD * D_MODEL), F32),
        compiler_params=_params("arbitrary", "arbitrary"),
        name="ada",
    )(cond, w_ada, b_ada)


def _head_rms(x, gain, ones_bd):
    ms = _seg_sum(x * x, ones_bd) * (1.0 / HEAD_DIM)
    return x * lax.rsqrt(ms + RMS_EPS) * gain


def _rope(x, cos, sin_up, sin_dn):
    return x * cos + pltpu.roll(x, LANES - 16, 1) * sin_up + pltpu.roll(x, 16, 1) * sin_dn


def _inproj_kernel(*refs, rope):
    if rope:
        (x_ref, mod_ref, n1_ref, w_ref, qn_ref, kn_ref, bd_ref, cos_ref, su_ref, sd_ref,
         q_ref, ko_ref, ka_ref, v_ref, pr_ref, lx_ref, lg_ref) = refs
    else:
        (x_ref, mod_ref, n1_ref, w_ref, qn_ref, kn_ref, bd_ref,
         q_ref, ko_ref, v_ref, pr_ref, lx_ref, lg_ref) = refs
    x = x_ref[...]
    rs = lax.rsqrt(jnp.mean(x * x, axis=-1, keepdims=True) + RMS_EPS)
    h = (x * rs * n1_ref[...]) * (1.0 + mod_ref[1:2, :]) + mod_ref[0:1, :]
    p = _dot(h.astype(BF16), w_ref[...])
    bd = bd_ref[...]
    for c in range(ATT_WIDTH // LANES):
        qc = _head_rms(p[:, c * LANES:(c + 1) * LANES], qn_ref[...], bd)
        if rope:
            qc = _rope(qc, cos_ref[...], su_ref[...], sd_ref[...])
        q_ref[:, c * LANES:(c + 1) * LANES] = qc
    o = ATT_WIDTH
    kc = _head_rms(p[:, o:o + KV_WIDTH], kn_ref[...], bd)
    ko_ref[...] = kc
    if rope:
        ka_ref[...] = _rope(kc, cos_ref[...], su_ref[...], sd_ref[...])
    o += KV_WIDTH
    v_ref[...] = p[:, o:o + KV_WIDTH]
    o += KV_WIDTH
    pr_ref[...] = p[:, o:o + RWKV_COLS]
    o += RWKV_COLS
    lx_ref[...] = p[:, o:o + LRU_WIDTH]
    o += LRU_WIDTH
    lg_ref[...] = p[:, o:o + LRU_WIDTH]


def _inproj(x, mod_l, norm1, w_in_bf, layer, qn, kn, bd, rope_tabs, seq, cond_row0):
    n_tok = x.shape[0]
    tm = 512 if seq >= 512 else seq * (512 // seq)
    per_seq = max(seq // tm, 1)
    n_t = n_tok // tm
    rope = rope_tabs is not None
    if cond_row0 == 0:
        row = lambda i: 0
    else:
        row = lambda i: cond_row0 + i // per_seq
    tok = lambda w: pl.BlockSpec((tm, w), lambda i: (i, 0))
    const = lambda shape: pl.BlockSpec(shape, lambda i: tuple(0 for _ in shape))
    in_specs = [
        tok(D_MODEL),
        pl.BlockSpec((None, N_MOD, D_MODEL), lambda i: (row(i), 0, 0)),
        const((1, D_MODEL)),
        pl.BlockSpec((None, D_MODEL, IN_COLS), lambda i: (layer, 0, 0)),
        const((1, LANES)), const((1, LANES)), const((LANES, LANES)),
    ]
    args = [x, mod_l, norm1, w_in_bf, qn, kn, bd]
    outs = [ATT_WIDTH, KV_WIDTH]
    if rope:
        in_specs += [pl.BlockSpec((tm, LANES), lambda i: (i % per_seq, 0))] * 3
        args += list(rope_tabs)
        outs.append(KV_WIDTH)
    outs += [KV_WIDTH, RWKV_COLS, LRU_WIDTH, LRU_WIDTH]
    return pl.pallas_call(
        functools.partial(_inproj_kernel, rope=rope),
        grid=(n_t,),
        in_specs=in_specs,
        out_specs=[tok(w) for w in outs],
        out_shape=[jax.ShapeDtypeStruct((n_tok, w), F32) for w in outs],
        compiler_params=_params("arbitrary"),
        name="inproj_rope" if rope else "inproj",
    )(*args)


def _attn_kernel(*refs, has_cache):
    if has_cache:
        q_ref, k_ref, v_ref, ck_ref, cv_ref, o_ref, kn_s, ks_s, vn_s, vs_s = refs
    else:
        q_ref, k_ref, v_ref, o_ref, kn_s, ks_s, vn_s, vs_s = refs

    @pl.when(pl.program_id(1) == 0)
    def _():
        k = k_ref[...]
        v = v_ref[...]
        if has_cache:
            k = jnp.concatenate([ck_ref[...], k], axis=0)
            v = jnp.concatenate([cv_ref[...], v], axis=0)
        kn_s[...] = k.astype(BF16)
        ks_s[...] = pltpu.roll(k, HEAD_DIM, 1).astype(BF16)
        vn_s[...] = v.astype(BF16)
        vs_s[...] = pltpu.roll(v, HEAD_DIM, 1).astype(BF16)

    tq = q_ref.shape[0]
    lo = lax.broadcasted_iota(jnp.int32, (tq, LANES), 1) < HEAD_DIM
    rep = ATT_HEADS // ATT_KV_HEADS
    for c in range(ATT_WIDTH // LANES):
        qc = q_ref[:, c * LANES:(c + 1) * LANES] * (HEAD_DIM ** -0.5)
        halves = []
        for half in range(2):
            g = (2 * c + half) // rep
            qm = jnp.where(lo if half == 0 else jnp.logical_not(lo), qc, 0.0).astype(BF16)
            k_s, v_s = (kn_s, vn_s) if half == g else (ks_s, vs_s)
            s = lax.dot_general(qm, k_s[...], (((1,), (1,)), ((), ())), preferred_element_type=F32)
            m = jnp.max(s, axis=-1, keepdims=True)
            e = jnp.exp(s - m)
            l = jnp.sum(e, axis=-1, keepdims=True)
            halves.append(_dot(e.astype(BF16), v_s[...]) / l)
        o_ref[:, c * LANES:(c + 1) * LANES] = jnp.where(lo, halves[0], halves[1])


def _attention(q, k, v, n_b, seq, cache=None):
    tq = 256
    n_q = seq // tq
    has_cache = cache is not None
    past = cache[0].shape[1] if has_cache else 0
    in_specs = [
        pl.BlockSpec((tq, ATT_WIDTH), lambda b, i: (b * n_q + i, 0)),
        pl.BlockSpec((seq, KV_WIDTH), lambda b, i: (b, 0)),
        pl.BlockSpec((seq, KV_WIDTH), lambda b, i: (b, 0)),
    ]
    args = [q, k, v]
    if has_cache:
        in_specs += [pl.BlockSpec((None, past, KV_WIDTH), lambda b, i: (b, 0, 0))] * 2
        args += list(cache)
    return pl.pallas_call(
        functools.partial(_attn_kernel, has_cache=has_cache),
        grid=(n_b, n_q),
        in_specs=in_specs,
        out_specs=pl.BlockSpec((tq, ATT_WIDTH), lambda b, i: (b * n_q + i, 0)),
        out_shape=jax.ShapeDtypeStruct((n_b * seq, ATT_WIDTH), F32),
        scratch_shapes=[pltpu.VMEM((past + seq, KV_WIDTH), BF16)] * 4,
        compiler_params=_params("arbitrary", "arbitrary"),
        name="attn_cache" if has_cache else "attn",
    )(*args)


def _rwkv_prep_kernel(p_ref, mu_ref, kk_ref, ka_ref, rk_ref, w0_ref, a0_ref, wup_ref, aup_ref,
                      gup_ref, bd_ref,
                      r_ref, v_ref, na_ref, w_ref, k_ref, b_ref, g_ref, bonus_ref):
    tm = r_ref.shape[0]
    seq = p_ref.shape[0]
    i = pl.program_id(1)
    n_t = pl.num_programs(1)
    start = pl.multiple_of(i * tm, tm)
    cur = p_ref[pl.ds(start, tm), :]
    prev_base = pl.multiple_of(jnp.maximum(start - SUBLANES, 0), SUBLANES)
    next_base = pl.multiple_of(jnp.minimum(start + tm, seq - SUBLANES), SUBLANES)
    prev_row = p_ref[pl.ds(prev_base, SUBLANES), :][SUBLANES - 1:SUBLANES]
    next_row = p_ref[pl.ds(next_base, SUBLANES), :][0:1]
    prev_row = jnp.where(i > 0, prev_row, 0.0)
    next_row = jnp.where(i < n_t - 1, next_row, 0.0)
    row = lax.broadcasted_iota(jnp.int32, cur.shape, 0)
    prev = jnp.where(row == 0, prev_row, pltpu.roll(cur, 1, 0))
    nxt = jnp.where(row == tm - 1, next_row, pltpu.roll(cur, tm - 1, 0))
    ps = cur + mu_ref[...] * (0.5 * (prev + nxt) - cur)

    W = RWKV_WIDTH
    r = ps[:, :W]
    k = ps[:, W:2 * W]
    v = ps[:, 2 * W:3 * W]
    lora = ps[:, 3 * W:]
    r_ref[...] = r
    v_ref[...] = v
    g_ref[...] = _dot(_sigmoid(lora), gup_ref[...], HIGHEST)
    lora_t = jnp.tanh(lora)
    kk = k * kk_ref[...]
    kk_parts = []
    for c in range(W // LANES):
        kc = kk[:, c * LANES:(c + 1) * LANES]
        nrm = jnp.sqrt(_seg_sum(kc * kc, bd_ref[...]))
        kk_parts.append(kc / jnp.maximum(nrm, 1e-12))
    kk = jnp.concatenate(kk_parts, axis=-1)
    na_ref[...] = -kk
    bonus = None
    for d in range(2):
        w_log = -_softplus(-(w0_ref[d:d + 1, :] + _dot(lora_t, wup_ref[d], HIGHEST))) - 0.5
        w_ref[d] = -jnp.exp(w_log)
        a_rate = _sigmoid(a0_ref[d:d + 1, :] + _dot(lora, aup_ref[d], HIGHEST))
        kd = k * (1.0 + (a_rate - 1.0) * ka_ref[...])
        k_ref[d] = kd
        b_ref[d] = kk * a_rate
        rkr = r * kd * rk_ref[...]
        parts = [_seg_sum(rkr[:, c * LANES:(c + 1) * LANES], bd_ref[...]) for c in range(W // LANES)]
        bd_term = jnp.concatenate(parts, axis=-1) * v
        bonus = bd_term if bonus is None else bonus + bd_term
    bonus_ref[...] = bonus


def _rwkv_prep(p, n_b, seq, wts, bd):
    mu, k_k, k_a, r_k, w0, a0, wup, aup, gup = wts
    tm = min(seq, 512)
    n_t = seq // tm
    W = RWKV_WIDTH
    const = lambda shape: pl.BlockSpec(shape, lambda b, i: tuple(0 for _ in shape))
    tok = pl.BlockSpec((tm, W), lambda b, i: (b * n_t + i, 0))
    tok2 = pl.BlockSpec((2, tm, W), lambda b, i: (0, b * n_t + i, 0))
    one = jax.ShapeDtypeStruct((n_b * seq, W), F32)
    two = jax.ShapeDtypeStruct((2, n_b * seq, W), F32)
    return pl.pallas_call(
        _rwkv_prep_kernel,
        grid=(n_b, n_t),
        in_specs=[
            pl.BlockSpec((None, seq, RWKV_COLS), lambda b, i: (b, 0, 0)),
            const((1, RWKV_COLS)), const((1, W)), const((1, W)), const((1, W)),
            const((2, W)), const((2, W)),
            const((2, LANES, W)), const((2, LANES, W)), const((LANES, W)),
            const((LANES, LANES)),
        ],
        out_specs=[tok, tok, tok, tok2, tok2, tok2, tok, tok],
        out_shape=[one, one, one, two, two, two, one, one],
        compiler_params=_params("arbitrary", "arbitrary"),
        name="rwkv_prep",
    )(p, mu, k_k, k_a, r_k, w0, a0, wup, aup, gup, bd)


RWKV_BB = 2
RWKV_PAIRS = RWKV_HEADS * RWKV_HEAD // LANES
RWKV_CHUNK = 256
RWKV_SUB = 64
NEUMANN_STEPS = 5


def _rwkv_scan_kernel(rf_ref, vf_ref, af_ref, wf_ref, kf_ref, bf_ref,
                      rb_ref, vb_ref, ab_ref, wb_ref, kb_ref, bb_ref, s0_ref,
                      yf_ref, yb_ref, sfin_ref, st_s):
    j = pl.program_id(1)
    tc = rf_ref.shape[1]
    C = RWKV_SUB

    @pl.when(j == 0)
    def _():
        st_s[...] = s0_ref[...]

    lane_c = lax.broadcasted_iota(jnp.int32, (C, LANES), 1)
    row_c = lax.broadcasted_iota(jnp.int32, (C, LANES), 0)
    lo = lane_c < C
    s_idx = lane_c % C
    eye2 = jnp.where(s_idx == row_c, 1.0, 0.0)
    lane_f = lax.broadcasted_iota(jnp.int32, (LANES, LANES), 1)
    row_f = lax.broadcasted_iota(jnp.int32, (LANES, LANES), 0)
    eye_f = lane_f == row_f
    lo_f = lane_f < C
    bd_mask = (lane_f < C) == (row_f < C)
    tt = lax.broadcasted_iota(jnp.int32, (C, C), 0)
    ss = lax.broadcasted_iota(jnp.int32, (C, C), 1)

    def bd(m):
        return jnp.concatenate([jnp.where(lo, m, 0.0), jnp.where(lo, 0.0, m)], axis=0)

    def bd_swap(m):
        return jnp.concatenate([jnp.where(lo, 0.0, m), jnp.where(lo, m, 0.0)], axis=0)

    def hdot(x, y):
        return _dot(x, y, HIGHEST)

    dirs = ((rf_ref, vf_ref, af_ref, wf_ref, kf_ref, bf_ref, yf_ref),
            (rb_ref, vb_ref, ab_ref, wb_ref, kb_ref, bb_ref, yb_ref))

    def sub_chunk(i, carry):
        for d in range(2):
            r_r, v_r, a_r, w_r, k_r, b_r, y_r = dirs[d]
            base = pl.multiple_of(i * C if d == 0 else tc - C - i * C, C)
            before = (ss < tt) if d == 0 else (ss > tt)
            cum = jnp.where(jnp.logical_or(before, ss == tt), 1.0, 0.0)
            strict = jnp.where((s_idx < row_c) if d == 0 else (s_idx > row_c), 1.0, 0.0)
            incl = jnp.where((s_idx <= row_c) if d == 0 else (s_idx >= row_c), 1.0, 0.0)
            tri_mask = jnp.concatenate([strict, incl], axis=0)
            last = C - 1 if d == 0 else 0
            for bi in range(RWKV_BB):
                for pr in range(RWKV_PAIRS):
                    ls = slice(pr * LANES, (pr + 1) * LANES)
                    r, v, a, lw, k, b = (ref[bi, pl.ds(base, C), ls] for ref in (r_r, v_r, a_r, w_r, k_r, b_r))
                    G = hdot(cum, lw)
                    g_inv = jnp.exp(-G)
                    at = a * jnp.exp(G - lw)
                    rt = r * jnp.exp(G)
                    bt = b * g_inv
                    kt = k * g_inv
                    g_last = jnp.exp(G[last:last + 1, :])
                    X = jnp.concatenate([at, rt], axis=0)
                    nt = (((1,), (1,)), ((), ()))
                    P0 = lax.dot_general(jnp.where(lo_f, X, 0.0),
                                         jnp.concatenate([bt, kt], axis=0), nt,
                                         precision=HIGHEST, preferred_element_type=F32) * tri_mask
                    P1 = lax.dot_general(jnp.where(lo_f, 0.0, X),
                                         jnp.concatenate([kt, bt], axis=0), nt,
                                         precision=HIGHEST, preferred_element_type=F32) * tri_mask
                    AB = jnp.where(lo, P0[:C], P1[:C])
                    AK = jnp.where(lo, P1[:C], P0[:C])
                    RB = jnp.where(lo, P0[C:], P1[C:])
                    RK = jnp.where(lo, P1[C:], P0[C:])
                    S0 = st_s[bi, d, pr]
                    XS = hdot(X, S0)
                    rhs = XS[:C] + hdot(AK, bd_swap(v))
                    Lp = AB
                    T = eye2 + AB
                    for _ in range(NEUMANN_STEPS):
                        Lp = hdot(Lp, bd(Lp))
                        T = T + hdot(T, bd(Lp))
                    U = hdot(T, bd(rhs))
                    y_r[bi, pl.ds(base, C), ls] = XS[C:] + hdot(RB, bd(U)) + hdot(RK, bd_swap(v))
                    left = jnp.concatenate([bt * g_last, kt * g_last, jnp.where(eye_f, g_last, 0.0)], axis=0)
                    right = jnp.concatenate([U, v, S0], axis=0)
                    S1 = lax.dot_general(left, right, (((0,), (0,)), ((), ())),
                                         precision=HIGHEST, preferred_element_type=F32)
                    st_s[bi, d, pr] = jnp.where(bd_mask, S1, 0.0)
        return carry

    lax.fori_loop(0, tc // C, sub_chunk, 0)

    @pl.when(j == pl.num_programs(1) - 1)
    def _():
        sfin_ref[...] = st_s[...]


def _rwkv_scan(r, v, na, lw, k, b, s0, n_b, seq):
    tc = min(seq, RWKV_CHUNK)
    n_c = seq // tc
    W = RWKV_WIDTH
    blk = (RWKV_BB, tc, W)
    fwd = pl.BlockSpec(blk, lambda g, j: (g, j, 0))
    bwd = pl.BlockSpec(blk, lambda g, j: (g, n_c - 1 - j, 0))
    fwd_d = pl.BlockSpec((None,) + blk, lambda g, j: (0, g, j, 0))
    bwd_d = pl.BlockSpec((None,) + blk, lambda g, j: (1, g, n_c - 1 - j, 0))
    st_blk = (RWKV_BB, 2, RWKV_PAIRS, LANES, LANES)
    st_spec = pl.BlockSpec(st_blk, lambda g, j: (g, 0, 0, 0, 0))
    y_shape = jax.ShapeDtypeStruct((n_b, seq, W), F32)
    return pl.pallas_call(
        _rwkv_scan_kernel,
        grid=(n_b // RWKV_BB, n_c),
        in_specs=[fwd, fwd, fwd, fwd_d, fwd_d, fwd_d, bwd, bwd, bwd, bwd_d, bwd_d, bwd_d, st_spec],
        out_specs=[fwd, bwd, st_spec],
        out_shape=[y_shape, y_shape, jax.ShapeDtypeStruct(s0.shape, F32)],
        scratch_shapes=[pltpu.VMEM(st_blk, F32)],
        compiler_params=_params("arbitrary", "arbitrary"),
        name="rwkv_scan",
    )(r, v, na, lw, k, b, r, v, na, lw, k, b, s0)


def _lru_kernel(x_ref, g_ref, h0_ref, cw_ref, cb_ref, wa_ref, ba_ref, wx_ref, bx_ref, lam_ref,
                o_ref, hfin_ref, a_s, u_s, h_s):
    seq = x_ref.shape[0]
    x = x_ref[...]
    row = lax.broadcasted_iota(jnp.int32, x.shape, 0)
    xm2 = jnp.where(row >= 2, pltpu.roll(x, 2, 0), 0.0)
    xm1 = jnp.where(row >= 1, pltpu.roll(x, 1, 0), 0.0)
    xp1 = jnp.where(row < seq - 1, pltpu.roll(x, seq - 1, 0), 0.0)
    xc = (xm2 * cw_ref[0:1, :] + xm1 * cw_ref[1:2, :] + x * cw_ref[2:3, :] + xp1 * cw_ref[3:4, :]
          + cb_ref[...])
    xb = xc.astype(BF16)
    for d in range(2):
        r_gate = _sigmoid(_dot(xb, wa_ref[d]) + ba_ref[d:d + 1, :])
        i_gate = _sigmoid(_dot(xb, wx_ref[d]) + bx_ref[d:d + 1, :])
        log_a = -LRU_C * r_gate * _softplus(-lam_ref[d:d + 1, :])
        a_s[d] = jnp.exp(log_a)
        u_s[d] = jnp.sqrt(1.0 - jnp.exp(2.0 * log_a)) * (i_gate * xc)

    n_t = seq // SUBLANES

    def body(i, carry):
        hf, hb = carry
        base = pl.multiple_of(i * SUBLANES, SUBLANES)
        a = a_s[0, pl.ds(base, SUBLANES), :]
        u = u_s[0, pl.ds(base, SUBLANES), :]
        rows = []
        for s in range(SUBLANES):
            hf = a[s:s + 1] * hf + u[s:s + 1]
            rows.append(hf)
        h_s[0, pl.ds(base, SUBLANES), :] = jnp.concatenate(rows, axis=0)
        base = pl.multiple_of(seq - SUBLANES - i * SUBLANES, SUBLANES)
        a = a_s[1, pl.ds(base, SUBLANES), :]
        u = u_s[1, pl.ds(base, SUBLANES), :]
        rows = [None] * SUBLANES
        for s in range(SUBLANES - 1, -1, -1):
            hb = a[s:s + 1] * hb + u[s:s + 1]
            rows[s] = hb
        h_s[1, pl.ds(base, SUBLANES), :] = jnp.concatenate(rows, axis=0)
        return hf, hb

    hf, hb = lax.fori_loop(0, n_t, body, (h0_ref[0:1, :], h0_ref[1:2, :]))
    hfin_ref[...] = jnp.concatenate([hf, hb], axis=0)
    g = g_ref[...]
    gelu = 0.5 * g * (1.0 + jnp.tanh(0.7978845608028654 * (g + 0.044715 * (g * g * g))))
    o_ref[...] = (h_s[0] + h_s[1]) * gelu


def _lru(xb, gb, h0, wts, n_b, seq):
    cw, cb, wa, ba, wx, bx, lam = wts
    C = LRU_WIDTH
    const = lambda shape: pl.BlockSpec(shape, lambda b: tuple(0 for _ in shape))
    tok = pl.BlockSpec((seq, C), lambda b: (b, 0))
    st = pl.BlockSpec((None, 2, C), lambda b: (b, 0, 0))
    return pl.pallas_call(
        _lru_kernel,
        grid=(n_b,),
        in_specs=[tok, tok, st, const((LRU_CONV_W, C)), const((1, C)), const((2, C, C)), const((2, C)),
                  const((2, C, C)), const((2, C)), const((2, C))],
        out_specs=[tok, st],
        out_shape=[jax.ShapeDtypeStruct((n_b * seq, C), F32), jax.ShapeDtypeStruct((n_b, 2, C), F32)],
        scratch_shapes=[pltpu.VMEM((2, seq, C), F32)] * 3,
        compiler_params=_params("arbitrary"),
        name="lru",
    )(xb, gb, h0, cw, cb, wa, ba, wx, bx, lam)


def _outproj_kernel(x_ref, att_ref, yf_ref, yb_ref, bonus_ref, g_ref, lru_ref, mod_ref, n2_ref,
                    lnw_ref, lnb_ref, w_ref, rt_ref, bd_ref,
                    x1_ref, h2_ref, lg_ref):
    y = yf_ref[...] + yb_ref[...]
    parts = []
    for c in range(RWKV_WIDTH // LANES):
        yc = y[:, c * LANES:(c + 1) * LANES]
        mean = _seg_sum(yc, bd_ref[...]) * (1.0 / RWKV_HEAD)
        dev = yc - mean
        var = _seg_sum(dev * dev, bd_ref[...]) * (1.0 / RWKV_HEAD)
        parts.append(dev * lax.rsqrt(var + GN_EPS))
    yn = jnp.concatenate(parts, axis=-1) * lnw_ref[...] + lnb_ref[...]
    rwkv = (yn + bonus_ref[...]) * g_ref[...]
    o1 = ATT_WIDTH
    o2 = o1 + RWKV_WIDTH
    mixed = (_dot(att_ref[...].astype(BF16), w_ref[:o1, :])
             + _dot(rwkv.astype(BF16), w_ref[o1:o2, :])
             + _dot(lru_ref[...].astype(BF16), w_ref[o2:, :]))
    x1 = x_ref[...] + mod_ref[2:3, :] * mixed
    x1_ref[...] = x1
    rs = lax.rsqrt(jnp.mean(x1 * x1, axis=-1, keepdims=True) + RMS_EPS)
    h2 = (x1 * rs * n2_ref[...]) * (1.0 + mod_ref[4:5, :]) + mod_ref[3:4, :]
    h2_ref[...] = h2.astype(BF16)
    lg_ref[...] = _dot(h2, rt_ref[...], HIGHEST)


def _outproj(x, att, yf, yb, bonus, g, lru, mod_l, norm2, lnw, lnb, w_out_bf, layer, router_pad, bd,
             seq, cond_row0):
    n_tok = x.shape[0]
    tm = 512 if seq >= 512 else seq * (512 // seq)
    per_seq = max(seq // tm, 1)
    n_t = n_tok // tm
    if cond_row0 == 0:
        row = lambda i: 0
    else:
        row = lambda i: cond_row0 + i // per_seq
    tok = lambda w: pl.BlockSpec((tm, w), lambda i: (i, 0))
    const = lambda shape: pl.BlockSpec(shape, lambda i: tuple(0 for _ in shape))
    W = RWKV_WIDTH
    return pl.pallas_call(
        _outproj_kernel,
        grid=(n_t,),
        in_specs=[
            tok(D_MODEL), tok(ATT_WIDTH), tok(W), tok(W), tok(W), tok(W), tok(LRU_WIDTH),
            pl.BlockSpec((None, N_MOD, D_MODEL), lambda i: (row(i), 0, 0)),
            const((1, D_MODEL)), const((1, W)), const((1, W)),
            pl.BlockSpec((None, D_MODEL, D_MODEL), lambda i: (layer, 0, 0)),
            const((D_MODEL, LANES)), const((LANES, LANES)),
        ],
        out_specs=[tok(D_MODEL), tok(D_MODEL), tok(LANES)],
        out_shape=[jax.ShapeDtypeStruct((n_tok, D_MODEL), F32),
                   jax.ShapeDtypeStruct((n_tok, D_MODEL), BF16),
                   jax.ShapeDtypeStruct((n_tok, LANES), F32)],
        compiler_params=_params("arbitrary"),
        name="outproj",
    )(x, att, yf, yb, bonus, g, lru, mod_l, norm2, lnw, lnb, w_out_bf, router_pad, bd)


PREFIX_BLOCK = 256


def _prefix_count(mask_f, tri):
    seq = mask_f.shape[0]
    outs = []
    carry = jnp.zeros((1, LANES), F32)
    for blk in range(seq // PREFIX_BLOCK):
        m = mask_f[blk * PREFIX_BLOCK:(blk + 1) * PREFIX_BLOCK]
        outs.append(_dot(tri, m.astype(BF16)) + carry)
        carry = carry + jnp.sum(m, axis=0, keepdims=True)
    return jnp.concatenate(outs, axis=0) if len(outs) > 1 else outs[0]


def _route_kernel(lg_ref, h2_ref, tri_ref, xs_ref, slot_ref, aff_ref, *, cap):
    seq = lg_ref.shape[0]
    lane = lax.broadcasted_iota(jnp.int32, (seq, LANES), 1)
    real = lane < N_EXPERTS
    lg = jnp.where(real, lg_ref[...], -jnp.inf)
    m = jnp.max(lg, axis=-1, keepdims=True)
    e = jnp.exp(lg - m)
    aff = e / jnp.sum(e, axis=-1, keepdims=True)
    aff_ref[...] = aff
    bits = lax.bitcast_convert_type(aff, jnp.int32)

    def bisect(_, carry):
        lo_b, hi_b = carry
        mid = lo_b + ((hi_b - lo_b) >> 1)
        cnt = jnp.sum(jnp.where(bits >= mid, 1.0, 0.0), axis=0, keepdims=True)
        ok = cnt >= cap
        return jnp.where(ok, mid, lo_b), jnp.where(ok, hi_b, mid)

    lo_b = jnp.zeros((1, LANES), jnp.int32)
    hi_b = jnp.full((1, LANES), ONE_BITS + 1, jnp.int32)
    thr, _ = lax.fori_loop(0, 32, bisect, (lo_b, hi_b))
    gt = jnp.where(bits > thr, 1.0, 0.0)
    eq = jnp.where(bits == thr, 1.0, 0.0)
    need = cap - jnp.sum(gt, axis=0, keepdims=True)
    tri = tri_ref[...]
    sel = jnp.where(real, gt + eq * jnp.where(_prefix_count(eq, tri) < need, 1.0, 0.0), 0.0)
    slot = jnp.where(sel > 0.0, _prefix_count(sel, tri), -1.0)
    slot_ref[...] = slot

    slot_t = slot.T
    h2 = h2_ref[...]
    c_iota = lax.broadcasted_iota(jnp.int32, (cap, seq), 0).astype(F32)
    for ex in range(N_EXPERTS):
        onehot = jnp.where(c_iota == slot_t[ex:ex + 1, :], 1.0, 0.0).astype(BF16)
        xs_ref[ex] = _dot(onehot, h2).astype(BF16)


def _route(logits, h2, tri, n_b, seq):
    cap = EC_FACTOR * seq // N_EXPERTS
    tok = lambda w: pl.BlockSpec((seq, w), lambda b: (b, 0))
    return pl.pallas_call(
        functools.partial(_route_kernel, cap=cap),
        grid=(n_b,),
        in_specs=[tok(LANES), tok(D_MODEL), pl.BlockSpec((PREFIX_BLOCK, PREFIX_BLOCK), lambda b: (0, 0))],
        out_specs=[pl.BlockSpec((N_EXPERTS, cap, D_MODEL), lambda b: (0, b, 0)), tok(LANES), tok(LANES)],
        out_shape=[jax.ShapeDtypeStruct((N_EXPERTS, n_b * cap, D_MODEL), BF16),
                   jax.ShapeDtypeStruct((n_b * seq, LANES), F32),
                   jax.ShapeDtypeStruct((n_b * seq, LANES), F32)],
        compiler_params=_params("arbitrary"),
        name="route",
    )(logits, h2, tri)


EXPERT_FT = 512
EXPERT_RB = 512


def _expert_kernel(xa_ref, xb_ref, wg_ref, wu_ref, wd_ref, ya_ref, yb_ref, acc_a, acc_b):
    f = pl.program_id(1)
    last = pl.num_programs(1) - 1
    wg = wg_ref[...].astype(BF16)
    wu = wu_ref[...].astype(BF16)
    wd = wd_ref[...].astype(BF16)
    for x_ref, y_ref, acc in ((xa_ref, ya_ref, acc_a), (xb_ref, yb_ref, acc_b)):
        for rb in range(x_ref.shape[0] // EXPERT_RB):
            rows = slice(rb * EXPERT_RB, (rb + 1) * EXPERT_RB)
            x = x_ref[rows, :]
            a = _dot(x, wg)
            u = _dot(x, wu)
            hid = (a * _sigmoid(a) * u).astype(BF16)
            y = _dot(hid, wd)

            @pl.when(f == 0)
            def _():
                acc[rows, :] = y

            @pl.when(jnp.logical_and(f > 0, f < last))
            def _():
                acc[rows, :] += y

            @pl.when(f == last)
            def _():
                y_ref[rows, :] = (acc[rows, :] + y).astype(BF16)


def _experts(xs_a, xs_b, w_gate, w_up, w_down, layer):
    n_f = EXPERT_FF // EXPERT_FT
    ma, mb = xs_a.shape[1], xs_b.shape[1]
    xspec = lambda m: pl.BlockSpec((None, m, D_MODEL), lambda e, f: (e, 0, 0))
    return pl.pallas_call(
        _expert_kernel,
        grid=(N_EXPERTS, n_f),
        in_specs=[
            xspec(ma), xspec(mb),
            pl.BlockSpec((None, None, D_MODEL, EXPERT_FT), lambda e, f: (layer, e, 0, f)),
            pl.BlockSpec((None, None, D_MODEL, EXPERT_FT), lambda e, f: (layer, e, 0, f)),
            pl.BlockSpec((None, None, EXPERT_FT, D_MODEL), lambda e, f: (layer, e, f, 0)),
        ],
        out_specs=[xspec(ma), xspec(mb)],
        out_shape=[jax.ShapeDtypeStruct(xs_a.shape, BF16), jax.ShapeDtypeStruct(xs_b.shape, BF16)],
        scratch_shapes=[pltpu.VMEM((ma, D_MODEL), F32), pltpu.VMEM((mb, D_MODEL), F32)],
        compiler_params=_params("arbitrary", "arbitrary"),
        name="experts",
    )(xs_a, xs_b, w_gate, w_up, w_down)


def _combine_kernel(y_ref, slot_ref, aff_ref, x1_ref, mod_ref, o_ref, *, cap):
    seq = x1_ref.shape[0]
    c_iota = lax.broadcasted_iota(jnp.int32, (seq, cap), 1).astype(F32)
    slot = slot_ref[...]
    aff = aff_ref[...]
    acc = jnp.zeros((seq, D_MODEL), F32)
    for ex in range(N_EXPERTS):
        onehot = jnp.where(slot[:, ex:ex + 1] == c_iota, 1.0, 0.0).astype(BF16)
        acc = acc + aff[:, ex:ex + 1] * _dot(onehot, y_ref[ex])
    o_ref[...] = x1_ref[...] + mod_ref[5:6, :] * acc


def _combine(y, slot, aff, x1, mod_l, n_b, seq, cond_row0):
    cap = EC_FACTOR * seq // N_EXPERTS
    if cond_row0 == 0:
        row = lambda b: 0
    else:
        row = lambda b: cond_row0 + b
    tm = min(seq, 512)
    n_t = seq // tm
    tok = lambda w: pl.BlockSpec((tm, w), lambda b, i: (b * n_t + i, 0))
    return pl.pallas_call(
        functools.partial(_combine_kernel, cap=cap),
        grid=(n_b, n_t),
        in_specs=[
            pl.BlockSpec((N_EXPERTS, cap, D_MODEL), lambda b, i: (0, b, 0)),
            tok(LANES), tok(LANES), tok(D_MODEL),
            pl.BlockSpec((None, N_MOD, D_MODEL), lambda b, i: (row(b), 0, 0)),
        ],
        out_specs=tok(D_MODEL),
        out_shape=jax.ShapeDtypeStruct((n_b * seq, D_MODEL), F32),
        compiler_params=_params("arbitrary", "arbitrary"),
        name="combine",
    )(y, slot, aff, x1, mod_l)


def _rope_tables(seq):
    t = jnp.arange(seq, dtype=jnp.int32)
    row = (t // GRID_W).astype(F32)
    col = (t % GRID_W).astype(F32)
    half = HEAD_DIM // 2
    inv = ROPE_THETA ** (-jnp.arange(0, half, 2, dtype=F32) / half)
    lane = jnp.arange(LANES)
    u = lane % HEAD_DIM
    pos = jnp.where((u // half)[None, :] == 0, row[:, None], col[:, None])
    ang = pos * inv[(u % half) % (half // 2)][None, :]
    first = ((u % half) < half // 2)[None, :]
    sin = jnp.sin(ang)
    return jnp.cos(ang), jnp.where(first, -sin, 0.0), jnp.where(first, 0.0, sin)


def _block_diag(w):
    n, k, _ = w.shape
    eye = jnp.eye(n, dtype=w.dtype)
    return (eye[:, None, :, None] * w[:, :, None, :]).reshape(n * k, n * k)


def _pad_rows(w, offset, total=LANES):
    return jnp.zeros((total, w.shape[1]), w.dtype).at[offset:offset + w.shape[0]].set(w)


def _pack_state(s):
    b = s.shape[0]
    st = s.reshape(b, 2, RWKV_PAIRS, 2, RWKV_HEAD, RWKV_HEAD).transpose(0, 1, 2, 3, 5, 4)
    eye = jnp.eye(2, dtype=s.dtype)
    out = st[:, :, :, :, :, None, :] * eye[None, None, None, :, None, :, None]
    return out.reshape(b, 2, RWKV_PAIRS, LANES, LANES)


def _unpack_state(s):
    b = s.shape[0]
    s = s.reshape(b, 2, RWKV_PAIRS, 2, RWKV_HEAD, 2, RWKV_HEAD)
    diag = jnp.stack([s[:, :, :, h, :, h, :] for h in range(2)], axis=3)
    return diag.transpose(0, 1, 2, 3, 5, 4).reshape(b, 2, RWKV_HEADS, RWKV_HEAD, RWKV_HEAD)


def kernel(x_prompt, x_sample, cache_k, cache_v, state_rwkv, state_lru, c, c_ctx, w_ada, b_ada, norm1, norm2, w_in, w_out, q_norm, k_norm, rwkv_mu, rwkv_w0, rwkv_w_up, rwkv_a0, rwkv_a_up, rwkv_g_up, rwkv_k_k, rwkv_k_a, rwkv_r_k, rwkv_ln_w, rwkv_ln_b, lru_conv_w, lru_conv_b, lru_wa, lru_ba, lru_wx, lru_bx, lru_lambda, router, exp_w_gate, exp_w_up, exp_w_down):
    n_ctx, seq_ctx, _ = x_prompt.shape
    n_lat, seq_lat, _ = x_sample.shape
    past = cache_k.shape[2]
    assert n_lat + 1 <= COND_ROWS

    cond = jnp.zeros((COND_ROWS, D_MODEL), F32).at[0].set(c_ctx).at[1:1 + n_lat].set(c)
    mod = _ada(cond, w_ada, b_ada.reshape(DEPTH, 1, N_MOD * D_MODEL))
    mod = mod.reshape(DEPTH, COND_ROWS, N_MOD, D_MODEL)

    w_in_bf = w_in.astype(BF16)
    w_out_bf = w_out.astype(BF16)
    lane = jnp.arange(LANES)
    bd = (lane[:, None] // HEAD_DIM == lane[None, :] // HEAD_DIM).astype(BF16)
    pidx = jnp.arange(PREFIX_BLOCK)
    tri = (pidx[None, :] < pidx[:, None]).astype(BF16)
    rope_tabs = _rope_tables(seq_lat)

    paths = [
        dict(x=x_prompt.reshape(n_ctx * seq_ctx, D_MODEL), n_b=n_ctx, seq=seq_ctx, row0=0, rope=None),
        dict(x=x_sample.reshape(n_lat * seq_lat, D_MODEL), n_b=n_lat, seq=seq_lat, row0=1, rope=rope_tabs),
    ]
    new_k, new_v, new_sr, new_sl = [], [], [], []
    for l in range(DEPTH):
        mod_l = mod[l]
        qn = jnp.tile(q_norm[l], LANES // HEAD_DIM)[None, :]
        kn = jnp.tile(k_norm[l], LANES // HEAD_DIM)[None, :]
        prep_w = (
            rwkv_mu[l][None, :], rwkv_k_k[l][None, :], rwkv_k_a[l][None, :],
            rwkv_r_k[l].reshape(1, RWKV_WIDTH), rwkv_w0[l], rwkv_a0[l],
            jnp.stack([_pad_rows(rwkv_w_up[l, d], 0) for d in range(2)]),
            jnp.stack([_pad_rows(rwkv_a_up[l, d], RWKV_DECAY_LORA) for d in range(2)]),
            _pad_rows(rwkv_g_up[l], RWKV_DECAY_LORA + RWKV_AAA_LORA),
        )
        lru_w = (
            lru_conv_w[l], lru_conv_b[l][None, :],
            jnp.stack([_block_diag(lru_wa[l, d]) for d in range(2)]).astype(BF16), lru_ba[l],
            jnp.stack([_block_diag(lru_wx[l, d]) for d in range(2)]).astype(BF16), lru_bx[l],
            lru_lambda[l],
        )
        router_pad = jnp.zeros((D_MODEL, LANES), F32).at[:, :N_EXPERTS].set(router[l])
        mids = []
        for pi, pth in enumerate(paths):
            n_b, seq, row0 = pth["n_b"], pth["seq"], pth["row0"]
            latent = pth["rope"] is not None
            outs = _inproj(pth["x"], mod_l, norm1[l][None, :], w_in_bf, l, qn, kn, bd, pth["rope"], seq, row0)
            if latent:
                q, k_n, k_att, v, p_rwkv, lru_x, lru_g = outs
                cache = (cache_k[:, l].reshape(n_b, past, KV_WIDTH), cache_v[:, l].reshape(n_b, past, KV_WIDTH))
                s0 = _pack_state(state_rwkv[:, l])
                h0 = state_lru[:, l]
            else:
                q, k_n, v, p_rwkv, lru_x, lru_g = outs
                k_att, cache = k_n, None
                s0 = jnp.zeros((n_b, 2, RWKV_PAIRS, LANES, LANES), F32)
                h0 = jnp.zeros((n_b, 2, LRU_WIDTH), F32)
                new_k.append(k_n.reshape(n_b, seq, ATT_KV_HEADS, HEAD_DIM))
                new_v.append(v.reshape(n_b, seq, ATT_KV_HEADS, HEAD_DIM))
            att = _attention(q, k_att, v, n_b, seq, cache)
            r, vv, na, w, kd, b, g, bonus = _rwkv_prep(p_rwkv.reshape(n_b, seq, RWKV_COLS), n_b, seq, prep_w, bd)
            W = RWKV_WIDTH
            yf, yb, s_fin = _rwkv_scan(r.reshape(n_b, seq, W), vv.reshape(n_b, seq, W), na.reshape(n_b, seq, W),
                                       w.reshape(2, n_b, seq, W), kd.reshape(2, n_b, seq, W),
                                       b.reshape(2, n_b, seq, W), s0, n_b, seq)
            lru_out, h_fin = _lru(lru_x, lru_g, h0, lru_w, n_b, seq)
            if not latent:
                new_sr.append(_unpack_state(s_fin))
                new_sl.append(h_fin)
            x1, h2, logits = _outproj(pth["x"], att, yf.reshape(n_b * seq, W), yb.reshape(n_b * seq, W), bonus, g,
                                      lru_out, mod_l, norm2[l][None, :], rwkv_ln_w[l][None, :],
                                      rwkv_ln_b[l][None, :], w_out_bf, l, router_pad, bd, seq, row0)
            xs, slot, aff = _route(logits, h2, tri, n_b, seq)
            mids.append((xs, slot, aff, x1))
        y_a, y_b = _experts(mids[0][0], mids[1][0], exp_w_gate, exp_w_up, exp_w_down, l)
        for pth, (xs, slot, aff, x1), y in zip(paths, mids, (y_a, y_b)):
            pth["x"] = _combine(y, slot, aff, x1, mod_l, pth["n_b"], pth["seq"], pth["row0"])

    y_prompt = paths[0]["x"].reshape(n_ctx, seq_ctx, D_MODEL)
    y_sample = paths[1]["x"].reshape(n_lat, seq_lat, D_MODEL)
    return (y_prompt, y_sample, jnp.stack(new_k, axis=1), jnp.stack(new_v, axis=1),
            jnp.stack(new_sr, axis=1), jnp.stack(new_sl, axis=1))
```

```python
import functools

import jax
import jax.numpy as jnp
from jax import lax
from jax.experimental import pallas as pl
from jax.experimental.pallas import tpu as pltpu

F32 = jnp.float32
BF16 = jnp.bfloat16
HIGHEST = lax.Precision.HIGHEST

D_MODEL = 1024
DEPTH = 2
GRID_W = 64
ATT_HEADS = 8
ATT_KV_HEADS = 2
HEAD_DIM = 64
ATT_WIDTH = ATT_HEADS * HEAD_DIM
KV_WIDTH = ATT_KV_HEADS * HEAD_DIM
ROPE_THETA = 10000.0
RWKV_HEADS = 4
RWKV_HEAD = 64
RWKV_WIDTH = RWKV_HEADS * RWKV_HEAD
RWKV_DECAY_LORA = 32
RWKV_AAA_LORA = 32
RWKV_GATE_LORA = 64
RWKV_COLS = 3 * RWKV_WIDTH + RWKV_DECAY_LORA + RWKV_AAA_LORA + RWKV_GATE_LORA
GN_EPS = 64e-5
LRU_BLOCKS = 4
LRU_BLOCK = 64
LRU_WIDTH = LRU_BLOCKS * LRU_BLOCK
LRU_CONV_W = 4
LRU_C = 8.0
IN_COLS = ATT_WIDTH + 2 * KV_WIDTH + RWKV_COLS + 2 * LRU_WIDTH
N_EXPERTS = 16
EC_FACTOR = 2
EXPERT_FF = 1024
RMS_EPS = 1e-6

LANES = 128
SUBLANES = 8
VMEM_LIMIT = 56 * 1024 * 1024
N_MOD = 6
COND_ROWS = 8
ONE_BITS = 0x3F800000


def _params(*sem):
    return pltpu.CompilerParams(dimension_semantics=sem, vmem_limit_bytes=VMEM_LIMIT)


def _dot(a, b, precision=None):
    return jnp.dot(a, b, preferred_element_type=F32, precision=precision)


def _sigmoid(x):
    return 1.0 / (1.0 + jnp.exp(-x))


def _softplus(x):
    return jnp.maximum(x, 0.0) + jnp.log1p(jnp.exp(-jnp.abs(x)))


def _seg_sum(x, ones_bd):
    hi = x.astype(BF16)
    lo = (x - hi.astype(F32)).astype(BF16)
    return _dot(hi, ones_bd) + _dot(lo, ones_bd)


def _ada_kernel(c_ref, w_ref, b_ref, o_ref):
    c = c_ref[...]
    s = c * _sigmoid(c)
    o_ref[...] = _dot(s, w_ref[...], HIGHEST) + b_ref[...]


def _ada(cond, w_ada, b_ada):
    n_l = w_ada.shape[0]
    tn = 1536
    n_t = N_MOD * D_MODEL // tn
    return pl.pallas_call(
        _ada_kernel,
        grid=(n_l, n_t),
        in_specs=[
            pl.BlockSpec((COND_ROWS, D_MODEL), lambda l, j: (0, 0)),
            pl.BlockSpec((None, D_MODEL, tn), lambda l, j: (l, 0, j)),
            pl.BlockSpec((None, 1, tn), lambda l, j: (l, 0, j)),
        ],
        out_specs=pl.BlockSpec((None, COND_ROWS, tn), lambda l, j: (l, 0, j)),
        out_shape=jax.ShapeDtypeStruct((n_l, COND_ROWS, N_MOD * D_MODEL), F32),
        compiler_params=_params("arbitrary", "arbitrary"),
        name="ada",
    )(cond, w_ada, b_ada)


def _head_rms(x, gain, ones_bd):
    ms = _seg_sum(x * x, ones_bd) * (1.0 / HEAD_DIM)
    return x * lax.rsqrt(ms + RMS_EPS) * gain


def _rope(x, cos, sin_up, sin_dn):
    return x * cos + pltpu.roll(x, LANES - 16, 1) * sin_up + pltpu.roll(x, 16, 1) * sin_dn


def _inproj_kernel(*refs, rope):
    if rope:
        (x_ref, mod_ref, n1_ref, w_ref, qn_ref, kn_ref, bd_ref, cos_ref, su_ref, sd_ref,
         q_ref, ko_ref, ka_ref, v_ref, pr_ref, lx_ref, lg_ref) = refs
    else:
        (x_ref, mod_ref, n1_ref, w_ref, qn_ref, kn_ref, bd_ref,
         q_ref, ko_ref, v_ref, pr_ref, lx_ref, lg_ref) = refs
    x = x_ref[...]
    rs = lax.rsqrt(jnp.mean(x * x, axis=-1, keepdims=True) + RMS_EPS)
    h = (x * rs * n1_ref[...]) * (1.0 + mod_ref[1:2, :]) + mod_ref[0:1, :]
    p = _dot(h.astype(BF16), w_ref[...])
    bd = bd_ref[...]
    for c in range(ATT_WIDTH // LANES):
        qc = _head_rms(p[:, c * LANES:(c + 1) * LANES], qn_ref[...], bd)
        if rope:
            qc = _rope(qc, cos_ref[...], su_ref[...], sd_ref[...])
        q_ref[:, c * LANES:(c + 1) * LANES] = qc
    o = ATT_WIDTH
    kc = _head_rms(p[:, o:o + KV_WIDTH], kn_ref[...], bd)
    ko_ref[...] = kc
    if rope:
        ka_ref[...] = _rope(kc, cos_ref[...], su_ref[...], sd_ref[...])
    o += KV_WIDTH
    v_ref[...] = p[:, o:o + KV_WIDTH]
    o += KV_WIDTH
    pr_ref[...] = p[:, o:o + RWKV_COLS]
    o += RWKV_COLS
    lx_ref[...] = p[:, o:o + LRU_WIDTH]
    o += LRU_WIDTH
    lg_ref[...] = p[:, o:o + LRU_WIDTH]


def _inproj(x, mod_l, norm1, w_in_bf, layer, qn, kn, bd, rope_tabs, seq, cond_row0):
    n_tok = x.shape[0]
    tm = 512 if seq >= 512 else seq * (512 // seq)
    per_seq = max(seq // tm, 1)
    n_t = n_tok // tm
    rope = rope_tabs is not None
    if cond_row0 == 0:
        row = lambda i: 0
    else:
        row = lambda i: cond_row0 + i // per_seq
    tok = lambda w: pl.BlockSpec((tm, w), lambda i: (i, 0))
    const = lambda shape: pl.BlockSpec(shape, lambda i: tuple(0 for _ in shape))
    in_specs = [
        tok(D_MODEL),
        pl.BlockSpec((None, N_MOD, D_MODEL), lambda i: (row(i), 0, 0)),
        const((1, D_MODEL)),
        pl.BlockSpec((None, D_MODEL, IN_COLS), lambda i: (layer, 0, 0)),
        const((1, LANES)), const((1, LANES)), const((LANES, LANES)),
    ]
    args = [x, mod_l, norm1, w_in_bf, qn, kn, bd]
    outs = [ATT_WIDTH, KV_WIDTH]
    if rope:
        in_specs += [pl.BlockSpec((tm, LANES), lambda i: (i % per_seq, 0))] * 3
        args += list(rope_tabs)
        outs.append(KV_WIDTH)
    outs += [KV_WIDTH, RWKV_COLS, LRU_WIDTH, LRU_WIDTH]
    return pl.pallas_call(
        functools.partial(_inproj_kernel, rope=rope),
        grid=(n_t,),
        in_specs=in_specs,
        out_specs=[tok(w) for w in outs],
        out_shape=[jax.ShapeDtypeStruct((n_tok, w), F32) for w in outs],
        compiler_params=_params("arbitrary"),
        name="inproj_rope" if rope else "inproj",
    )(*args)


def _attn_kernel(*refs, has_cache):
    if has_cache:
        q_ref, k_ref, v_ref, ck_ref, cv_ref, o_ref, kn_s, ks_s, vn_s, vs_s = refs
    else:
        q_ref, k_ref, v_ref, o_ref, kn_s, ks_s, vn_s, vs_s = refs

    @pl.when(pl.program_id(1) == 0)
    def _():
        k = k_ref[...]
        v = v_ref[...]
        if has_cache:
            k = jnp.concatenate([ck_ref[...], k], axis=0)
            v = jnp.concatenate([cv_ref[...], v], axis=0)
        kn_s[...] = k.astype(BF16)
        ks_s[...] = pltpu.roll(k, HEAD_DIM, 1).astype(BF16)
        vn_s[...] = v.astype(BF16)
        vs_s[...] = pltpu.roll(v, HEAD_DIM, 1).astype(BF16)

    tq = q_ref.shape[0]
    lo = lax.broadcasted_iota(jnp.int32, (tq, LANES), 1) < HEAD_DIM
    rep = ATT_HEADS // ATT_KV_HEADS
    for c in range(ATT_WIDTH // LANES):
        qc = q_ref[:, c * LANES:(c + 1) * LANES] * (HEAD_DIM ** -0.5)
        halves = []
        for half in range(2):
            g = (2 * c + half) // rep
            qm = jnp.where(lo if half == 0 else jnp.logical_not(lo), qc, 0.0).astype(BF16)
            k_s, v_s = (kn_s, vn_s) if half == g else (ks_s, vs_s)
            s = lax.dot_general(qm, k_s[...], (((1,), (1,)), ((), ())), preferred_element_type=F32)
            m = jnp.max(s, axis=-1, keepdims=True)
            e = jnp.exp(s - m)
            l = jnp.sum(e, axis=-1, keepdims=True)
            halves.append(_dot(e.astype(BF16), v_s[...]) / l)
        o_ref[:, c * LANES:(c + 1) * LANES] = jnp.where(lo, halves[0], halves[1])


def _attention(q, k, v, n_b, seq, cache=None):
    tq = 256
    n_q = seq // tq
    has_cache = cache is not None
    past = cache[0].shape[1] if has_cache else 0
    in_specs = [
        pl.BlockSpec((tq, ATT_WIDTH), lambda b, i: (b * n_q + i, 0)),
        pl.BlockSpec((seq, KV_WIDTH), lambda b, i: (b, 0)),
        pl.BlockSpec((seq, KV_WIDTH), lambda b, i: (b, 0)),
    ]
    args = [q, k, v]
    if has_cache:
        in_specs += [pl.BlockSpec((None, past, KV_WIDTH), lambda b, i: (b, 0, 0))] * 2
        args += list(cache)
    return pl.pallas_call(
        functools.partial(_attn_kernel, has_cache=has_cache),
        grid=(n_b, n_q),
        in_specs=in_specs,
        out_specs=pl.BlockSpec((tq, ATT_WIDTH), lambda b, i: (b * n_q + i, 0)),
        out_shape=jax.ShapeDtypeStruct((n_b * seq, ATT_WIDTH), F32),
        scratch_shapes=[pltpu.VMEM((past + seq, KV_WIDTH), BF16)] * 4,
        compiler_params=_params("arbitrary", "arbitrary"),
        name="attn_cache" if has_cache else "attn",
    )(*args)


def _rwkv_prep_kernel(p_ref, mu_ref, kk_ref, ka_ref, rk_ref, w0_ref, a0_ref, wup_ref, aup_ref,
                      gup_ref, bd_ref,
                      r_ref, v_ref, na_ref, w_ref, k_ref, b_ref, g_ref, bonus_ref):
    tm = r_ref.shape[0]
    seq = p_ref.shape[0]
    i = pl.program_id(1)
    n_t = pl.num_programs(1)
    start = pl.multiple_of(i * tm, tm)
    cur = p_ref[pl.ds(start, tm), :]
    prev_base = pl.multiple_of(jnp.maximum(start - SUBLANES, 0), SUBLANES)
    next_base = pl.multiple_of(jnp.minimum(start + tm, seq - SUBLANES), SUBLANES)
    prev_row = p_ref[pl.ds(prev_base, SUBLANES), :][SUBLANES - 1:SUBLANES]
    next_row = p_ref[pl.ds(next_base, SUBLANES), :][0:1]
    prev_row = jnp.where(i > 0, prev_row, 0.0)
    next_row = jnp.where(i < n_t - 1, next_row, 0.0)
    row = lax.broadcasted_iota(jnp.int32, cur.shape, 0)
    prev = jnp.where(row == 0, prev_row, pltpu.roll(cur, 1, 0))
    nxt = jnp.where(row == tm - 1, next_row, pltpu.roll(cur, tm - 1, 0))
    ps = cur + mu_ref[...] * (0.5 * (prev + nxt) - cur)

    W = RWKV_WIDTH
    r = ps[:, :W]
    k = ps[:, W:2 * W]
    v = ps[:, 2 * W:3 * W]
    lora = ps[:, 3 * W:]
    r_ref[...] = r
    v_ref[...] = v
    g_ref[...] = _dot(_sigmoid(lora), gup_ref[...], HIGHEST)
    lora_t = jnp.tanh(lora)
    kk = k * kk_ref[...]
    kk_parts = []
    for c in range(W // LANES):
        kc = kk[:, c * LANES:(c + 1) * LANES]
        nrm = jnp.sqrt(_seg_sum(kc * kc, bd_ref[...]))
        kk_parts.append(kc / jnp.maximum(nrm, 1e-12))
    kk = jnp.concatenate(kk_parts, axis=-1)
    na_ref[...] = -kk
    bonus = None
    for d in range(2):
        w_log = -_softplus(-(w0_ref[d:d + 1, :] + _dot(lora_t, wup_ref[d], HIGHEST))) - 0.5
        w_ref[d] = -jnp.exp(w_log)
        a_rate = _sigmoid(a0_ref[d:d + 1, :] + _dot(lora, aup_ref[d], HIGHEST))
        kd = k * (1.0 + (a_rate - 1.0) * ka_ref[...])
        k_ref[d] = kd
        b_ref[d] = kk * a_rate
        rkr = r * kd * rk_ref[...]
        parts = [_seg_sum(rkr[:, c * LANES:(c + 1) * LANES], bd_ref[...]) for c in range(W // LANES)]
        bd_term = jnp.concatenate(parts, axis=-1) * v
        bonus = bd_term if bonus is None else bonus + bd_term
    bonus_ref[...] = bonus


def _rwkv_prep(p, n_b, seq, wts, bd):
    mu, k_k, k_a, r_k, w0, a0, wup, aup, gup = wts
    tm = min(seq, 512)
    n_t = seq // tm
    W = RWKV_WIDTH
    const = lambda shape: pl.BlockSpec(shape, lambda b, i: tuple(0 for _ in shape))
    tok = pl.BlockSpec((tm, W), lambda b, i: (b * n_t + i, 0))
    tok2 = pl.BlockSpec((2, tm, W), lambda b, i: (0, b * n_t + i, 0))
    one = jax.ShapeDtypeStruct((n_b * seq, W), F32)
    two = jax.ShapeDtypeStruct((2, n_b * seq, W), F32)
    return pl.pallas_call(
        _rwkv_prep_kernel,
        grid=(n_b, n_t),
        in_specs=[
            pl.BlockSpec((None, seq, RWKV_COLS), lambda b, i: (b, 0, 0)),
            const((1, RWKV_COLS)), const((1, W)), const((1, W)), const((1, W)),
            const((2, W)), const((2, W)),
            const((2, LANES, W)), const((2, LANES, W)), const((LANES, W)),
            const((LANES, LANES)),
        ],
        out_specs=[tok, tok, tok, tok2, tok2, tok2, tok, tok],
        out_shape=[one, one, one, two, two, two, one, one],
        compiler_params=_params("arbitrary", "arbitrary"),
        name="rwkv_prep",
    )(p, mu, k_k, k_a, r_k, w0, a0, wup, aup, gup, bd)


RWKV_BB = 2
RWKV_PAIRS = RWKV_HEADS * RWKV_HEAD // LANES
RWKV_CHUNK = 256
RWKV_SUB = 64
INV_BASE_SHIFT = 3


def _rwkv_scan_kernel(rf_ref, vf_ref, af_ref, wf_ref, kf_ref, bf_ref,
                      rb_ref, vb_ref, ab_ref, wb_ref, kb_ref, bb_ref, s0_ref,
                      yf_ref, yb_ref, sfin_ref, st_s):
    j = pl.program_id(1)
    tc = rf_ref.shape[1]
    C = RWKV_SUB

    @pl.when(j == 0)
    def _():
        st_s[...] = s0_ref[...]

    lane_c = lax.broadcasted_iota(jnp.int32, (C, LANES), 1)
    row_c = lax.broadcasted_iota(jnp.int32, (C, LANES), 0)
    lo = lane_c < C
    s_idx = lane_c % C
    eye2 = jnp.where(s_idx == row_c, 1.0, 0.0)
    blk_masks = [jnp.where((s_idx >> sh) == (row_c >> sh), 1.0, 0.0) for sh in range(INV_BASE_SHIFT, 7)]
    lane_f = lax.broadcasted_iota(jnp.int32, (LANES, LANES), 1)
    row_f = lax.broadcasted_iota(jnp.int32, (LANES, LANES), 0)
    eye_f = lane_f == row_f
    lo_f = lane_f < C
    bd_mask = (lane_f < C) == (row_f < C)
    tt = lax.broadcasted_iota(jnp.int32, (C, C), 0)
    ss = lax.broadcasted_iota(jnp.int32, (C, C), 1)

    def bd(m):
        return jnp.concatenate([jnp.where(lo, m, 0.0), jnp.where(lo, 0.0, m)], axis=0)

    def bd_swap(m):
        return jnp.concatenate([jnp.where(lo, 0.0, m), jnp.where(lo, m, 0.0)], axis=0)

    def bdot(x, y):
        return _dot(x.astype(BF16), y.astype(BF16))

    dirs = ((rf_ref, vf_ref, af_ref, wf_ref, kf_ref, bf_ref, yf_ref),
            (rb_ref, vb_ref, ab_ref, wb_ref, kb_ref, bb_ref, yb_ref))

    def sub_chunk(i, carry):
        chains = []
        for d in range(2):
            base = pl.multiple_of(i * C if d == 0 else tc - C - i * C, C)
            before = (ss < tt) if d == 0 else (ss > tt)
            cum = jnp.where(jnp.logical_or(before, ss == tt), 1.0, 0.0).astype(BF16)
            strict = jnp.where((s_idx < row_c) if d == 0 else (s_idx > row_c), 1.0, 0.0)
            incl = jnp.where((s_idx <= row_c) if d == 0 else (s_idx >= row_c), 1.0, 0.0)
            tri_mask = jnp.concatenate([strict, incl], axis=0)
            last = C - 1 if d == 0 else 0
            for bi in range(RWKV_BB):
                for pr in range(RWKV_PAIRS):
                    chains.append(dict(d=d, bi=bi, pr=pr, base=base, refs=dirs[d], cum=cum, tri=tri_mask, last=last,
                                       ls=slice(pr * LANES, (pr + 1) * LANES)))

        for c in chains:
            r_r, v_r, a_r, w_r, k_r, b_r, _ = c["refs"]
            r, v, a, lw, k, b = (ref[c["bi"], pl.ds(c["base"], C), c["ls"]] for ref in (r_r, v_r, a_r, w_r, k_r, b_r))
            l1 = lw.astype(BF16)
            e1 = lw - l1.astype(F32)
            l2 = e1.astype(BF16)
            l3 = (e1 - l2.astype(F32)).astype(BF16)
            G = _dot(c["cum"], l1) + _dot(c["cum"], l2) + _dot(c["cum"], l3)
            g_inv = jnp.exp(-G)
            at = a * jnp.exp(G - lw)
            rt = r * jnp.exp(G)
            bt = b * g_inv
            kt = k * g_inv
            g_last = jnp.exp(G[c["last"]:c["last"] + 1, :])
            X = jnp.concatenate([at, rt], axis=0)
            c["X"] = X.astype(BF16)
            bt_b, kt_b = bt.astype(BF16), kt.astype(BF16)
            nt = (((1,), (1,)), ((), ()))
            P0 = lax.dot_general(jnp.where(lo_f, X, 0.0).astype(BF16), jnp.concatenate([bt_b, kt_b], axis=0), nt,
                                 preferred_element_type=F32) * c["tri"]
            P1 = lax.dot_general(jnp.where(lo_f, 0.0, X).astype(BF16), jnp.concatenate([kt_b, bt_b], axis=0), nt,
                                 preferred_element_type=F32) * c["tri"]
            ABRB = jnp.where(lo_f, P0, P1)
            c["AKRK"] = jnp.where(lo_f, P1, P0)
            c["AB"], c["RB"] = ABRB[:C], ABRB[C:]
            c["v"] = v
            c["ygt"] = jnp.concatenate([bt * g_last, kt * g_last], axis=0).T.astype(BF16)
            c["g_col"] = jnp.broadcast_to(g_last, (LANES, LANES)).T
        for c in chains:
            l8 = c["AB"] * blk_masks[0]
            c["T"] = eye2 + l8
            c["Lp"] = bdot(l8, bd(l8))
        for c in chains:
            R = bdot(jnp.concatenate([c["Lp"], c["T"]], axis=0), bd(c["Lp"]))
            c["T"] = c["T"] + R[C:]
            c["Lp"] = R[:C]
        for c in chains:
            c["T"] = c["T"] + bdot(c["T"], bd(c["Lp"]))
        for lvl in range(1, len(blk_masks)):
            for c in chains:
                c["Lp"] = bdot(c["AB"] * (blk_masks[lvl] - blk_masks[lvl - 1]), bd(c["T"]))
            for c in chains:
                c["T"] = c["T"] + bdot(c["T"], bd(c["Lp"]))
        for c in chains:
            c["S0"] = st_s[c["bi"], c["d"], c["pr"]]
            c["XS"] = _dot(c["X"], c["S0"].astype(BF16))
            c["VK"] = bdot(c["AKRK"], bd_swap(c["v"]))
        for c in chains:
            c["U"] = bdot(c["T"], bd(c["XS"][:C] + c["VK"][:C]))
        for c in chains:
            y_r = c["refs"][6]
            y_r[c["bi"], pl.ds(c["base"], C), c["ls"]] = c["XS"][C:] + c["VK"][C:] + bdot(c["RB"], bd(c["U"]))
            uv = jnp.concatenate([c["U"], c["v"]], axis=0).astype(BF16)
            st_s[c["bi"], c["d"], c["pr"]] = jnp.where(bd_mask, c["g_col"] * c["S0"] + _dot(c["ygt"], uv), 0.0)
        return carry

    lax.fori_loop(0, tc // C, sub_chunk, 0)

    @pl.when(j == pl.num_programs(1) - 1)
    def _():
        sfin_ref[...] = st_s[...]


def _rwkv_scan(r, v, na, lw, k, b, s0, n_b, seq):
    tc = min(seq, RWKV_CHUNK)
    n_c = seq // tc
    W = RWKV_WIDTH
    blk = (RWKV_BB, tc, W)
    fwd = pl.BlockSpec(blk, lambda g, j: (g, j, 0))
    bwd = pl.BlockSpec(blk, lambda g, j: (g, n_c - 1 - j, 0))
    fwd_d = pl.BlockSpec((None,) + blk, lambda g, j: (0, g, j, 0))
    bwd_d = pl.BlockSpec((None,) + blk, lambda g, j: (1, g, n_c - 1 - j, 0))
    st_blk = (RWKV_BB, 2, RWKV_PAIRS, LANES, LANES)
    st_spec = pl.BlockSpec(st_blk, lambda g, j: (g, 0, 0, 0, 0))
    y_shape = jax.ShapeDtypeStruct((n_b, seq, W), F32)
    return pl.pallas_call(
        _rwkv_scan_kernel,
        grid=(n_b // RWKV_BB, n_c),
        in_specs=[fwd, fwd, fwd, fwd_d, fwd_d, fwd_d, bwd, bwd, bwd, bwd_d, bwd_d, bwd_d, st_spec],
        out_specs=[fwd, bwd, st_spec],
        out_shape=[y_shape, y_shape, jax.ShapeDtypeStruct(s0.shape, F32)],
        scratch_shapes=[pltpu.VMEM(st_blk, F32)],
        compiler_params=_params("arbitrary", "arbitrary"),
        name="rwkv_scan",
    )(r, v, na, lw, k, b, r, v, na, lw, k, b, s0)


def _lru_kernel(x_ref, g_ref, h0_ref, cw_ref, cb_ref, wa_ref, ba_ref, wx_ref, bx_ref, lam_ref,
                o_ref, hfin_ref, a_s, u_s, h_s):
    seq = x_ref.shape[0]
    x = x_ref[...]
    row = lax.broadcasted_iota(jnp.int32, x.shape, 0)
    xm2 = jnp.where(row >= 2, pltpu.roll(x, 2, 0), 0.0)
    xm1 = jnp.where(row >= 1, pltpu.roll(x, 1, 0), 0.0)
    xp1 = jnp.where(row < seq - 1, pltpu.roll(x, seq - 1, 0), 0.0)
    xc = (xm2 * cw_ref[0:1, :] + xm1 * cw_ref[1:2, :] + x * cw_ref[2:3, :] + xp1 * cw_ref[3:4, :]
          + cb_ref[...])
    xb = xc.astype(BF16)
    for d in range(2):
        r_gate = _sigmoid(_dot(xb, wa_ref[d]) + ba_ref[d:d + 1, :])
        i_gate = _sigmoid(_dot(xb, wx_ref[d]) + bx_ref[d:d + 1, :])
        log_a = -LRU_C * r_gate * _softplus(-lam_ref[d:d + 1, :])
        a_s[d] = jnp.exp(log_a)
        u_s[d] = jnp.sqrt(1.0 - jnp.exp(2.0 * log_a)) * (i_gate * xc)

    n_t = seq // SUBLANES

    def body(i, carry):
        hf, hb = carry
        base = pl.multiple_of(i * SUBLANES, SUBLANES)
        a = a_s[0, pl.ds(base, SUBLANES), :]
        u = u_s[0, pl.ds(base, SUBLANES), :]
        rows = []
        for s in range(SUBLANES):
            hf = a[s:s + 1] * hf + u[s:s + 1]
            rows.append(hf)
        h_s[0, pl.ds(base, SUBLANES), :] = jnp.concatenate(rows, axis=0)
        base = pl.multiple_of(seq - SUBLANES - i * SUBLANES, SUBLANES)
        a = a_s[1, pl.ds(base, SUBLANES), :]
        u = u_s[1, pl.ds(base, SUBLANES), :]
        rows = [None] * SUBLANES
        for s in range(SUBLANES - 1, -1, -1):
            hb = a[s:s + 1] * hb + u[s:s + 1]
            rows[s] = hb
        h_s[1, pl.ds(base, SUBLANES), :] = jnp.concatenate(rows, axis=0)
        return hf, hb

    hf, hb = lax.fori_loop(0, n_t, body, (h0_ref[0:1, :], h0_ref[1:2, :]))
    hfin_ref[...] = jnp.concatenate([hf, hb], axis=0)
    g = g_ref[...]
    gelu = 0.5 * g * (1.0 + jnp.tanh(0.7978845608028654 * (g + 0.044715 * (g * g * g))))
    o_ref[...] = (h_s[0] + h_s[1]) * gelu


def _lru(xb, gb, h0, wts, n_b, seq):
    cw, cb, wa, ba, wx, bx, lam = wts
    C = LRU_WIDTH
    const = lambda shape: pl.BlockSpec(shape, lambda b: tuple(0 for _ in shape))
    tok = pl.BlockSpec((seq, C), lambda b: (b, 0))
    st = pl.BlockSpec((None, 2, C), lambda b: (b, 0, 0))
    return pl.pallas_call(
        _lru_kernel,
        grid=(n_b,),
        in_specs=[tok, tok, st, const((LRU_CONV_W, C)), const((1, C)), const((2, C, C)), const((2, C)),
                  const((2, C, C)), const((2, C)), const((2, C))],
        out_specs=[tok, st],
        out_shape=[jax.ShapeDtypeStruct((n_b * seq, C), F32), jax.ShapeDtypeStruct((n_b, 2, C), F32)],
        scratch_shapes=[pltpu.VMEM((2, seq, C), F32)] * 3,
        compiler_params=_params("arbitrary"),
        name="lru",
    )(xb, gb, h0, cw, cb, wa, ba, wx, bx, lam)


def _outproj_kernel(x_ref, att_ref, yf_ref, yb_ref, bonus_ref, g_ref, lru_ref, mod_ref, n2_ref,
                    lnw_ref, lnb_ref, w_ref, rt_ref, bd_ref,
                    x1_ref, h2_ref, lg_ref):
    y = yf_ref[...] + yb_ref[...]
    parts = []
    for c in range(RWKV_WIDTH // LANES):
        yc = y[:, c * LANES:(c + 1) * LANES]
        mean = _seg_sum(yc, bd_ref[...]) * (1.0 / RWKV_HEAD)
        dev = yc - mean
        var = _seg_sum(dev * dev, bd_ref[...]) * (1.0 / RWKV_HEAD)
        parts.append(dev * lax.rsqrt(var + GN_EPS))
    yn = jnp.concatenate(parts, axis=-1) * lnw_ref[...] + lnb_ref[...]
    rwkv = (yn + bonus_ref[...]) * g_ref[...]
    o1 = ATT_WIDTH
    o2 = o1 + RWKV_WIDTH
    mixed = (_dot(att_ref[...].astype(BF16), w_ref[:o1, :])
             + _dot(rwkv.astype(BF16), w_ref[o1:o2, :])
             + _dot(lru_ref[...].astype(BF16), w_ref[o2:, :]))
    x1 = x_ref[...] + mod_ref[2:3, :] * mixed
    x1_ref[...] = x1
    rs = lax.rsqrt(jnp.mean(x1 * x1, axis=-1, keepdims=True) + RMS_EPS)
    h2 = (x1 * rs * n2_ref[...]) * (1.0 + mod_ref[4:5, :]) + mod_ref[3:4, :]
    h2_ref[...] = h2.astype(BF16)
    lg_ref[...] = _dot(h2, rt_ref[...], HIGHEST)


def _outproj(x, att, yf, yb, bonus, g, lru, mod_l, norm2, lnw, lnb, w_out_bf, layer, router_pad, bd,
             seq, cond_row0):
    n_tok = x.shape[0]
    tm = 512 if seq >= 512 else seq * (512 // seq)
    per_seq = max(seq // tm, 1)
    n_t = n_tok // tm
    if cond_row0 == 0:
        row = lambda i: 0
    else:
        row = lambda i: cond_row0 + i // per_seq
    tok = lambda w: pl.BlockSpec((tm, w), lambda i: (i, 0))
    const = lambda shape: pl.BlockSpec(shape, lambda i: tuple(0 for _ in shape))
    W = RWKV_WIDTH
    return pl.pallas_call(
        _outproj_kernel,
        grid=(n_t,),
        in_specs=[
            tok(D_MODEL), tok(ATT_WIDTH), tok(W), tok(W), tok(W), tok(W), tok(LRU_WIDTH),
            pl.BlockSpec((None, N_MOD, D_MODEL), lambda i: (row(i), 0, 0)),
            const((1, D_MODEL)), const((1, W)), const((1, W)),
            pl.BlockSpec((None, D_MODEL, D_MODEL), lambda i: (layer, 0, 0)),
            const((D_MODEL, LANES)), const((LANES, LANES)),
        ],
        out_specs=[tok(D_MODEL), tok(D_MODEL), tok(LANES)],
        out_shape=[jax.ShapeDtypeStruct((n_tok, D_MODEL), F32),
                   jax.ShapeDtypeStruct((n_tok, D_MODEL), BF16),
                   jax.ShapeDtypeStruct((n_tok, LANES), F32)],
        compiler_params=_params("arbitrary"),
        name="outproj",
    )(x, att, yf, yb, bonus, g, lru, mod_l, norm2, lnw, lnb, w_out_bf, router_pad, bd)


PREFIX_BLOCK = 256
GATHER_ROWS = 1024


def _prefix_count(mask_f, tri):
    seq = mask_f.shape[0]
    outs = []
    carry = jnp.zeros((1, LANES), F32)
    for blk in range(seq // PREFIX_BLOCK):
        m = mask_f[blk * PREFIX_BLOCK:(blk + 1) * PREFIX_BLOCK]
        outs.append(_dot(tri, m.astype(BF16)) + carry)
        carry = carry + jnp.sum(m, axis=0, keepdims=True)
    return jnp.concatenate(outs, axis=0) if len(outs) > 1 else outs[0]


def _route_kernel(lg_ref, h2_ref, tri_ref, xs_ref, slot_ref, aff_ref, *, cap):
    seq = lg_ref.shape[0]
    lane = lax.broadcasted_iota(jnp.int32, (seq, LANES), 1)
    real = lane < N_EXPERTS
    lg = jnp.where(real, lg_ref[...], -jnp.inf)
    m = jnp.max(lg, axis=-1, keepdims=True)
    e = jnp.exp(lg - m)
    aff = e / jnp.sum(e, axis=-1, keepdims=True)
    aff_ref[...] = aff
    bits = lax.bitcast_convert_type(aff, jnp.int32)

    def bisect(_, carry):
        lo_b, hi_b = carry
        mid = lo_b + ((hi_b - lo_b) >> 1)
        cnt = jnp.sum(jnp.where(bits >= mid, 1.0, 0.0), axis=0, keepdims=True)
        ok = cnt >= cap
        return jnp.where(ok, mid, lo_b), jnp.where(ok, hi_b, mid)

    lo_b = jnp.zeros((1, LANES), jnp.int32)
    hi_b = jnp.full((1, LANES), ONE_BITS + 1, jnp.int32)
    thr, _ = lax.fori_loop(0, 32, bisect, (lo_b, hi_b))
    gt = jnp.where(bits > thr, 1.0, 0.0)
    eq = jnp.where(bits == thr, 1.0, 0.0)
    need = cap - jnp.sum(gt, axis=0, keepdims=True)
    tri = tri_ref[...]
    sel = jnp.where(real, gt + eq * jnp.where(_prefix_count(eq, tri) < need, 1.0, 0.0), 0.0)
    slot = jnp.where(sel > 0.0, _prefix_count(sel, tri), -1.0)
    slot_ref[...] = slot

    slot_t = slot.T
    h2 = h2_ref[...]
    c_iota = lax.broadcasted_iota(jnp.int32, (cap, seq), 0).astype(F32)
    group = max(1, min(N_EXPERTS, GATHER_ROWS // cap))
    for g0 in range(0, N_EXPERTS, group):
        onehot = jnp.concatenate(
            [jnp.where(c_iota == slot_t[ex:ex + 1, :], 1.0, 0.0) for ex in range(g0, g0 + group)], axis=0)
        rows = _dot(onehot.astype(BF16), h2)
        for k in range(group):
            xs_ref[g0 + k] = rows[k * cap:(k + 1) * cap].astype(BF16)


def _route(logits, h2, tri, n_b, seq):
    cap = EC_FACTOR * seq // N_EXPERTS
    tok = lambda w: pl.BlockSpec((seq, w), lambda b: (b, 0))
    return pl.pallas_call(
        functools.partial(_route_kernel, cap=cap),
        grid=(n_b,),
        in_specs=[tok(LANES), tok(D_MODEL), pl.BlockSpec((PREFIX_BLOCK, PREFIX_BLOCK), lambda b: (0, 0))],
        out_specs=[pl.BlockSpec((N_EXPERTS, cap, D_MODEL), lambda b: (0, b, 0)), tok(LANES), tok(LANES)],
        out_shape=[jax.ShapeDtypeStruct((N_EXPERTS, n_b * cap, D_MODEL), BF16),
                   jax.ShapeDtypeStruct((n_b * seq, LANES), F32),
                   jax.ShapeDtypeStruct((n_b * seq, LANES), F32)],
        compiler_params=_params("arbitrary"),
        name="route",
    )(logits, h2, tri)


EXPERT_FT = 512
EXPERT_RB = 512


def _expert_kernel(xa_ref, xb_ref, wg_ref, wu_ref, wd_ref, ya_ref, yb_ref, acc_a, acc_b):
    f = pl.program_id(1)
    last = pl.num_programs(1) - 1
    wg = wg_ref[...].astype(BF16)
    wu = wu_ref[...].astype(BF16)
    wd = wd_ref[...].astype(BF16)
    for x_ref, y_ref, acc in ((xa_ref, ya_ref, acc_a), (xb_ref, yb_ref, acc_b)):
        for rb in range(x_ref.shape[0] // EXPERT_RB):
            rows = slice(rb * EXPERT_RB, (rb + 1) * EXPERT_RB)
            x = x_ref[rows, :]
            a = _dot(x, wg)
            u = _dot(x, wu)
            hid = (a * _sigmoid(a) * u).astype(BF16)
            y = _dot(hid, wd)

            @pl.when(f == 0)
            def _():
                acc[rows, :] = y

            @pl.when(jnp.logical_and(f > 0, f < last))
            def _():
                acc[rows, :] += y

            @pl.when(f == last)
            def _():
                y_ref[rows, :] = (acc[rows, :] + y).astype(BF16)


def _experts(xs_a, xs_b, w_gate, w_up, w_down, layer):
    n_f = EXPERT_FF // EXPERT_FT
    ma, mb = xs_a.shape[1], xs_b.shape[1]
    xspec = lambda m: pl.BlockSpec((None, m, D_MODEL), lambda e, f: (e, 0, 0))
    return pl.pallas_call(
        _expert_kernel,
        grid=(N_EXPERTS, n_f),
        in_specs=[
            xspec(ma), xspec(mb),
            pl.BlockSpec((None, None, D_MODEL, EXPERT_FT), lambda e, f: (layer, e, 0, f)),
            pl.BlockSpec((None, None, D_MODEL, EXPERT_FT), lambda e, f: (layer, e, 0, f)),
            pl.BlockSpec((None, None, EXPERT_FT, D_MODEL), lambda e, f: (layer, e, f, 0)),
        ],
        out_specs=[xspec(ma), xspec(mb)],
        out_shape=[jax.ShapeDtypeStruct(xs_a.shape, BF16), jax.ShapeDtypeStruct(xs_b.shape, BF16)],
        scratch_shapes=[pltpu.VMEM((ma, D_MODEL), F32), pltpu.VMEM((mb, D_MODEL), F32)],
        compiler_params=_params("arbitrary", "arbitrary"),
        name="experts",
    )(xs_a, xs_b, w_gate, w_up, w_down)


def _combine_kernel(y_ref, slot_ref, aff_ref, x1_ref, mod_ref, o_ref, *, cap):
    seq = x1_ref.shape[0]
    c_iota = lax.broadcasted_iota(jnp.int32, (seq, cap), 1).astype(F32)
    slot = slot_ref[...]
    aff = aff_ref[...]
    acc = jnp.zeros((seq, D_MODEL), F32)
    for ex in range(N_EXPERTS):
        onehot = jnp.where(slot[:, ex:ex + 1] == c_iota, 1.0, 0.0).astype(BF16)
        acc = acc + aff[:, ex:ex + 1] * _dot(onehot, y_ref[ex])
    o_ref[...] = x1_ref[...] + mod_ref[5:6, :] * acc


def _combine(y, slot, aff, x1, mod_l, n_b, seq, cond_row0):
    cap = EC_FACTOR * seq // N_EXPERTS
    if cond_row0 == 0:
        row = lambda b: 0
    else:
        row = lambda b: cond_row0 + b
    tm = min(seq, 512)
    n_t = seq // tm
    tok = lambda w: pl.BlockSpec((tm, w), lambda b, i: (b * n_t + i, 0))
    return pl.pallas_call(
        functools.partial(_combine_kernel, cap=cap),
        grid=(n_b, n_t),
        in_specs=[
            pl.BlockSpec((N_EXPERTS, cap, D_MODEL), lambda b, i: (0, b, 0)),
            tok(LANES), tok(LANES), tok(D_MODEL),
            pl.BlockSpec((None, N_MOD, D_MODEL), lambda b, i: (row(b), 0, 0)),
        ],
        out_specs=tok(D_MODEL),
        out_shape=jax.ShapeDtypeStruct((n_b * seq, D_MODEL), F32),
        compiler_params=_params("arbitrary", "arbitrary"),
        name="combine",
    )(y, slot, aff, x1, mod_l)


def _rope_tables(seq):
    t = jnp.arange(seq, dtype=jnp.int32)
    row = (t // GRID_W).astype(F32)
    col = (t % GRID_W).astype(F32)
    half = HEAD_DIM // 2
    inv = ROPE_THETA ** (-jnp.arange(0, half, 2, dtype=F32) / half)
    lane = jnp.arange(LANES)
    u = lane % HEAD_DIM
    pos = jnp.where((u // half)[None, :] == 0, row[:, None], col[:, None])
    ang = pos * inv[(u % half) % (half // 2)][None, :]
    first = ((u % half) < half // 2)[None, :]
    sin = jnp.sin(ang)
    return jnp.cos(ang), jnp.where(first, -sin, 0.0), jnp.where(first, 0.0, sin)


def _block_diag(w):
    n, k, _ = w.shape
    eye = jnp.eye(n, dtype=w.dtype)
    return (eye[:, None, :, None] * w[:, :, None, :]).reshape(n * k, n * k)


def _pad_rows(w, offset, total=LANES):
    return jnp.zeros((total, w.shape[1]), w.dtype).at[offset:offset + w.shape[0]].set(w)


def _pack_state(s):
    b = s.shape[0]
    st = s.reshape(b, 2, RWKV_PAIRS, 2, RWKV_HEAD, RWKV_HEAD).transpose(0, 1, 2, 3, 5, 4)
    eye = jnp.eye(2, dtype=s.dtype)
    out = st[:, :, :, :, :, None, :] * eye[None, None, None, :, None, :, None]
    return out.reshape(b, 2, RWKV_PAIRS, LANES, LANES)


def _unpack_state(s):
    b = s.shape[0]
    s = s.reshape(b, 2, RWKV_PAIRS, 2, RWKV_HEAD, 2, RWKV_HEAD)
    diag = jnp.stack([s[:, :, :, h, :, h, :] for h in range(2)], axis=3)
    return diag.transpose(0, 1, 2, 3, 5, 4).reshape(b, 2, RWKV_HEADS, RWKV_HEAD, RWKV_HEAD)


def kernel(x_prompt, x_sample, cache_k, cache_v, state_rwkv, state_lru, c, c_ctx, w_ada, b_ada, norm1, norm2, w_in, w_out, q_norm, k_norm, rwkv_mu, rwkv_w0, rwkv_w_up, rwkv_a0, rwkv_a_up, rwkv_g_up, rwkv_k_k, rwkv_k_a, rwkv_r_k, rwkv_ln_w, rwkv_ln_b, lru_conv_w, lru_conv_b, lru_wa, lru_ba, lru_wx, lru_bx, lru_lambda, router, exp_w_gate, exp_w_up, exp_w_down):
    n_ctx, seq_ctx, _ = x_prompt.shape
    n_lat, seq_lat, _ = x_sample.shape
    past = cache_k.shape[2]
    assert n_lat + 1 <= COND_ROWS

    cond = jnp.zeros((COND_ROWS, D_MODEL), F32).at[0].set(c_ctx).at[1:1 + n_lat].set(c)
    mod = _ada(cond, w_ada, b_ada.reshape(DEPTH, 1, N_MOD * D_MODEL))
    mod = mod.reshape(DEPTH, COND_ROWS, N_MOD, D_MODEL)

    w_in_bf = w_in.astype(BF16)
    w_out_bf = w_out.astype(BF16)
    lane = jnp.arange(LANES)
    bd = (lane[:, None] // HEAD_DIM == lane[None, :] // HEAD_DIM).astype(BF16)
    pidx = jnp.arange(PREFIX_BLOCK)
    tri = (pidx[None, :] < pidx[:, None]).astype(BF16)
    rope_tabs = _rope_tables(seq_lat)

    paths = [
        dict(x=x_prompt.reshape(n_ctx * seq_ctx, D_MODEL), n_b=n_ctx, seq=seq_ctx, row0=0, rope=None),
        dict(x=x_sample.reshape(n_lat * seq_lat, D_MODEL), n_b=n_lat, seq=seq_lat, row0=1, rope=rope_tabs),
    ]
    new_k, new_v, new_sr, new_sl = [], [], [], []
    for l in range(DEPTH):
        mod_l = mod[l]
        qn = jnp.tile(q_norm[l], LANES // HEAD_DIM)[None, :]
        kn = jnp.tile(k_norm[l], LANES // HEAD_DIM)[None, :]
        prep_w = (
            rwkv_mu[l][None, :], rwkv_k_k[l][None, :], rwkv_k_a[l][None, :],
            rwkv_r_k[l].reshape(1, RWKV_WIDTH), rwkv_w0[l], rwkv_a0[l],
            jnp.stack([_pad_rows(rwkv_w_up[l, d], 0) for d in range(2)]),
            jnp.stack([_pad_rows(rwkv_a_up[l, d], RWKV_DECAY_LORA) for d in range(2)]),
            _pad_rows(rwkv_g_up[l], RWKV_DECAY_LORA + RWKV_AAA_LORA),
        )
        lru_w = (
            lru_conv_w[l], lru_conv_b[l][None, :],
            jnp.stack([_block_diag(lru_wa[l, d]) for d in range(2)]).astype(BF16), lru_ba[l],
            jnp.stack([_block_diag(lru_wx[l, d]) for d in range(2)]).astype(BF16), lru_bx[l],
            lru_lambda[l],
        )
        router_pad = jnp.zeros((D_MODEL, LANES), F32).at[:, :N_EXPERTS].set(router[l])
        mids = []
        for pi, pth in enumerate(paths):
            n_b, seq, row0 = pth["n_b"], pth["seq"], pth["row0"]
            latent = pth["rope"] is not None
            outs = _inproj(pth["x"], mod_l, norm1[l][None, :], w_in_bf, l, qn, kn, bd, pth["rope"], seq, row0)
            if latent:
                q, k_n, k_att, v, p_rwkv, lru_x, lru_g = outs
                cache = (cache_k[:, l].reshape(n_b, past, KV_WIDTH), cache_v[:, l].reshape(n_b, past, KV_WIDTH))
                s0 = _pack_state(state_rwkv[:, l])
                h0 = state_lru[:, l]
            else:
                q, k_n, v, p_rwkv, lru_x, lru_g = outs
                k_att, cache = k_n, None
                s0 = jnp.zeros((n_b, 2, RWKV_PAIRS, LANES, LANES), F32)
                h0 = jnp.zeros((n_b, 2, LRU_WIDTH), F32)
                new_k.append(k_n.reshape(n_b, seq, ATT_KV_HEADS, HEAD_DIM))
                new_v.append(v.reshape(n_b, seq, ATT_KV_HEADS, HEAD_DIM))
            att = _attention(q, k_att, v, n_b, seq, cache)
            r, vv, na, w, kd, b, g, bonus = _rwkv_prep(p_rwkv.reshape(n_b, seq, RWKV_COLS), n_b, seq, prep_w, bd)
            W = RWKV_WIDTH
            yf, yb, s_fin = _rwkv_scan(r.reshape(n_b, seq, W), vv.reshape(n_b, seq, W), na.reshape(n_b, seq, W),
                                       w.reshape(2, n_b, seq, W), kd.reshape(2, n_b, seq, W),
                                       b.reshape(2, n_b, seq, W), s0, n_b, seq)
            lru_out, h_fin = _lru(lru_x, lru_g, h0, lru_w, n_b, seq)
            if not latent:
                new_sr.append(_unpack_state(s_fin))
                new_sl.append(h_fin)
            x1, h2, logits = _outproj(pth["x"], att, yf.reshape(n_b * seq, W), yb.reshape(n_b * seq, W), bonus, g,
                                      lru_out, mod_l, norm2[l][None, :], rwkv_ln_w[l][None, :],
                                      rwkv_ln_b[l][None, :], w_out_bf, l, router_pad, bd, seq, row0)
            xs, slot, aff = _route(logits, h2, tri, n_b, seq)
            mids.append((xs, slot, aff, x1))
        y_a, y_b = _experts(mids[0][0], mids[1][0], exp_w_gate, exp_w_up, exp_w_down, l)
        for pth, (xs, slot, aff, x1), y in zip(paths, mids, (y_a, y_b)):
            pth["x"] = _combine(y, slot, aff, x1, mod_l, pth["n_b"], pth["seq"], pth["row0"])

    y_prompt = paths[0]["x"].reshape(n_ctx, seq_ctx, D_MODEL)
    y_sample = paths[1]["x"].reshape(n_lat, seq_lat, D_MODEL)
    return (y_prompt, y_sample, jnp.stack(new_k, axis=1), jnp.stack(new_v, axis=1),
            jnp.stack(new_sr, axis=1), jnp.stack(new_sl, axis=1))
```

```python
import functools

import jax
import jax.numpy as jnp
from jax import lax
from jax.experimental import pallas as pl
from jax.experimental.pallas import tpu as pltpu

F32 = jnp.float32
BF16 = jnp.bfloat16
HIGHEST = lax.Precision.HIGHEST

D_MODEL = 1024
DEPTH = 2
GRID_W = 64
ATT_HEADS = 8
ATT_KV_HEADS = 2
HEAD_DIM = 64
ATT_WIDTH = ATT_HEADS * HEAD_DIM
KV_WIDTH = ATT_KV_HEADS * HEAD_DIM
ROPE_THETA = 10000.0
RWKV_HEADS = 4
RWKV_HEAD = 64
RWKV_WIDTH = RWKV_HEADS * RWKV_HEAD
RWKV_DECAY_LORA = 32
RWKV_AAA_LORA = 32
RWKV_GATE_LORA = 64
RWKV_COLS = 3 * RWKV_WIDTH + RWKV_DECAY_LORA + RWKV_AAA_LORA + RWKV_GATE_LORA
GN_EPS = 64e-5
LRU_BLOCKS = 4
LRU_BLOCK = 64
LRU_WIDTH = LRU_BLOCKS * LRU_BLOCK
LRU_CONV_W = 4
LRU_C = 8.0
IN_COLS = ATT_WIDTH + 2 * KV_WIDTH + RWKV_COLS + 2 * LRU_WIDTH
N_EXPERTS = 16
EC_FACTOR = 2
EXPERT_FF = 1024
RMS_EPS = 1e-6

LANES = 128
SUBLANES = 8
VMEM_LIMIT = 56 * 1024 * 1024
N_MOD = 6
COND_ROWS = 8
BRACKET_LO, BRACKET_HI = -1.0, 2.0


def _params(*sem):
    return pltpu.CompilerParams(dimension_semantics=sem, vmem_limit_bytes=VMEM_LIMIT)


def _dot(a, b, precision=None):
    return jnp.dot(a, b, preferred_element_type=F32, precision=precision)


def _sigmoid(x):
    return 1.0 / (1.0 + jnp.exp(-x))


def _softplus(x):
    return jnp.maximum(x, 0.0) + jnp.log1p(jnp.exp(-jnp.abs(x)))


def _split(x):
    hi = x.astype(BF16)
    return hi, (x - hi.astype(F32)).astype(BF16)


def _dot_split(x, w_ref):
    hi, lo = _split(x)
    return _dot(hi, w_ref[0]) + _dot(hi, w_ref[1]) + _dot(lo, w_ref[0])


def _seg_sum(x, ones_bd):
    hi = x.astype(BF16)
    lo = (x - hi.astype(F32)).astype(BF16)
    return _dot(hi, ones_bd) + _dot(lo, ones_bd)


def _ada_kernel(c_ref, w_ref, b_ref, o_ref):
    c = c_ref[...]
    s = c * _sigmoid(c)
    o_ref[...] = _dot(s, w_ref[...], HIGHEST) + b_ref[...]


def _ada(cond, w_ada, b_ada):
    n_l = w_ada.shape[0]
    tn = 1536
    n_t = N_MOD * D_MODEL // tn
    return pl.pallas_call(
        _ada_kernel,
        grid=(n_l, n_t),
        in_specs=[
            pl.BlockSpec((COND_ROWS, D_MODEL), lambda l, j: (0, 0)),
            pl.BlockSpec((None, D_MODEL, tn), lambda l, j: (l, 0, j)),
            pl.BlockSpec((None, 1, tn), lambda l, j: (l, 0, j)),
        ],
        out_specs=pl.BlockSpec((None, COND_ROWS, tn), lambda l, j: (l, 0, j)),
        out_shape=jax.ShapeDtypeStruct((n_l, COND_ROWS, N_MOD * D_MODEL), F32),
        compiler_params=_params("arbitrary", "arbitrary"),
        name="ada",
    )(cond, w_ada, b_ada)


def _head_rms(x, gain, ones_bd):
    ms = _seg_sum(x * x, ones_bd) * (1.0 / HEAD_DIM)
    return x * lax.rsqrt(ms + RMS_EPS) * gain


def _rope(x, cos, sin_up, sin_dn):
    return x * cos + pltpu.roll(x, LANES - 16, 1) * sin_up + pltpu.roll(x, 16, 1) * sin_dn


def _inproj_kernel(*refs, rope):
    if rope:
        (x_ref, mod_ref, n1_ref, w_ref, qn_ref, kn_ref, bd_ref, cos_ref, su_ref, sd_ref,
         q_ref, ko_ref, ka_ref, v_ref, pr_ref, lx_ref, lg_ref) = refs
    else:
        (x_ref, mod_ref, n1_ref, w_ref, qn_ref, kn_ref, bd_ref,
         q_ref, ko_ref, v_ref, pr_ref, lx_ref, lg_ref) = refs
    x = x_ref[...]
    rs = lax.rsqrt(jnp.mean(x * x, axis=-1, keepdims=True) + RMS_EPS)
    h = (x * rs * n1_ref[...]) * (1.0 + mod_ref[1:2, :]) + mod_ref[0:1, :]
    p = _dot(h.astype(BF16), w_ref[...])
    bd = bd_ref[...]
    for c in range(ATT_WIDTH // LANES):
        qc = _head_rms(p[:, c * LANES:(c + 1) * LANES], qn_ref[...], bd)
        if rope:
            qc = _rope(qc, cos_ref[...], su_ref[...], sd_ref[...])
        q_ref[:, c * LANES:(c + 1) * LANES] = qc
    o = ATT_WIDTH
    kc = _head_rms(p[:, o:o + KV_WIDTH], kn_ref[...], bd)
    ko_ref[...] = kc
    if rope:
        ka_ref[...] = _rope(kc, cos_ref[...], su_ref[...], sd_ref[...])
    o += KV_WIDTH
    v_ref[...] = p[:, o:o + KV_WIDTH]
    o += KV_WIDTH
    pr_ref[...] = p[:, o:o + RWKV_COLS]
    o += RWKV_COLS
    lx_ref[...] = p[:, o:o + LRU_WIDTH]
    o += LRU_WIDTH
    lg_ref[...] = p[:, o:o + LRU_WIDTH]


def _inproj(x, mod_l, norm1, w_in_bf, layer, qn, kn, bd, rope_tabs, seq, cond_row0):
    n_tok = x.shape[0]
    tm = 512 if seq >= 512 else seq * (512 // seq)
    per_seq = max(seq // tm, 1)
    n_t = n_tok // tm
    rope = rope_tabs is not None
    if cond_row0 == 0:
        row = lambda i: 0
    else:
        row = lambda i: cond_row0 + i // per_seq
    tok = lambda w: pl.BlockSpec((tm, w), lambda i: (i, 0))
    const = lambda shape: pl.BlockSpec(shape, lambda i: tuple(0 for _ in shape))
    in_specs = [
        tok(D_MODEL),
        pl.BlockSpec((None, N_MOD, D_MODEL), lambda i: (row(i), 0, 0)),
        const((1, D_MODEL)),
        pl.BlockSpec((None, D_MODEL, IN_COLS), lambda i: (layer, 0, 0)),
        const((1, LANES)), const((1, LANES)), const((LANES, LANES)),
    ]
    args = [x, mod_l, norm1, w_in_bf, qn, kn, bd]
    outs = [ATT_WIDTH, KV_WIDTH]
    if rope:
        in_specs += [pl.BlockSpec((tm, LANES), lambda i: (i % per_seq, 0))] * 3
        args += list(rope_tabs)
        outs.append(KV_WIDTH)
    outs += [KV_WIDTH, RWKV_COLS, LRU_WIDTH, LRU_WIDTH]
    return pl.pallas_call(
        functools.partial(_inproj_kernel, rope=rope),
        grid=(n_t,),
        in_specs=in_specs,
        out_specs=[tok(w) for w in outs],
        out_shape=[jax.ShapeDtypeStruct((n_tok, w), F32) for w in outs],
        compiler_params=_params("arbitrary"),
        name="inproj_rope" if rope else "inproj",
    )(*args)


def _attn_kernel(*refs, has_cache):
    if has_cache:
        q_ref, k_ref, v_ref, ck_ref, cv_ref, o_ref, kn_s, ks_s, vn_s, vs_s = refs
    else:
        q_ref, k_ref, v_ref, o_ref, kn_s, ks_s, vn_s, vs_s = refs

    @pl.when(pl.program_id(1) == 0)
    def _():
        k = k_ref[...]
        v = v_ref[...]
        if has_cache:
            k = jnp.concatenate([ck_ref[...], k], axis=0)
            v = jnp.concatenate([cv_ref[...], v], axis=0)
        kn_s[...] = k.astype(BF16)
        ks_s[...] = pltpu.roll(k, HEAD_DIM, 1).astype(BF16)
        vn_s[...] = v.astype(BF16)
        vs_s[...] = pltpu.roll(v, HEAD_DIM, 1).astype(BF16)

    tq = q_ref.shape[0]
    lo = lax.broadcasted_iota(jnp.int32, (tq, LANES), 1) < HEAD_DIM
    rep = ATT_HEADS // ATT_KV_HEADS
    for c in range(ATT_WIDTH // LANES):
        qc = q_ref[:, c * LANES:(c + 1) * LANES] * (HEAD_DIM ** -0.5)
        halves = []
        for half in range(2):
            g = (2 * c + half) // rep
            qm = jnp.where(lo if half == 0 else jnp.logical_not(lo), qc, 0.0).astype(BF16)
            k_s, v_s = (kn_s, vn_s) if half == g else (ks_s, vs_s)
            s = lax.dot_general(qm, k_s[...], (((1,), (1,)), ((), ())), preferred_element_type=F32)
            m = jnp.max(s, axis=-1, keepdims=True)
            e = jnp.exp(s - m)
            l = jnp.sum(e, axis=-1, keepdims=True)
            halves.append(_dot(e.astype(BF16), v_s[...]) / l)
        o_ref[:, c * LANES:(c + 1) * LANES] = jnp.where(lo, halves[0], halves[1])


def _attention(q, k, v, n_b, seq, cache=None):
    tq = 256
    n_q = seq // tq
    has_cache = cache is not None
    past = cache[0].shape[1] if has_cache else 0
    in_specs = [
        pl.BlockSpec((tq, ATT_WIDTH), lambda b, i: (b * n_q + i, 0)),
        pl.BlockSpec((seq, KV_WIDTH), lambda b, i: (b, 0)),
        pl.BlockSpec((seq, KV_WIDTH), lambda b, i: (b, 0)),
    ]
    args = [q, k, v]
    if has_cache:
        in_specs += [pl.BlockSpec((None, past, KV_WIDTH), lambda b, i: (b, 0, 0))] * 2
        args += list(cache)
    return pl.pallas_call(
        functools.partial(_attn_kernel, has_cache=has_cache),
        grid=(n_b, n_q),
        in_specs=in_specs,
        out_specs=pl.BlockSpec((tq, ATT_WIDTH), lambda b, i: (b * n_q + i, 0)),
        out_shape=jax.ShapeDtypeStruct((n_b * seq, ATT_WIDTH), F32),
        scratch_shapes=[pltpu.VMEM((past + seq, KV_WIDTH), BF16)] * 4,
        compiler_params=_params("arbitrary", "arbitrary"),
        name="attn_cache" if has_cache else "attn",
    )(*args)


def _rwkv_prep_kernel(p_ref, mu_ref, kk_ref, ka_ref, rk_ref, w0_ref, a0_ref, wup_ref, aup_ref,
                      gup_ref, bd_ref,
                      r_ref, v_ref, na_ref, w_ref, k_ref, b_ref, g_ref, bonus_ref):
    tm = r_ref.shape[0]
    seq = p_ref.shape[0]
    i = pl.program_id(1)
    n_t = pl.num_programs(1)
    start = pl.multiple_of(i * tm, tm)
    cur = p_ref[pl.ds(start, tm), :]
    prev_base = pl.multiple_of(jnp.maximum(start - SUBLANES, 0), SUBLANES)
    next_base = pl.multiple_of(jnp.minimum(start + tm, seq - SUBLANES), SUBLANES)
    prev_row = p_ref[pl.ds(prev_base, SUBLANES), :][SUBLANES - 1:SUBLANES]
    next_row = p_ref[pl.ds(next_base, SUBLANES), :][0:1]
    prev_row = jnp.where(i > 0, prev_row, 0.0)
    next_row = jnp.where(i < n_t - 1, next_row, 0.0)
    row = lax.broadcasted_iota(jnp.int32, cur.shape, 0)
    prev = jnp.where(row == 0, prev_row, pltpu.roll(cur, 1, 0))
    nxt = jnp.where(row == tm - 1, next_row, pltpu.roll(cur, tm - 1, 0))
    ps = cur + mu_ref[...] * (0.5 * (prev + nxt) - cur)

    W = RWKV_WIDTH
    r = ps[:, :W]
    k = ps[:, W:2 * W]
    v = ps[:, 2 * W:3 * W]
    lora = ps[:, 3 * W:]
    r_ref[...] = r
    v_ref[...] = v
    g_ref[...] = _dot_split(_sigmoid(lora), gup_ref)
    lora_t = jnp.tanh(lora)
    kk = k * kk_ref[...]
    kk_parts = []
    for c in range(W // LANES):
        kc = kk[:, c * LANES:(c + 1) * LANES]
        nrm = jnp.sqrt(_seg_sum(kc * kc, bd_ref[...]))
        kk_parts.append(kc / jnp.maximum(nrm, 1e-12))
    kk = jnp.concatenate(kk_parts, axis=-1)
    na_ref[...] = -kk
    bonus = None
    for d in range(2):
        w_log = -_softplus(-(w0_ref[d:d + 1, :] + _dot_split(lora_t, wup_ref.at[d]))) - 0.5
        w_ref[d] = -jnp.exp(w_log)
        a_rate = _sigmoid(a0_ref[d:d + 1, :] + _dot_split(lora, aup_ref.at[d]))
        kd = k * (1.0 + (a_rate - 1.0) * ka_ref[...])
        k_ref[d] = kd
        b_ref[d] = kk * a_rate
        rkr = r * kd * rk_ref[...]
        parts = [_seg_sum(rkr[:, c * LANES:(c + 1) * LANES], bd_ref[...]) for c in range(W // LANES)]
        bd_term = jnp.concatenate(parts, axis=-1) * v
        bonus = bd_term if bonus is None else bonus + bd_term
    bonus_ref[...] = bonus


def _rwkv_prep(p, n_b, seq, wts, bd):
    mu, k_k, k_a, r_k, w0, a0, wup, aup, gup = wts
    tm = min(seq, 512)
    n_t = seq // tm
    W = RWKV_WIDTH
    const = lambda shape: pl.BlockSpec(shape, lambda b, i: tuple(0 for _ in shape))
    tok = pl.BlockSpec((tm, W), lambda b, i: (b * n_t + i, 0))
    tok2 = pl.BlockSpec((2, tm, W), lambda b, i: (0, b * n_t + i, 0))
    one = jax.ShapeDtypeStruct((n_b * seq, W), F32)
    two = jax.ShapeDtypeStruct((2, n_b * seq, W), F32)
    return pl.pallas_call(
        _rwkv_prep_kernel,
        grid=(n_b, n_t),
        in_specs=[
            pl.BlockSpec((None, seq, RWKV_COLS), lambda b, i: (b, 0, 0)),
            const((1, RWKV_COLS)), const((1, W)), const((1, W)), const((1, W)),
            const((2, W)), const((2, W)),
            const((2, 2, LANES, W)), const((2, 2, LANES, W)), const((2, LANES, W)),
            const((LANES, LANES)),
        ],
        out_specs=[tok, tok, tok, tok2, tok2, tok2, tok, tok],
        out_shape=[one, one, one, two, two, two, one, one],
        compiler_params=_params("arbitrary", "arbitrary"),
        name="rwkv_prep",
    )(p, mu, k_k, k_a, r_k, w0, a0, wup, aup, gup, bd)


RWKV_BB = 4
RWKV_PAIRS = RWKV_HEADS * RWKV_HEAD // LANES
RWKV_CHUNK = 256
RWKV_SUB = 64
INV_BASE_SHIFT = 3


def _rwkv_scan_kernel(rf_ref, vf_ref, af_ref, wf_ref, kf_ref, bf_ref,
                      rb_ref, vb_ref, ab_ref, wb_ref, kb_ref, bb_ref, s0_ref,
                      yf_ref, yb_ref, sfin_ref, st_s):
    j = pl.program_id(1)
    n_req, tc = rf_ref.shape[0], rf_ref.shape[1]
    C = RWKV_SUB

    @pl.when(j == 0)
    def _():
        st_s[...] = s0_ref[...]

    lane_c = lax.broadcasted_iota(jnp.int32, (C, LANES), 1)
    row_c = lax.broadcasted_iota(jnp.int32, (C, LANES), 0)
    lo = lane_c < C
    s_idx = lane_c % C
    eye2 = jnp.where(s_idx == row_c, 1.0, 0.0)
    blk_masks = [jnp.where((s_idx >> sh) == (row_c >> sh), 1.0, 0.0) for sh in range(INV_BASE_SHIFT, 7)]
    lane_f = lax.broadcasted_iota(jnp.int32, (LANES, LANES), 1)
    row_f = lax.broadcasted_iota(jnp.int32, (LANES, LANES), 0)
    eye_f = lane_f == row_f
    lo_f = lane_f < C
    bd_mask = (lane_f < C) == (row_f < C)
    tt = lax.broadcasted_iota(jnp.int32, (C, C), 0)
    ss = lax.broadcasted_iota(jnp.int32, (C, C), 1)

    def bd(m):
        return jnp.concatenate([jnp.where(lo, m, 0.0), jnp.where(lo, 0.0, m)], axis=0)

    def bd_swap(m):
        return jnp.concatenate([jnp.where(lo, 0.0, m), jnp.where(lo, m, 0.0)], axis=0)

    def bdot(x, y):
        return _dot(x.astype(BF16), y.astype(BF16))

    dirs = ((rf_ref, vf_ref, af_ref, wf_ref, kf_ref, bf_ref, yf_ref),
            (rb_ref, vb_ref, ab_ref, wb_ref, kb_ref, bb_ref, yb_ref))

    def sub_chunk(i, carry):
        chains = []
        for d in range(2):
            base = pl.multiple_of(i * C if d == 0 else tc - C - i * C, C)
            before = (ss < tt) if d == 0 else (ss > tt)
            cum = jnp.where(jnp.logical_or(before, ss == tt), 1.0, 0.0).astype(BF16)
            strict = jnp.where((s_idx < row_c) if d == 0 else (s_idx > row_c), 1.0, 0.0)
            incl = jnp.where((s_idx <= row_c) if d == 0 else (s_idx >= row_c), 1.0, 0.0)
            tri_mask = jnp.concatenate([strict, incl], axis=0)
            last = C - 1 if d == 0 else 0
            for bi in range(n_req):
                for pr in range(RWKV_PAIRS):
                    chains.append(dict(d=d, bi=bi, pr=pr, base=base, refs=dirs[d], cum=cum, tri=tri_mask, last=last,
                                       ls=slice(pr * LANES, (pr + 1) * LANES)))

        for c in chains:
            r_r, v_r, a_r, w_r, k_r, b_r, _ = c["refs"]
            r, v, a, lw, k, b = (ref[c["bi"], pl.ds(c["base"], C), c["ls"]] for ref in (r_r, v_r, a_r, w_r, k_r, b_r))
            l1 = lw.astype(BF16)
            e1 = lw - l1.astype(F32)
            l2 = e1.astype(BF16)
            l3 = (e1 - l2.astype(F32)).astype(BF16)
            c["G"] = _dot(c["cum"], l1) + _dot(c["cum"], l2) + _dot(c["cum"], l3)
            c["in"] = (r, v, a, lw, k, b)
        for c in chains:
            r, v, a, lw, k, b = c.pop("in")
            G = c.pop("G")
            g_inv = jnp.exp(-G)
            at = a * jnp.exp(G - lw)
            rt = r * jnp.exp(G)
            bt = b * g_inv
            kt = k * g_inv
            g_last = jnp.exp(G[c["last"]:c["last"] + 1, :])
            X = jnp.concatenate([at, rt], axis=0)
            c["X"] = X.astype(BF16)
            bt_b, kt_b = bt.astype(BF16), kt.astype(BF16)
            nt = (((1,), (1,)), ((), ()))
            c["P0"] = lax.dot_general(jnp.where(lo_f, X, 0.0).astype(BF16), jnp.concatenate([bt_b, kt_b], axis=0),
                                      nt, preferred_element_type=F32)
            c["P1"] = lax.dot_general(jnp.where(lo_f, 0.0, X).astype(BF16), jnp.concatenate([kt_b, bt_b], axis=0),
                                      nt, preferred_element_type=F32)
            c["v"] = v
            c["ygt"] = jnp.concatenate([bt * g_last, kt * g_last], axis=0).T.astype(BF16)
            c["g_col"] = jnp.broadcast_to(g_last, (LANES, LANES)).T
        for c in chains:
            P0 = c.pop("P0") * c["tri"]
            P1 = c.pop("P1") * c["tri"]
            ABRB = jnp.where(lo_f, P0, P1)
            c["AKRK"] = jnp.where(lo_f, P1, P0)
            c["AB"], c["RB"] = ABRB[:C], ABRB[C:]
        for c in chains:
            l8 = c["AB"] * blk_masks[0]
            c["T"] = eye2 + l8
            c["Lp"] = bdot(l8, bd(l8))
        for c in chains:
            R = bdot(jnp.concatenate([c["Lp"], c["T"]], axis=0), bd(c["Lp"]))
            c["T"] = c["T"] + R[C:]
            c["Lp"] = R[:C]
        for c in chains:
            c["T"] = c["T"] + bdot(c["T"], bd(c["Lp"]))
        for lvl in range(1, len(blk_masks)):
            for c in chains:
                c["Lp"] = bdot(c["AB"] * (blk_masks[lvl] - blk_masks[lvl - 1]), bd(c["T"]))
            for c in chains:
                c["T"] = c["T"] + bdot(c["T"], bd(c["Lp"]))
        for c in chains:
            c["S0"] = st_s[c["bi"], c["d"], c["pr"]]
            c["XS"] = _dot(c["X"], c["S0"].astype(BF16))
            c["VK"] = bdot(c["AKRK"], bd_swap(c["v"]))
        for c in chains:
            c["U"] = bdot(c["T"], bd(c["XS"][:C] + c["VK"][:C]))
        for c in chains:
            y_r = c["refs"][6]
            y_r[c["bi"], pl.ds(c["base"], C), c["ls"]] = c["XS"][C:] + c["VK"][C:] + bdot(c["RB"], bd(c["U"]))
            uv = jnp.concatenate([c["U"], c["v"]], axis=0).astype(BF16)
            st_s[c["bi"], c["d"], c["pr"]] = jnp.where(bd_mask, c["g_col"] * c["S0"] + _dot(c["ygt"], uv), 0.0)
        return carry

    lax.fori_loop(0, tc // C, sub_chunk, 0)

    @pl.when(j == pl.num_programs(1) - 1)
    def _():
        for bi in range(n_req):
            for d in range(2):
                for pr in range(RWKV_PAIRS):
                    mt = st_s[bi, d, pr].T
                    sfin_ref[bi, d, pr] = jnp.where(row_f < C, mt, pltpu.roll(mt, C, 1))[:, :C]


def _rwkv_scan(r, v, na, lw, k, b, s0, n_b, seq):
    tc = min(seq, RWKV_CHUNK)
    n_c = seq // tc
    W = RWKV_WIDTH
    bb = min(RWKV_BB, n_b)
    blk = (bb, tc, W)
    fwd = pl.BlockSpec(blk, lambda g, j: (g, j, 0))
    bwd = pl.BlockSpec(blk, lambda g, j: (g, n_c - 1 - j, 0))
    fwd_d = pl.BlockSpec((None,) + blk, lambda g, j: (0, g, j, 0))
    bwd_d = pl.BlockSpec((None,) + blk, lambda g, j: (1, g, n_c - 1 - j, 0))
    st_blk = (bb, 2, RWKV_PAIRS, LANES, LANES)
    st_spec = pl.BlockSpec(st_blk, lambda g, j: (g, 0, 0, 0, 0))
    y_shape = jax.ShapeDtypeStruct((n_b, seq, W), F32)
    fin_blk = (bb, 2, RWKV_PAIRS, LANES, RWKV_HEAD)
    fin_spec = pl.BlockSpec(fin_blk, lambda g, j: (g, 0, 0, 0, 0))
    return pl.pallas_call(
        _rwkv_scan_kernel,
        grid=(n_b // bb, n_c),
        in_specs=[fwd, fwd, fwd, fwd_d, fwd_d, fwd_d, bwd, bwd, bwd, bwd_d, bwd_d, bwd_d, st_spec],
        out_specs=[fwd, bwd, fin_spec],
        out_shape=[y_shape, y_shape, jax.ShapeDtypeStruct((n_b,) + fin_blk[1:], F32)],
        scratch_shapes=[pltpu.VMEM(st_blk, F32)],
        compiler_params=_params("arbitrary", "arbitrary"),
        name="rwkv_scan",
    )(r, v, na, lw, k, b, r, v, na, lw, k, b, s0)


def _lru_kernel(x_ref, g_ref, h0_ref, cw_ref, cb_ref, wa_ref, ba_ref, wx_ref, bx_ref, lam_ref,
                o_ref, hfin_ref, a_s, u_s, h_s):
    seq = x_ref.shape[0]
    x = x_ref[...]
    row = lax.broadcasted_iota(jnp.int32, x.shape, 0)
    xm2 = jnp.where(row >= 2, pltpu.roll(x, 2, 0), 0.0)
    xm1 = jnp.where(row >= 1, pltpu.roll(x, 1, 0), 0.0)
    xp1 = jnp.where(row < seq - 1, pltpu.roll(x, seq - 1, 0), 0.0)
    xc = (xm2 * cw_ref[0:1, :] + xm1 * cw_ref[1:2, :] + x * cw_ref[2:3, :] + xp1 * cw_ref[3:4, :]
          + cb_ref[...])
    xb = xc.astype(BF16)
    for d in range(2):
        r_gate = _sigmoid(_dot(xb, wa_ref[d]) + ba_ref[d:d + 1, :])
        i_gate = _sigmoid(_dot(xb, wx_ref[d]) + bx_ref[d:d + 1, :])
        log_a = -LRU_C * r_gate * _softplus(-lam_ref[d:d + 1, :])
        a = jnp.exp(log_a)
        u = jnp.sqrt(1.0 - jnp.exp(2.0 * log_a)) * (i_gate * xc)
        for sh in (1, 2, 4):
            if d == 0:
                a_n, u_n = pltpu.roll(a, sh, 0), pltpu.roll(u, sh, 0)
                m = (row % SUBLANES) >= sh
            else:
                a_n, u_n = pltpu.roll(a, seq - sh, 0), pltpu.roll(u, seq - sh, 0)
                m = (row % SUBLANES) < SUBLANES - sh
            u = jnp.where(m, a * u_n + u, u)
            a = jnp.where(m, a * a_n, a)
        a_s[d] = a
        u_s[d] = u

    n_t = seq // SUBLANES
    unroll = 4

    def body(i, carry):
        hf, hb = carry
        for k in range(unroll):
            base = pl.multiple_of((i * unroll + k) * SUBLANES, SUBLANES)
            h8 = a_s[0, pl.ds(base, SUBLANES), :] * hf + u_s[0, pl.ds(base, SUBLANES), :]
            h_s[0, pl.ds(base, SUBLANES), :] = h8
            hf = h8[SUBLANES - 1:SUBLANES]
            base = pl.multiple_of(seq - SUBLANES - (i * unroll + k) * SUBLANES, SUBLANES)
            h8 = a_s[1, pl.ds(base, SUBLANES), :] * hb + u_s[1, pl.ds(base, SUBLANES), :]
            h_s[1, pl.ds(base, SUBLANES), :] = h8
            hb = h8[0:1]
        return hf, hb

    hf, hb = lax.fori_loop(0, n_t // unroll, body, (h0_ref[0:1, :], h0_ref[1:2, :]))
    hfin_ref[...] = jnp.concatenate([hf, hb], axis=0)
    g = g_ref[...]
    gelu = 0.5 * g * (1.0 + jnp.tanh(0.7978845608028654 * (g + 0.044715 * (g * g * g))))
    o_ref[...] = (h_s[0] + h_s[1]) * gelu


def _lru(xb, gb, h0, wts, n_b, seq):
    cw, cb, wa, ba, wx, bx, lam = wts
    C = LRU_WIDTH
    const = lambda shape: pl.BlockSpec(shape, lambda b: tuple(0 for _ in shape))
    tok = pl.BlockSpec((seq, C), lambda b: (b, 0))
    st = pl.BlockSpec((None, 2, C), lambda b: (b, 0, 0))
    return pl.pallas_call(
        _lru_kernel,
        grid=(n_b,),
        in_specs=[tok, tok, st, const((LRU_CONV_W, C)), const((1, C)), const((2, C, C)), const((2, C)),
                  const((2, C, C)), const((2, C)), const((2, C))],
        out_specs=[tok, st],
        out_shape=[jax.ShapeDtypeStruct((n_b * seq, C), F32), jax.ShapeDtypeStruct((n_b, 2, C), F32)],
        scratch_shapes=[pltpu.VMEM((2, seq, C), F32)] * 3,
        compiler_params=_params("arbitrary"),
        name="lru",
    )(xb, gb, h0, cw, cb, wa, ba, wx, bx, lam)


def _outproj_kernel(x_ref, att_ref, yf_ref, yb_ref, bonus_ref, g_ref, lru_ref, mod_ref, n2_ref,
                    lnw_ref, lnb_ref, w_ref, rt_ref, bd_ref,
                    x1_ref, h2_ref, lg_ref):
    y = yf_ref[...] + yb_ref[...]
    parts = []
    for c in range(RWKV_WIDTH // LANES):
        yc = y[:, c * LANES:(c + 1) * LANES]
        mean = _seg_sum(yc, bd_ref[...]) * (1.0 / RWKV_HEAD)
        dev = yc - mean
        var = _seg_sum(dev * dev, bd_ref[...]) * (1.0 / RWKV_HEAD)
        parts.append(dev * lax.rsqrt(var + GN_EPS))
    yn = jnp.concatenate(parts, axis=-1) * lnw_ref[...] + lnb_ref[...]
    rwkv = (yn + bonus_ref[...]) * g_ref[...]
    o1 = ATT_WIDTH
    o2 = o1 + RWKV_WIDTH
    mixed = (_dot(att_ref[...].astype(BF16), w_ref[:o1, :])
             + _dot(rwkv.astype(BF16), w_ref[o1:o2, :])
             + _dot(lru_ref[...].astype(BF16), w_ref[o2:, :]))
    x1 = x_ref[...] + mod_ref[2:3, :] * mixed
    x1_ref[...] = x1
    rs = lax.rsqrt(jnp.mean(x1 * x1, axis=-1, keepdims=True) + RMS_EPS)
    h2 = (x1 * rs * n2_ref[...]) * (1.0 + mod_ref[4:5, :]) + mod_ref[3:4, :]
    h2_hi = h2.astype(BF16)
    h2_lo = (h2 - h2_hi.astype(F32)).astype(BF16)
    h2_ref[...] = h2_hi
    lg_ref[...] = _dot(h2_hi, rt_ref[0]) + _dot(h2_hi, rt_ref[1]) + _dot(h2_lo, rt_ref[0])


def _outproj(x, att, yf, yb, bonus, g, lru, mod_l, norm2, lnw, lnb, w_out_bf, layer, router_pad, bd,
             seq, cond_row0):
    n_tok = x.shape[0]
    tm = 512 if seq >= 512 else seq * (512 // seq)
    per_seq = max(seq // tm, 1)
    n_t = n_tok // tm
    if cond_row0 == 0:
        row = lambda i: 0
    else:
        row = lambda i: cond_row0 + i // per_seq
    tok = lambda w: pl.BlockSpec((tm, w), lambda i: (i, 0))
    const = lambda shape: pl.BlockSpec(shape, lambda i: tuple(0 for _ in shape))
    W = RWKV_WIDTH
    return pl.pallas_call(
        _outproj_kernel,
        grid=(n_t,),
        in_specs=[
            tok(D_MODEL), tok(ATT_WIDTH), tok(W), tok(W), tok(W), tok(W), tok(LRU_WIDTH),
            pl.BlockSpec((None, N_MOD, D_MODEL), lambda i: (row(i), 0, 0)),
            const((1, D_MODEL)), const((1, W)), const((1, W)),
            pl.BlockSpec((None, D_MODEL, D_MODEL), lambda i: (layer, 0, 0)),
            const((2, D_MODEL, LANES)), const((LANES, LANES)),
        ],
        out_specs=[tok(D_MODEL), tok(D_MODEL), tok(LANES)],
        out_shape=[jax.ShapeDtypeStruct((n_tok, D_MODEL), F32),
                   jax.ShapeDtypeStruct((n_tok, D_MODEL), BF16),
                   jax.ShapeDtypeStruct((n_tok, LANES), F32)],
        compiler_params=_params("arbitrary"),
        name="outproj",
    )(x, att, yf, yb, bonus, g, lru, mod_l, norm2, lnw, lnb, w_out_bf, router_pad, bd)


PREFIX_BLOCK = 256
GATHER_ROWS = 1024


def _prefix_count(mask_f, tri):
    seq = mask_f.shape[0]
    outs = []
    carry = jnp.zeros((1, LANES), F32)
    for blk in range(seq // PREFIX_BLOCK):
        m = mask_f[blk * PREFIX_BLOCK:(blk + 1) * PREFIX_BLOCK]
        outs.append(_dot(tri, m.astype(BF16)) + carry)
        carry = carry + jnp.sum(m, axis=0, keepdims=True)
    return jnp.concatenate(outs, axis=0) if len(outs) > 1 else outs[0]


def _route_kernel(lg_ref, h2_ref, tri_ref, xs_ref, slot_ref, aff_ref, *, cap):
    seq = lg_ref.shape[0]
    lane = lax.broadcasted_iota(jnp.int32, (seq, LANES), 1)
    real = lane < N_EXPERTS
    lg = jnp.where(real, lg_ref[...], -jnp.inf)
    m = jnp.max(lg, axis=-1, keepdims=True)
    e = jnp.exp(lg - m)
    aff = e / jnp.sum(e, axis=-1, keepdims=True)
    aff_ref[...] = aff

    a_min = jnp.min(aff, axis=0, keepdims=True)
    a_max = jnp.max(aff, axis=0, keepdims=True)

    def search(carry):
        lo, hi, _ = carry
        mid = 0.5 * (jnp.maximum(lo, a_min) + jnp.minimum(hi, a_max))
        inside = jnp.where(aff > lo, jnp.where(aff < hi, 1.0, 0.0), 0.0)
        upper = inside * jnp.where(aff >= mid, 1.0, 0.0)
        up = jnp.min(jnp.where(upper > 0.0, aff, BRACKET_HI), axis=0, keepdims=True)
        dn = jnp.max(jnp.where(inside - upper > 0.0, aff, BRACKET_LO), axis=0, keepdims=True)
        pivot = jnp.where(up < BRACKET_HI, up, dn)
        found = jnp.where(up < BRACKET_HI, 1.0, jnp.where(dn > BRACKET_LO, 1.0, 0.0))
        cnt = jnp.sum(jnp.where(aff >= pivot, 1.0, 0.0), axis=0, keepdims=True)
        take_lo = found * jnp.where(cnt >= cap, 1.0, 0.0)
        take_hi = found - take_lo
        return (jnp.where(take_lo > 0.0, pivot, lo), jnp.where(take_hi > 0.0, pivot, hi), found)

    init = (jnp.full((1, LANES), BRACKET_LO, F32), jnp.full((1, LANES), BRACKET_HI, F32), jnp.ones((1, LANES), F32))
    thr, _, _ = lax.while_loop(lambda c: jnp.max(c[2]) > 0.0, search, init)
    gt = jnp.where(aff > thr, 1.0, 0.0)
    eq = jnp.where(aff == thr, 1.0, 0.0)
    need = cap - jnp.sum(gt, axis=0, keepdims=True)
    tri = tri_ref[...]
    sel = jnp.where(real, gt + eq * jnp.where(_prefix_count(eq, tri) < need, 1.0, 0.0), 0.0)
    slot = jnp.where(sel > 0.0, _prefix_count(sel, tri), -1.0)
    slot_ref[...] = slot

    slot_t = slot.T
    h2 = h2_ref[...]
    c_iota = lax.broadcasted_iota(jnp.int32, (cap, seq), 0).astype(F32)
    group = max(1, min(N_EXPERTS, GATHER_ROWS // cap))
    for g0 in range(0, N_EXPERTS, group):
        onehot = jnp.concatenate(
            [jnp.where(c_iota == slot_t[ex:ex + 1, :], 1.0, 0.0) for ex in range(g0, g0 + group)], axis=0)
        rows = _dot(onehot.astype(BF16), h2)
        for k in range(group):
            xs_ref[g0 + k] = rows[k * cap:(k + 1) * cap].astype(BF16)


def _route(logits, h2, tri, n_b, seq):
    cap = EC_FACTOR * seq // N_EXPERTS
    tok = lambda w: pl.BlockSpec((seq, w), lambda b: (b, 0))
    return pl.pallas_call(
        functools.partial(_route_kernel, cap=cap),
        grid=(n_b,),
        in_specs=[tok(LANES), tok(D_MODEL), pl.BlockSpec((PREFIX_BLOCK, PREFIX_BLOCK), lambda b: (0, 0))],
        out_specs=[pl.BlockSpec((N_EXPERTS, cap, D_MODEL), lambda b: (0, b, 0)), tok(LANES), tok(LANES)],
        out_shape=[jax.ShapeDtypeStruct((N_EXPERTS, n_b * cap, D_MODEL), BF16),
                   jax.ShapeDtypeStruct((n_b * seq, LANES), F32),
                   jax.ShapeDtypeStruct((n_b * seq, LANES), F32)],
        compiler_params=_params("arbitrary"),
        name="route",
    )(logits, h2, tri)


EXPERT_RB = 256


def _expert_kernel(xa_ref, xb_ref, wg_ref, wu_ref, wd_ref, ya_ref, yb_ref, wg_s, wu_s, wd_s):
    wg_s[...] = wg_ref[...].astype(BF16)
    wu_s[...] = wu_ref[...].astype(BF16)
    wd_s[...] = wd_ref[...].astype(BF16)
    for x_ref, y_ref in ((xa_ref, ya_ref), (xb_ref, yb_ref)):
        for rb in range(x_ref.shape[0] // EXPERT_RB):
            rows = slice(rb * EXPERT_RB, (rb + 1) * EXPERT_RB)
            x = x_ref[rows, :]
            a = _dot(x, wg_s[...])
            u = _dot(x, wu_s[...])
            hid = (a * _sigmoid(a) * u).astype(BF16)
            y_ref[rows, :] = _dot(hid, wd_s[...]).astype(BF16)


def _experts(xs_a, xs_b, w_gate, w_up, w_down, layer):
    ma, mb = xs_a.shape[1], xs_b.shape[1]
    xspec = lambda m: pl.BlockSpec((None, m, D_MODEL), lambda e: (e, 0, 0))
    return pl.pallas_call(
        _expert_kernel,
        grid=(N_EXPERTS,),
        in_specs=[
            xspec(ma), xspec(mb),
            pl.BlockSpec((None, None, D_MODEL, EXPERT_FF), lambda e: (layer, e, 0, 0)),
            pl.BlockSpec((None, None, D_MODEL, EXPERT_FF), lambda e: (layer, e, 0, 0)),
            pl.BlockSpec((None, None, EXPERT_FF, D_MODEL), lambda e: (layer, e, 0, 0)),
        ],
        out_specs=[xspec(ma), xspec(mb)],
        out_shape=[jax.ShapeDtypeStruct(xs_a.shape, BF16), jax.ShapeDtypeStruct(xs_b.shape, BF16)],
        scratch_shapes=[pltpu.VMEM((D_MODEL, EXPERT_FF), BF16), pltpu.VMEM((D_MODEL, EXPERT_FF), BF16),
                        pltpu.VMEM((EXPERT_FF, D_MODEL), BF16)],
        compiler_params=_params("arbitrary"),
        name="experts",
    )(xs_a, xs_b, w_gate, w_up, w_down)


COMBINE_FUSED_COLS = 512


def _combine_kernel(*refs, cap, fused):
    if fused:
        y_ref, slot_ref, aff_ref, x1_ref, mod_ref, ex_ref, o_ref = refs
    else:
        y_ref, slot_ref, aff_ref, x1_ref, mod_ref, o_ref = refs
    seq = x1_ref.shape[0]
    slot = slot_ref[...]
    aff = aff_ref[...]
    if fused:
        spread = ex_ref[...]
        a1 = aff.astype(BF16)
        r1 = aff - a1.astype(F32)
        a2 = r1.astype(BF16)
        a3 = (r1 - a2.astype(F32)).astype(BF16)
        slot_x = _dot(slot.astype(BF16), spread)
        aff_x = _dot(a1, spread) + _dot(a2, spread) + _dot(a3, spread)
        cols = N_EXPERTS * cap
        c_pat = (lax.broadcasted_iota(jnp.int32, (seq, cols), 1) % cap).astype(F32)
        w_hi, w_lo = _split(jnp.where(slot_x == c_pat, aff_x, 0.0))
        y2 = y_ref[...].reshape(cols, D_MODEL)
        acc = _dot(w_hi, y2) + _dot(w_lo, y2)
    else:
        c_iota = lax.broadcasted_iota(jnp.int32, (seq, cap), 1).astype(F32)
        acc = jnp.zeros((seq, D_MODEL), F32)
        for ex in range(N_EXPERTS):
            onehot = jnp.where(slot[:, ex:ex + 1] == c_iota, 1.0, 0.0).astype(BF16)
            acc = acc + aff[:, ex:ex + 1] * _dot(onehot, y_ref[ex])
    o_ref[...] = x1_ref[...] + mod_ref[5:6, :] * acc


def _combine(y, slot, aff, x1, mod_l, n_b, seq, cond_row0):
    cap = EC_FACTOR * seq // N_EXPERTS
    fused = N_EXPERTS * cap <= COMBINE_FUSED_COLS
    if cond_row0 == 0:
        row = lambda b: 0
    else:
        row = lambda b: cond_row0 + b
    tm = min(seq, 512)
    n_t = seq // tm
    tok = lambda w: pl.BlockSpec((tm, w), lambda b, i: (b * n_t + i, 0))
    in_specs = [
        pl.BlockSpec((N_EXPERTS, cap, D_MODEL), lambda b, i: (0, b, 0)),
        tok(LANES), tok(LANES), tok(D_MODEL),
        pl.BlockSpec((None, N_MOD, D_MODEL), lambda b, i: (row(b), 0, 0)),
    ]
    args = [y, slot, aff, x1, mod_l]
    if fused:
        cols = N_EXPERTS * cap
        spread = (jnp.arange(LANES)[:, None] == (jnp.arange(cols) // cap)[None, :]).astype(BF16)
        in_specs.append(pl.BlockSpec((LANES, cols), lambda b, i: (0, 0)))
        args.append(spread)
    return pl.pallas_call(
        functools.partial(_combine_kernel, cap=cap, fused=fused),
        grid=(n_b, n_t),
        in_specs=in_specs,
        out_specs=tok(D_MODEL),
        out_shape=jax.ShapeDtypeStruct((n_b * seq, D_MODEL), F32),
        compiler_params=_params("arbitrary", "arbitrary"),
        name="combine_fused" if fused else "combine",
    )(*args)


def _rope_tables(seq):
    t = jnp.arange(seq, dtype=jnp.int32)
    row = (t // GRID_W).astype(F32)
    col = (t % GRID_W).astype(F32)
    half = HEAD_DIM // 2
    inv = ROPE_THETA ** (-jnp.arange(0, half, 2, dtype=F32) / half)
    lane = jnp.arange(LANES)
    u = lane % HEAD_DIM
    pos = jnp.where((u // half)[None, :] == 0, row[:, None], col[:, None])
    ang = pos * inv[(u % half) % (half // 2)][None, :]
    first = ((u % half) < half // 2)[None, :]
    sin = jnp.sin(ang)
    return jnp.cos(ang), jnp.where(first, -sin, 0.0), jnp.where(first, 0.0, sin)


def _block_diag(w):
    n, k, _ = w.shape
    eye = jnp.eye(n, dtype=w.dtype)
    return (eye[:, None, :, None] * w[:, :, None, :]).reshape(n * k, n * k)


def _split_weight(w):
    return jnp.stack(_split(w))


def _pad_rows(w, offset, total=LANES):
    return jnp.zeros((total, w.shape[1]), w.dtype).at[offset:offset + w.shape[0]].set(w)


def _pack_state(s):
    b = s.shape[0]
    st = s.reshape(b, 2, RWKV_PAIRS, 2, RWKV_HEAD, RWKV_HEAD).transpose(0, 1, 2, 3, 5, 4)
    eye = jnp.eye(2, dtype=s.dtype)
    out = st[:, :, :, :, :, None, :] * eye[None, None, None, :, None, :, None]
    return out.reshape(b, 2, RWKV_PAIRS, LANES, LANES)


def _unpack_state(s):
    return s.reshape(s.shape[0], 2, RWKV_HEADS, RWKV_HEAD, RWKV_HEAD)


def kernel(x_prompt, x_sample, cache_k, cache_v, state_rwkv, state_lru, c, c_ctx, w_ada, b_ada, norm1, norm2, w_in, w_out, q_norm, k_norm, rwkv_mu, rwkv_w0, rwkv_w_up, rwkv_a0, rwkv_a_up, rwkv_g_up, rwkv_k_k, rwkv_k_a, rwkv_r_k, rwkv_ln_w, rwkv_ln_b, lru_conv_w, lru_conv_b, lru_wa, lru_ba, lru_wx, lru_bx, lru_lambda, router, exp_w_gate, exp_w_up, exp_w_down):
    n_ctx, seq_ctx, _ = x_prompt.shape
    n_lat, seq_lat, _ = x_sample.shape
    past = cache_k.shape[2]
    assert n_lat + 1 <= COND_ROWS

    cond = jnp.zeros((COND_ROWS, D_MODEL), F32).at[0].set(c_ctx).at[1:1 + n_lat].set(c)
    mod = _ada(cond, w_ada, b_ada.reshape(DEPTH, 1, N_MOD * D_MODEL))
    mod = mod.reshape(DEPTH, COND_ROWS, N_MOD, D_MODEL)

    w_in_bf = w_in.astype(BF16)
    w_out_bf = w_out.astype(BF16)
    lane = jnp.arange(LANES)
    bd = (lane[:, None] // HEAD_DIM == lane[None, :] // HEAD_DIM).astype(BF16)
    pidx = jnp.arange(PREFIX_BLOCK)
    tri = (pidx[None, :] < pidx[:, None]).astype(BF16)
    rope_tabs = _rope_tables(seq_lat)

    paths = [
        dict(x=x_prompt.reshape(n_ctx * seq_ctx, D_MODEL), n_b=n_ctx, seq=seq_ctx, row0=0, rope=None),
        dict(x=x_sample.reshape(n_lat * seq_lat, D_MODEL), n_b=n_lat, seq=seq_lat, row0=1, rope=rope_tabs),
    ]
    new_k, new_v, new_sr, new_sl = [], [], [], []
    for l in range(DEPTH):
        mod_l = mod[l]
        qn = jnp.tile(q_norm[l], LANES // HEAD_DIM)[None, :]
        kn = jnp.tile(k_norm[l], LANES // HEAD_DIM)[None, :]
        prep_w = (
            rwkv_mu[l][None, :], rwkv_k_k[l][None, :], rwkv_k_a[l][None, :],
            rwkv_r_k[l].reshape(1, RWKV_WIDTH), rwkv_w0[l], rwkv_a0[l],
            jnp.stack([_split_weight(_pad_rows(rwkv_w_up[l, d], 0)) for d in range(2)]),
            jnp.stack([_split_weight(_pad_rows(rwkv_a_up[l, d], RWKV_DECAY_LORA)) for d in range(2)]),
            _split_weight(_pad_rows(rwkv_g_up[l], RWKV_DECAY_LORA + RWKV_AAA_LORA)),
        )
        lru_w = (
            lru_conv_w[l], lru_conv_b[l][None, :],
            jnp.stack([_block_diag(lru_wa[l, d]) for d in range(2)]).astype(BF16), lru_ba[l],
            jnp.stack([_block_diag(lru_wx[l, d]) for d in range(2)]).astype(BF16), lru_bx[l],
            lru_lambda[l],
        )
        router_pad = _split_weight(jnp.zeros((D_MODEL, LANES), F32).at[:, :N_EXPERTS].set(router[l]))
        mids = []
        for pi, pth in enumerate(paths):
            n_b, seq, row0 = pth["n_b"], pth["seq"], pth["row0"]
            latent = pth["rope"] is not None
            outs = _inproj(pth["x"], mod_l, norm1[l][None, :], w_in_bf, l, qn, kn, bd, pth["rope"], seq, row0)
            if latent:
                q, k_n, k_att, v, p_rwkv, lru_x, lru_g = outs
                cache = (cache_k[:, l].reshape(n_b, past, KV_WIDTH), cache_v[:, l].reshape(n_b, past, KV_WIDTH))
                s0 = _pack_state(state_rwkv[:, l])
                h0 = state_lru[:, l]
            else:
                q, k_n, v, p_rwkv, lru_x, lru_g = outs
                k_att, cache = k_n, None
                s0 = jnp.zeros((n_b, 2, RWKV_PAIRS, LANES, LANES), F32)
                h0 = jnp.zeros((n_b, 2, LRU_WIDTH), F32)
                new_k.append(k_n.reshape(n_b, seq, ATT_KV_HEADS, HEAD_DIM))
                new_v.append(v.reshape(n_b, seq, ATT_KV_HEADS, HEAD_DIM))
            att = _attention(q, k_att, v, n_b, seq, cache)
            r, vv, na, w, kd, b, g, bonus = _rwkv_prep(p_rwkv.reshape(n_b, seq, RWKV_COLS), n_b, seq, prep_w, bd)
            W = RWKV_WIDTH
            yf, yb, s_fin = _rwkv_scan(r.reshape(n_b, seq, W), vv.reshape(n_b, seq, W), na.reshape(n_b, seq, W),
                                       w.reshape(2, n_b, seq, W), kd.reshape(2, n_b, seq, W),
                                       b.reshape(2, n_b, seq, W), s0, n_b, seq)
            lru_out, h_fin = _lru(lru_x, lru_g, h0, lru_w, n_b, seq)
            if not latent:
                new_sr.append(_unpack_state(s_fin))
                new_sl.append(h_fin)
            x1, h2, logits = _outproj(pth["x"], att, yf.reshape(n_b * seq, W), yb.reshape(n_b * seq, W), bonus, g,
                                      lru_out, mod_l, norm2[l][None, :], rwkv_ln_w[l][None, :],
                                      rwkv_ln_b[l][None, :], w_out_bf, l, router_pad, bd, seq, row0)
            xs, slot, aff = _route(logits, h2, tri, n_b, seq)
            mids.append((xs, slot, aff, x1))
        y_a, y_b = _experts(mids[0][0], mids[1][0], exp_w_gate, exp_w_up, exp_w_down, l)
        for pth, (xs, slot, aff, x1), y in zip(paths, mids, (y_a, y_b)):
            pth["x"] = _combine(y, slot, aff, x1, mod_l, pth["n_b"], pth["seq"], pth["row0"])

    y_prompt = paths[0]["x"].reshape(n_ctx, seq_ctx, D_MODEL)
    y_sample = paths[1]["x"].reshape(n_lat, seq_lat, D_MODEL)
    return (y_prompt, y_sample, jnp.stack(new_k, axis=1), jnp.stack(new_v, axis=1),
            jnp.stack(new_sr, axis=1), jnp.stack(new_sl, axis=1))
```

```python
import functools

import jax
import jax.numpy as jnp
from jax import lax
from jax.experimental import pallas as pl
from jax.experimental.pallas import tpu as pltpu

F32 = jnp.float32
BF16 = jnp.bfloat16
HIGHEST = lax.Precision.HIGHEST

D_MODEL = 1024
DEPTH = 2
GRID_W = 64
ATT_HEADS = 8
ATT_KV_HEADS = 2
HEAD_DIM = 64
ATT_WIDTH = ATT_HEADS * HEAD_DIM
KV_WIDTH = ATT_KV_HEADS * HEAD_DIM
ROPE_THETA = 10000.0
RWKV_HEADS = 4
RWKV_HEAD = 64
RWKV_WIDTH = RWKV_HEADS * RWKV_HEAD
RWKV_DECAY_LORA = 32
RWKV_AAA_LORA = 32
RWKV_GATE_LORA = 64
RWKV_COLS = 3 * RWKV_WIDTH + RWKV_DECAY_LORA + RWKV_AAA_LORA + RWKV_GATE_LORA
GN_EPS = 64e-5
LRU_BLOCKS = 4
LRU_BLOCK = 64
LRU_WIDTH = LRU_BLOCKS * LRU_BLOCK
LRU_CONV_W = 4
LRU_C = 8.0
IN_COLS = ATT_WIDTH + 2 * KV_WIDTH + RWKV_COLS + 2 * LRU_WIDTH
N_EXPERTS = 16
EC_FACTOR = 2
EXPERT_FF = 1024
RMS_EPS = 1e-6

LANES = 128
SUBLANES = 8
VMEM_LIMIT = 56 * 1024 * 1024
N_MOD = 6
COND_ROWS = 8
BRACKET_LO, BRACKET_HI = -1.0, 2.0


def _params(*sem):
    return pltpu.CompilerParams(dimension_semantics=sem, vmem_limit_bytes=VMEM_LIMIT)


def _dot(a, b, precision=None):
    return jnp.dot(a, b, preferred_element_type=F32, precision=precision)


def _sigmoid(x):
    return 1.0 / (1.0 + jnp.exp(-x))


def _softplus(x):
    return jnp.maximum(x, 0.0) + jnp.log1p(jnp.exp(-jnp.abs(x)))


def _split(x):
    hi = x.astype(BF16)
    return hi, (x - hi.astype(F32)).astype(BF16)


def _dot_split(x, w_ref):
    hi, lo = _split(x)
    return _dot(hi, w_ref[0]) + _dot(hi, w_ref[1]) + _dot(lo, w_ref[0])


def _seg_sum(x, ones_bd):
    hi = x.astype(BF16)
    lo = (x - hi.astype(F32)).astype(BF16)
    return _dot(hi, ones_bd) + _dot(lo, ones_bd)


def _ada_kernel(c_ref, w_ref, b_ref, o_ref):
    c = c_ref[...]
    s = c * _sigmoid(c)
    o_ref[...] = _dot(s, w_ref[...], HIGHEST) + b_ref[...]


def _ada(cond, w_ada, b_ada):
    n_l = w_ada.shape[0]
    tn = 1536
    n_t = N_MOD * D_MODEL // tn
    return pl.pallas_call(
        _ada_kernel,
        grid=(n_l, n_t),
        in_specs=[
            pl.BlockSpec((COND_ROWS, D_MODEL), lambda l, j: (0, 0)),
            pl.BlockSpec((None, D_MODEL, tn), lambda l, j: (l, 0, j)),
            pl.BlockSpec((None, 1, tn), lambda l, j: (l, 0, j)),
        ],
        out_specs=pl.BlockSpec((None, COND_ROWS, tn), lambda l, j: (l, 0, j)),
        out_shape=jax.ShapeDtypeStruct((n_l, COND_ROWS, N_MOD * D_MODEL), F32),
        compiler_params=_params("arbitrary", "arbitrary"),
        name="ada",
    )(cond, w_ada, b_ada)


def _head_rms(x, gain, ones_bd):
    ms = _seg_sum(x * x, ones_bd) * (1.0 / HEAD_DIM)
    return x * lax.rsqrt(ms + RMS_EPS) * gain


def _rope(x, cos, sin_up, sin_dn):
    return x * cos + pltpu.roll(x, LANES - 16, 1) * sin_up + pltpu.roll(x, 16, 1) * sin_dn


def _inproj_kernel(*refs, rope):
    if rope:
        (x_ref, mod_ref, n1_ref, w_ref, qn_ref, kn_ref, bd_ref, cos_ref, su_ref, sd_ref,
         q_ref, ko_ref, ka_ref, v_ref, pr_ref, lx_ref, lg_ref) = refs
    else:
        (x_ref, mod_ref, n1_ref, w_ref, qn_ref, kn_ref, bd_ref,
         q_ref, ko_ref, v_ref, pr_ref, lx_ref, lg_ref) = refs
    x = x_ref[...]
    rs = lax.rsqrt(jnp.mean(x * x, axis=-1, keepdims=True) + RMS_EPS)
    h = (x * rs * n1_ref[...]) * (1.0 + mod_ref[1:2, :]) + mod_ref[0:1, :]
    p = _dot(h.astype(BF16), w_ref[...])
    bd = bd_ref[...]
    for c in range(ATT_WIDTH // LANES):
        qc = _head_rms(p[:, c * LANES:(c + 1) * LANES], qn_ref[...], bd)
        if rope:
            qc = _rope(qc, cos_ref[...], su_ref[...], sd_ref[...])
        q_ref[:, c * LANES:(c + 1) * LANES] = qc
    o = ATT_WIDTH
    kc = _head_rms(p[:, o:o + KV_WIDTH], kn_ref[...], bd)
    ko_ref[...] = kc
    if rope:
        ka_ref[...] = _rope(kc, cos_ref[...], su_ref[...], sd_ref[...])
    o += KV_WIDTH
    v_ref[...] = p[:, o:o + KV_WIDTH]
    o += KV_WIDTH
    pr_ref[...] = p[:, o:o + RWKV_COLS]
    o += RWKV_COLS
    lx_ref[...] = p[:, o:o + LRU_WIDTH]
    o += LRU_WIDTH
    lg_ref[...] = p[:, o:o + LRU_WIDTH]


def _inproj(x, mod_l, norm1, w_in_bf, layer, qn, kn, bd, rope_tabs, seq, cond_row0):
    n_tok = x.shape[0]
    tm = 512 if seq >= 512 else seq * (512 // seq)
    per_seq = max(seq // tm, 1)
    n_t = n_tok // tm
    rope = rope_tabs is not None
    if cond_row0 == 0:
        row = lambda i: 0
    else:
        row = lambda i: cond_row0 + i // per_seq
    tok = lambda w: pl.BlockSpec((tm, w), lambda i: (i, 0))
    const = lambda shape: pl.BlockSpec(shape, lambda i: tuple(0 for _ in shape))
    in_specs = [
        tok(D_MODEL),
        pl.BlockSpec((None, N_MOD, D_MODEL), lambda i: (row(i), 0, 0)),
        const((1, D_MODEL)),
        pl.BlockSpec((None, D_MODEL, IN_COLS), lambda i: (layer, 0, 0)),
        const((1, LANES)), const((1, LANES)), const((LANES, LANES)),
    ]
    args = [x, mod_l, norm1, w_in_bf, qn, kn, bd]
    outs = [ATT_WIDTH, KV_WIDTH]
    if rope:
        in_specs += [pl.BlockSpec((tm, LANES), lambda i: (i % per_seq, 0))] * 3
        args += list(rope_tabs)
        outs.append(KV_WIDTH)
    outs += [KV_WIDTH, RWKV_COLS, LRU_WIDTH, LRU_WIDTH]
    return pl.pallas_call(
        functools.partial(_inproj_kernel, rope=rope),
        grid=(n_t,),
        in_specs=in_specs,
        out_specs=[tok(w) for w in outs],
        out_shape=[jax.ShapeDtypeStruct((n_tok, w), F32) for w in outs],
        compiler_params=_params("arbitrary"),
        name="inproj_rope" if rope else "inproj",
    )(*args)


def _attn_kernel(*refs, has_cache):
    if has_cache:
        q_ref, k_ref, v_ref, ck_ref, cv_ref, o_ref, kn_s, ks_s, vn_s, vs_s = refs
    else:
        q_ref, k_ref, v_ref, o_ref, kn_s, ks_s, vn_s, vs_s = refs

    @pl.when(pl.program_id(1) == 0)
    def _():
        k = k_ref[...]
        v = v_ref[...]
        if has_cache:
            k = jnp.concatenate([ck_ref[...], k], axis=0)
            v = jnp.concatenate([cv_ref[...], v], axis=0)
        kn_s[...] = k.astype(BF16)
        ks_s[...] = pltpu.roll(k, HEAD_DIM, 1).astype(BF16)
        vn_s[...] = v.astype(BF16)
        vs_s[...] = pltpu.roll(v, HEAD_DIM, 1).astype(BF16)

    tq = q_ref.shape[0]
    lo = lax.broadcasted_iota(jnp.int32, (tq, LANES), 1) < HEAD_DIM
    rep = ATT_HEADS // ATT_KV_HEADS
    for c in range(ATT_WIDTH // LANES):
        qc = q_ref[:, c * LANES:(c + 1) * LANES] * (HEAD_DIM ** -0.5)
        halves = []
        for half in range(2):
            g = (2 * c + half) // rep
            qm = jnp.where(lo if half == 0 else jnp.logical_not(lo), qc, 0.0).astype(BF16)
            k_s, v_s = (kn_s, vn_s) if half == g else (ks_s, vs_s)
            s = lax.dot_general(qm, k_s[...], (((1,), (1,)), ((), ())), preferred_element_type=F32)
            m = jnp.max(s, axis=-1, keepdims=True)
            e = jnp.exp(s - m)
            l = jnp.sum(e, axis=-1, keepdims=True)
            halves.append(_dot(e.astype(BF16), v_s[...]) / l)
        o_ref[:, c * LANES:(c + 1) * LANES] = jnp.where(lo, halves[0], halves[1])


def _attention(q, k, v, n_b, seq, cache=None):
    tq = 256
    n_q = seq // tq
    has_cache = cache is not None
    past = cache[0].shape[1] if has_cache else 0
    in_specs = [
        pl.BlockSpec((tq, ATT_WIDTH), lambda b, i: (b * n_q + i, 0)),
        pl.BlockSpec((seq, KV_WIDTH), lambda b, i: (b, 0)),
        pl.BlockSpec((seq, KV_WIDTH), lambda b, i: (b, 0)),
    ]
    args = [q, k, v]
    if has_cache:
        in_specs += [pl.BlockSpec((None, past, KV_WIDTH), lambda b, i: (b, 0, 0))] * 2
        args += list(cache)
    return pl.pallas_call(
        functools.partial(_attn_kernel, has_cache=has_cache),
        grid=(n_b, n_q),
        in_specs=in_specs,
        out_specs=pl.BlockSpec((tq, ATT_WIDTH), lambda b, i: (b * n_q + i, 0)),
        out_shape=jax.ShapeDtypeStruct((n_b * seq, ATT_WIDTH), F32),
        scratch_shapes=[pltpu.VMEM((past + seq, KV_WIDTH), BF16)] * 4,
        compiler_params=_params("arbitrary", "arbitrary"),
        name="attn_cache" if has_cache else "attn",
    )(*args)


def _rwkv_prep_kernel(p_ref, mu_ref, kk_ref, ka_ref, rk_ref, w0_ref, a0_ref, wup_ref, aup_ref,
                      gup_ref, bd_ref,
                      r_ref, v_ref, na_ref, w_ref, k_ref, b_ref, g_ref, bonus_ref):
    tm = r_ref.shape[0]
    seq = p_ref.shape[0]
    i = pl.program_id(1)
    n_t = pl.num_programs(1)
    start = pl.multiple_of(i * tm, tm)
    cur = p_ref[pl.ds(start, tm), :]
    prev_base = pl.multiple_of(jnp.maximum(start - SUBLANES, 0), SUBLANES)
    next_base = pl.multiple_of(jnp.minimum(start + tm, seq - SUBLANES), SUBLANES)
    prev_row = p_ref[pl.ds(prev_base, SUBLANES), :][SUBLANES - 1:SUBLANES]
    next_row = p_ref[pl.ds(next_base, SUBLANES), :][0:1]
    prev_row = jnp.where(i > 0, prev_row, 0.0)
    next_row = jnp.where(i < n_t - 1, next_row, 0.0)
    row = lax.broadcasted_iota(jnp.int32, cur.shape, 0)
    prev = jnp.where(row == 0, prev_row, pltpu.roll(cur, 1, 0))
    nxt = jnp.where(row == tm - 1, next_row, pltpu.roll(cur, tm - 1, 0))
    ps = cur + mu_ref[...] * (0.5 * (prev + nxt) - cur)

    W = RWKV_WIDTH
    r = ps[:, :W]
    k = ps[:, W:2 * W]
    v = ps[:, 2 * W:3 * W]
    lora = ps[:, 3 * W:]
    r_ref[...] = r
    v_ref[...] = v
    g_ref[...] = _dot_split(_sigmoid(lora), gup_ref)
    lora_t = jnp.tanh(lora)
    kk = k * kk_ref[...]
    kk_parts = []
    for c in range(W // LANES):
        kc = kk[:, c * LANES:(c + 1) * LANES]
        nrm = jnp.sqrt(_seg_sum(kc * kc, bd_ref[...]))
        kk_parts.append(kc / jnp.maximum(nrm, 1e-12))
    kk = jnp.concatenate(kk_parts, axis=-1)
    na_ref[...] = -kk
    bonus = None
    for d in range(2):
        w_log = -_softplus(-(w0_ref[d:d + 1, :] + _dot_split(lora_t, wup_ref.at[d]))) - 0.5
        w_ref[d] = -jnp.exp(w_log)
        a_rate = _sigmoid(a0_ref[d:d + 1, :] + _dot_split(lora, aup_ref.at[d]))
        kd = k * (1.0 + (a_rate - 1.0) * ka_ref[...])
        k_ref[d] = kd
        b_ref[d] = kk * a_rate
        rkr = r * kd * rk_ref[...]
        parts = [_seg_sum(rkr[:, c * LANES:(c + 1) * LANES], bd_ref[...]) for c in range(W // LANES)]
        bd_term = jnp.concatenate(parts, axis=-1) * v
        bonus = bd_term if bonus is None else bonus + bd_term
    bonus_ref[...] = bonus


def _rwkv_prep(p, n_b, seq, wts, bd):
    mu, k_k, k_a, r_k, w0, a0, wup, aup, gup = wts
    tm = min(seq, 512)
    n_t = seq // tm
    W = RWKV_WIDTH
    const = lambda shape: pl.BlockSpec(shape, lambda b, i: tuple(0 for _ in shape))
    tok = pl.BlockSpec((tm, W), lambda b, i: (b * n_t + i, 0))
    tok2 = pl.BlockSpec((2, tm, W), lambda b, i: (0, b * n_t + i, 0))
    one = jax.ShapeDtypeStruct((n_b * seq, W), F32)
    two = jax.ShapeDtypeStruct((2, n_b * seq, W), F32)
    return pl.pallas_call(
        _rwkv_prep_kernel,
        grid=(n_b, n_t),
        in_specs=[
            pl.BlockSpec((None, seq, RWKV_COLS), lambda b, i: (b, 0, 0)),
            const((1, RWKV_COLS)), const((1, W)), const((1, W)), const((1, W)),
            const((2, W)), const((2, W)),
            const((2, 2, LANES, W)), const((2, 2, LANES, W)), const((2, LANES, W)),
            const((LANES, LANES)),
        ],
        out_specs=[tok, tok, tok, tok2, tok2, tok2, tok, tok],
        out_shape=[one, one, one, two, two, two, one, one],
        compiler_params=_params("arbitrary", "arbitrary"),
        name="rwkv_prep",
    )(p, mu, k_k, k_a, r_k, w0, a0, wup, aup, gup, bd)


RWKV_BB = 4
RWKV_PAIRS = RWKV_HEADS * RWKV_HEAD // LANES
RWKV_CHUNK = 256
RWKV_SUB = 64
RWKV_STAGE_WIDTH = 32
INV_BASE_SHIFT = 3


def _rwkv_scan_kernel(rf_ref, vf_ref, af_ref, wf_ref, kf_ref, bf_ref,
                      rb_ref, vb_ref, ab_ref, wb_ref, kb_ref, bb_ref, s0_ref,
                      yf_ref, yb_ref, sfin_ref, st_s):
    j = pl.program_id(1)
    n_req, tc = rf_ref.shape[0], rf_ref.shape[1]
    C = RWKV_SUB

    @pl.when(j == 0)
    def _():
        st_s[...] = s0_ref[...]

    lane_c = lax.broadcasted_iota(jnp.int32, (C, LANES), 1)
    row_c = lax.broadcasted_iota(jnp.int32, (C, LANES), 0)
    lo = lane_c < C
    s_idx = lane_c % C
    eye2 = jnp.where(s_idx == row_c, 1.0, 0.0)
    blk_masks = [jnp.where((s_idx >> sh) == (row_c >> sh), 1.0, 0.0) for sh in range(INV_BASE_SHIFT, 7)]
    lane_f = lax.broadcasted_iota(jnp.int32, (LANES, LANES), 1)
    row_f = lax.broadcasted_iota(jnp.int32, (LANES, LANES), 0)
    eye_f = lane_f == row_f
    lo_f = lane_f < C
    bd_mask = (lane_f < C) == (row_f < C)
    tt = lax.broadcasted_iota(jnp.int32, (C, C), 0)
    ss = lax.broadcasted_iota(jnp.int32, (C, C), 1)

    def bd(m):
        return jnp.concatenate([jnp.where(lo, m, 0.0), jnp.where(lo, 0.0, m)], axis=0)

    def bd_swap(m):
        return jnp.concatenate([jnp.where(lo, 0.0, m), jnp.where(lo, m, 0.0)], axis=0)

    def bdot(x, y):
        return _dot(x.astype(BF16), y.astype(BF16))

    dirs = ((rf_ref, vf_ref, af_ref, wf_ref, kf_ref, bf_ref, yf_ref),
            (rb_ref, vb_ref, ab_ref, wb_ref, kb_ref, bb_ref, yb_ref))

    n_sub = max(1, min(tc // C, RWKV_STAGE_WIDTH // (2 * n_req * RWKV_PAIRS)))

    def sub_chunk(i, carry):
        chains = []
        for d in range(2):
            before = (ss < tt) if d == 0 else (ss > tt)
            cum = jnp.where(jnp.logical_or(before, ss == tt), 1.0, 0.0).astype(BF16)
            strict = jnp.where((s_idx < row_c) if d == 0 else (s_idx > row_c), 1.0, 0.0)
            incl = jnp.where((s_idx <= row_c) if d == 0 else (s_idx >= row_c), 1.0, 0.0)
            tri_mask = jnp.concatenate([strict, incl], axis=0)
            last = C - 1 if d == 0 else 0
            for k in range(n_sub):
                ii = i * n_sub + k
                base = pl.multiple_of(ii * C if d == 0 else tc - C - ii * C, C)
                for bi in range(n_req):
                    for pr in range(RWKV_PAIRS):
                        chains.append(dict(d=d, k=k, bi=bi, pr=pr, base=base, refs=dirs[d], cum=cum, tri=tri_mask,
                                           last=last, ls=slice(pr * LANES, (pr + 1) * LANES)))

        for c in chains:
            r_r, v_r, a_r, w_r, k_r, b_r, _ = c["refs"]
            r, v, a, lw, k, b = (ref[c["bi"], pl.ds(c["base"], C), c["ls"]] for ref in (r_r, v_r, a_r, w_r, k_r, b_r))
            l1 = lw.astype(BF16)
            e1 = lw - l1.astype(F32)
            l2 = e1.astype(BF16)
            l3 = (e1 - l2.astype(F32)).astype(BF16)
            c["G"] = _dot(c["cum"], l1) + _dot(c["cum"], l2) + _dot(c["cum"], l3)
            c["in"] = (r, v, a, lw, k, b)
        for c in chains:
            r, v, a, lw, k, b = c.pop("in")
            G = c.pop("G")
            g_inv = jnp.exp(-G)
            at = a * jnp.exp(G - lw)
            rt = r * jnp.exp(G)
            bt = b * g_inv
            kt = k * g_inv
            g_last = jnp.exp(G[c["last"]:c["last"] + 1, :])
            X = jnp.concatenate([at, rt], axis=0)
            c["X"] = X.astype(BF16)
            bt_b, kt_b = bt.astype(BF16), kt.astype(BF16)
            nt = (((1,), (1,)), ((), ()))
            c["P0"] = lax.dot_general(jnp.where(lo_f, X, 0.0).astype(BF16), jnp.concatenate([bt_b, kt_b], axis=0),
                                      nt, preferred_element_type=F32)
            c["P1"] = lax.dot_general(jnp.where(lo_f, 0.0, X).astype(BF16), jnp.concatenate([kt_b, bt_b], axis=0),
                                      nt, preferred_element_type=F32)
            c["v"] = v
            c["ygt"] = jnp.concatenate([bt * g_last, kt * g_last], axis=0).T.astype(BF16)
            c["g_col"] = jnp.broadcast_to(g_last, (LANES, LANES)).T
        for c in chains:
            P0 = c.pop("P0") * c["tri"]
            P1 = c.pop("P1") * c["tri"]
            ABRB = jnp.where(lo_f, P0, P1)
            c["AKRK"] = jnp.where(lo_f, P1, P0)
            c["AB"], c["RB"] = ABRB[:C], ABRB[C:]
        for c in chains:
            l8 = c["AB"] * blk_masks[0]
            c["T"] = eye2 + l8
            c["Lp"] = bdot(l8, bd(l8))
        for c in chains:
            R = bdot(jnp.concatenate([c["Lp"], c["T"]], axis=0), bd(c["Lp"]))
            c["T"] = c["T"] + R[C:]
            c["Lp"] = R[:C]
        for c in chains:
            c["T"] = c["T"] + bdot(c["T"], bd(c["Lp"]))
        for lvl in range(1, len(blk_masks)):
            for c in chains:
                c["Lp"] = bdot(c["AB"] * (blk_masks[lvl] - blk_masks[lvl - 1]), bd(c["T"]))
            for c in chains:
                c["T"] = c["T"] + bdot(c["T"], bd(c["Lp"]))
        for c in chains:
            c["VK"] = bdot(c["AKRK"], bd_swap(c["v"]))
        for k in range(n_sub):
            now = [c for c in chains if c["k"] == k]
            for c in now:
                c["S0"] = st_s[c["bi"], c["d"], c["pr"]]
                c["XS"] = _dot(c["X"], c["S0"].astype(BF16))
            for c in now:
                c["U"] = bdot(c["T"], bd(c["XS"][:C] + c["VK"][:C]))
            for c in now:
                y_r = c["refs"][6]
                y_r[c["bi"], pl.ds(c["base"], C), c["ls"]] = c["XS"][C:] + c["VK"][C:] + bdot(c["RB"], bd(c["U"]))
                uv = jnp.concatenate([c["U"], c["v"]], axis=0).astype(BF16)
                st_s[c["bi"], c["d"], c["pr"]] = jnp.where(bd_mask, c["g_col"] * c["S0"] + _dot(c["ygt"], uv), 0.0)
        return carry

    lax.fori_loop(0, tc // (C * n_sub), sub_chunk, 0)

    @pl.when(j == pl.num_programs(1) - 1)
    def _():
        for bi in range(n_req):
            for d in range(2):
                for pr in range(RWKV_PAIRS):
                    mt = st_s[bi, d, pr].T
                    sfin_ref[bi, d, pr] = jnp.where(row_f < C, mt, pltpu.roll(mt, C, 1))[:, :C]


def _rwkv_scan(r, v, na, lw, k, b, s0, n_b, seq):
    tc = min(seq, RWKV_CHUNK)
    n_c = seq // tc
    W = RWKV_WIDTH
    bb = min(RWKV_BB, n_b)
    blk = (bb, tc, W)
    fwd = pl.BlockSpec(blk, lambda g, j: (g, j, 0))
    bwd = pl.BlockSpec(blk, lambda g, j: (g, n_c - 1 - j, 0))
    fwd_d = pl.BlockSpec((None,) + blk, lambda g, j: (0, g, j, 0))
    bwd_d = pl.BlockSpec((None,) + blk, lambda g, j: (1, g, n_c - 1 - j, 0))
    st_blk = (bb, 2, RWKV_PAIRS, LANES, LANES)
    st_spec = pl.BlockSpec(st_blk, lambda g, j: (g, 0, 0, 0, 0))
    y_shape = jax.ShapeDtypeStruct((n_b, seq, W), F32)
    fin_blk = (bb, 2, RWKV_PAIRS, LANES, RWKV_HEAD)
    fin_spec = pl.BlockSpec(fin_blk, lambda g, j: (g, 0, 0, 0, 0))
    return pl.pallas_call(
        _rwkv_scan_kernel,
        grid=(n_b // bb, n_c),
        in_specs=[fwd, fwd, fwd, fwd_d, fwd_d, fwd_d, bwd, bwd, bwd, bwd_d, bwd_d, bwd_d, st_spec],
        out_specs=[fwd, bwd, fin_spec],
        out_shape=[y_shape, y_shape, jax.ShapeDtypeStruct((n_b,) + fin_blk[1:], F32)],
        scratch_shapes=[pltpu.VMEM(st_blk, F32)],
        compiler_params=_params("arbitrary", "arbitrary"),
        name="rwkv_scan",
    )(r, v, na, lw, k, b, r, v, na, lw, k, b, s0)


def _lru_kernel(x_ref, g_ref, h0_ref, cw_ref, cb_ref, wa_ref, ba_ref, wx_ref, bx_ref, lam_ref,
                o_ref, hfin_ref, a_s, u_s, h_s):
    seq = x_ref.shape[0]
    x = x_ref[...]
    row = lax.broadcasted_iota(jnp.int32, x.shape, 0)
    xm2 = jnp.where(row >= 2, pltpu.roll(x, 2, 0), 0.0)
    xm1 = jnp.where(row >= 1, pltpu.roll(x, 1, 0), 0.0)
    xp1 = jnp.where(row < seq - 1, pltpu.roll(x, seq - 1, 0), 0.0)
    xc = (xm2 * cw_ref[0:1, :] + xm1 * cw_ref[1:2, :] + x * cw_ref[2:3, :] + xp1 * cw_ref[3:4, :]
          + cb_ref[...])
    xb = xc.astype(BF16)
    for d in range(2):
        r_gate = _sigmoid(_dot(xb, wa_ref[d]) + ba_ref[d:d + 1, :])
        i_gate = _sigmoid(_dot(xb, wx_ref[d]) + bx_ref[d:d + 1, :])
        log_a = -LRU_C * r_gate * _softplus(-lam_ref[d:d + 1, :])
        a = jnp.exp(log_a)
        u = jnp.sqrt(1.0 - a * a) * (i_gate * xc)
        for sh in (1, 2, 4):
            if d == 0:
                a_n, u_n = pltpu.roll(a, sh, 0), pltpu.roll(u, sh, 0)
                m = (row % SUBLANES) >= sh
            else:
                a_n, u_n = pltpu.roll(a, seq - sh, 0), pltpu.roll(u, seq - sh, 0)
                m = (row % SUBLANES) < SUBLANES - sh
            u = jnp.where(m, a * u_n + u, u)
            a = jnp.where(m, a * a_n, a)
        a_s[d] = a
        u_s[d] = u

    n_t = seq // SUBLANES
    unroll = 4

    def body(i, carry):
        hf, hb = carry
        for k in range(unroll):
            base = pl.multiple_of((i * unroll + k) * SUBLANES, SUBLANES)
            h8 = a_s[0, pl.ds(base, SUBLANES), :] * hf + u_s[0, pl.ds(base, SUBLANES), :]
            h_s[0, pl.ds(base, SUBLANES), :] = h8
            hf = h8[SUBLANES - 1:SUBLANES]
            base = pl.multiple_of(seq - SUBLANES - (i * unroll + k) * SUBLANES, SUBLANES)
            h8 = a_s[1, pl.ds(base, SUBLANES), :] * hb + u_s[1, pl.ds(base, SUBLANES), :]
            h_s[1, pl.ds(base, SUBLANES), :] = h8
            hb = h8[0:1]
        return hf, hb

    hf, hb = lax.fori_loop(0, n_t // unroll, body, (h0_ref[0:1, :], h0_ref[1:2, :]))
    hfin_ref[...] = jnp.concatenate([hf, hb], axis=0)
    g = g_ref[...]
    gelu = 0.5 * g * (1.0 + jnp.tanh(0.7978845608028654 * (g + 0.044715 * (g * g * g))))
    o_ref[...] = (h_s[0] + h_s[1]) * gelu


def _lru(xb, gb, h0, wts, n_b, seq):
    cw, cb, wa, ba, wx, bx, lam = wts
    C = LRU_WIDTH
    const = lambda shape: pl.BlockSpec(shape, lambda b: tuple(0 for _ in shape))
    tok = pl.BlockSpec((seq, C), lambda b: (b, 0))
    st = pl.BlockSpec((None, 2, C), lambda b: (b, 0, 0))
    return pl.pallas_call(
        _lru_kernel,
        grid=(n_b,),
        in_specs=[tok, tok, st, const((LRU_CONV_W, C)), const((1, C)), const((2, C, C)), const((2, C)),
                  const((2, C, C)), const((2, C)), const((2, C))],
        out_specs=[tok, st],
        out_shape=[jax.ShapeDtypeStruct((n_b * seq, C), F32), jax.ShapeDtypeStruct((n_b, 2, C), F32)],
        scratch_shapes=[pltpu.VMEM((2, seq, C), F32)] * 3,
        compiler_params=_params("arbitrary"),
        name="lru",
    )(xb, gb, h0, cw, cb, wa, ba, wx, bx, lam)


def _outproj_kernel(x_ref, att_ref, yf_ref, yb_ref, bonus_ref, g_ref, lru_ref, mod_ref, n2_ref,
                    lnw_ref, lnb_ref, w_ref, rt_ref, bd_ref,
                    x1_ref, h2_ref, lg_ref):
    y = yf_ref[...] + yb_ref[...]
    parts = []
    for c in range(RWKV_WIDTH // LANES):
        yc = y[:, c * LANES:(c + 1) * LANES]
        mean = _seg_sum(yc, bd_ref[...]) * (1.0 / RWKV_HEAD)
        dev = yc - mean
        var = _seg_sum(dev * dev, bd_ref[...]) * (1.0 / RWKV_HEAD)
        parts.append(dev * lax.rsqrt(var + GN_EPS))
    yn = jnp.concatenate(parts, axis=-1) * lnw_ref[...] + lnb_ref[...]
    rwkv = (yn + bonus_ref[...]) * g_ref[...]
    o1 = ATT_WIDTH
    o2 = o1 + RWKV_WIDTH
    mixed = (_dot(att_ref[...].astype(BF16), w_ref[:o1, :])
             + _dot(rwkv.astype(BF16), w_ref[o1:o2, :])
             + _dot(lru_ref[...].astype(BF16), w_ref[o2:, :]))
    x1 = x_ref[...] + mod_ref[2:3, :] * mixed
    x1_ref[...] = x1
    rs = lax.rsqrt(jnp.mean(x1 * x1, axis=-1, keepdims=True) + RMS_EPS)
    h2 = (x1 * rs * n2_ref[...]) * (1.0 + mod_ref[4:5, :]) + mod_ref[3:4, :]
    h2_hi = h2.astype(BF16)
    h2_lo = (h2 - h2_hi.astype(F32)).astype(BF16)
    h2_ref[...] = h2_hi
    lg_ref[...] = _dot(h2_hi, rt_ref[0]) + _dot(h2_hi, rt_ref[1]) + _dot(h2_lo, rt_ref[0])


def _outproj(x, att, yf, yb, bonus, g, lru, mod_l, norm2, lnw, lnb, w_out_bf, layer, router_pad, bd,
             seq, cond_row0):
    n_tok = x.shape[0]
    tm = 512 if seq >= 512 else seq * (512 // seq)
    per_seq = max(seq // tm, 1)
    n_t = n_tok // tm
    if cond_row0 == 0:
        row = lambda i: 0
    else:
        row = lambda i: cond_row0 + i // per_seq
    tok = lambda w: pl.BlockSpec((tm, w), lambda i: (i, 0))
    const = lambda shape: pl.BlockSpec(shape, lambda i: tuple(0 for _ in shape))
    W = RWKV_WIDTH
    return pl.pallas_call(
        _outproj_kernel,
        grid=(n_t,),
        in_specs=[
            tok(D_MODEL), tok(ATT_WIDTH), tok(W), tok(W), tok(W), tok(W), tok(LRU_WIDTH),
            pl.BlockSpec((None, N_MOD, D_MODEL), lambda i: (row(i), 0, 0)),
            const((1, D_MODEL)), const((1, W)), const((1, W)),
            pl.BlockSpec((None, D_MODEL, D_MODEL), lambda i: (layer, 0, 0)),
            const((2, D_MODEL, LANES)), const((LANES, LANES)),
        ],
        out_specs=[tok(D_MODEL), tok(D_MODEL), tok(LANES)],
        out_shape=[jax.ShapeDtypeStruct((n_tok, D_MODEL), F32),
                   jax.ShapeDtypeStruct((n_tok, D_MODEL), BF16),
                   jax.ShapeDtypeStruct((n_tok, LANES), F32)],
        compiler_params=_params("arbitrary"),
        name="outproj",
    )(x, att, yf, yb, bonus, g, lru, mod_l, norm2, lnw, lnb, w_out_bf, router_pad, bd)


PREFIX_BLOCK = 256
GATHER_ROWS = 1024


def _prefix_count(mask_f, tri):
    seq = mask_f.shape[0]
    outs = []
    carry = jnp.zeros((1, LANES), F32)
    for blk in range(seq // PREFIX_BLOCK):
        m = mask_f[blk * PREFIX_BLOCK:(blk + 1) * PREFIX_BLOCK]
        outs.append(_dot(tri, m.astype(BF16)) + carry)
        carry = carry + jnp.sum(m, axis=0, keepdims=True)
    return jnp.concatenate(outs, axis=0) if len(outs) > 1 else outs[0]


ROUTE_ROWS = 1024


def _route_kernel(lg_ref, h2_ref, tri_ref, xs_ref, slot_ref, aff_ref, *, cap, seq):
    rows = lg_ref.shape[0]
    n_req = rows // seq
    lane = lax.broadcasted_iota(jnp.int32, (rows, LANES), 1)
    real = lane < N_EXPERTS
    lg = jnp.where(real, lg_ref[...], -jnp.inf)
    m = jnp.max(lg, axis=-1, keepdims=True)
    e = jnp.exp(lg - m)
    aff = e / jnp.sum(e, axis=-1, keepdims=True)
    aff_ref[...] = aff

    a3 = aff.reshape(n_req, seq, LANES)
    a_min = jnp.min(a3, axis=1, keepdims=True)
    a_max = jnp.max(a3, axis=1, keepdims=True)

    def search(carry):
        lo, hi, _ = carry
        mid = 0.5 * (jnp.maximum(lo, a_min) + jnp.minimum(hi, a_max))
        inside = jnp.where(a3 > lo, jnp.where(a3 < hi, 1.0, 0.0), 0.0)
        upper = inside * jnp.where(a3 >= mid, 1.0, 0.0)
        up = jnp.min(jnp.where(upper > 0.0, a3, BRACKET_HI), axis=1, keepdims=True)
        dn = jnp.max(jnp.where(inside - upper > 0.0, a3, BRACKET_LO), axis=1, keepdims=True)
        pivot = jnp.where(up < BRACKET_HI, up, dn)
        found = jnp.where(up < BRACKET_HI, 1.0, jnp.where(dn > BRACKET_LO, 1.0, 0.0))
        cnt = jnp.sum(jnp.where(a3 >= pivot, 1.0, 0.0), axis=1, keepdims=True)
        take_lo = found * jnp.where(cnt >= cap, 1.0, 0.0)
        take_hi = found - take_lo
        return (jnp.where(take_lo > 0.0, pivot, lo), jnp.where(take_hi > 0.0, pivot, hi), found)

    shape = (n_req, 1, LANES)
    init = (jnp.full(shape, BRACKET_LO, F32), jnp.full(shape, BRACKET_HI, F32), jnp.ones(shape, F32))
    thr = lax.while_loop(lambda c: jnp.max(c[2]) > 0.0, search, init)[0]
    gt3 = jnp.where(a3 > thr, 1.0, 0.0)
    eq3 = jnp.where(a3 == thr, 1.0, 0.0)
    need3 = cap - jnp.sum(gt3, axis=1, keepdims=True)
    tri = tri_ref[...]
    c_iota = lax.broadcasted_iota(jnp.int32, (cap, seq), 0).astype(F32)
    real_seq = lax.broadcasted_iota(jnp.int32, (seq, LANES), 1) < N_EXPERTS
    group = max(1, min(N_EXPERTS, GATHER_ROWS // cap))
    for r in range(n_req):
        tok = slice(r * seq, (r + 1) * seq)
        gt, eq = gt3[r], eq3[r]
        sel = jnp.where(real_seq, gt + eq * jnp.where(_prefix_count(eq, tri) < need3[r], 1.0, 0.0), 0.0)
        slot = jnp.where(sel > 0.0, _prefix_count(sel, tri), -1.0)
        slot_ref[tok, :] = slot
        slot_t = slot.T
        h2 = h2_ref[tok, :]
        for g0 in range(0, N_EXPERTS, group):
            onehot = jnp.concatenate(
                [jnp.where(c_iota == slot_t[ex:ex + 1, :], 1.0, 0.0) for ex in range(g0, g0 + group)], axis=0)
            picked = _dot(onehot.astype(BF16), h2)
            for k in range(group):
                xs_ref[g0 + k, r * cap:(r + 1) * cap, :] = picked[k * cap:(k + 1) * cap].astype(BF16)


def _route(logits, h2, tri, n_b, seq):
    cap = EC_FACTOR * seq // N_EXPERTS
    n_req = max(1, min(n_b, ROUTE_ROWS // seq))
    tok = lambda w: pl.BlockSpec((n_req * seq, w), lambda b: (b, 0))
    return pl.pallas_call(
        functools.partial(_route_kernel, cap=cap, seq=seq),
        grid=(n_b // n_req,),
        in_specs=[tok(LANES), tok(D_MODEL), pl.BlockSpec((PREFIX_BLOCK, PREFIX_BLOCK), lambda b: (0, 0))],
        out_specs=[pl.BlockSpec((N_EXPERTS, n_req * cap, D_MODEL), lambda b: (0, b, 0)), tok(LANES), tok(LANES)],
        out_shape=[jax.ShapeDtypeStruct((N_EXPERTS, n_b * cap, D_MODEL), BF16),
                   jax.ShapeDtypeStruct((n_b * seq, LANES), F32),
                   jax.ShapeDtypeStruct((n_b * seq, LANES), F32)],
        compiler_params=_params("arbitrary"),
        name="route",
    )(logits, h2, tri)


EXPERT_RB = 256


def _expert_kernel(xa_ref, xb_ref, wg_ref, wu_ref, wd_ref, ya_ref, yb_ref, wg_s, wu_s, wd_s):
    wg_s[...] = wg_ref[...].astype(BF16)
    wu_s[...] = wu_ref[...].astype(BF16)
    wd_s[...] = wd_ref[...].astype(BF16)
    for x_ref, y_ref in ((xa_ref, ya_ref), (xb_ref, yb_ref)):
        for rb in range(x_ref.shape[0] // EXPERT_RB):
            rows = slice(rb * EXPERT_RB, (rb + 1) * EXPERT_RB)
            x = x_ref[rows, :]
            a = _dot(x, wg_s[...])
            u = _dot(x, wu_s[...])
            hid = (a * _sigmoid(a) * u).astype(BF16)
            y_ref[rows, :] = _dot(hid, wd_s[...]).astype(BF16)


def _experts(xs_a, xs_b, w_gate, w_up, w_down, layer):
    ma, mb = xs_a.shape[1], xs_b.shape[1]
    xspec = lambda m: pl.BlockSpec((None, m, D_MODEL), lambda e: (e, 0, 0))
    return pl.pallas_call(
        _expert_kernel,
        grid=(N_EXPERTS,),
        in_specs=[
            xspec(ma), xspec(mb),
            pl.BlockSpec((None, None, D_MODEL, EXPERT_FF), lambda e: (layer, e, 0, 0)),
            pl.BlockSpec((None, None, D_MODEL, EXPERT_FF), lambda e: (layer, e, 0, 0)),
            pl.BlockSpec((None, None, EXPERT_FF, D_MODEL), lambda e: (layer, e, 0, 0)),
        ],
        out_specs=[xspec(ma), xspec(mb)],
        out_shape=[jax.ShapeDtypeStruct(xs_a.shape, BF16), jax.ShapeDtypeStruct(xs_b.shape, BF16)],
        scratch_shapes=[pltpu.VMEM((D_MODEL, EXPERT_FF), BF16), pltpu.VMEM((D_MODEL, EXPERT_FF), BF16),
                        pltpu.VMEM((EXPERT_FF, D_MODEL), BF16)],
        compiler_params=_params("arbitrary"),
        name="experts",
    )(xs_a, xs_b, w_gate, w_up, w_down)


COMBINE_FUSED_COLS = 512


def _combine_kernel(*refs, cap, fused):
    if fused:
        y_ref, slot_ref, aff_ref, x1_ref, mod_ref, ex_ref, o_ref = refs
    else:
        y_ref, slot_ref, aff_ref, x1_ref, mod_ref, o_ref = refs
    seq = x1_ref.shape[0]
    slot = slot_ref[...]
    aff = aff_ref[...]
    if fused:
        spread = ex_ref[...]
        a1 = aff.astype(BF16)
        r1 = aff - a1.astype(F32)
        a2 = r1.astype(BF16)
        a3 = (r1 - a2.astype(F32)).astype(BF16)
        slot_x = _dot(slot.astype(BF16), spread)
        aff_x = _dot(a1, spread) + _dot(a2, spread) + _dot(a3, spread)
        cols = N_EXPERTS * cap
        c_pat = (lax.broadcasted_iota(jnp.int32, (seq, cols), 1) % cap).astype(F32)
        w_hi, w_lo = _split(jnp.where(slot_x == c_pat, aff_x, 0.0))
        y2 = y_ref[...].reshape(cols, D_MODEL)
        acc = _dot(w_hi, y2) + _dot(w_lo, y2)
    else:
        c_iota = lax.broadcasted_iota(jnp.int32, (seq, cap), 1).astype(F32)
        acc = jnp.zeros((seq, D_MODEL), F32)
        for ex in range(N_EXPERTS):
            onehot = jnp.where(slot[:, ex:ex + 1] == c_iota, 1.0, 0.0).astype(BF16)
            acc = acc + aff[:, ex:ex + 1] * _dot(onehot, y_ref[ex])
    o_ref[...] = x1_ref[...] + mod_ref[5:6, :] * acc


def _combine(y, slot, aff, x1, mod_l, n_b, seq, cond_row0):
    cap = EC_FACTOR * seq // N_EXPERTS
    fused = N_EXPERTS * cap <= COMBINE_FUSED_COLS
    if cond_row0 == 0:
        row = lambda b: 0
    else:
        row = lambda b: cond_row0 + b
    tm = min(seq, 512)
    n_t = seq // tm
    tok = lambda w: pl.BlockSpec((tm, w), lambda b, i: (b * n_t + i, 0))
    in_specs = [
        pl.BlockSpec((N_EXPERTS, cap, D_MODEL), lambda b, i: (0, b, 0)),
        tok(LANES), tok(LANES), tok(D_MODEL),
        pl.BlockSpec((None, N_MOD, D_MODEL), lambda b, i: (row(b), 0, 0)),
    ]
    args = [y, slot, aff, x1, mod_l]
    if fused:
        cols = N_EXPERTS * cap
        spread = (jnp.arange(LANES)[:, None] == (jnp.arange(cols) // cap)[None, :]).astype(BF16)
        in_specs.append(pl.BlockSpec((LANES, cols), lambda b, i: (0, 0)))
        args.append(spread)
    return pl.pallas_call(
        functools.partial(_combine_kernel, cap=cap, fused=fused),
        grid=(n_b, n_t),
        in_specs=in_specs,
        out_specs=tok(D_MODEL),
        out_shape=jax.ShapeDtypeStruct((n_b * seq, D_MODEL), F32),
        compiler_params=_params("arbitrary", "arbitrary"),
        name="combine_fused" if fused else "combine",
    )(*args)


def _rope_tables(seq):
    t = jnp.arange(seq, dtype=jnp.int32)
    row = (t // GRID_W).astype(F32)
    col = (t % GRID_W).astype(F32)
    half = HEAD_DIM // 2
    inv = ROPE_THETA ** (-jnp.arange(0, half, 2, dtype=F32) / half)
    lane = jnp.arange(LANES)
    u = lane % HEAD_DIM
    pos = jnp.where((u // half)[None, :] == 0, row[:, None], col[:, None])
    ang = pos * inv[(u % half) % (half // 2)][None, :]
    first = ((u % half) < half // 2)[None, :]
    sin = jnp.sin(ang)
    return jnp.cos(ang), jnp.where(first, -sin, 0.0), jnp.where(first, 0.0, sin)


def _block_diag(w):
    n, k, _ = w.shape
    eye = jnp.eye(n, dtype=w.dtype)
    return (eye[:, None, :, None] * w[:, :, None, :]).reshape(n * k, n * k)


def _split_weight(w):
    return jnp.stack(_split(w))


def _pad_rows(w, offset, total=LANES):
    return jnp.zeros((total, w.shape[1]), w.dtype).at[offset:offset + w.shape[0]].set(w)


def _pack_state(s):
    b = s.shape[0]
    st = s.reshape(b, 2, RWKV_PAIRS, 2, RWKV_HEAD, RWKV_HEAD).transpose(0, 1, 2, 3, 5, 4)
    eye = jnp.eye(2, dtype=s.dtype)
    out = st[:, :, :, :, :, None, :] * eye[None, None, None, :, None, :, None]
    return out.reshape(b, 2, RWKV_PAIRS, LANES, LANES)


def _unpack_state(s):
    return s.reshape(s.shape[0], 2, RWKV_HEADS, RWKV_HEAD, RWKV_HEAD)


def kernel(x_prompt, x_sample, cache_k, cache_v, state_rwkv, state_lru, c, c_ctx, w_ada, b_ada, norm1, norm2, w_in, w_out, q_norm, k_norm, rwkv_mu, rwkv_w0, rwkv_w_up, rwkv_a0, rwkv_a_up, rwkv_g_up, rwkv_k_k, rwkv_k_a, rwkv_r_k, rwkv_ln_w, rwkv_ln_b, lru_conv_w, lru_conv_b, lru_wa, lru_ba, lru_wx, lru_bx, lru_lambda, router, exp_w_gate, exp_w_up, exp_w_down):
    n_ctx, seq_ctx, _ = x_prompt.shape
    n_lat, seq_lat, _ = x_sample.shape
    past = cache_k.shape[2]
    assert n_lat + 1 <= COND_ROWS

    cond = jnp.zeros((COND_ROWS, D_MODEL), F32).at[0].set(c_ctx).at[1:1 + n_lat].set(c)
    mod = _ada(cond, w_ada, b_ada.reshape(DEPTH, 1, N_MOD * D_MODEL))
    mod = mod.reshape(DEPTH, COND_ROWS, N_MOD, D_MODEL)

    w_in_bf = w_in.astype(BF16)
    w_out_bf = w_out.astype(BF16)
    lane = jnp.arange(LANES)
    bd = (lane[:, None] // HEAD_DIM == lane[None, :] // HEAD_DIM).astype(BF16)
    pidx = jnp.arange(PREFIX_BLOCK)
    tri = (pidx[None, :] < pidx[:, None]).astype(BF16)
    rope_tabs = _rope_tables(seq_lat)

    paths = [
        dict(x=x_prompt.reshape(n_ctx * seq_ctx, D_MODEL), n_b=n_ctx, seq=seq_ctx, row0=0, rope=None),
        dict(x=x_sample.reshape(n_lat * seq_lat, D_MODEL), n_b=n_lat, seq=seq_lat, row0=1, rope=rope_tabs),
    ]
    new_k, new_v, new_sr, new_sl = [], [], [], []
    for l in range(DEPTH):
        mod_l = mod[l]
        qn = jnp.tile(q_norm[l], LANES // HEAD_DIM)[None, :]
        kn = jnp.tile(k_norm[l], LANES // HEAD_DIM)[None, :]
        prep_w = (
            rwkv_mu[l][None, :], rwkv_k_k[l][None, :], rwkv_k_a[l][None, :],
            rwkv_r_k[l].reshape(1, RWKV_WIDTH), rwkv_w0[l], rwkv_a0[l],
            jnp.stack([_split_weight(_pad_rows(rwkv_w_up[l, d], 0)) for d in range(2)]),
            jnp.stack([_split_weight(_pad_rows(rwkv_a_up[l, d], RWKV_DECAY_LORA)) for d in range(2)]),
            _split_weight(_pad_rows(rwkv_g_up[l], RWKV_DECAY_LORA + RWKV_AAA_LORA)),
        )
        lru_w = (
            lru_conv_w[l], lru_conv_b[l][None, :],
            jnp.stack([_block_diag(lru_wa[l, d]) for d in range(2)]).astype(BF16), lru_ba[l],
            jnp.stack([_block_diag(lru_wx[l, d]) for d in range(2)]).astype(BF16), lru_bx[l],
            lru_lambda[l],
        )
        router_pad = _split_weight(jnp.zeros((D_MODEL, LANES), F32).at[:, :N_EXPERTS].set(router[l]))
        mids = []
        for pi, pth in enumerate(paths):
            n_b, seq, row0 = pth["n_b"], pth["seq"], pth["row0"]
            latent = pth["rope"] is not None
            outs = _inproj(pth["x"], mod_l, norm1[l][None, :], w_in_bf, l, qn, kn, bd, pth["rope"], seq, row0)
            if latent:
                q, k_n, k_att, v, p_rwkv, lru_x, lru_g = outs
                cache = (cache_k[:, l].reshape(n_b, past, KV_WIDTH), cache_v[:, l].reshape(n_b, past, KV_WIDTH))
                s0 = _pack_state(state_rwkv[:, l])
                h0 = state_lru[:, l]
            else:
                q, k_n, v, p_rwkv, lru_x, lru_g = outs
                k_att, cache = k_n, None
                s0 = jnp.zeros((n_b, 2, RWKV_PAIRS, LANES, LANES), F32)
                h0 = jnp.zeros((n_b, 2, LRU_WIDTH), F32)
                new_k.append(k_n.reshape(n_b, seq, ATT_KV_HEADS, HEAD_DIM))
                new_v.append(v.reshape(n_b, seq, ATT_KV_HEADS, HEAD_DIM))
            att = _attention(q, k_att, v, n_b, seq, cache)
            r, vv, na, w, kd, b, g, bonus = _rwkv_prep(p_rwkv.reshape(n_b, seq, RWKV_COLS), n_b, seq, prep_w, bd)
            W = RWKV_WIDTH
            yf, yb, s_fin = _rwkv_scan(r.reshape(n_b, seq, W), vv.reshape(n_b, seq, W), na.reshape(n_b, seq, W),
                                       w.reshape(2, n_b, seq, W), kd.reshape(2, n_b, seq, W),
                                       b.reshape(2, n_b, seq, W), s0, n_b, seq)
            lru_out, h_fin = _lru(lru_x, lru_g, h0, lru_w, n_b, seq)
            if not latent:
                new_sr.append(_unpack_state(s_fin))
                new_sl.append(h_fin)
            x1, h2, logits = _outproj(pth["x"], att, yf.reshape(n_b * seq, W), yb.reshape(n_b * seq, W), bonus, g,
                                      lru_out, mod_l, norm2[l][None, :], rwkv_ln_w[l][None, :],
                                      rwkv_ln_b[l][None, :], w_out_bf, l, router_pad, bd, seq, row0)
            xs, slot, aff = _route(logits, h2, tri, n_b, seq)
            mids.append((xs, slot, aff, x1))
        y_a, y_b = _experts(mids[0][0], mids[1][0], exp_w_gate, exp_w_up, exp_w_down, l)
        for pth, (xs, slot, aff, x1), y in zip(paths, mids, (y_a, y_b)):
            pth["x"] = _combine(y, slot, aff, x1, mod_l, pth["n_b"], pth["seq"], pth["row0"])

    y_prompt = paths[0]["x"].reshape(n_ctx, seq_ctx, D_MODEL)
    y_sample = paths[1]["x"].reshape(n_lat, seq_lat, D_MODEL)
    return (y_prompt, y_sample, jnp.stack(new_k, axis=1), jnp.stack(new_v, axis=1),
            jnp.stack(new_sr, axis=1), jnp.stack(new_sl, axis=1))
```

```python
import functools

import jax
import jax.numpy as jnp
from jax import lax
from jax.experimental import pallas as pl
from jax.experimental.pallas import tpu as pltpu

F32 = jnp.float32
BF16 = jnp.bfloat16
HIGHEST = lax.Precision.HIGHEST

D_MODEL = 1024
DEPTH = 2
GRID_W = 64
ATT_HEADS = 8
ATT_KV_HEADS = 2
HEAD_DIM = 64
ATT_WIDTH = ATT_HEADS * HEAD_DIM
KV_WIDTH = ATT_KV_HEADS * HEAD_DIM
ROPE_THETA = 10000.0
RWKV_HEADS = 4
RWKV_HEAD = 64
RWKV_WIDTH = RWKV_HEADS * RWKV_HEAD
RWKV_DECAY_LORA = 32
RWKV_AAA_LORA = 32
RWKV_GATE_LORA = 64
RWKV_COLS = 3 * RWKV_WIDTH + RWKV_DECAY_LORA + RWKV_AAA_LORA + RWKV_GATE_LORA
GN_EPS = 64e-5
LRU_BLOCKS = 4
LRU_BLOCK = 64
LRU_WIDTH = LRU_BLOCKS * LRU_BLOCK
LRU_CONV_W = 4
LRU_C = 8.0
IN_COLS = ATT_WIDTH + 2 * KV_WIDTH + RWKV_COLS + 2 * LRU_WIDTH
N_EXPERTS = 16
EC_FACTOR = 2
EXPERT_FF = 1024
RMS_EPS = 1e-6

LANES = 128
SUBLANES = 8
VMEM_LIMIT = 56 * 1024 * 1024
N_MOD = 6
COND_ROWS = 8
BRACKET_LO, BRACKET_HI = -1.0, 2.0


def _params(*sem):
    return pltpu.CompilerParams(dimension_semantics=sem, vmem_limit_bytes=VMEM_LIMIT)


def _dot(a, b, precision=None):
    return jnp.dot(a, b, preferred_element_type=F32, precision=precision)


def _sigmoid(x):
    return 1.0 / (1.0 + jnp.exp(-x))


def _softplus(x):
    return jnp.maximum(x, 0.0) + jnp.log1p(jnp.exp(-jnp.abs(x)))


def _split(x):
    hi = x.astype(BF16)
    return hi, (x - hi.astype(F32)).astype(BF16)


def _dot_split(x, w_ref):
    hi, lo = _split(x)
    return _dot(hi, w_ref[0]) + _dot(hi, w_ref[1]) + _dot(lo, w_ref[0])


def _seg_sum(x, ones_bd):
    return _dot(x.astype(BF16), ones_bd)


def _ada_kernel(c_ref, w_ref, b_ref, o_ref):
    c = c_ref[...]
    s = c * _sigmoid(c)
    o_ref[...] = _dot(s, w_ref[...], HIGHEST) + b_ref[...]


def _ada(cond, w_ada, b_ada):
    n_l = w_ada.shape[0]
    tn = 1536
    n_t = N_MOD * D_MODEL // tn
    return pl.pallas_call(
        _ada_kernel,
        grid=(n_l, n_t),
        in_specs=[
            pl.BlockSpec((COND_ROWS, D_MODEL), lambda l, j: (0, 0)),
            pl.BlockSpec((None, D_MODEL, tn), lambda l, j: (l, 0, j)),
            pl.BlockSpec((None, 1, tn), lambda l, j: (l, 0, j)),
        ],
        out_specs=pl.BlockSpec((None, COND_ROWS, tn), lambda l, j: (l, 0, j)),
        out_shape=jax.ShapeDtypeStruct((n_l, COND_ROWS, N_MOD * D_MODEL), F32),
        compiler_params=_params("arbitrary", "arbitrary"),
        name="ada",
    )(cond, w_ada, b_ada)


def _head_rms(x, gain, ones_bd):
    ms = _seg_sum(x * x, ones_bd) * (1.0 / HEAD_DIM)
    return x * lax.rsqrt(ms + RMS_EPS) * gain


def _rope(x, cos, sin_up, sin_dn):
    return x * cos + pltpu.roll(x, LANES - 16, 1) * sin_up + pltpu.roll(x, 16, 1) * sin_dn


def _inproj_kernel(*refs, rope):
    if rope:
        (x_ref, mod_ref, n1_ref, w_ref, qn_ref, kn_ref, bd_ref, cos_ref, su_ref, sd_ref,
         q_ref, ko_ref, ka_ref, v_ref, pr_ref, lx_ref, lg_ref) = refs
    else:
        (x_ref, mod_ref, n1_ref, w_ref, qn_ref, kn_ref, bd_ref,
         q_ref, ko_ref, v_ref, pr_ref, lx_ref, lg_ref) = refs
    x = x_ref[...]
    rs = lax.rsqrt(jnp.mean(x * x, axis=-1, keepdims=True) + RMS_EPS)
    h = (x * rs * n1_ref[...]) * (1.0 + mod_ref[1:2, :]) + mod_ref[0:1, :]
    p = _dot(h.astype(BF16), w_ref[...])
    bd = bd_ref[...]
    for c in range(ATT_WIDTH // LANES):
        qc = _head_rms(p[:, c * LANES:(c + 1) * LANES], qn_ref[...], bd)
        if rope:
            qc = _rope(qc, cos_ref[...], su_ref[...], sd_ref[...])
        q_ref[:, c * LANES:(c + 1) * LANES] = (qc * (HEAD_DIM ** -0.5)).astype(BF16)
    o = ATT_WIDTH
    kc = _head_rms(p[:, o:o + KV_WIDTH], kn_ref[...], bd)
    ko_ref[...] = kc
    if rope:
        ka_ref[...] = _rope(kc, cos_ref[...], su_ref[...], sd_ref[...]).astype(BF16)
    o += KV_WIDTH
    v_ref[...] = p[:, o:o + KV_WIDTH]
    o += KV_WIDTH
    pr_ref[...] = p[:, o:o + RWKV_COLS]
    o += RWKV_COLS
    lx_ref[...] = p[:, o:o + LRU_WIDTH]
    o += LRU_WIDTH
    lg_ref[...] = p[:, o:o + LRU_WIDTH]


def _inproj(x, mod_l, norm1, w_in_bf, layer, qn, kn, bd, rope_tabs, seq, cond_row0):
    n_tok = x.shape[0]
    tm = 512 if seq >= 512 else seq * (512 // seq)
    per_seq = max(seq // tm, 1)
    n_t = n_tok // tm
    rope = rope_tabs is not None
    if cond_row0 == 0:
        row = lambda i: 0
    else:
        row = lambda i: cond_row0 + i // per_seq
    tok = lambda w: pl.BlockSpec((tm, w), lambda i: (i, 0))
    const = lambda shape: pl.BlockSpec(shape, lambda i: tuple(0 for _ in shape))
    in_specs = [
        tok(D_MODEL),
        pl.BlockSpec((None, N_MOD, D_MODEL), lambda i: (row(i), 0, 0)),
        const((1, D_MODEL)),
        pl.BlockSpec((None, D_MODEL, IN_COLS), lambda i: (layer, 0, 0)),
        const((1, LANES)), const((1, LANES)), const((LANES, LANES)),
    ]
    args = [x, mod_l, norm1, w_in_bf, qn, kn, bd]
    outs = [(ATT_WIDTH, BF16), (KV_WIDTH, F32)]
    if rope:
        in_specs += [pl.BlockSpec((tm, LANES), lambda i: (i % per_seq, 0))] * 3
        args += list(rope_tabs)
        outs.append((KV_WIDTH, BF16))
    outs += [(KV_WIDTH, F32), (RWKV_COLS, F32), (LRU_WIDTH, F32), (LRU_WIDTH, F32)]
    return pl.pallas_call(
        functools.partial(_inproj_kernel, rope=rope),
        grid=(n_t,),
        in_specs=in_specs,
        out_specs=[tok(w) for w, _ in outs],
        out_shape=[jax.ShapeDtypeStruct((n_tok, w), dt) for w, dt in outs],
        compiler_params=_params("arbitrary"),
        name="inproj_rope" if rope else "inproj",
    )(*args)


def _attn_kernel(*refs, has_cache):
    if has_cache:
        q_ref, k_ref, v_ref, ck_ref, cv_ref, o_ref, kn_s, ks_s, vn_s, vs_s = refs
    else:
        q_ref, k_ref, v_ref, o_ref, kn_s, ks_s, vn_s, vs_s = refs

    @pl.when(pl.program_id(1) == 0)
    def _():
        k = k_ref[...].astype(F32)
        v = v_ref[...]
        if has_cache:
            k = jnp.concatenate([ck_ref[...], k], axis=0)
            v = jnp.concatenate([cv_ref[...], v], axis=0)
        kn_s[...] = k.astype(BF16)
        ks_s[...] = pltpu.roll(k, HEAD_DIM, 1).astype(BF16)
        vn_s[...] = v.astype(BF16)
        vs_s[...] = pltpu.roll(v, HEAD_DIM, 1).astype(BF16)

    tq = q_ref.shape[0]
    lo = lax.broadcasted_iota(jnp.int32, (tq, LANES), 1) < HEAD_DIM
    rep = ATT_HEADS // ATT_KV_HEADS
    for c in range(ATT_WIDTH // LANES):
        qc = q_ref[:, c * LANES:(c + 1) * LANES].astype(F32)
        halves = []
        for half in range(2):
            g = (2 * c + half) // rep
            qm = jnp.where(lo if half == 0 else jnp.logical_not(lo), qc, 0.0).astype(BF16)
            k_s, v_s = (kn_s, vn_s) if half == g else (ks_s, vs_s)
            s = lax.dot_general(qm, k_s[...], (((1,), (1,)), ((), ())), preferred_element_type=F32)
            m = jnp.max(s, axis=-1, keepdims=True)
            e = jnp.exp(s - m)
            l = jnp.sum(e, axis=-1, keepdims=True)
            halves.append(_dot(e.astype(BF16), v_s[...]) / l)
        o_ref[:, c * LANES:(c + 1) * LANES] = jnp.where(lo, halves[0], halves[1]).astype(BF16)


def _attention(q, k, v, n_b, seq, cache=None):
    tq = 256
    n_q = seq // tq
    has_cache = cache is not None
    past = cache[0].shape[1] if has_cache else 0
    in_specs = [
        pl.BlockSpec((tq, ATT_WIDTH), lambda b, i: (b * n_q + i, 0)),
        pl.BlockSpec((seq, KV_WIDTH), lambda b, i: (b, 0)),
        pl.BlockSpec((seq, KV_WIDTH), lambda b, i: (b, 0)),
    ]
    args = [q, k, v]
    if has_cache:
        in_specs += [pl.BlockSpec((None, past, KV_WIDTH), lambda b, i: (b, 0, 0))] * 2
        args += list(cache)
    return pl.pallas_call(
        functools.partial(_attn_kernel, has_cache=has_cache),
        grid=(n_b, n_q),
        in_specs=in_specs,
        out_specs=pl.BlockSpec((tq, ATT_WIDTH), lambda b, i: (b * n_q + i, 0)),
        out_shape=jax.ShapeDtypeStruct((n_b * seq, ATT_WIDTH), BF16),
        scratch_shapes=[pltpu.VMEM((past + seq, KV_WIDTH), BF16)] * 4,
        compiler_params=_params("arbitrary", "arbitrary"),
        name="attn_cache" if has_cache else "attn",
    )(*args)


def _rwkv_prep_kernel(p_ref, mu_ref, kk_ref, ka_ref, rk_ref, w0_ref, a0_ref, wup_ref, aup_ref,
                      gup_ref, bd_ref,
                      r_ref, v_ref, na_ref, w_ref, k_ref, b_ref, g_ref, bonus_ref):
    tm = r_ref.shape[0]
    seq = p_ref.shape[0]
    i = pl.program_id(1)
    n_t = pl.num_programs(1)
    start = pl.multiple_of(i * tm, tm)
    cur = p_ref[pl.ds(start, tm), :]
    prev_base = pl.multiple_of(jnp.maximum(start - SUBLANES, 0), SUBLANES)
    next_base = pl.multiple_of(jnp.minimum(start + tm, seq - SUBLANES), SUBLANES)
    prev_row = p_ref[pl.ds(prev_base, SUBLANES), :][SUBLANES - 1:SUBLANES]
    next_row = p_ref[pl.ds(next_base, SUBLANES), :][0:1]
    prev_row = jnp.where(i > 0, prev_row, 0.0)
    next_row = jnp.where(i < n_t - 1, next_row, 0.0)
    row = lax.broadcasted_iota(jnp.int32, cur.shape, 0)
    prev = jnp.where(row == 0, prev_row, pltpu.roll(cur, 1, 0))
    nxt = jnp.where(row == tm - 1, next_row, pltpu.roll(cur, tm - 1, 0))
    ps = cur + mu_ref[...] * (0.5 * (prev + nxt) - cur)

    W = RWKV_WIDTH
    r = ps[:, :W]
    k = ps[:, W:2 * W]
    v = ps[:, 2 * W:3 * W]
    lora = ps[:, 3 * W:]
    r_ref[...] = r
    v_ref[...] = v.astype(BF16)
    g_ref[...] = _dot_split(_sigmoid(lora), gup_ref)
    lora_t = jnp.tanh(lora)
    kk = k * kk_ref[...]
    kk_parts = []
    for c in range(W // LANES):
        kc = kk[:, c * LANES:(c + 1) * LANES]
        nrm = jnp.sqrt(_seg_sum(kc * kc, bd_ref[...]))
        kk_parts.append(kc / jnp.maximum(nrm, 1e-12))
    kk = jnp.concatenate(kk_parts, axis=-1)
    na_ref[...] = -kk
    bonus = None
    for d in range(2):
        w_log = -_softplus(-(w0_ref[d:d + 1, :] + _dot_split(lora_t, wup_ref.at[d]))) - 0.5
        w_ref[d] = -jnp.exp(w_log)
        a_rate = _sigmoid(a0_ref[d:d + 1, :] + _dot_split(lora, aup_ref.at[d]))
        kd = k * (1.0 + (a_rate - 1.0) * ka_ref[...])
        k_ref[d] = kd
        b_ref[d] = kk * a_rate
        rkr = r * kd * rk_ref[...]
        parts = [_seg_sum(rkr[:, c * LANES:(c + 1) * LANES], bd_ref[...]) for c in range(W // LANES)]
        bd_term = jnp.concatenate(parts, axis=-1) * v
        bonus = bd_term if bonus is None else bonus + bd_term
    bonus_ref[...] = bonus


def _rwkv_prep(p, n_b, seq, wts, bd):
    mu, k_k, k_a, r_k, w0, a0, wup, aup, gup = wts
    tm = min(seq, 512)
    n_t = seq // tm
    W = RWKV_WIDTH
    const = lambda shape: pl.BlockSpec(shape, lambda b, i: tuple(0 for _ in shape))
    tok = pl.BlockSpec((tm, W), lambda b, i: (b * n_t + i, 0))
    tok2 = pl.BlockSpec((2, tm, W), lambda b, i: (0, b * n_t + i, 0))
    one = jax.ShapeDtypeStruct((n_b * seq, W), F32)
    two = jax.ShapeDtypeStruct((2, n_b * seq, W), F32)
    return pl.pallas_call(
        _rwkv_prep_kernel,
        grid=(n_b, n_t),
        in_specs=[
            pl.BlockSpec((None, seq, RWKV_COLS), lambda b, i: (b, 0, 0)),
            const((1, RWKV_COLS)), const((1, W)), const((1, W)), const((1, W)),
            const((2, W)), const((2, W)),
            const((2, 2, LANES, W)), const((2, 2, LANES, W)), const((2, LANES, W)),
            const((LANES, LANES)),
        ],
        out_specs=[tok, tok, tok, tok2, tok2, tok2, tok, tok],
        out_shape=[one, jax.ShapeDtypeStruct((n_b * seq, W), BF16), one, two, two, two, one, one],
        compiler_params=_params("arbitrary", "arbitrary"),
        name="rwkv_prep",
    )(p, mu, k_k, k_a, r_k, w0, a0, wup, aup, gup, bd)


RWKV_BB = 4
RWKV_PAIRS = RWKV_HEADS * RWKV_HEAD // LANES
RWKV_CHUNK = 256
RWKV_SUB = 64
RWKV_STAGE_WIDTH = 32
INV_BASE_SHIFT = 3


def _rwkv_scan_kernel(rf_ref, vf_ref, af_ref, wf_ref, kf_ref, bf_ref,
                      rb_ref, vb_ref, ab_ref, wb_ref, kb_ref, bb_ref, s0_ref,
                      yf_ref, yb_ref, sfin_ref, st_s):
    j = pl.program_id(1)
    n_req, tc = rf_ref.shape[0], rf_ref.shape[1]
    C = RWKV_SUB

    @pl.when(j == 0)
    def _():
        st_s[...] = s0_ref[...]

    lane_c = lax.broadcasted_iota(jnp.int32, (C, LANES), 1)
    row_c = lax.broadcasted_iota(jnp.int32, (C, LANES), 0)
    lo = lane_c < C
    s_idx = lane_c % C
    eye2 = jnp.where(s_idx == row_c, 1.0, 0.0)
    blk_masks = [jnp.where((s_idx >> sh) == (row_c >> sh), 1.0, 0.0) for sh in range(INV_BASE_SHIFT, 7)]
    lane_f = lax.broadcasted_iota(jnp.int32, (LANES, LANES), 1)
    row_f = lax.broadcasted_iota(jnp.int32, (LANES, LANES), 0)
    eye_f = lane_f == row_f
    lo_f = lane_f < C
    bd_mask = (lane_f < C) == (row_f < C)
    tt = lax.broadcasted_iota(jnp.int32, (C, C), 0)
    ss = lax.broadcasted_iota(jnp.int32, (C, C), 1)

    def bd(m):
        return jnp.concatenate([jnp.where(lo, m, 0.0), jnp.where(lo, 0.0, m)], axis=0)

    def bd_swap(m):
        return jnp.concatenate([jnp.where(lo, 0.0, m), jnp.where(lo, m, 0.0)], axis=0)

    def bdot(x, y):
        return _dot(x.astype(BF16), y.astype(BF16))

    dirs = ((rf_ref, vf_ref, af_ref, wf_ref, kf_ref, bf_ref, yf_ref),
            (rb_ref, vb_ref, ab_ref, wb_ref, kb_ref, bb_ref, yb_ref))

    n_sub = max(1, min(tc // C, RWKV_STAGE_WIDTH // (2 * n_req * RWKV_PAIRS)))

    def sub_chunk(i, carry):
        chains = []
        for d in range(2):
            before = (ss < tt) if d == 0 else (ss > tt)
            cum = jnp.where(jnp.logical_or(before, ss == tt), 1.0, 0.0).astype(BF16)
            strict = jnp.where((s_idx < row_c) if d == 0 else (s_idx > row_c), 1.0, 0.0)
            incl = jnp.where((s_idx <= row_c) if d == 0 else (s_idx >= row_c), 1.0, 0.0)
            tri_mask = jnp.concatenate([strict, incl], axis=0)
            last = C - 1 if d == 0 else 0
            for k in range(n_sub):
                ii = i * n_sub + k
                base = pl.multiple_of(ii * C if d == 0 else tc - C - ii * C, C)
                for bi in range(n_req):
                    for pr in range(RWKV_PAIRS):
                        chains.append(dict(d=d, k=k, bi=bi, pr=pr, base=base, refs=dirs[d], cum=cum, tri=tri_mask,
                                           last=last, ls=slice(pr * LANES, (pr + 1) * LANES)))

        for c in chains:
            r_r, v_r, a_r, w_r, k_r, b_r, _ = c["refs"]
            r, v, a, lw, k, b = (ref[c["bi"], pl.ds(c["base"], C), c["ls"]].astype(F32)
                                 for ref in (r_r, v_r, a_r, w_r, k_r, b_r))
            l1 = lw.astype(BF16)
            e1 = lw - l1.astype(F32)
            l2 = e1.astype(BF16)
            l3 = (e1 - l2.astype(F32)).astype(BF16)
            c["G"] = _dot(c["cum"], l1) + _dot(c["cum"], l2) + _dot(c["cum"], l3)
            c["in"] = (r, v, a, lw, k, b)
        for c in chains:
            r, v, a, lw, k, b = c.pop("in")
            G = c.pop("G")
            g_inv = jnp.exp(-G)
            at = a * jnp.exp(G - lw)
            rt = r * jnp.exp(G)
            bt = b * g_inv
            kt = k * g_inv
            g_last = jnp.exp(G[c["last"]:c["last"] + 1, :])
            X = jnp.concatenate([at, rt], axis=0)
            c["X"] = X.astype(BF16)
            bt_b, kt_b = bt.astype(BF16), kt.astype(BF16)
            nt = (((1,), (1,)), ((), ()))
            c["P0"] = lax.dot_general(jnp.where(lo_f, X, 0.0).astype(BF16), jnp.concatenate([bt_b, kt_b], axis=0),
                                      nt, preferred_element_type=F32)
            c["P1"] = lax.dot_general(jnp.where(lo_f, 0.0, X).astype(BF16), jnp.concatenate([kt_b, bt_b], axis=0),
                                      nt, preferred_element_type=F32)
            c["v"] = v
            c["ygt"] = jnp.concatenate([bt * g_last, kt * g_last], axis=0).T.astype(BF16)
            c["g_col"] = jnp.broadcast_to(g_last, (LANES, LANES)).T
        for c in chains:
            P0 = c.pop("P0") * c["tri"]
            P1 = c.pop("P1") * c["tri"]
            ABRB = jnp.where(lo_f, P0, P1)
            c["AKRK"] = jnp.where(lo_f, P1, P0)
            c["AB"], c["RB"] = ABRB[:C], ABRB[C:]
        for c in chains:
            l8 = c["AB"] * blk_masks[0]
            c["T"] = eye2 + l8
            c["Lp"] = bdot(l8, bd(l8))
        for c in chains:
            R = bdot(jnp.concatenate([c["Lp"], c["T"]], axis=0), bd(c["Lp"]))
            c["T"] = c["T"] + R[C:]
            c["Lp"] = R[:C]
        for c in chains:
            c["T"] = c["T"] + bdot(c["T"], bd(c["Lp"]))
        for lvl in range(1, len(blk_masks)):
            for c in chains:
                c["Lp"] = bdot(c["AB"] * (blk_masks[lvl] - blk_masks[lvl - 1]), bd(c["T"]))
            for c in chains:
                c["T"] = c["T"] + bdot(c["T"], bd(c["Lp"]))
        for c in chains:
            c["VK"] = bdot(c["AKRK"], bd_swap(c["v"]))
        for k in range(n_sub):
            now = [c for c in chains if c["k"] == k]
            for c in now:
                c["S0"] = st_s[c["bi"], c["d"], c["pr"]]
                c["XS"] = _dot(c["X"], c["S0"].astype(BF16))
            for c in now:
                c["U"] = bdot(c["T"], bd(c["XS"][:C] + c["VK"][:C]))
            for c in now:
                y_r = c["refs"][6]
                y_r[c["bi"], pl.ds(c["base"], C), c["ls"]] = c["XS"][C:] + c["VK"][C:] + bdot(c["RB"], bd(c["U"]))
                uv = jnp.concatenate([c["U"], c["v"]], axis=0).astype(BF16)
                st_s[c["bi"], c["d"], c["pr"]] = jnp.where(bd_mask, c["g_col"] * c["S0"] + _dot(c["ygt"], uv), 0.0)
        return carry

    lax.fori_loop(0, tc // (C * n_sub), sub_chunk, 0)

    @pl.when(j == pl.num_programs(1) - 1)
    def _():
        for bi in range(n_req):
            for d in range(2):
                for pr in range(RWKV_PAIRS):
                    mt = st_s[bi, d, pr].T
                    sfin_ref[bi, d, pr] = jnp.where(row_f < C, mt, pltpu.roll(mt, C, 1))[:, :C]


def _rwkv_scan(r, v, na, lw, k, b, s0, n_b, seq):
    tc = min(seq, RWKV_CHUNK)
    n_c = seq // tc
    W = RWKV_WIDTH
    bb = min(RWKV_BB, n_b)
    blk = (bb, tc, W)
    fwd = pl.BlockSpec(blk, lambda g, j: (g, j, 0))
    bwd = pl.BlockSpec(blk, lambda g, j: (g, n_c - 1 - j, 0))
    fwd_d = pl.BlockSpec((None,) + blk, lambda g, j: (0, g, j, 0))
    bwd_d = pl.BlockSpec((None,) + blk, lambda g, j: (1, g, n_c - 1 - j, 0))
    st_blk = (bb, 2, RWKV_PAIRS, LANES, LANES)
    st_spec = pl.BlockSpec(st_blk, lambda g, j: (g, 0, 0, 0, 0))
    y_shape = jax.ShapeDtypeStruct((n_b, seq, W), F32)
    fin_blk = (bb, 2, RWKV_PAIRS, LANES, RWKV_HEAD)
    fin_spec = pl.BlockSpec(fin_blk, lambda g, j: (g, 0, 0, 0, 0))
    return pl.pallas_call(
        _rwkv_scan_kernel,
        grid=(n_b // bb, n_c),
        in_specs=[fwd, fwd, fwd, fwd_d, fwd_d, fwd_d, bwd, bwd, bwd, bwd_d, bwd_d, bwd_d, st_spec],
        out_specs=[fwd, bwd, fin_spec],
        out_shape=[y_shape, y_shape, jax.ShapeDtypeStruct((n_b,) + fin_blk[1:], F32)],
        scratch_shapes=[pltpu.VMEM(st_blk, F32)],
        compiler_params=_params("arbitrary", "arbitrary"),
        name="rwkv_scan",
    )(r, v, na, lw, k, b, r, v, na, lw, k, b, s0)


def _lru_kernel(x_ref, g_ref, h0_ref, cw_ref, cb_ref, wa_ref, ba_ref, wx_ref, bx_ref, lam_ref,
                o_ref, hfin_ref, a_s, u_s, h_s):
    seq = x_ref.shape[0]
    x = x_ref[...]
    row = lax.broadcasted_iota(jnp.int32, x.shape, 0)
    xm2 = jnp.where(row >= 2, pltpu.roll(x, 2, 0), 0.0)
    xm1 = jnp.where(row >= 1, pltpu.roll(x, 1, 0), 0.0)
    xp1 = jnp.where(row < seq - 1, pltpu.roll(x, seq - 1, 0), 0.0)
    xc = (xm2 * cw_ref[0:1, :] + xm1 * cw_ref[1:2, :] + x * cw_ref[2:3, :] + xp1 * cw_ref[3:4, :]
          + cb_ref[...])
    xb = xc.astype(BF16)
    for d in range(2):
        r_gate = _sigmoid(_dot(xb, wa_ref[d]) + ba_ref[d:d + 1, :])
        i_gate = _sigmoid(_dot(xb, wx_ref[d]) + bx_ref[d:d + 1, :])
        log_a = -LRU_C * r_gate * _softplus(-lam_ref[d:d + 1, :])
        a = jnp.exp(log_a)
        u = jnp.sqrt(1.0 - a * a) * (i_gate * xc)
        for sh in (1, 2, 4):
            if d == 0:
                a_n, u_n = pltpu.roll(a, sh, 0), pltpu.roll(u, sh, 0)
                m = (row % SUBLANES) >= sh
            else:
                a_n, u_n = pltpu.roll(a, seq - sh, 0), pltpu.roll(u, seq - sh, 0)
                m = (row % SUBLANES) < SUBLANES - sh
            u = jnp.where(m, a * u_n + u, u)
            a = jnp.where(m, a * a_n, a)
        a_s[d] = a
        u_s[d] = u

    n_t = seq // SUBLANES
    unroll = 4

    def body(i, carry):
        hf, hb = carry
        for k in range(unroll):
            base = pl.multiple_of((i * unroll + k) * SUBLANES, SUBLANES)
            h8 = a_s[0, pl.ds(base, SUBLANES), :] * hf + u_s[0, pl.ds(base, SUBLANES), :]
            h_s[0, pl.ds(base, SUBLANES), :] = h8
            hf = h8[SUBLANES - 1:SUBLANES]
            base = pl.multiple_of(seq - SUBLANES - (i * unroll + k) * SUBLANES, SUBLANES)
            h8 = a_s[1, pl.ds(base, SUBLANES), :] * hb + u_s[1, pl.ds(base, SUBLANES), :]
            h_s[1, pl.ds(base, SUBLANES), :] = h8
            hb = h8[0:1]
        return hf, hb

    hf, hb = lax.fori_loop(0, n_t // unroll, body, (h0_ref[0:1, :], h0_ref[1:2, :]))
    hfin_ref[...] = jnp.concatenate([hf, hb], axis=0)
    g = g_ref[...]
    gelu = 0.5 * g * (1.0 + jnp.tanh(0.7978845608028654 * (g + 0.044715 * (g * g * g))))
    o_ref[...] = ((h_s[0] + h_s[1]) * gelu).astype(BF16)


def _lru(xb, gb, h0, wts, n_b, seq):
    cw, cb, wa, ba, wx, bx, lam = wts
    C = LRU_WIDTH
    const = lambda shape: pl.BlockSpec(shape, lambda b: tuple(0 for _ in shape))
    tok = pl.BlockSpec((seq, C), lambda b: (b, 0))
    st = pl.BlockSpec((None, 2, C), lambda b: (b, 0, 0))
    return pl.pallas_call(
        _lru_kernel,
        grid=(n_b,),
        in_specs=[tok, tok, st, const((LRU_CONV_W, C)), const((1, C)), const((2, C, C)), const((2, C)),
                  const((2, C, C)), const((2, C)), const((2, C))],
        out_specs=[tok, st],
        out_shape=[jax.ShapeDtypeStruct((n_b * seq, C), BF16), jax.ShapeDtypeStruct((n_b, 2, C), F32)],
        scratch_shapes=[pltpu.VMEM((2, seq, C), F32)] * 3,
        compiler_params=_params("arbitrary"),
        name="lru",
    )(xb, gb, h0, cw, cb, wa, ba, wx, bx, lam)


def _outproj_kernel(x_ref, att_ref, yf_ref, yb_ref, bonus_ref, g_ref, lru_ref, mod_ref, n2_ref,
                    lnw_ref, lnb_ref, w_ref, rt_ref, bd_ref,
                    x1_ref, h2_ref, lg_ref):
    y = yf_ref[...] + yb_ref[...]
    parts = []
    for c in range(RWKV_WIDTH // LANES):
        yc = y[:, c * LANES:(c + 1) * LANES]
        mean = _seg_sum(yc, bd_ref[...]) * (1.0 / RWKV_HEAD)
        dev = yc - mean
        var = _seg_sum(dev * dev, bd_ref[...]) * (1.0 / RWKV_HEAD)
        parts.append(dev * lax.rsqrt(var + GN_EPS))
    yn = jnp.concatenate(parts, axis=-1) * lnw_ref[...] + lnb_ref[...]
    rwkv = (yn + bonus_ref[...]) * g_ref[...]
    o1 = ATT_WIDTH
    o2 = o1 + RWKV_WIDTH
    mixed = (_dot(att_ref[...].astype(BF16), w_ref[:o1, :])
             + _dot(rwkv.astype(BF16), w_ref[o1:o2, :])
             + _dot(lru_ref[...].astype(BF16), w_ref[o2:, :]))
    x1 = x_ref[...] + mod_ref[2:3, :] * mixed
    x1_ref[...] = x1
    rs = lax.rsqrt(jnp.mean(x1 * x1, axis=-1, keepdims=True) + RMS_EPS)
    h2 = (x1 * rs * n2_ref[...]) * (1.0 + mod_ref[4:5, :]) + mod_ref[3:4, :]
    h2_hi = h2.astype(BF16)
    h2_lo = (h2 - h2_hi.astype(F32)).astype(BF16)
    h2_ref[...] = h2_hi
    both = _dot(h2_hi, rt_ref[...])
    lg_ref[...] = both[:, :LANES] + both[:, LANES:] + _dot(h2_lo, rt_ref[:, :LANES])


def _outproj(x, att, yf, yb, bonus, g, lru, mod_l, norm2, lnw, lnb, w_out_bf, layer, router_pad, bd,
             seq, cond_row0):
    n_tok = x.shape[0]
    tm = 512 if seq >= 512 else seq * (512 // seq)
    per_seq = max(seq // tm, 1)
    n_t = n_tok // tm
    if cond_row0 == 0:
        row = lambda i: 0
    else:
        row = lambda i: cond_row0 + i // per_seq
    tok = lambda w: pl.BlockSpec((tm, w), lambda i: (i, 0))
    const = lambda shape: pl.BlockSpec(shape, lambda i: tuple(0 for _ in shape))
    W = RWKV_WIDTH
    return pl.pallas_call(
        _outproj_kernel,
        grid=(n_t,),
        in_specs=[
            tok(D_MODEL), tok(ATT_WIDTH), tok(W), tok(W), tok(W), tok(W), tok(LRU_WIDTH),
            pl.BlockSpec((None, N_MOD, D_MODEL), lambda i: (row(i), 0, 0)),
            const((1, D_MODEL)), const((1, W)), const((1, W)),
            pl.BlockSpec((None, D_MODEL, D_MODEL), lambda i: (layer, 0, 0)),
            const((D_MODEL, 2 * LANES)), const((LANES, LANES)),
        ],
        out_specs=[tok(D_MODEL), tok(D_MODEL), tok(LANES)],
        out_shape=[jax.ShapeDtypeStruct((n_tok, D_MODEL), F32),
                   jax.ShapeDtypeStruct((n_tok, D_MODEL), BF16),
                   jax.ShapeDtypeStruct((n_tok, LANES), F32)],
        compiler_params=_params("arbitrary"),
        name="outproj",
    )(x, att, yf, yb, bonus, g, lru, mod_l, norm2, lnw, lnb, w_out_bf, router_pad, bd)


PREFIX_BLOCK = 256
GATHER_ROWS = 1024


def _prefix_count(mask_f, tri):
    seq = mask_f.shape[0]
    outs = []
    carry = jnp.zeros((1, LANES), F32)
    for blk in range(seq // PREFIX_BLOCK):
        m = mask_f[blk * PREFIX_BLOCK:(blk + 1) * PREFIX_BLOCK]
        outs.append(_dot(tri, m.astype(BF16)) + carry)
        carry = carry + jnp.sum(m, axis=0, keepdims=True)
    return jnp.concatenate(outs, axis=0) if len(outs) > 1 else outs[0]


ROUTE_ROWS = 1024


def _route_kernel(lg_ref, h2_ref, tri_ref, xs_ref, slot_ref, aff_ref, *, cap, seq):
    rows = lg_ref.shape[0]
    n_req = rows // seq
    lane = lax.broadcasted_iota(jnp.int32, (rows, LANES), 1)
    real = lane < N_EXPERTS
    lg = jnp.where(real, lg_ref[...], -jnp.inf)
    m = jnp.max(lg, axis=-1, keepdims=True)
    e = jnp.exp(lg - m)
    aff = e / jnp.sum(e, axis=-1, keepdims=True)
    aff_ref[...] = aff

    a3 = aff.reshape(n_req, seq, LANES)
    a_min = jnp.min(a3, axis=1, keepdims=True)
    a_max = jnp.max(a3, axis=1, keepdims=True)

    def search(carry):
        lo, hi, _ = carry
        mid = 0.5 * (jnp.maximum(lo, a_min) + jnp.minimum(hi, a_max))
        inside = jnp.where(a3 > lo, jnp.where(a3 < hi, 1.0, 0.0), 0.0)
        upper = inside * jnp.where(a3 >= mid, 1.0, 0.0)
        up = jnp.min(jnp.where(upper > 0.0, a3, BRACKET_HI), axis=1, keepdims=True)
        dn = jnp.max(jnp.where(inside - upper > 0.0, a3, BRACKET_LO), axis=1, keepdims=True)
        pivot = jnp.where(up < BRACKET_HI, up, dn)
        found = jnp.where(up < BRACKET_HI, 1.0, jnp.where(dn > BRACKET_LO, 1.0, 0.0))
        cnt = jnp.sum(jnp.where(a3 >= pivot, 1.0, 0.0), axis=1, keepdims=True)
        take_lo = found * jnp.where(cnt >= cap, 1.0, 0.0)
        take_hi = found - take_lo
        return (jnp.where(take_lo > 0.0, pivot, lo), jnp.where(take_hi > 0.0, pivot, hi), found)

    shape = (n_req, 1, LANES)
    init = (jnp.full(shape, BRACKET_LO, F32), jnp.full(shape, BRACKET_HI, F32), jnp.ones(shape, F32))
    thr = lax.while_loop(lambda c: jnp.max(c[2]) > 0.0, search, init)[0]
    gt3 = jnp.where(a3 > thr, 1.0, 0.0)
    eq3 = jnp.where(a3 == thr, 1.0, 0.0)
    need3 = cap - jnp.sum(gt3, axis=1, keepdims=True)
    tri = tri_ref[...]
    c_iota = lax.broadcasted_iota(jnp.int32, (cap, seq), 0).astype(F32)
    real_seq = lax.broadcasted_iota(jnp.int32, (seq, LANES), 1) < N_EXPERTS
    group = max(1, min(N_EXPERTS, GATHER_ROWS // cap))
    for r in range(n_req):
        tok = slice(r * seq, (r + 1) * seq)
        gt, eq = gt3[r], eq3[r]
        sel = jnp.where(real_seq, gt + eq * jnp.where(_prefix_count(eq, tri) < need3[r], 1.0, 0.0), 0.0)
        slot = jnp.where(sel > 0.0, _prefix_count(sel, tri), -1.0)
        slot_ref[tok, :] = slot
        slot_t = slot.T
        h2 = h2_ref[tok, :]
        for g0 in range(0, N_EXPERTS, group):
            onehot = jnp.concatenate(
                [jnp.where(c_iota == slot_t[ex:ex + 1, :], 1.0, 0.0) for ex in range(g0, g0 + group)], axis=0)
            picked = _dot(onehot.astype(BF16), h2)
            for k in range(group):
                xs_ref[g0 + k, r * cap:(r + 1) * cap, :] = picked[k * cap:(k + 1) * cap].astype(BF16)


def _route(logits, h2, tri, n_b, seq):
    cap = EC_FACTOR * seq // N_EXPERTS
    n_req = max(1, min(n_b, ROUTE_ROWS // seq))
    tok = lambda w: pl.BlockSpec((n_req * seq, w), lambda b: (b, 0))
    return pl.pallas_call(
        functools.partial(_route_kernel, cap=cap, seq=seq),
        grid=(n_b // n_req,),
        in_specs=[tok(LANES), tok(D_MODEL), pl.BlockSpec((PREFIX_BLOCK, PREFIX_BLOCK), lambda b: (0, 0))],
        out_specs=[pl.BlockSpec((N_EXPERTS, n_req * cap, D_MODEL), lambda b: (0, b, 0)), tok(LANES), tok(LANES)],
        out_shape=[jax.ShapeDtypeStruct((N_EXPERTS, n_b * cap, D_MODEL), BF16),
                   jax.ShapeDtypeStruct((n_b * seq, LANES), F32),
                   jax.ShapeDtypeStruct((n_b * seq, LANES), F32)],
        compiler_params=_params("arbitrary"),
        name="route",
    )(logits, h2, tri)


EXPERT_RB = 256


def _expert_kernel(xa_ref, xb_ref, wg_ref, wu_ref, wd_ref, ya_ref, yb_ref, wg_s, wu_s, wd_s):
    wg_s[...] = wg_ref[...].astype(BF16)
    wu_s[...] = wu_ref[...].astype(BF16)
    wd_s[...] = wd_ref[...].astype(BF16)
    for x_ref, y_ref in ((xa_ref, ya_ref), (xb_ref, yb_ref)):
        for rb in range(x_ref.shape[0] // EXPERT_RB):
            rows = slice(rb * EXPERT_RB, (rb + 1) * EXPERT_RB)
            x = x_ref[rows, :]
            a = _dot(x, wg_s[...])
            u = _dot(x, wu_s[...])
            hid = (a * _sigmoid(a) * u).astype(BF16)
            y_ref[rows, :] = _dot(hid, wd_s[...]).astype(BF16)


def _experts(xs_a, xs_b, w_gate, w_up, w_down, layer):
    ma, mb = xs_a.shape[1], xs_b.shape[1]
    xspec = lambda m: pl.BlockSpec((None, m, D_MODEL), lambda e: (e, 0, 0))
    return pl.pallas_call(
        _expert_kernel,
        grid=(N_EXPERTS,),
        in_specs=[
            xspec(ma), xspec(mb),
            pl.BlockSpec((None, None, D_MODEL, EXPERT_FF), lambda e: (layer, e, 0, 0)),
            pl.BlockSpec((None, None, D_MODEL, EXPERT_FF), lambda e: (layer, e, 0, 0)),
            pl.BlockSpec((None, None, EXPERT_FF, D_MODEL), lambda e: (layer, e, 0, 0)),
        ],
        out_specs=[xspec(ma), xspec(mb)],
        out_shape=[jax.ShapeDtypeStruct(xs_a.shape, BF16), jax.ShapeDtypeStruct(xs_b.shape, BF16)],
        scratch_shapes=[pltpu.VMEM((D_MODEL, EXPERT_FF), BF16), pltpu.VMEM((D_MODEL, EXPERT_FF), BF16),
                        pltpu.VMEM((EXPERT_FF, D_MODEL), BF16)],
        compiler_params=_params("arbitrary"),
        name="experts",
    )(xs_a, xs_b, w_gate, w_up, w_down)


COMBINE_FUSED_COLS = 512


def _combine_kernel(*refs, cap, fused):
    if fused:
        y_ref, slot_ref, aff_ref, x1_ref, mod_ref, ex_ref, o_ref = refs
    else:
        y_ref, slot_ref, aff_ref, x1_ref, mod_ref, o_ref = refs
    seq = x1_ref.shape[0]
    slot = slot_ref[...]
    aff = aff_ref[...]
    if fused:
        spread = ex_ref[...]
        a1 = aff.astype(BF16)
        r1 = aff - a1.astype(F32)
        a2 = r1.astype(BF16)
        a3 = (r1 - a2.astype(F32)).astype(BF16)
        slot_x = _dot(slot.astype(BF16), spread)
        aff_x = _dot(a1, spread) + _dot(a2, spread) + _dot(a3, spread)
        cols = N_EXPERTS * cap
        c_pat = (lax.broadcasted_iota(jnp.int32, (seq, cols), 1) % cap).astype(F32)
        w_hi, w_lo = _split(jnp.where(slot_x == c_pat, aff_x, 0.0))
        y2 = y_ref[...].reshape(cols, D_MODEL)
        acc = _dot(w_hi, y2) + _dot(w_lo, y2)
    else:
        c_iota = lax.broadcasted_iota(jnp.int32, (seq, cap), 1).astype(F32)
        acc = jnp.zeros((seq, D_MODEL), F32)
        for ex in range(N_EXPERTS):
            onehot = jnp.where(slot[:, ex:ex + 1] == c_iota, 1.0, 0.0).astype(BF16)
            acc = acc + aff[:, ex:ex + 1] * _dot(onehot, y_ref[ex])
    o_ref[...] = x1_ref[...] + mod_ref[5:6, :] * acc


def _combine(y, slot, aff, x1, mod_l, n_b, seq, cond_row0):
    cap = EC_FACTOR * seq // N_EXPERTS
    fused = N_EXPERTS * cap <= COMBINE_FUSED_COLS
    if cond_row0 == 0:
        row = lambda b: 0
    else:
        row = lambda b: cond_row0 + b
    tm = min(seq, 512)
    n_t = seq // tm
    tok = lambda w: pl.BlockSpec((tm, w), lambda b, i: (b * n_t + i, 0))
    in_specs = [
        pl.BlockSpec((N_EXPERTS, cap, D_MODEL), lambda b, i: (0, b, 0)),
        tok(LANES), tok(LANES), tok(D_MODEL),
        pl.BlockSpec((None, N_MOD, D_MODEL), lambda b, i: (row(b), 0, 0)),
    ]
    args = [y, slot, aff, x1, mod_l]
    if fused:
        cols = N_EXPERTS * cap
        spread = (jnp.arange(LANES)[:, None] == (jnp.arange(cols) // cap)[None, :]).astype(BF16)
        in_specs.append(pl.BlockSpec((LANES, cols), lambda b, i: (0, 0)))
        args.append(spread)
    return pl.pallas_call(
        functools.partial(_combine_kernel, cap=cap, fused=fused),
        grid=(n_b, n_t),
        in_specs=in_specs,
        out_specs=tok(D_MODEL),
        out_shape=jax.ShapeDtypeStruct((n_b * seq, D_MODEL), F32),
        compiler_params=_params("arbitrary", "arbitrary"),
        name="combine_fused" if fused else "combine",
    )(*args)


def _rope_tables(seq):
    t = jnp.arange(seq, dtype=jnp.int32)
    row = (t // GRID_W).astype(F32)
    col = (t % GRID_W).astype(F32)
    half = HEAD_DIM // 2
    inv = ROPE_THETA ** (-jnp.arange(0, half, 2, dtype=F32) / half)
    lane = jnp.arange(LANES)
    u = lane % HEAD_DIM
    pos = jnp.where((u // half)[None, :] == 0, row[:, None], col[:, None])
    ang = pos * inv[(u % half) % (half // 2)][None, :]
    first = ((u % half) < half // 2)[None, :]
    sin = jnp.sin(ang)
    return jnp.cos(ang), jnp.where(first, -sin, 0.0), jnp.where(first, 0.0, sin)


def _block_diag(w):
    n, k, _ = w.shape
    eye = jnp.eye(n, dtype=w.dtype)
    return (eye[:, None, :, None] * w[:, :, None, :]).reshape(n * k, n * k)


def _split_weight(w):
    return jnp.stack(_split(w))


def _pad_rows(w, offset, total=LANES):
    return jnp.zeros((total, w.shape[1]), w.dtype).at[offset:offset + w.shape[0]].set(w)


def _pack_state(s):
    b = s.shape[0]
    st = s.reshape(b, 2, RWKV_PAIRS, 2, RWKV_HEAD, RWKV_HEAD).transpose(0, 1, 2, 3, 5, 4)
    eye = jnp.eye(2, dtype=s.dtype)
    out = st[:, :, :, :, :, None, :] * eye[None, None, None, :, None, :, None]
    return out.reshape(b, 2, RWKV_PAIRS, LANES, LANES)


def _unpack_state(s):
    return s.reshape(s.shape[0], 2, RWKV_HEADS, RWKV_HEAD, RWKV_HEAD)


def kernel(x_prompt, x_sample, cache_k, cache_v, state_rwkv, state_lru, c, c_ctx, w_ada, b_ada, norm1, norm2, w_in, w_out, q_norm, k_norm, rwkv_mu, rwkv_w0, rwkv_w_up, rwkv_a0, rwkv_a_up, rwkv_g_up, rwkv_k_k, rwkv_k_a, rwkv_r_k, rwkv_ln_w, rwkv_ln_b, lru_conv_w, lru_conv_b, lru_wa, lru_ba, lru_wx, lru_bx, lru_lambda, router, exp_w_gate, exp_w_up, exp_w_down):
    n_ctx, seq_ctx, _ = x_prompt.shape
    n_lat, seq_lat, _ = x_sample.shape
    past = cache_k.shape[2]
    assert n_lat + 1 <= COND_ROWS

    cond = jnp.zeros((COND_ROWS, D_MODEL), F32).at[0].set(c_ctx).at[1:1 + n_lat].set(c)
    mod = _ada(cond, w_ada, b_ada.reshape(DEPTH, 1, N_MOD * D_MODEL))
    mod = mod.reshape(DEPTH, COND_ROWS, N_MOD, D_MODEL)

    w_in_bf = w_in.astype(BF16)
    w_out_bf = w_out.astype(BF16)
    lane = jnp.arange(LANES)
    bd = (lane[:, None] // HEAD_DIM == lane[None, :] // HEAD_DIM).astype(BF16)
    pidx = jnp.arange(PREFIX_BLOCK)
    tri = (pidx[None, :] < pidx[:, None]).astype(BF16)
    rope_tabs = _rope_tables(seq_lat)

    paths = [
        dict(x=x_prompt.reshape(n_ctx * seq_ctx, D_MODEL), n_b=n_ctx, seq=seq_ctx, row0=0, rope=None),
        dict(x=x_sample.reshape(n_lat * seq_lat, D_MODEL), n_b=n_lat, seq=seq_lat, row0=1, rope=rope_tabs),
    ]
    new_k, new_v, new_sr, new_sl = [], [], [], []
    for l in range(DEPTH):
        mod_l = mod[l]
        qn = jnp.tile(q_norm[l], LANES // HEAD_DIM)[None, :]
        kn = jnp.tile(k_norm[l], LANES // HEAD_DIM)[None, :]
        prep_w = (
            rwkv_mu[l][None, :], rwkv_k_k[l][None, :], rwkv_k_a[l][None, :],
            rwkv_r_k[l].reshape(1, RWKV_WIDTH), rwkv_w0[l], rwkv_a0[l],
            jnp.stack([_split_weight(_pad_rows(rwkv_w_up[l, d], 0)) for d in range(2)]),
            jnp.stack([_split_weight(_pad_rows(rwkv_a_up[l, d], RWKV_DECAY_LORA)) for d in range(2)]),
            _split_weight(_pad_rows(rwkv_g_up[l], RWKV_DECAY_LORA + RWKV_AAA_LORA)),
        )
        lru_w = (
            lru_conv_w[l], lru_conv_b[l][None, :],
            jnp.stack([_block_diag(lru_wa[l, d]) for d in range(2)]).astype(BF16), lru_ba[l],
            jnp.stack([_block_diag(lru_wx[l, d]) for d in range(2)]).astype(BF16), lru_bx[l],
            lru_lambda[l],
        )
        router_pad = jnp.concatenate(_split(jnp.zeros((D_MODEL, LANES), F32).at[:, :N_EXPERTS].set(router[l])), axis=1)
        mids = []
        for pi, pth in enumerate(paths):
            n_b, seq, row0 = pth["n_b"], pth["seq"], pth["row0"]
            latent = pth["rope"] is not None
            outs = _inproj(pth["x"], mod_l, norm1[l][None, :], w_in_bf, l, qn, kn, bd, pth["rope"], seq, row0)
            if latent:
                q, k_n, k_att, v, p_rwkv, lru_x, lru_g = outs
                cache = (cache_k[:, l].reshape(n_b, past, KV_WIDTH), cache_v[:, l].reshape(n_b, past, KV_WIDTH))
                s0 = _pack_state(state_rwkv[:, l])
                h0 = state_lru[:, l]
            else:
                q, k_n, v, p_rwkv, lru_x, lru_g = outs
                k_att, cache = k_n, None
                s0 = jnp.zeros((n_b, 2, RWKV_PAIRS, LANES, LANES), F32)
                h0 = jnp.zeros((n_b, 2, LRU_WIDTH), F32)
                new_k.append(k_n.reshape(n_b, seq, ATT_KV_HEADS, HEAD_DIM))
                new_v.append(v.reshape(n_b, seq, ATT_KV_HEADS, HEAD_DIM))
            att = _attention(q, k_att, v, n_b, seq, cache)
            r, vv, na, w, kd, b, g, bonus = _rwkv_prep(p_rwkv.reshape(n_b, seq, RWKV_COLS), n_b, seq, prep_w, bd)
            W = RWKV_WIDTH
            yf, yb, s_fin = _rwkv_scan(r.reshape(n_b, seq, W), vv.reshape(n_b, seq, W), na.reshape(n_b, seq, W),
                                       w.reshape(2, n_b, seq, W), kd.reshape(2, n_b, seq, W),
                                       b.reshape(2, n_b, seq, W), s0, n_b, seq)
            lru_out, h_fin = _lru(lru_x, lru_g, h0, lru_w, n_b, seq)
            if not latent:
                new_sr.append(_unpack_state(s_fin))
                new_sl.append(h_fin)
            x1, h2, logits = _outproj(pth["x"], att, yf.reshape(n_b * seq, W), yb.reshape(n_b * seq, W), bonus, g,
                                      lru_out, mod_l, norm2[l][None, :], rwkv_ln_w[l][None, :],
                                      rwkv_ln_b[l][None, :], w_out_bf, l, router_pad, bd, seq, row0)
            xs, slot, aff = _route(logits, h2, tri, n_b, seq)
            mids.append((xs, slot, aff, x1))
        y_a, y_b = _experts(mids[0][0], mids[1][0], exp_w_gate, exp_w_up, exp_w_down, l)
        for pth, (xs, slot, aff, x1), y in zip(paths, mids, (y_a, y_b)):
            pth["x"] = _combine(y, slot, aff, x1, mod_l, pth["n_b"], pth["seq"], pth["row0"])

    y_prompt = paths[0]["x"].reshape(n_ctx, seq_ctx, D_MODEL)
    y_sample = paths[1]["x"].reshape(n_lat, seq_lat, D_MODEL)
    return (y_prompt, y_sample, jnp.stack(new_k, axis=1), jnp.stack(new_v, axis=1),
            jnp.stack(new_sr, axis=1), jnp.stack(new_sl, axis=1))
```

```python
import functools

import jax
import jax.numpy as jnp
from jax import lax
from jax.experimental import pallas as pl
from jax.experimental.pallas import tpu as pltpu

F32 = jnp.float32
BF16 = jnp.bfloat16
HIGHEST = lax.Precision.HIGHEST

D_MODEL = 1024
DEPTH = 2
GRID_W = 64
ATT_HEADS = 8
ATT_KV_HEADS = 2
HEAD_DIM = 64
ATT_WIDTH = ATT_HEADS * HEAD_DIM
KV_WIDTH = ATT_KV_HEADS * HEAD_DIM
ROPE_THETA = 10000.0
RWKV_HEADS = 4
RWKV_HEAD = 64
RWKV_WIDTH = RWKV_HEADS * RWKV_HEAD
RWKV_DECAY_LORA = 32
RWKV_AAA_LORA = 32
RWKV_GATE_LORA = 64
RWKV_COLS = 3 * RWKV_WIDTH + RWKV_DECAY_LORA + RWKV_AAA_LORA + RWKV_GATE_LORA
GN_EPS = 64e-5
LRU_BLOCKS = 4
LRU_BLOCK = 64
LRU_WIDTH = LRU_BLOCKS * LRU_BLOCK
LRU_CONV_W = 4
LRU_C = 8.0
IN_COLS = ATT_WIDTH + 2 * KV_WIDTH + RWKV_COLS + 2 * LRU_WIDTH
N_EXPERTS = 16
EC_FACTOR = 2
EXPERT_FF = 1024
RMS_EPS = 1e-6

LANES = 128
SUBLANES = 8
VMEM_LIMIT = 56 * 1024 * 1024
N_MOD = 6
COND_ROWS = 8
BRACKET_LO, BRACKET_HI = -1.0, 2.0


def _params(*sem):
    return pltpu.CompilerParams(dimension_semantics=sem, vmem_limit_bytes=VMEM_LIMIT)


def _dot(a, b, precision=None):
    return jnp.dot(a, b, preferred_element_type=F32, precision=precision)


def _sigmoid(x):
    return 1.0 / (1.0 + jnp.exp(-x))


def _softplus(x):
    return jnp.maximum(x, 0.0) + jnp.log1p(jnp.exp(-jnp.abs(x)))


def _split(x):
    hi = x.astype(BF16)
    return hi, (x - hi.astype(F32)).astype(BF16)


def _dot_split(x, w_ref):
    hi, lo = _split(x)
    return _dot(hi, w_ref[0]) + _dot(hi, w_ref[1]) + _dot(lo, w_ref[0])


def _seg_sum(x, ones_bd):
    return _dot(x.astype(BF16), ones_bd)


def _ada_kernel(c_ref, w_ref, b_ref, o_ref):
    c = c_ref[...]
    s = c * _sigmoid(c)
    o_ref[...] = _dot(s, w_ref[...], HIGHEST) + b_ref[...]


def _ada(cond, w_ada, b_ada):
    n_l = w_ada.shape[0]
    tn = 1536
    n_t = N_MOD * D_MODEL // tn
    return pl.pallas_call(
        _ada_kernel,
        grid=(n_l, n_t),
        in_specs=[
            pl.BlockSpec((COND_ROWS, D_MODEL), lambda l, j: (0, 0)),
            pl.BlockSpec((None, D_MODEL, tn), lambda l, j: (l, 0, j)),
            pl.BlockSpec((None, 1, tn), lambda l, j: (l, 0, j)),
        ],
        out_specs=pl.BlockSpec((None, COND_ROWS, tn), lambda l, j: (l, 0, j)),
        out_shape=jax.ShapeDtypeStruct((n_l, COND_ROWS, N_MOD * D_MODEL), F32),
        compiler_params=_params("arbitrary", "arbitrary"),
        name="ada",
    )(cond, w_ada, b_ada)


def _head_rms(x, gain, ones_bd):
    ms = _seg_sum(x * x, ones_bd) * (1.0 / HEAD_DIM)
    return x * lax.rsqrt(ms + RMS_EPS) * gain


def _rope(x, cos, sin_up, sin_dn):
    return x * cos + pltpu.roll(x, LANES - 16, 1) * sin_up + pltpu.roll(x, 16, 1) * sin_dn


def _inproj_kernel(*refs, rope):
    if rope:
        (x_ref, mod_ref, n1_ref, w_ref, qn_ref, kn_ref, bd_ref, cos_ref, su_ref, sd_ref,
         q_ref, ko_ref, ka_ref, v_ref, pr_ref, lx_ref, lg_ref) = refs
    else:
        (x_ref, mod_ref, n1_ref, w_ref, qn_ref, kn_ref, bd_ref,
         q_ref, ko_ref, v_ref, pr_ref, lx_ref, lg_ref) = refs
    x = x_ref[...]
    rs = lax.rsqrt(jnp.mean(x * x, axis=-1, keepdims=True) + RMS_EPS)
    h = (x * rs * n1_ref[...]) * (1.0 + mod_ref[1:2, :]) + mod_ref[0:1, :]
    p = _dot(h.astype(BF16), w_ref[...])
    bd = bd_ref[...]
    for c in range(ATT_WIDTH // LANES):
        qc = _head_rms(p[:, c * LANES:(c + 1) * LANES], qn_ref[...], bd)
        if rope:
            qc = _rope(qc, cos_ref[...], su_ref[...], sd_ref[...])
        q_ref[:, c * LANES:(c + 1) * LANES] = (qc * (HEAD_DIM ** -0.5)).astype(BF16)
    o = ATT_WIDTH
    kc = _head_rms(p[:, o:o + KV_WIDTH], kn_ref[...], bd)
    ko_ref[...] = kc
    if rope:
        ka_ref[...] = _rope(kc, cos_ref[...], su_ref[...], sd_ref[...]).astype(BF16)
    o += KV_WIDTH
    v_ref[...] = p[:, o:o + KV_WIDTH]
    o += KV_WIDTH
    pr_ref[...] = p[:, o:o + RWKV_COLS]
    o += RWKV_COLS
    lx_ref[...] = p[:, o:o + LRU_WIDTH]
    o += LRU_WIDTH
    lg_ref[...] = p[:, o:o + LRU_WIDTH]


def _inproj(x, mod_l, norm1, w_in_bf, layer, qn, kn, bd, rope_tabs, seq, cond_row0):
    n_tok = x.shape[0]
    tm = 512 if seq >= 512 else seq * (512 // seq)
    per_seq = max(seq // tm, 1)
    n_t = n_tok // tm
    rope = rope_tabs is not None
    if cond_row0 == 0:
        row = lambda i: 0
    else:
        row = lambda i: cond_row0 + i // per_seq
    tok = lambda w: pl.BlockSpec((tm, w), lambda i: (i, 0))
    const = lambda shape: pl.BlockSpec(shape, lambda i: tuple(0 for _ in shape))
    in_specs = [
        tok(D_MODEL),
        pl.BlockSpec((None, N_MOD, D_MODEL), lambda i: (row(i), 0, 0)),
        const((1, D_MODEL)),
        pl.BlockSpec((None, D_MODEL, IN_COLS), lambda i: (layer, 0, 0)),
        const((1, LANES)), const((1, LANES)), const((LANES, LANES)),
    ]
    args = [x, mod_l, norm1, w_in_bf, qn, kn, bd]
    outs = [(ATT_WIDTH, BF16), (KV_WIDTH, F32)]
    if rope:
        in_specs += [pl.BlockSpec((tm, LANES), lambda i: (i % per_seq, 0))] * 3
        args += list(rope_tabs)
        outs.append((KV_WIDTH, BF16))
    outs += [(KV_WIDTH, F32), (RWKV_COLS, F32), (LRU_WIDTH, F32), (LRU_WIDTH, F32)]
    return pl.pallas_call(
        functools.partial(_inproj_kernel, rope=rope),
        grid=(n_t,),
        in_specs=in_specs,
        out_specs=[tok(w) for w, _ in outs],
        out_shape=[jax.ShapeDtypeStruct((n_tok, w), dt) for w, dt in outs],
        compiler_params=_params("arbitrary"),
        name="inproj_rope" if rope else "inproj",
    )(*args)


def _attn_kernel(*refs, has_cache):
    if has_cache:
        q_ref, k_ref, v_ref, ck_ref, cv_ref, o_ref, kn_s, ks_s, vn_s, vs_s = refs
    else:
        q_ref, k_ref, v_ref, o_ref, kn_s, ks_s, vn_s, vs_s = refs

    @pl.when(pl.program_id(1) == 0)
    def _():
        k = k_ref[...].astype(F32)
        v = v_ref[...]
        if has_cache:
            k = jnp.concatenate([ck_ref[...], k], axis=0)
            v = jnp.concatenate([cv_ref[...], v], axis=0)
        kn_s[...] = k.astype(BF16)
        ks_s[...] = pltpu.roll(k, HEAD_DIM, 1).astype(BF16)
        vn_s[...] = v.astype(BF16)
        vs_s[...] = pltpu.roll(v, HEAD_DIM, 1).astype(BF16)

    tq = q_ref.shape[0]
    lo = lax.broadcasted_iota(jnp.int32, (tq, LANES), 1) < HEAD_DIM
    rep = ATT_HEADS // ATT_KV_HEADS
    for c in range(ATT_WIDTH // LANES):
        qc = q_ref[:, c * LANES:(c + 1) * LANES].astype(F32)
        halves = []
        for half in range(2):
            g = (2 * c + half) // rep
            qm = jnp.where(lo if half == 0 else jnp.logical_not(lo), qc, 0.0).astype(BF16)
            k_s, v_s = (kn_s, vn_s) if half == g else (ks_s, vs_s)
            s = lax.dot_general(qm, k_s[...], (((1,), (1,)), ((), ())), preferred_element_type=F32)
            m = jnp.max(s, axis=-1, keepdims=True)
            e = jnp.exp(s - m)
            l = jnp.sum(e, axis=-1, keepdims=True)
            halves.append(_dot(e.astype(BF16), v_s[...]) / l)
        o_ref[:, c * LANES:(c + 1) * LANES] = jnp.where(lo, halves[0], halves[1]).astype(BF16)


def _attention(q, k, v, n_b, seq, cache=None):
    tq = 256
    n_q = seq // tq
    has_cache = cache is not None
    past = cache[0].shape[1] if has_cache else 0
    in_specs = [
        pl.BlockSpec((tq, ATT_WIDTH), lambda b, i: (b * n_q + i, 0)),
        pl.BlockSpec((seq, KV_WIDTH), lambda b, i: (b, 0)),
        pl.BlockSpec((seq, KV_WIDTH), lambda b, i: (b, 0)),
    ]
    args = [q, k, v]
    if has_cache:
        in_specs += [pl.BlockSpec((None, past, KV_WIDTH), lambda b, i: (b, 0, 0))] * 2
        args += list(cache)
    return pl.pallas_call(
        functools.partial(_attn_kernel, has_cache=has_cache),
        grid=(n_b, n_q),
        in_specs=in_specs,
        out_specs=pl.BlockSpec((tq, ATT_WIDTH), lambda b, i: (b * n_q + i, 0)),
        out_shape=jax.ShapeDtypeStruct((n_b * seq, ATT_WIDTH), BF16),
        scratch_shapes=[pltpu.VMEM((past + seq, KV_WIDTH), BF16)] * 4,
        compiler_params=_params("arbitrary", "arbitrary"),
        name="attn_cache" if has_cache else "attn",
    )(*args)


def _rwkv_prep_kernel(p_ref, mu_ref, kk_ref, ka_ref, rk_ref, w0_ref, a0_ref, wup_ref, aup_ref,
                      gup_ref, bd_ref,
                      r_ref, v_ref, na_ref, w_ref, k_ref, b_ref, g_ref, bonus_ref):
    tm = r_ref.shape[0]
    seq = p_ref.shape[0]
    i = pl.program_id(1)
    n_t = pl.num_programs(1)
    start = pl.multiple_of(i * tm, tm)
    cur = p_ref[pl.ds(start, tm), :]
    prev_base = pl.multiple_of(jnp.maximum(start - SUBLANES, 0), SUBLANES)
    next_base = pl.multiple_of(jnp.minimum(start + tm, seq - SUBLANES), SUBLANES)
    prev_row = p_ref[pl.ds(prev_base, SUBLANES), :][SUBLANES - 1:SUBLANES]
    next_row = p_ref[pl.ds(next_base, SUBLANES), :][0:1]
    prev_row = jnp.where(i > 0, prev_row, 0.0)
    next_row = jnp.where(i < n_t - 1, next_row, 0.0)
    row = lax.broadcasted_iota(jnp.int32, cur.shape, 0)
    prev = jnp.where(row == 0, prev_row, pltpu.roll(cur, 1, 0))
    nxt = jnp.where(row == tm - 1, next_row, pltpu.roll(cur, tm - 1, 0))
    ps = cur + mu_ref[...] * (0.5 * (prev + nxt) - cur)

    W = RWKV_WIDTH
    r = ps[:, :W]
    k = ps[:, W:2 * W]
    v = ps[:, 2 * W:3 * W]
    lora = ps[:, 3 * W:]
    r_ref[...] = r.astype(BF16)
    v_ref[...] = v.astype(BF16)
    g_ref[...] = _dot_split(_sigmoid(lora), gup_ref)
    lora_t = jnp.tanh(lora)
    kk = k * kk_ref[...]
    kk_parts = []
    for c in range(W // LANES):
        kc = kk[:, c * LANES:(c + 1) * LANES]
        nrm = jnp.sqrt(_seg_sum(kc * kc, bd_ref[...]))
        kk_parts.append(kc / jnp.maximum(nrm, 1e-12))
    kk = jnp.concatenate(kk_parts, axis=-1)
    na_ref[...] = (-kk).astype(BF16)
    bonus = None
    for d in range(2):
        w_log = -_softplus(-(w0_ref[d:d + 1, :] + _dot_split(lora_t, wup_ref.at[d]))) - 0.5
        w_ref[d] = -jnp.exp(w_log)
        a_rate = _sigmoid(a0_ref[d:d + 1, :] + _dot_split(lora, aup_ref.at[d]))
        kd = k * (1.0 + (a_rate - 1.0) * ka_ref[...])
        k_ref[d] = kd.astype(BF16)
        b_ref[d] = (kk * a_rate).astype(BF16)
        rkr = r * kd * rk_ref[...]
        parts = [_seg_sum(rkr[:, c * LANES:(c + 1) * LANES], bd_ref[...]) for c in range(W // LANES)]
        bd_term = jnp.concatenate(parts, axis=-1) * v
        bonus = bd_term if bonus is None else bonus + bd_term
    bonus_ref[...] = bonus


def _rwkv_prep(p, n_b, seq, wts, bd):
    mu, k_k, k_a, r_k, w0, a0, wup, aup, gup = wts
    tm = min(seq, 512)
    n_t = seq // tm
    W = RWKV_WIDTH
    const = lambda shape: pl.BlockSpec(shape, lambda b, i: tuple(0 for _ in shape))
    tok = pl.BlockSpec((tm, W), lambda b, i: (b * n_t + i, 0))
    tok2 = pl.BlockSpec((2, tm, W), lambda b, i: (0, b * n_t + i, 0))
    one = jax.ShapeDtypeStruct((n_b * seq, W), F32)
    two = jax.ShapeDtypeStruct((2, n_b * seq, W), F32)
    one_bf = jax.ShapeDtypeStruct((n_b * seq, W), BF16)
    two_bf = jax.ShapeDtypeStruct((2, n_b * seq, W), BF16)
    return pl.pallas_call(
        _rwkv_prep_kernel,
        grid=(n_b, n_t),
        in_specs=[
            pl.BlockSpec((None, seq, RWKV_COLS), lambda b, i: (b, 0, 0)),
            const((1, RWKV_COLS)), const((1, W)), const((1, W)), const((1, W)),
            const((2, W)), const((2, W)),
            const((2, 2, LANES, W)), const((2, 2, LANES, W)), const((2, LANES, W)),
            const((LANES, LANES)),
        ],
        out_specs=[tok, tok, tok, tok2, tok2, tok2, tok, tok],
        out_shape=[one_bf, one_bf, one_bf, two, two_bf, two_bf, one, one],
        compiler_params=_params("arbitrary", "arbitrary"),
        name="rwkv_prep",
    )(p, mu, k_k, k_a, r_k, w0, a0, wup, aup, gup, bd)


RWKV_BB = 4
RWKV_PAIRS = RWKV_HEADS * RWKV_HEAD // LANES
RWKV_CHUNK = 256
RWKV_SUB = 64
RWKV_STAGE_WIDTH = 32
INV_BASE_SHIFT = 3


def _rwkv_scan_kernel(rf_ref, vf_ref, af_ref, wf_ref, kf_ref, bf_ref,
                      rb_ref, vb_ref, ab_ref, wb_ref, kb_ref, bb_ref, s0_ref,
                      yf_ref, yb_ref, sfin_ref, st_s):
    j = pl.program_id(1)
    n_req, tc = rf_ref.shape[0], rf_ref.shape[1]
    C = RWKV_SUB

    @pl.when(j == 0)
    def _():
        st_s[...] = s0_ref[...]

    lane_c = lax.broadcasted_iota(jnp.int32, (C, LANES), 1)
    row_c = lax.broadcasted_iota(jnp.int32, (C, LANES), 0)
    lo = lane_c < C
    s_idx = lane_c % C
    eye2 = jnp.where(s_idx == row_c, 1.0, 0.0)
    blk_masks = [jnp.where((s_idx >> sh) == (row_c >> sh), 1.0, 0.0) for sh in range(INV_BASE_SHIFT, 7)]
    lane_f = lax.broadcasted_iota(jnp.int32, (LANES, LANES), 1)
    row_f = lax.broadcasted_iota(jnp.int32, (LANES, LANES), 0)
    eye_f = lane_f == row_f
    lo_f = lane_f < C
    bd_mask = (lane_f < C) == (row_f < C)
    tt = lax.broadcasted_iota(jnp.int32, (C, C), 0)
    ss = lax.broadcasted_iota(jnp.int32, (C, C), 1)

    def bd(m):
        return jnp.concatenate([jnp.where(lo, m, 0.0), jnp.where(lo, 0.0, m)], axis=0)

    def bd_swap(m):
        return jnp.concatenate([jnp.where(lo, 0.0, m), jnp.where(lo, m, 0.0)], axis=0)

    def bdot(x, y):
        return _dot(x.astype(BF16), y.astype(BF16))

    dirs = ((rf_ref, vf_ref, af_ref, wf_ref, kf_ref, bf_ref, yf_ref),
            (rb_ref, vb_ref, ab_ref, wb_ref, kb_ref, bb_ref, yb_ref))

    n_sub = max(1, min(tc // C, RWKV_STAGE_WIDTH // (2 * n_req * RWKV_PAIRS)))

    def sub_chunk(i, carry):
        chains = []
        for d in range(2):
            before = (ss < tt) if d == 0 else (ss > tt)
            cum = jnp.where(jnp.logical_or(before, ss == tt), 1.0, 0.0).astype(BF16)
            strict = jnp.where((s_idx < row_c) if d == 0 else (s_idx > row_c), 1.0, 0.0)
            incl = jnp.where((s_idx <= row_c) if d == 0 else (s_idx >= row_c), 1.0, 0.0)
            tri_mask = jnp.concatenate([strict, incl], axis=0)
            last = C - 1 if d == 0 else 0
            for k in range(n_sub):
                ii = i * n_sub + k
                base = pl.multiple_of(ii * C if d == 0 else tc - C - ii * C, C)
                for bi in range(n_req):
                    for pr in range(RWKV_PAIRS):
                        chains.append(dict(d=d, k=k, bi=bi, pr=pr, base=base, refs=dirs[d], cum=cum, tri=tri_mask,
                                           last=last, ls=slice(pr * LANES, (pr + 1) * LANES)))

        for c in chains:
            r_r, v_r, a_r, w_r, k_r, b_r, _ = c["refs"]
            r, v, a, lw, k, b = (ref[c["bi"], pl.ds(c["base"], C), c["ls"]].astype(F32)
                                 for ref in (r_r, v_r, a_r, w_r, k_r, b_r))
            l1 = lw.astype(BF16)
            e1 = lw - l1.astype(F32)
            l2 = e1.astype(BF16)
            l3 = (e1 - l2.astype(F32)).astype(BF16)
            c["G"] = _dot(c["cum"], l1) + _dot(c["cum"], l2) + _dot(c["cum"], l3)
            c["in"] = (r, v, a, lw, k, b)
        for c in chains:
            r, v, a, lw, k, b = c.pop("in")
            G = c.pop("G")
            g_inv = jnp.exp(-G)
            at = a * jnp.exp(G - lw)
            rt = r * jnp.exp(G)
            bt = b * g_inv
            kt = k * g_inv
            g_last = jnp.exp(G[c["last"]:c["last"] + 1, :])
            X = jnp.concatenate([at, rt], axis=0)
            c["X"] = X.astype(BF16)
            bt_b, kt_b = bt.astype(BF16), kt.astype(BF16)
            nt = (((1,), (1,)), ((), ()))
            c["P0"] = lax.dot_general(jnp.where(lo_f, X, 0.0).astype(BF16), jnp.concatenate([bt_b, kt_b], axis=0),
                                      nt, preferred_element_type=F32)
            c["P1"] = lax.dot_general(jnp.where(lo_f, 0.0, X).astype(BF16), jnp.concatenate([kt_b, bt_b], axis=0),
                                      nt, preferred_element_type=F32)
            c["v"] = v
            c["ygt"] = jnp.concatenate([bt * g_last, kt * g_last], axis=0).T.astype(BF16)
            c["g_col"] = jnp.broadcast_to(g_last, (LANES, LANES)).T
        for c in chains:
            P0 = c.pop("P0") * c["tri"]
            P1 = c.pop("P1") * c["tri"]
            ABRB = jnp.where(lo_f, P0, P1)
            c["AKRK"] = jnp.where(lo_f, P1, P0)
            c["AB"], c["RB"] = ABRB[:C], ABRB[C:]
        for c in chains:
            l8 = c["AB"] * blk_masks[0]
            c["T"] = eye2 + l8
            c["Lp"] = bdot(l8, bd(l8))
        for c in chains:
            R = bdot(jnp.concatenate([c["Lp"], c["T"]], axis=0), bd(c["Lp"]))
            c["T"] = c["T"] + R[C:]
            c["Lp"] = R[:C]
        for c in chains:
            c["T"] = c["T"] + bdot(c["T"], bd(c["Lp"]))
        for lvl in range(1, len(blk_masks)):
            for c in chains:
                c["Lp"] = bdot(c["AB"] * (blk_masks[lvl] - blk_masks[lvl - 1]), bd(c["T"]))
            for c in chains:
                c["T"] = c["T"] + bdot(c["T"], bd(c["Lp"]))
        for c in chains:
            c["VK"] = bdot(c["AKRK"], bd_swap(c["v"]))
        for k in range(n_sub):
            now = [c for c in chains if c["k"] == k]
            for c in now:
                c["S0"] = st_s[c["bi"], c["d"], c["pr"]]
                c["XS"] = _dot(c["X"], c["S0"].astype(BF16))
            for c in now:
                c["U"] = bdot(c["T"], bd(c["XS"][:C] + c["VK"][:C]))
            for c in now:
                y_r = c["refs"][6]
                y_r[c["bi"], pl.ds(c["base"], C), c["ls"]] = c["XS"][C:] + c["VK"][C:] + bdot(c["RB"], bd(c["U"]))
                uv = jnp.concatenate([c["U"], c["v"]], axis=0).astype(BF16)
                st_s[c["bi"], c["d"], c["pr"]] = jnp.where(bd_mask, c["g_col"] * c["S0"] + _dot(c["ygt"], uv), 0.0)
        return carry

    lax.fori_loop(0, tc // (C * n_sub), sub_chunk, 0)

    @pl.when(j == pl.num_programs(1) - 1)
    def _():
        for bi in range(n_req):
            for d in range(2):
                for pr in range(RWKV_PAIRS):
                    mt = st_s[bi, d, pr].T
                    sfin_ref[bi, d, pr] = jnp.where(row_f < C, mt, pltpu.roll(mt, C, 1))[:, :C]


def _rwkv_scan(r, v, na, lw, k, b, s0, n_b, seq):
    tc = min(seq, RWKV_CHUNK)
    n_c = seq // tc
    W = RWKV_WIDTH
    bb = min(RWKV_BB, n_b)
    blk = (bb, tc, W)
    fwd = pl.BlockSpec(blk, lambda g, j: (g, j, 0))
    bwd = pl.BlockSpec(blk, lambda g, j: (g, n_c - 1 - j, 0))
    fwd_d = pl.BlockSpec((None,) + blk, lambda g, j: (0, g, j, 0))
    bwd_d = pl.BlockSpec((None,) + blk, lambda g, j: (1, g, n_c - 1 - j, 0))
    st_blk = (bb, 2, RWKV_PAIRS, LANES, LANES)
    st_spec = pl.BlockSpec(st_blk, lambda g, j: (g, 0, 0, 0, 0))
    y_shape = jax.ShapeDtypeStruct((n_b, seq, W), F32)
    fin_blk = (bb, 2, RWKV_PAIRS, LANES, RWKV_HEAD)
    fin_spec = pl.BlockSpec(fin_blk, lambda g, j: (g, 0, 0, 0, 0))
    return pl.pallas_call(
        _rwkv_scan_kernel,
        grid=(n_b // bb, n_c),
        in_specs=[fwd, fwd, fwd, fwd_d, fwd_d, fwd_d, bwd, bwd, bwd, bwd_d, bwd_d, bwd_d, st_spec],
        out_specs=[fwd, bwd, fin_spec],
        out_shape=[y_shape, y_shape, jax.ShapeDtypeStruct((n_b,) + fin_blk[1:], F32)],
        scratch_shapes=[pltpu.VMEM(st_blk, F32)],
        compiler_params=_params("arbitrary", "arbitrary"),
        name="rwkv_scan",
    )(r, v, na, lw, k, b, r, v, na, lw, k, b, s0)


LRU_ROWS = 1024

def _lru_kernel(x_ref, g_ref, h0_ref, cw_ref, cb_ref, wa_ref, ba_ref, wx_ref, bx_ref, lam_ref,
                o_ref, hfin_ref, a_s, u_s, h_s, *, seq):
    rows = x_ref.shape[0]
    n_req = rows // seq
    x = x_ref[...]
    row = lax.broadcasted_iota(jnp.int32, x.shape, 0) % seq
    xm2 = jnp.where(row >= 2, pltpu.roll(x, 2, 0), 0.0)
    xm1 = jnp.where(row >= 1, pltpu.roll(x, 1, 0), 0.0)
    xp1 = jnp.where(row < seq - 1, pltpu.roll(x, rows - 1, 0), 0.0)
    xc = (xm2 * cw_ref[0:1, :] + xm1 * cw_ref[1:2, :] + x * cw_ref[2:3, :] + xp1 * cw_ref[3:4, :]
          + cb_ref[...])
    xb = xc.astype(BF16)
    for d in range(2):
        r_gate = _sigmoid(_dot(xb, wa_ref[d]) + ba_ref[d:d + 1, :])
        i_gate = _sigmoid(_dot(xb, wx_ref[d]) + bx_ref[d:d + 1, :])
        log_a = -LRU_C * r_gate * _softplus(-lam_ref[d:d + 1, :])
        a = jnp.exp(log_a)
        u = jnp.sqrt(1.0 - a * a) * (i_gate * xc)
        for sh in (1, 2, 4):
            if d == 0:
                a_n, u_n = pltpu.roll(a, sh, 0), pltpu.roll(u, sh, 0)
                m = (row % SUBLANES) >= sh
            else:
                a_n, u_n = pltpu.roll(a, rows - sh, 0), pltpu.roll(u, rows - sh, 0)
                m = (row % SUBLANES) < SUBLANES - sh
            u = jnp.where(m, a * u_n + u, u)
            a = jnp.where(m, a * a_n, a)
        a_s[d] = a
        u_s[d] = u

    n_t = seq // SUBLANES
    unroll = 4 // n_req if n_req <= 4 else 1

    def body(i, carry):
        carry = list(carry)
        for k in range(unroll):
            for r in range(n_req):
                hf, hb = carry[2 * r], carry[2 * r + 1]
                base = pl.multiple_of(r * seq + (i * unroll + k) * SUBLANES, SUBLANES)
                h8 = a_s[0, pl.ds(base, SUBLANES), :] * hf + u_s[0, pl.ds(base, SUBLANES), :]
                h_s[0, pl.ds(base, SUBLANES), :] = h8
                carry[2 * r] = h8[SUBLANES - 1:SUBLANES]
                base = pl.multiple_of((r + 1) * seq - SUBLANES - (i * unroll + k) * SUBLANES, SUBLANES)
                h8 = a_s[1, pl.ds(base, SUBLANES), :] * hb + u_s[1, pl.ds(base, SUBLANES), :]
                h_s[1, pl.ds(base, SUBLANES), :] = h8
                carry[2 * r + 1] = h8[0:1]
        return tuple(carry)

    init = tuple(h0_ref[r, d:d + 1, :] for r in range(n_req) for d in range(2))
    fin = lax.fori_loop(0, n_t // unroll, body, init)
    for r in range(n_req):
        hfin_ref[r] = jnp.concatenate([fin[2 * r], fin[2 * r + 1]], axis=0)
    g = g_ref[...]
    gelu = 0.5 * g * (1.0 + jnp.tanh(0.7978845608028654 * (g + 0.044715 * (g * g * g))))
    o_ref[...] = ((h_s[0] + h_s[1]) * gelu).astype(BF16)


def _lru(xb, gb, h0, wts, n_b, seq):
    cw, cb, wa, ba, wx, bx, lam = wts
    C = LRU_WIDTH
    const = lambda shape: pl.BlockSpec(shape, lambda b: tuple(0 for _ in shape))
    n_req = max(1, min(n_b, LRU_ROWS // seq))
    tok = pl.BlockSpec((n_req * seq, C), lambda b: (b, 0))
    st = pl.BlockSpec((n_req, 2, C), lambda b: (b, 0, 0))
    return pl.pallas_call(
        functools.partial(_lru_kernel, seq=seq),
        grid=(n_b // n_req,),
        in_specs=[tok, tok, st, const((LRU_CONV_W, C)), const((1, C)), const((2, C, C)), const((2, C)),
                  const((2, C, C)), const((2, C)), const((2, C))],
        out_specs=[tok, st],
        out_shape=[jax.ShapeDtypeStruct((n_b * seq, C), BF16), jax.ShapeDtypeStruct((n_b, 2, C), F32)],
        scratch_shapes=[pltpu.VMEM((2, n_req * seq, C), F32)] * 3,
        compiler_params=_params("arbitrary"),
        name="lru",
    )(xb, gb, h0, cw, cb, wa, ba, wx, bx, lam)


def _outproj_kernel(x_ref, att_ref, yf_ref, yb_ref, bonus_ref, g_ref, lru_ref, mod_ref, n2_ref,
                    lnw_ref, lnb_ref, w_ref, rt_ref, bd_ref,
                    x1_ref, h2_ref, lg_ref):
    y = yf_ref[...] + yb_ref[...]
    parts = []
    for c in range(RWKV_WIDTH // LANES):
        yc = y[:, c * LANES:(c + 1) * LANES]
        mean = _seg_sum(yc, bd_ref[...]) * (1.0 / RWKV_HEAD)
        dev = yc - mean
        var = _seg_sum(dev * dev, bd_ref[...]) * (1.0 / RWKV_HEAD)
        parts.append(dev * lax.rsqrt(var + GN_EPS))
    yn = jnp.concatenate(parts, axis=-1) * lnw_ref[...] + lnb_ref[...]
    rwkv = (yn + bonus_ref[...]) * g_ref[...]
    o1 = ATT_WIDTH
    o2 = o1 + RWKV_WIDTH
    mixed = (_dot(att_ref[...].astype(BF16), w_ref[:o1, :])
             + _dot(rwkv.astype(BF16), w_ref[o1:o2, :])
             + _dot(lru_ref[...].astype(BF16), w_ref[o2:, :]))
    x1 = x_ref[...] + mod_ref[2:3, :] * mixed
    x1_ref[...] = x1
    rs = lax.rsqrt(jnp.mean(x1 * x1, axis=-1, keepdims=True) + RMS_EPS)
    h2 = (x1 * rs * n2_ref[...]) * (1.0 + mod_ref[4:5, :]) + mod_ref[3:4, :]
    h2_hi = h2.astype(BF16)
    h2_lo = (h2 - h2_hi.astype(F32)).astype(BF16)
    h2_ref[...] = h2_hi
    both = _dot(h2_hi, rt_ref[...])
    lg_ref[...] = both[:, :LANES] + both[:, LANES:] + _dot(h2_lo, rt_ref[:, :LANES])


def _outproj(x, att, yf, yb, bonus, g, lru, mod_l, norm2, lnw, lnb, w_out_bf, layer, router_pad, bd,
             seq, cond_row0):
    n_tok = x.shape[0]
    tm = 512 if seq >= 512 else seq * (512 // seq)
    per_seq = max(seq // tm, 1)
    n_t = n_tok // tm
    if cond_row0 == 0:
        row = lambda i: 0
    else:
        row = lambda i: cond_row0 + i // per_seq
    tok = lambda w: pl.BlockSpec((tm, w), lambda i: (i, 0))
    const = lambda shape: pl.BlockSpec(shape, lambda i: tuple(0 for _ in shape))
    W = RWKV_WIDTH
    return pl.pallas_call(
        _outproj_kernel,
        grid=(n_t,),
        in_specs=[
            tok(D_MODEL), tok(ATT_WIDTH), tok(W), tok(W), tok(W), tok(W), tok(LRU_WIDTH),
            pl.BlockSpec((None, N_MOD, D_MODEL), lambda i: (row(i), 0, 0)),
            const((1, D_MODEL)), const((1, W)), const((1, W)),
            pl.BlockSpec((None, D_MODEL, D_MODEL), lambda i: (layer, 0, 0)),
            const((D_MODEL, 2 * LANES)), const((LANES, LANES)),
        ],
        out_specs=[tok(D_MODEL), tok(D_MODEL), tok(LANES)],
        out_shape=[jax.ShapeDtypeStruct((n_tok, D_MODEL), F32),
                   jax.ShapeDtypeStruct((n_tok, D_MODEL), BF16),
                   jax.ShapeDtypeStruct((n_tok, LANES), F32)],
        compiler_params=_params("arbitrary"),
        name="outproj",
    )(x, att, yf, yb, bonus, g, lru, mod_l, norm2, lnw, lnb, w_out_bf, router_pad, bd)


PREFIX_BLOCK = 256
GATHER_ROWS = 1024


def _prefix_count(mask_f, tri):
    seq = mask_f.shape[0]
    outs = []
    carry = jnp.zeros((1, LANES), F32)
    for blk in range(seq // PREFIX_BLOCK):
        m = mask_f[blk * PREFIX_BLOCK:(blk + 1) * PREFIX_BLOCK]
        outs.append(_dot(tri, m.astype(BF16)) + carry)
        carry = carry + jnp.sum(m, axis=0, keepdims=True)
    return jnp.concatenate(outs, axis=0) if len(outs) > 1 else outs[0]


ROUTE_ROWS = 1024


def _route_kernel(lg_ref, h2_ref, tri_ref, xs_ref, slot_ref, aff_ref, *, cap, seq):
    rows = lg_ref.shape[0]
    n_req = rows // seq
    lane = lax.broadcasted_iota(jnp.int32, (rows, LANES), 1)
    real = lane < N_EXPERTS
    lg = jnp.where(real, lg_ref[...], -jnp.inf)
    m = jnp.max(lg, axis=-1, keepdims=True)
    e = jnp.exp(lg - m)
    aff = e / jnp.sum(e, axis=-1, keepdims=True)
    aff_ref[...] = aff

    a3 = aff.reshape(n_req, seq, LANES)
    a_min = jnp.min(a3, axis=1, keepdims=True)
    a_max = jnp.max(a3, axis=1, keepdims=True)

    def search(carry):
        lo, hi, _ = carry
        mid = 0.5 * (jnp.maximum(lo, a_min) + jnp.minimum(hi, a_max))
        inside = jnp.where(a3 > lo, jnp.where(a3 < hi, 1.0, 0.0), 0.0)
        upper = inside * jnp.where(a3 >= mid, 1.0, 0.0)
        up = jnp.min(jnp.where(upper > 0.0, a3, BRACKET_HI), axis=1, keepdims=True)
        dn = jnp.max(jnp.where(inside - upper > 0.0, a3, BRACKET_LO), axis=1, keepdims=True)
        pivot = jnp.where(up < BRACKET_HI, up, dn)
        found = jnp.where(up < BRACKET_HI, 1.0, jnp.where(dn > BRACKET_LO, 1.0, 0.0))
        cnt = jnp.sum(jnp.where(a3 >= pivot, 1.0, 0.0), axis=1, keepdims=True)
        take_lo = found * jnp.where(cnt >= cap, 1.0, 0.0)
        take_hi = found - take_lo
        return (jnp.where(take_lo > 0.0, pivot, lo), jnp.where(take_hi > 0.0, pivot, hi), found)

    shape = (n_req, 1, LANES)
    init = (jnp.full(shape, BRACKET_LO, F32), jnp.full(shape, BRACKET_HI, F32), jnp.ones(shape, F32))
    thr = lax.while_loop(lambda c: jnp.max(c[2]) > 0.0, search, init)[0]
    gt3 = jnp.where(a3 > thr, 1.0, 0.0)
    eq3 = jnp.where(a3 == thr, 1.0, 0.0)
    need3 = cap - jnp.sum(gt3, axis=1, keepdims=True)
    tri = tri_ref[...]
    c_iota = lax.broadcasted_iota(jnp.int32, (cap, seq), 0).astype(F32)
    real_seq = lax.broadcasted_iota(jnp.int32, (seq, LANES), 1) < N_EXPERTS
    group = max(1, min(N_EXPERTS, GATHER_ROWS // cap))
    for r in range(n_req):
        tok = slice(r * seq, (r + 1) * seq)
        gt, eq = gt3[r], eq3[r]
        sel = jnp.where(real_seq, gt + eq * jnp.where(_prefix_count(eq, tri) < need3[r], 1.0, 0.0), 0.0)
        slot = jnp.where(sel > 0.0, _prefix_count(sel, tri), -1.0)
        slot_ref[tok, :] = slot
        slot_t = slot.T
        h2 = h2_ref[tok, :]
        for g0 in range(0, N_EXPERTS, group):
            onehot = jnp.concatenate(
                [jnp.where(c_iota == slot_t[ex:ex + 1, :], 1.0, 0.0) for ex in range(g0, g0 + group)], axis=0)
            picked = _dot(onehot.astype(BF16), h2)
            for k in range(group):
                xs_ref[g0 + k, r * cap:(r + 1) * cap, :] = picked[k * cap:(k + 1) * cap].astype(BF16)


def _route(logits, h2, tri, n_b, seq):
    cap = EC_FACTOR * seq // N_EXPERTS
    n_req = max(1, min(n_b, ROUTE_ROWS // seq))
    tok = lambda w: pl.BlockSpec((n_req * seq, w), lambda b: (b, 0))
    return pl.pallas_call(
        functools.partial(_route_kernel, cap=cap, seq=seq),
        grid=(n_b // n_req,),
        in_specs=[tok(LANES), tok(D_MODEL), pl.BlockSpec((PREFIX_BLOCK, PREFIX_BLOCK), lambda b: (0, 0))],
        out_specs=[pl.BlockSpec((N_EXPERTS, n_req * cap, D_MODEL), lambda b: (0, b, 0)), tok(LANES), tok(LANES)],
        out_shape=[jax.ShapeDtypeStruct((N_EXPERTS, n_b * cap, D_MODEL), BF16),
                   jax.ShapeDtypeStruct((n_b * seq, LANES), F32),
                   jax.ShapeDtypeStruct((n_b * seq, LANES), F32)],
        compiler_params=_params("arbitrary"),
        name="route",
    )(logits, h2, tri)


EXPERT_RB = 256


def _expert_kernel(xa_ref, xb_ref, wg_ref, wu_ref, wd_ref, ya_ref, yb_ref, wg_s, wu_s, wd_s):
    wg_s[...] = wg_ref[...].astype(BF16)
    wu_s[...] = wu_ref[...].astype(BF16)
    wd_s[...] = wd_ref[...].astype(BF16)
    for x_ref, y_ref in ((xa_ref, ya_ref), (xb_ref, yb_ref)):
        for rb in range(x_ref.shape[0] // EXPERT_RB):
            rows = slice(rb * EXPERT_RB, (rb + 1) * EXPERT_RB)
            x = x_ref[rows, :]
            a = _dot(x, wg_s[...])
            u = _dot(x, wu_s[...])
            hid = (a * _sigmoid(a) * u).astype(BF16)
            y_ref[rows, :] = _dot(hid, wd_s[...]).astype(BF16)


def _experts(xs_a, xs_b, w_gate, w_up, w_down, layer):
    ma, mb = xs_a.shape[1], xs_b.shape[1]
    xspec = lambda m: pl.BlockSpec((None, m, D_MODEL), lambda e: (e, 0, 0))
    return pl.pallas_call(
        _expert_kernel,
        grid=(N_EXPERTS,),
        in_specs=[
            xspec(ma), xspec(mb),
            pl.BlockSpec((None, None, D_MODEL, EXPERT_FF), lambda e: (layer, e, 0, 0)),
            pl.BlockSpec((None, None, D_MODEL, EXPERT_FF), lambda e: (layer, e, 0, 0)),
            pl.BlockSpec((None, None, EXPERT_FF, D_MODEL), lambda e: (layer, e, 0, 0)),
        ],
        out_specs=[xspec(ma), xspec(mb)],
        out_shape=[jax.ShapeDtypeStruct(xs_a.shape, BF16), jax.ShapeDtypeStruct(xs_b.shape, BF16)],
        scratch_shapes=[pltpu.VMEM((D_MODEL, EXPERT_FF), BF16), pltpu.VMEM((D_MODEL, EXPERT_FF), BF16),
                        pltpu.VMEM((EXPERT_FF, D_MODEL), BF16)],
        compiler_params=_params("arbitrary"),
        name="experts",
    )(xs_a, xs_b, w_gate, w_up, w_down)


COMBINE_FUSED_COLS = 512
COMBINE_ROWS = 1024


def _combine_kernel(*refs, cap, fused, n_req):
    if fused:
        y_ref, slot_ref, aff_ref, x1_ref, mod_ref, ex_ref, o_ref = refs
    else:
        y_ref, slot_ref, aff_ref, x1_ref, mod_ref, o_ref = refs
    seq = x1_ref.shape[0]
    slot = slot_ref[...]
    aff = aff_ref[...]
    if fused:
        spread = ex_ref[...]
        a1 = aff.astype(BF16)
        r1 = aff - a1.astype(F32)
        a2 = r1.astype(BF16)
        a3 = (r1 - a2.astype(F32)).astype(BF16)
        slot_x = _dot(slot.astype(BF16), spread)
        aff_x = _dot(a1, spread) + _dot(a2, spread) + _dot(a3, spread)
        cols = N_EXPERTS * cap
        c_pat = (lax.broadcasted_iota(jnp.int32, (seq, cols), 1) % cap).astype(F32)
        w_hi, w_lo = _split(jnp.where(slot_x == c_pat, aff_x, 0.0))
        parts = []
        for r in range(n_req):
            rows = slice(r * (seq // n_req), (r + 1) * (seq // n_req))
            y2 = y_ref[:, r * cap:(r + 1) * cap, :].reshape(cols, D_MODEL)
            parts.append(_dot(w_hi[rows], y2) + _dot(w_lo[rows], y2))
        acc = jnp.concatenate(parts, axis=0) if n_req > 1 else parts[0]
    else:
        c_iota = lax.broadcasted_iota(jnp.int32, (seq, cap), 1).astype(F32)
        acc = jnp.zeros((seq, D_MODEL), F32)
        for ex in range(N_EXPERTS):
            onehot = jnp.where(slot[:, ex:ex + 1] == c_iota, 1.0, 0.0).astype(BF16)
            acc = acc + aff[:, ex:ex + 1] * _dot(onehot, y_ref[ex])
    o_ref[...] = x1_ref[...] + mod_ref[5:6, :] * acc


def _combine(y, slot, aff, x1, mod_l, n_b, seq, cond_row0):
    cap = EC_FACTOR * seq // N_EXPERTS
    fused = N_EXPERTS * cap <= COMBINE_FUSED_COLS
    if cond_row0 == 0:
        row = lambda b: 0
    else:
        row = lambda b: cond_row0 + b
    tm = min(seq, 512)
    n_t = seq // tm
    n_req = max(1, min(n_b, COMBINE_ROWS // seq)) if (fused and cond_row0 == 0) else 1
    tm *= n_req
    tok = lambda w: pl.BlockSpec((tm, w), lambda b, i: (b * n_t + i, 0))
    in_specs = [
        pl.BlockSpec((N_EXPERTS, n_req * cap, D_MODEL), lambda b, i: (0, b, 0)),
        tok(LANES), tok(LANES), tok(D_MODEL),
        pl.BlockSpec((None, N_MOD, D_MODEL), lambda b, i: (row(b), 0, 0)),
    ]
    args = [y, slot, aff, x1, mod_l]
    if fused:
        cols = N_EXPERTS * cap
        spread = (jnp.arange(LANES)[:, None] == (jnp.arange(cols) // cap)[None, :]).astype(BF16)
        in_specs.append(pl.BlockSpec((LANES, cols), lambda b, i: (0, 0)))
        args.append(spread)
    return pl.pallas_call(
        functools.partial(_combine_kernel, cap=cap, fused=fused, n_req=n_req),
        grid=(n_b // n_req, n_t),
        in_specs=in_specs,
        out_specs=tok(D_MODEL),
        out_shape=jax.ShapeDtypeStruct((n_b * seq, D_MODEL), F32),
        compiler_params=_params("arbitrary", "arbitrary"),
        name="combine_fused" if fused else "combine",
    )(*args)


def _rope_tables(seq):
    t = jnp.arange(seq, dtype=jnp.int32)
    row = (t // GRID_W).astype(F32)
    col = (t % GRID_W).astype(F32)
    half = HEAD_DIM // 2
    inv = ROPE_THETA ** (-jnp.arange(0, half, 2, dtype=F32) / half)
    lane = jnp.arange(LANES)
    u = lane % HEAD_DIM
    pos = jnp.where((u // half)[None, :] == 0, row[:, None], col[:, None])
    ang = pos * inv[(u % half) % (half // 2)][None, :]
    first = ((u % half) < half // 2)[None, :]
    sin = jnp.sin(ang)
    return jnp.cos(ang), jnp.where(first, -sin, 0.0), jnp.where(first, 0.0, sin)


def _block_diag(w):
    n, k, _ = w.shape
    eye = jnp.eye(n, dtype=w.dtype)
    return (eye[:, None, :, None] * w[:, :, None, :]).reshape(n * k, n * k)


def _split_weight(w):
    return jnp.stack(_split(w))


def _pad_rows(w, offset, total=LANES):
    return jnp.zeros((total, w.shape[1]), w.dtype).at[offset:offset + w.shape[0]].set(w)


def _pack_state(s):
    b = s.shape[0]
    st = s.reshape(b, 2, RWKV_PAIRS, 2, RWKV_HEAD, RWKV_HEAD).transpose(0, 1, 2, 3, 5, 4)
    eye = jnp.eye(2, dtype=s.dtype)
    out = st[:, :, :, :, :, None, :] * eye[None, None, None, :, None, :, None]
    return out.reshape(b, 2, RWKV_PAIRS, LANES, LANES)


def _unpack_state(s):
    return s.reshape(s.shape[0], 2, RWKV_HEADS, RWKV_HEAD, RWKV_HEAD)


def kernel(x_prompt, x_sample, cache_k, cache_v, state_rwkv, state_lru, c, c_ctx, w_ada, b_ada, norm1, norm2, w_in, w_out, q_norm, k_norm, rwkv_mu, rwkv_w0, rwkv_w_up, rwkv_a0, rwkv_a_up, rwkv_g_up, rwkv_k_k, rwkv_k_a, rwkv_r_k, rwkv_ln_w, rwkv_ln_b, lru_conv_w, lru_conv_b, lru_wa, lru_ba, lru_wx, lru_bx, lru_lambda, router, exp_w_gate, exp_w_up, exp_w_down):
    n_ctx, seq_ctx, _ = x_prompt.shape
    n_lat, seq_lat, _ = x_sample.shape
    past = cache_k.shape[2]
    assert n_lat + 1 <= COND_ROWS

    cond = jnp.zeros((COND_ROWS, D_MODEL), F32).at[0].set(c_ctx).at[1:1 + n_lat].set(c)
    mod = _ada(cond, w_ada, b_ada.reshape(DEPTH, 1, N_MOD * D_MODEL))
    mod = mod.reshape(DEPTH, COND_ROWS, N_MOD, D_MODEL)

    w_in_bf = w_in.astype(BF16)
    w_out_bf = w_out.astype(BF16)
    lane = jnp.arange(LANES)
    bd = (lane[:, None] // HEAD_DIM == lane[None, :] // HEAD_DIM).astype(BF16)
    pidx = jnp.arange(PREFIX_BLOCK)
    tri = (pidx[None, :] < pidx[:, None]).astype(BF16)
    rope_tabs = _rope_tables(seq_lat)

    paths = [
        dict(x=x_prompt.reshape(n_ctx * seq_ctx, D_MODEL), n_b=n_ctx, seq=seq_ctx, row0=0, rope=None),
        dict(x=x_sample.reshape(n_lat * seq_lat, D_MODEL), n_b=n_lat, seq=seq_lat, row0=1, rope=rope_tabs),
    ]
    new_k, new_v, new_sr, new_sl = [], [], [], []
    for l in range(DEPTH):
        mod_l = mod[l]
        qn = jnp.tile(q_norm[l], LANES // HEAD_DIM)[None, :]
        kn = jnp.tile(k_norm[l], LANES // HEAD_DIM)[None, :]
        prep_w = (
            rwkv_mu[l][None, :], rwkv_k_k[l][None, :], rwkv_k_a[l][None, :],
            rwkv_r_k[l].reshape(1, RWKV_WIDTH), rwkv_w0[l], rwkv_a0[l],
            jnp.stack([_split_weight(_pad_rows(rwkv_w_up[l, d], 0)) for d in range(2)]),
            jnp.stack([_split_weight(_pad_rows(rwkv_a_up[l, d], RWKV_DECAY_LORA)) for d in range(2)]),
            _split_weight(_pad_rows(rwkv_g_up[l], RWKV_DECAY_LORA + RWKV_AAA_LORA)),
        )
        lru_w = (
            lru_conv_w[l], lru_conv_b[l][None, :],
            jnp.stack([_block_diag(lru_wa[l, d]) for d in range(2)]).astype(BF16), lru_ba[l],
            jnp.stack([_block_diag(lru_wx[l, d]) for d in range(2)]).astype(BF16), lru_bx[l],
            lru_lambda[l],
        )
        router_pad = jnp.concatenate(_split(jnp.zeros((D_MODEL, LANES), F32).at[:, :N_EXPERTS].set(router[l])), axis=1)
        mids = []
        for pi, pth in enumerate(paths):
            n_b, seq, row0 = pth["n_b"], pth["seq"], pth["row0"]
            latent = pth["rope"] is not None
            outs = _inproj(pth["x"], mod_l, norm1[l][None, :], w_in_bf, l, qn, kn, bd, pth["rope"], seq, row0)
            if latent:
                q, k_n, k_att, v, p_rwkv, lru_x, lru_g = outs
                cache = (cache_k[:, l].reshape(n_b, past, KV_WIDTH), cache_v[:, l].reshape(n_b, past, KV_WIDTH))
                s0 = _pack_state(state_rwkv[:, l])
                h0 = state_lru[:, l]
            else:
                q, k_n, v, p_rwkv, lru_x, lru_g = outs
                k_att, cache = k_n, None
                s0 = jnp.zeros((n_b, 2, RWKV_PAIRS, LANES, LANES), F32)
                h0 = jnp.zeros((n_b, 2, LRU_WIDTH), F32)
                new_k.append(k_n.reshape(n_b, seq, ATT_KV_HEADS, HEAD_DIM))
                new_v.append(v.reshape(n_b, seq, ATT_KV_HEADS, HEAD_DIM))
            att = _attention(q, k_att, v, n_b, seq, cache)
            r, vv, na, w, kd, b, g, bonus = _rwkv_prep(p_rwkv.reshape(n_b, seq, RWKV_COLS), n_b, seq, prep_w, bd)
            W = RWKV_WIDTH
            yf, yb, s_fin = _rwkv_scan(r.reshape(n_b, seq, W), vv.reshape(n_b, seq, W), na.reshape(n_b, seq, W),
                                       w.reshape(2, n_b, seq, W), kd.reshape(2, n_b, seq, W),
                                       b.reshape(2, n_b, seq, W), s0, n_b, seq)
            lru_out, h_fin = _lru(lru_x, lru_g, h0, lru_w, n_b, seq)
            if not latent:
                new_sr.append(_unpack_state(s_fin))
                new_sl.append(h_fin)
            x1, h2, logits = _outproj(pth["x"], att, yf.reshape(n_b * seq, W), yb.reshape(n_b * seq, W), bonus, g,
                                      lru_out, mod_l, norm2[l][None, :], rwkv_ln_w[l][None, :],
                                      rwkv_ln_b[l][None, :], w_out_bf, l, router_pad, bd, seq, row0)
            xs, slot, aff = _route(logits, h2, tri, n_b, seq)
            mids.append((xs, slot, aff, x1))
        y_a, y_b = _experts(mids[0][0], mids[1][0], exp_w_gate, exp_w_up, exp_w_down, l)
        for pth, (xs, slot, aff, x1), y in zip(paths, mids, (y_a, y_b)):
            pth["x"] = _combine(y, slot, aff, x1, mod_l, pth["n_b"], pth["seq"], pth["row0"])

    y_prompt = paths[0]["x"].reshape(n_ctx, seq_ctx, D_MODEL)
    y_sample = paths[1]["x"].reshape(n_lat, seq_lat, D_MODEL)
    return (y_prompt, y_sample, jnp.stack(new_k, axis=1), jnp.stack(new_v, axis=1),
            jnp.stack(new_sr, axis=1), jnp.stack(new_sl, axis=1))
```

```python
import functools

import numpy as np
import jax
import jax.numpy as jnp
from jax import lax
from jax.experimental import pallas as pl
from jax.experimental.pallas import tpu as pltpu

F32 = jnp.float32
BF16 = jnp.bfloat16

D_MODEL = 1024
DEPTH = 2
GRID_W = 64
ATT_HEADS = 8
ATT_KV_HEADS = 2
HEAD_DIM = 64
ATT_WIDTH = ATT_HEADS * HEAD_DIM
KV_WIDTH = ATT_KV_HEADS * HEAD_DIM
ROPE_THETA = 10000.0
RWKV_HEADS = 4
RWKV_HEAD = 64
RWKV_WIDTH = RWKV_HEADS * RWKV_HEAD
RWKV_DECAY_LORA = 32
RWKV_AAA_LORA = 32
RWKV_GATE_LORA = 64
RWKV_COLS = 3 * RWKV_WIDTH + RWKV_DECAY_LORA + RWKV_AAA_LORA + RWKV_GATE_LORA
GN_EPS = 64e-5
LRU_BLOCKS = 4
LRU_BLOCK = 64
LRU_WIDTH = LRU_BLOCKS * LRU_BLOCK
LRU_CONV_W = 4
LRU_C = 8.0
IN_COLS = ATT_WIDTH + 2 * KV_WIDTH + RWKV_COLS + 2 * LRU_WIDTH
N_EXPERTS = 16
EC_FACTOR = 2
EXPERT_FF = 1024
RMS_EPS = 1e-6

LANES = 128
SUBLANES = 8
VMEM_LIMIT = 56 * 1024 * 1024
N_MOD = 6
COND_ROWS = 8
BRACKET_LO, BRACKET_HI = -1.0, 2.0


def _params(*sem):
    return pltpu.CompilerParams(dimension_semantics=sem, vmem_limit_bytes=VMEM_LIMIT)


def _dot(a, b, precision=None):
    return jnp.dot(a, b, preferred_element_type=F32, precision=precision)


def _sigmoid(x):
    return 1.0 / (1.0 + jnp.exp(-x))


def _softplus(x):
    return jnp.maximum(x, 0.0) + jnp.log1p(jnp.exp(-jnp.abs(x)))


def _split(x):
    hi = x.astype(BF16)
    return hi, (x - hi.astype(F32)).astype(BF16)


def _dot_split(x, w_ref):
    hi, lo = _split(x)
    return _dot(hi, w_ref[0]) + _dot(hi, w_ref[1]) + _dot(lo, w_ref[0])


def _seg_sum(x, ones_bd):
    return _dot(x.astype(BF16), ones_bd)


def _ada_kernel(c_ref, w_ref, b_ref, o_ref):
    c = c_ref[...]
    s_hi, s_lo = _split(c * _sigmoid(c))
    w_hi, w_lo = _split(w_ref[...])
    o_ref[...] = _dot(s_hi, w_hi) + _dot(s_hi, w_lo) + _dot(s_lo, w_hi) + b_ref[...]


def _ada(cond, w_ada, b_ada):
    n_l = w_ada.shape[0]
    tn = 1536
    n_t = N_MOD * D_MODEL // tn
    return pl.pallas_call(
        _ada_kernel,
        grid=(n_l, n_t),
        in_specs=[
            pl.BlockSpec((COND_ROWS, D_MODEL), lambda l, j: (0, 0)),
            pl.BlockSpec((None, D_MODEL, tn), lambda l, j: (l, 0, j)),
            pl.BlockSpec((None, 1, tn), lambda l, j: (l, 0, j)),
        ],
        out_specs=pl.BlockSpec((None, COND_ROWS, tn), lambda l, j: (l, 0, j)),
        out_shape=jax.ShapeDtypeStruct((n_l, COND_ROWS, N_MOD * D_MODEL), F32),
        compiler_params=_params("arbitrary", "arbitrary"),
        name="ada",
    )(cond, w_ada, b_ada)


def _head_rms(x, gain, ones_bd):
    ms = _seg_sum(x * x, ones_bd) * (1.0 / HEAD_DIM)
    return x * lax.rsqrt(ms + RMS_EPS) * gain


def _rope(x, cos, sin_up, sin_dn):
    return x * cos + pltpu.roll(x, LANES - 16, 1) * sin_up + pltpu.roll(x, 16, 1) * sin_dn


def _inproj_kernel(*refs, rope):
    if rope:
        (x_ref, mod_ref, n1_ref, w_ref, qn_ref, kn_ref, bd_ref, cos_ref, su_ref, sd_ref,
         q_ref, ko_ref, ka_ref, v_ref, pr_ref, lx_ref, lg_ref) = refs
    else:
        (x_ref, mod_ref, n1_ref, w_ref, qn_ref, kn_ref, bd_ref,
         q_ref, ko_ref, v_ref, pr_ref, lx_ref, lg_ref) = refs
    x = x_ref[...]
    rs = lax.rsqrt(jnp.mean(x * x, axis=-1, keepdims=True) + RMS_EPS)
    h = (x * rs * n1_ref[...]) * (1.0 + mod_ref[1:2, :]) + mod_ref[0:1, :]
    p = _dot(h.astype(BF16), w_ref[...])
    bd = bd_ref[...]
    for c in range(ATT_WIDTH // LANES):
        qc = _head_rms(p[:, c * LANES:(c + 1) * LANES], qn_ref[...], bd)
        if rope:
            qc = _rope(qc, cos_ref[...], su_ref[...], sd_ref[...])
        q_ref[:, c * LANES:(c + 1) * LANES] = (qc * (HEAD_DIM ** -0.5)).astype(BF16)
    o = ATT_WIDTH
    kc = _head_rms(p[:, o:o + KV_WIDTH], kn_ref[...], bd)
    ko_ref[...] = kc
    if rope:
        ka_ref[...] = _rope(kc, cos_ref[...], su_ref[...], sd_ref[...]).astype(BF16)
    o += KV_WIDTH
    v_ref[...] = p[:, o:o + KV_WIDTH]
    o += KV_WIDTH
    pr_ref[...] = p[:, o:o + RWKV_COLS]
    o += RWKV_COLS
    lx_ref[...] = p[:, o:o + LRU_WIDTH]
    o += LRU_WIDTH
    lg_ref[...] = p[:, o:o + LRU_WIDTH]


def _inproj(x, mod_l, norm1, w_in_bf, layer, qn, kn, bd, rope_tabs, seq, cond_row0):
    n_tok = x.shape[0]
    tm = 512 if seq >= 512 else seq * (512 // seq)
    per_seq = max(seq // tm, 1)
    n_t = n_tok // tm
    rope = rope_tabs is not None
    if cond_row0 == 0:
        row = lambda i: 0
    else:
        row = lambda i: cond_row0 + i // per_seq
    tok = lambda w: pl.BlockSpec((tm, w), lambda i: (i, 0))
    const = lambda shape: pl.BlockSpec(shape, lambda i: tuple(0 for _ in shape))
    in_specs = [
        tok(D_MODEL),
        pl.BlockSpec((None, N_MOD, D_MODEL), lambda i: (row(i), 0, 0)),
        const((1, D_MODEL)),
        pl.BlockSpec((None, D_MODEL, IN_COLS), lambda i: (layer, 0, 0)),
        const((1, LANES)), const((1, LANES)), const((LANES, LANES)),
    ]
    args = [x, mod_l, norm1, w_in_bf, qn, kn, bd]
    outs = [(ATT_WIDTH, BF16), (KV_WIDTH, F32)]
    if rope:
        in_specs += [pl.BlockSpec((tm, LANES), lambda i: (i % per_seq, 0))] * 3
        args += list(rope_tabs)
        outs.append((KV_WIDTH, BF16))
    outs += [(KV_WIDTH, F32), (RWKV_COLS, F32), (LRU_WIDTH, F32), (LRU_WIDTH, F32)]
    return pl.pallas_call(
        functools.partial(_inproj_kernel, rope=rope),
        grid=(n_t,),
        in_specs=in_specs,
        out_specs=[tok(w) for w, _ in outs],
        out_shape=[jax.ShapeDtypeStruct((n_tok, w), dt) for w, dt in outs],
        compiler_params=_params("arbitrary"),
        name="inproj_rope" if rope else "inproj",
    )(*args)


ATTN_ROWS = 1024


def _attn_kernel(*refs, has_cache, n_req):
    if has_cache:
        q_ref, k_ref, v_ref, ck_ref, cv_ref, o_ref, kn_s, ks_s, vn_s, vs_s = refs
    else:
        q_ref, k_ref, v_ref, o_ref, kn_s, ks_s, vn_s, vs_s = refs

    tq = q_ref.shape[0] // n_req
    seq = k_ref.shape[0] // n_req
    lo = lax.broadcasted_iota(jnp.int32, (tq, LANES), 1) < HEAD_DIM
    rep = ATT_HEADS // ATT_KV_HEADS
    for r in range(n_req):
        def stage_keys():
            k = k_ref[r * seq:(r + 1) * seq, :].astype(F32)
            v = v_ref[r * seq:(r + 1) * seq, :]
            if has_cache:
                k = jnp.concatenate([ck_ref[...], k], axis=0)
                v = jnp.concatenate([cv_ref[...], v], axis=0)
            kn_s[...] = k.astype(BF16)
            ks_s[...] = pltpu.roll(k, HEAD_DIM, 1).astype(BF16)
            vn_s[...] = v.astype(BF16)
            vs_s[...] = pltpu.roll(v, HEAD_DIM, 1).astype(BF16)

        if n_req == 1:
            pl.when(pl.program_id(1) == 0)(stage_keys)
        else:
            stage_keys()
        for c in range(ATT_WIDTH // LANES):
            qc = q_ref[r * tq:(r + 1) * tq, c * LANES:(c + 1) * LANES].astype(F32)
            halves = []
            for half in range(2):
                g = (2 * c + half) // rep
                qm = jnp.where(lo if half == 0 else jnp.logical_not(lo), qc, 0.0).astype(BF16)
                k_s, v_s = (kn_s, vn_s) if half == g else (ks_s, vs_s)
                s = lax.dot_general(qm, k_s[...], (((1,), (1,)), ((), ())), preferred_element_type=F32)
                m = jnp.max(s, axis=-1, keepdims=True)
                e = jnp.exp(s - m)
                l = jnp.sum(e, axis=-1, keepdims=True)
                halves.append(_dot(e.astype(BF16), v_s[...]) / l)
            o_ref[r * tq:(r + 1) * tq, c * LANES:(c + 1) * LANES] = (
                jnp.where(lo, halves[0], halves[1]).astype(BF16))


def _attention(q, k, v, n_b, seq, cache=None):
    tq = 256
    n_q = seq // tq
    has_cache = cache is not None
    past = cache[0].shape[1] if has_cache else 0
    n_req = max(1, min(n_b, ATTN_ROWS // seq)) if (n_q == 1 and not has_cache) else 1
    in_specs = [
        pl.BlockSpec((n_req * tq, ATT_WIDTH), lambda b, i: (b * n_q + i, 0)),
        pl.BlockSpec((n_req * seq, KV_WIDTH), lambda b, i: (b, 0)),
        pl.BlockSpec((n_req * seq, KV_WIDTH), lambda b, i: (b, 0)),
    ]
    args = [q, k, v]
    if has_cache:
        in_specs += [pl.BlockSpec((None, past, KV_WIDTH), lambda b, i: (b, 0, 0))] * 2
        args += list(cache)
    return pl.pallas_call(
        functools.partial(_attn_kernel, has_cache=has_cache, n_req=n_req),
        grid=(n_b // n_req, n_q),
        in_specs=in_specs,
        out_specs=pl.BlockSpec((n_req * tq, ATT_WIDTH), lambda b, i: (b * n_q + i, 0)),
        out_shape=jax.ShapeDtypeStruct((n_b * seq, ATT_WIDTH), BF16),
        scratch_shapes=[pltpu.VMEM((past + seq, KV_WIDTH), BF16)] * 4,
        compiler_params=_params("arbitrary", "arbitrary"),
        name="attn_cache" if has_cache else "attn",
    )(*args)


def _rwkv_prep_kernel(p_ref, mu_ref, kk_ref, ka_ref, rk_ref, w0_ref, a0_ref, wup_ref, aup_ref,
                      gup_ref, bd_ref,
                      r_ref, v_ref, na_ref, w_ref, k_ref, b_ref, g_ref, bonus_ref):
    tm = r_ref.shape[0]
    seq = p_ref.shape[0]
    i = pl.program_id(1)
    n_t = pl.num_programs(1)
    start = pl.multiple_of(i * tm, tm)
    cur = p_ref[pl.ds(start, tm), :]
    prev_base = pl.multiple_of(jnp.maximum(start - SUBLANES, 0), SUBLANES)
    next_base = pl.multiple_of(jnp.minimum(start + tm, seq - SUBLANES), SUBLANES)
    prev_row = p_ref[pl.ds(prev_base, SUBLANES), :][SUBLANES - 1:SUBLANES]
    next_row = p_ref[pl.ds(next_base, SUBLANES), :][0:1]
    prev_row = jnp.where(i > 0, prev_row, 0.0)
    next_row = jnp.where(i < n_t - 1, next_row, 0.0)
    row = lax.broadcasted_iota(jnp.int32, cur.shape, 0)
    prev = jnp.where(row == 0, prev_row, pltpu.roll(cur, 1, 0))
    nxt = jnp.where(row == tm - 1, next_row, pltpu.roll(cur, tm - 1, 0))
    ps = cur + mu_ref[...] * (0.5 * (prev + nxt) - cur)

    W = RWKV_WIDTH
    r = ps[:, :W]
    k = ps[:, W:2 * W]
    v = ps[:, 2 * W:3 * W]
    lora = ps[:, 3 * W:]
    r_ref[...] = r.astype(BF16)
    v_ref[...] = v.astype(BF16)
    g_ref[...] = _dot_split(_sigmoid(lora), gup_ref).astype(BF16)
    lora_t = jnp.tanh(lora)
    kk = k * kk_ref[...]
    kk_parts = []
    for c in range(W // LANES):
        kc = kk[:, c * LANES:(c + 1) * LANES]
        nrm = jnp.sqrt(_seg_sum(kc * kc, bd_ref[...]))
        kk_parts.append(kc / jnp.maximum(nrm, 1e-12))
    kk = jnp.concatenate(kk_parts, axis=-1)
    na_ref[...] = (-kk).astype(BF16)
    bonus = None
    for d in range(2):
        w_log = -_softplus(-(w0_ref[d:d + 1, :] + _dot_split(lora_t, wup_ref.at[d]))) - 0.5
        w_ref[d] = -jnp.exp(w_log)
        a_rate = _sigmoid(a0_ref[d:d + 1, :] + _dot_split(lora, aup_ref.at[d]))
        kd = k * (1.0 + (a_rate - 1.0) * ka_ref[...])
        k_ref[d] = kd.astype(BF16)
        b_ref[d] = (kk * a_rate).astype(BF16)
        rkr = r * kd * rk_ref[...]
        parts = [_seg_sum(rkr[:, c * LANES:(c + 1) * LANES], bd_ref[...]) for c in range(W // LANES)]
        bd_term = jnp.concatenate(parts, axis=-1) * v
        bonus = bd_term if bonus is None else bonus + bd_term
    bonus_ref[...] = bonus.astype(BF16)


def _rwkv_prep(p, n_b, seq, wts, bd):
    mu, k_k, k_a, r_k, w0, a0, wup, aup, gup = wts
    tm = min(seq, 512)
    n_t = seq // tm
    W = RWKV_WIDTH
    const = lambda shape: pl.BlockSpec(shape, lambda b, i: tuple(0 for _ in shape))
    tok = pl.BlockSpec((tm, W), lambda b, i: (b * n_t + i, 0))
    tok2 = pl.BlockSpec((2, tm, W), lambda b, i: (0, b * n_t + i, 0))
    two = jax.ShapeDtypeStruct((2, n_b * seq, W), F32)
    one_bf = jax.ShapeDtypeStruct((n_b * seq, W), BF16)
    two_bf = jax.ShapeDtypeStruct((2, n_b * seq, W), BF16)
    return pl.pallas_call(
        _rwkv_prep_kernel,
        grid=(n_b, n_t),
        in_specs=[
            pl.BlockSpec((None, seq, RWKV_COLS), lambda b, i: (b, 0, 0)),
            const((1, RWKV_COLS)), const((1, W)), const((1, W)), const((1, W)),
            const((2, W)), const((2, W)),
            const((2, 2, LANES, W)), const((2, 2, LANES, W)), const((2, LANES, W)),
            const((LANES, LANES)),
        ],
        out_specs=[tok, tok, tok, tok2, tok2, tok2, tok, tok],
        out_shape=[one_bf, one_bf, one_bf, two, two_bf, two_bf, one_bf, one_bf],
        compiler_params=_params("arbitrary", "arbitrary"),
        name="rwkv_prep",
    )(p, mu, k_k, k_a, r_k, w0, a0, wup, aup, gup, bd)


RWKV_BB = 4
RWKV_PAIRS = RWKV_HEADS * RWKV_HEAD // LANES
RWKV_CHUNK = 256
RWKV_SUB = 64
RWKV_STAGE_WIDTH = 32
INV_BASE_SHIFT = 3


def _rwkv_scan_kernel(*refs, has_s0):
    (rf_ref, vf_ref, af_ref, wf_ref, kf_ref, bf_ref, rb_ref, vb_ref, ab_ref, wb_ref, kb_ref, bb_ref) = refs[:12]
    s0_ref = refs[12] if has_s0 else None
    yf_ref, yb_ref, sfin_ref, st_s = refs[-4:]
    _rwkv_scan_body(rf_ref, vf_ref, af_ref, wf_ref, kf_ref, bf_ref, rb_ref, vb_ref, ab_ref, wb_ref, kb_ref, bb_ref,
                    s0_ref, yf_ref, yb_ref, sfin_ref, st_s)


def _rwkv_scan_body(rf_ref, vf_ref, af_ref, wf_ref, kf_ref, bf_ref,
                    rb_ref, vb_ref, ab_ref, wb_ref, kb_ref, bb_ref, s0_ref,
                    yf_ref, yb_ref, sfin_ref, st_s):
    j = pl.program_id(1)
    n_req, tc = rf_ref.shape[0], rf_ref.shape[1]
    C = RWKV_SUB

    @pl.when(j == 0)
    def _():
        st_s[...] = jnp.zeros(st_s.shape, F32) if s0_ref is None else s0_ref[...]

    lane_c = lax.broadcasted_iota(jnp.int32, (C, LANES), 1)
    row_c = lax.broadcasted_iota(jnp.int32, (C, LANES), 0)
    lo = lane_c < C
    s_idx = lane_c % C
    eye2 = jnp.where(s_idx == row_c, 1.0, 0.0)
    blk_masks = [jnp.where((s_idx >> sh) == (row_c >> sh), 1.0, 0.0) for sh in range(INV_BASE_SHIFT, 7)]
    lane_f = lax.broadcasted_iota(jnp.int32, (LANES, LANES), 1)
    row_f = lax.broadcasted_iota(jnp.int32, (LANES, LANES), 0)
    eye_f = lane_f == row_f
    lo_f = lane_f < C
    bd_mask = (lane_f < C) == (row_f < C)
    tt = lax.broadcasted_iota(jnp.int32, (C, C), 0)
    ss = lax.broadcasted_iota(jnp.int32, (C, C), 1)

    def bd(m):
        return jnp.concatenate([jnp.where(lo, m, 0.0), jnp.where(lo, 0.0, m)], axis=0)

    def bd_swap(m):
        return jnp.concatenate([jnp.where(lo, 0.0, m), jnp.where(lo, m, 0.0)], axis=0)

    def bdot(x, y):
        return _dot(x.astype(BF16), y.astype(BF16))

    dirs = ((rf_ref, vf_ref, af_ref, wf_ref, kf_ref, bf_ref, yf_ref),
            (rb_ref, vb_ref, ab_ref, wb_ref, kb_ref, bb_ref, yb_ref))

    n_sub = max(1, min(tc // C, RWKV_STAGE_WIDTH // (2 * n_req * RWKV_PAIRS)))

    def sub_chunk(i, carry):
        chains = []
        for d in range(2):
            before = (ss < tt) if d == 0 else (ss > tt)
            cum = jnp.where(jnp.logical_or(before, ss == tt), 1.0, 0.0).astype(BF16)
            strict = jnp.where((s_idx < row_c) if d == 0 else (s_idx > row_c), 1.0, 0.0)
            incl = jnp.where((s_idx <= row_c) if d == 0 else (s_idx >= row_c), 1.0, 0.0)
            tri_mask = jnp.concatenate([strict, incl], axis=0)
            last = C - 1 if d == 0 else 0
            for k in range(n_sub):
                ii = i * n_sub + k
                base = pl.multiple_of(ii * C if d == 0 else tc - C - ii * C, C)
                for bi in range(n_req):
                    for pr in range(RWKV_PAIRS):
                        chains.append(dict(d=d, k=k, bi=bi, pr=pr, base=base, refs=dirs[d], cum=cum, tri=tri_mask,
                                           last=last, ls=slice(pr * LANES, (pr + 1) * LANES)))

        for c in chains:
            r_r, v_r, a_r, w_r, k_r, b_r, _ = c["refs"]
            r, v, a, lw, k, b = (ref[c["bi"], pl.ds(c["base"], C), c["ls"]].astype(F32)
                                 for ref in (r_r, v_r, a_r, w_r, k_r, b_r))
            l1 = lw.astype(BF16)
            e1 = lw - l1.astype(F32)
            l2 = e1.astype(BF16)
            l3 = (e1 - l2.astype(F32)).astype(BF16)
            c["G"] = _dot(c["cum"], l1) + _dot(c["cum"], l2) + _dot(c["cum"], l3)
            c["in"] = (r, v, a, lw, k, b)
        for c in chains:
            r, v, a, lw, k, b = c.pop("in")
            G = c.pop("G")
            g_inv = jnp.exp(-G)
            at = a * jnp.exp(G - lw)
            rt = r * jnp.exp(G)
            bt = b * g_inv
            kt = k * g_inv
            g_last = jnp.exp(G[c["last"]:c["last"] + 1, :])
            X = jnp.concatenate([at, rt], axis=0)
            c["X"] = X.astype(BF16)
            bt_b, kt_b = bt.astype(BF16), kt.astype(BF16)
            nt = (((1,), (1,)), ((), ()))
            c["P0"] = lax.dot_general(jnp.where(lo_f, X, 0.0).astype(BF16), jnp.concatenate([bt_b, kt_b], axis=0),
                                      nt, preferred_element_type=F32)
            c["P1"] = lax.dot_general(jnp.where(lo_f, 0.0, X).astype(BF16), jnp.concatenate([kt_b, bt_b], axis=0),
                                      nt, preferred_element_type=F32)
            c["v"] = v
            c["ygt"] = jnp.concatenate([bt * g_last, kt * g_last], axis=0).T.astype(BF16)
            c["g_col"] = jnp.broadcast_to(g_last, (LANES, LANES)).T
        for c in chains:
            P0 = c.pop("P0") * c["tri"]
            P1 = c.pop("P1") * c["tri"]
            ABRB = jnp.where(lo_f, P0, P1)
            c["AKRK"] = jnp.where(lo_f, P1, P0)
            c["AB"], c["RB"] = ABRB[:C], ABRB[C:]
        for c in chains:
            l8 = c["AB"] * blk_masks[0]
            c["T"] = eye2 + l8
            c["Lp"] = bdot(l8, bd(l8))
        for c in chains:
            R = bdot(jnp.concatenate([c["Lp"], c["T"]], axis=0), bd(c["Lp"]))
            c["T"] = c["T"] + R[C:]
            c["Lp"] = R[:C]
        for c in chains:
            c["T"] = c["T"] + bdot(c["T"], bd(c["Lp"]))
        for lvl in range(1, len(blk_masks)):
            for c in chains:
                c["Lp"] = bdot(c["AB"] * (blk_masks[lvl] - blk_masks[lvl - 1]), bd(c["T"]))
            for c in chains:
                c["T"] = c["T"] + bdot(c["T"], bd(c["Lp"]))
        for c in chains:
            c["VK"] = bdot(c["AKRK"], bd_swap(c["v"]))
        for k in range(n_sub):
            now = [c for c in chains if c["k"] == k]
            for c in now:
                c["S0"] = st_s[c["bi"], c["d"], c["pr"]]
                c["XS"] = _dot(c["X"], c["S0"].astype(BF16))
            for c in now:
                c["U"] = bdot(c["T"], bd(c["XS"][:C] + c["VK"][:C]))
            for c in now:
                y_r = c["refs"][6]
                y_r[c["bi"], pl.ds(c["base"], C), c["ls"]] = c["XS"][C:] + c["VK"][C:] + bdot(c["RB"], bd(c["U"]))
                uv = jnp.concatenate([c["U"], c["v"]], axis=0).astype(BF16)
                st_s[c["bi"], c["d"], c["pr"]] = jnp.where(bd_mask, c["g_col"] * c["S0"] + _dot(c["ygt"], uv), 0.0)
        return carry

    lax.fori_loop(0, tc // (C * n_sub), sub_chunk, 0)

    @pl.when(j == pl.num_programs(1) - 1)
    def _():
        for bi in range(n_req):
            for d in range(2):
                for pr in range(RWKV_PAIRS):
                    mt = st_s[bi, d, pr].T
                    sfin_ref[bi, d, pr] = jnp.where(row_f < C, mt, pltpu.roll(mt, C, 1))[:, :C]


def _rwkv_scan(r, v, na, lw, k, b, s0, n_b, seq):
    tc = min(seq, RWKV_CHUNK)
    n_c = seq // tc
    W = RWKV_WIDTH
    bb = min(RWKV_BB, n_b)
    blk = (bb, tc, W)
    fwd = pl.BlockSpec(blk, lambda g, j: (g, j, 0))
    bwd = pl.BlockSpec(blk, lambda g, j: (g, n_c - 1 - j, 0))
    fwd_d = pl.BlockSpec((None,) + blk, lambda g, j: (0, g, j, 0))
    bwd_d = pl.BlockSpec((None,) + blk, lambda g, j: (1, g, n_c - 1 - j, 0))
    st_blk = (bb, 2, RWKV_PAIRS, LANES, LANES)
    st_spec = pl.BlockSpec(st_blk, lambda g, j: (g, 0, 0, 0, 0))
    y_shape = jax.ShapeDtypeStruct((n_b, seq, W), F32)
    fin_blk = (bb, 2, RWKV_PAIRS, LANES, RWKV_HEAD)
    fin_spec = pl.BlockSpec(fin_blk, lambda g, j: (g, 0, 0, 0, 0))
    has_s0 = s0 is not None
    return pl.pallas_call(
        functools.partial(_rwkv_scan_kernel, has_s0=has_s0),
        grid=(n_b // bb, n_c),
        in_specs=[fwd, fwd, fwd, fwd_d, fwd_d, fwd_d, bwd, bwd, bwd, bwd_d, bwd_d, bwd_d] + [st_spec] * has_s0,
        out_specs=[fwd, bwd, fin_spec],
        out_shape=[y_shape, y_shape, jax.ShapeDtypeStruct((n_b,) + fin_blk[1:], F32)],
        scratch_shapes=[pltpu.VMEM(st_blk, F32)],
        compiler_params=_params("arbitrary", "arbitrary"),
        name="rwkv_scan_s0" if has_s0 else "rwkv_scan",
    )(*([r, v, na, lw, k, b, r, v, na, lw, k, b] + [s0] * has_s0))


LRU_ROWS = 1024

def _lru_kernel(x_ref, g_ref, h0_ref, cw_ref, cb_ref, wa_ref, ba_ref, wx_ref, bx_ref, lam_ref,
                o_ref, hfin_ref, a_s, u_s, h_s, *, seq):
    rows = x_ref.shape[0]
    n_req = rows // seq
    x = x_ref[...]
    row = lax.broadcasted_iota(jnp.int32, x.shape, 0) % seq
    xm2 = jnp.where(row >= 2, pltpu.roll(x, 2, 0), 0.0)
    xm1 = jnp.where(row >= 1, pltpu.roll(x, 1, 0), 0.0)
    xp1 = jnp.where(row < seq - 1, pltpu.roll(x, rows - 1, 0), 0.0)
    xc = (xm2 * cw_ref[0:1, :] + xm1 * cw_ref[1:2, :] + x * cw_ref[2:3, :] + xp1 * cw_ref[3:4, :]
          + cb_ref[...])
    xb = xc.astype(BF16)
    for d in range(2):
        r_gate = _sigmoid(_dot(xb, wa_ref[d]) + ba_ref[d:d + 1, :])
        i_gate = _sigmoid(_dot(xb, wx_ref[d]) + bx_ref[d:d + 1, :])
        log_a = -LRU_C * r_gate * _softplus(-lam_ref[d:d + 1, :])
        a = jnp.exp(log_a)
        u = jnp.sqrt(1.0 - a * a) * (i_gate * xc)
        for sh in (1, 2, 4):
            if d == 0:
                a_n, u_n = pltpu.roll(a, sh, 0), pltpu.roll(u, sh, 0)
                m = (row % SUBLANES) >= sh
            else:
                a_n, u_n = pltpu.roll(a, rows - sh, 0), pltpu.roll(u, rows - sh, 0)
                m = (row % SUBLANES) < SUBLANES - sh
            u = jnp.where(m, a * u_n + u, u)
            a = jnp.where(m, a * a_n, a)
        a_s[d] = a
        u_s[d] = u

    n_t = seq // SUBLANES
    unroll = 4 // n_req if n_req <= 4 else 1

    def body(i, carry):
        carry = list(carry)
        for k in range(unroll):
            for r in range(n_req):
                hf, hb = carry[2 * r], carry[2 * r + 1]
                base = pl.multiple_of(r * seq + (i * unroll + k) * SUBLANES, SUBLANES)
                h8 = a_s[0, pl.ds(base, SUBLANES), :] * hf + u_s[0, pl.ds(base, SUBLANES), :]
                h_s[0, pl.ds(base, SUBLANES), :] = h8
                carry[2 * r] = h8[SUBLANES - 1:SUBLANES]
                base = pl.multiple_of((r + 1) * seq - SUBLANES - (i * unroll + k) * SUBLANES, SUBLANES)
                h8 = a_s[1, pl.ds(base, SUBLANES), :] * hb + u_s[1, pl.ds(base, SUBLANES), :]
                h_s[1, pl.ds(base, SUBLANES), :] = h8
                carry[2 * r + 1] = h8[0:1]
        return tuple(carry)

    init = tuple(h0_ref[r, d:d + 1, :] for r in range(n_req) for d in range(2))
    fin = lax.fori_loop(0, n_t // unroll, body, init)
    for r in range(n_req):
        hfin_ref[r] = jnp.concatenate([fin[2 * r], fin[2 * r + 1]], axis=0)
    g = g_ref[...]
    gelu = 0.5 * g * (1.0 + jnp.tanh(0.7978845608028654 * (g + 0.044715 * (g * g * g))))
    o_ref[...] = ((h_s[0] + h_s[1]) * gelu).astype(BF16)


def _lru(xb, gb, h0, wts, n_b, seq):
    cw, cb, wa, ba, wx, bx, lam = wts
    C = LRU_WIDTH
    const = lambda shape: pl.BlockSpec(shape, lambda b: tuple(0 for _ in shape))
    n_req = max(1, min(n_b, LRU_ROWS // seq))
    tok = pl.BlockSpec((n_req * seq, C), lambda b: (b, 0))
    st = pl.BlockSpec((n_req, 2, C), lambda b: (b, 0, 0))
    return pl.pallas_call(
        functools.partial(_lru_kernel, seq=seq),
        grid=(n_b // n_req,),
        in_specs=[tok, tok, st, const((LRU_CONV_W, C)), const((1, C)), const((2, C, C)), const((2, C)),
                  const((2, C, C)), const((2, C)), const((2, C))],
        out_specs=[tok, st],
        out_shape=[jax.ShapeDtypeStruct((n_b * seq, C), BF16), jax.ShapeDtypeStruct((n_b, 2, C), F32)],
        scratch_shapes=[pltpu.VMEM((2, n_req * seq, C), F32)] * 3,
        compiler_params=_params("arbitrary"),
        name="lru",
    )(xb, gb, h0, cw, cb, wa, ba, wx, bx, lam)


def _outproj_kernel(x_ref, att_ref, yf_ref, yb_ref, bonus_ref, g_ref, lru_ref, mod_ref, n2_ref,
                    lnw_ref, lnb_ref, w_ref, rt_ref, bd_ref,
                    x1_ref, h2_ref, lg_ref):
    y = yf_ref[...] + yb_ref[...]
    parts = []
    for c in range(RWKV_WIDTH // LANES):
        yc = y[:, c * LANES:(c + 1) * LANES]
        mean = _seg_sum(yc, bd_ref[...]) * (1.0 / RWKV_HEAD)
        dev = yc - mean
        var = _seg_sum(dev * dev, bd_ref[...]) * (1.0 / RWKV_HEAD)
        parts.append(dev * lax.rsqrt(var + GN_EPS))
    yn = jnp.concatenate(parts, axis=-1) * lnw_ref[...] + lnb_ref[...]
    rwkv = (yn + bonus_ref[...]) * g_ref[...]
    o1 = ATT_WIDTH
    o2 = o1 + RWKV_WIDTH
    mixed = (_dot(att_ref[...].astype(BF16), w_ref[:o1, :])
             + _dot(rwkv.astype(BF16), w_ref[o1:o2, :])
             + _dot(lru_ref[...].astype(BF16), w_ref[o2:, :]))
    x1 = x_ref[...] + mod_ref[2:3, :] * mixed
    x1_ref[...] = x1
    rs = lax.rsqrt(jnp.mean(x1 * x1, axis=-1, keepdims=True) + RMS_EPS)
    h2 = (x1 * rs * n2_ref[...]) * (1.0 + mod_ref[4:5, :]) + mod_ref[3:4, :]
    h2_hi = h2.astype(BF16)
    h2_lo = (h2 - h2_hi.astype(F32)).astype(BF16)
    h2_ref[...] = h2_hi
    both = _dot(h2_hi, rt_ref[...])
    lg_ref[...] = both[:, :LANES] + both[:, LANES:] + _dot(h2_lo, rt_ref[:, :LANES])


def _outproj(x, att, yf, yb, bonus, g, lru, mod_l, norm2, lnw, lnb, w_out_bf, layer, router_pad, bd,
             seq, cond_row0):
    n_tok = x.shape[0]
    tm = 512 if seq >= 512 else seq * (512 // seq)
    per_seq = max(seq // tm, 1)
    n_t = n_tok // tm
    if cond_row0 == 0:
        row = lambda i: 0
    else:
        row = lambda i: cond_row0 + i // per_seq
    tok = lambda w: pl.BlockSpec((tm, w), lambda i: (i, 0))
    const = lambda shape: pl.BlockSpec(shape, lambda i: tuple(0 for _ in shape))
    W = RWKV_WIDTH
    return pl.pallas_call(
        _outproj_kernel,
        grid=(n_t,),
        in_specs=[
            tok(D_MODEL), tok(ATT_WIDTH), tok(W), tok(W), tok(W), tok(W), tok(LRU_WIDTH),
            pl.BlockSpec((None, N_MOD, D_MODEL), lambda i: (row(i), 0, 0)),
            const((1, D_MODEL)), const((1, W)), const((1, W)),
            pl.BlockSpec((None, D_MODEL, D_MODEL), lambda i: (layer, 0, 0)),
            const((D_MODEL, 2 * LANES)), const((LANES, LANES)),
        ],
        out_specs=[tok(D_MODEL), tok(D_MODEL), tok(LANES)],
        out_shape=[jax.ShapeDtypeStruct((n_tok, D_MODEL), F32),
                   jax.ShapeDtypeStruct((n_tok, D_MODEL), BF16),
                   jax.ShapeDtypeStruct((n_tok, LANES), F32)],
        compiler_params=_params("arbitrary"),
        name="outproj",
    )(x, att, yf, yb, bonus, g, lru, mod_l, norm2, lnw, lnb, w_out_bf, router_pad, bd)


PREFIX_BLOCK = 256
GATHER_ROWS = 1024


def _prefix_count(mask_f, tri):
    seq = mask_f.shape[0]
    outs = []
    carry = jnp.zeros((1, LANES), F32)
    for blk in range(seq // PREFIX_BLOCK):
        m = mask_f[blk * PREFIX_BLOCK:(blk + 1) * PREFIX_BLOCK]
        outs.append(_dot(tri, m.astype(BF16)) + carry)
        carry = carry + jnp.sum(m, axis=0, keepdims=True)
    return jnp.concatenate(outs, axis=0) if len(outs) > 1 else outs[0]


ROUTE_ROWS = 1024


def _route_kernel(lg_ref, h2_ref, tri_ref, xs_ref, slot_ref, aff_ref, *, cap, seq):
    rows = lg_ref.shape[0]
    n_req = rows // seq
    lane = lax.broadcasted_iota(jnp.int32, (rows, LANES), 1)
    real = lane < N_EXPERTS
    lg = jnp.where(real, lg_ref[...], -jnp.inf)
    m = jnp.max(lg, axis=-1, keepdims=True)
    e = jnp.exp(lg - m)
    aff = e / jnp.sum(e, axis=-1, keepdims=True)
    aff_ref[...] = aff

    a3 = aff.reshape(n_req, seq, LANES)
    a_min = jnp.min(a3, axis=1, keepdims=True)
    a_max = jnp.max(a3, axis=1, keepdims=True)

    def search(carry):
        lo, hi, _ = carry
        mid = 0.5 * (jnp.maximum(lo, a_min) + jnp.minimum(hi, a_max))
        inside = jnp.where(a3 > lo, jnp.where(a3 < hi, 1.0, 0.0), 0.0)
        upper = inside * jnp.where(a3 >= mid, 1.0, 0.0)
        up = jnp.min(jnp.where(upper > 0.0, a3, BRACKET_HI), axis=1, keepdims=True)
        dn = jnp.max(jnp.where(inside - upper > 0.0, a3, BRACKET_LO), axis=1, keepdims=True)
        pivot = jnp.where(up < BRACKET_HI, up, dn)
        found = jnp.where(up < BRACKET_HI, 1.0, jnp.where(dn > BRACKET_LO, 1.0, 0.0))
        cnt = jnp.sum(jnp.where(a3 >= pivot, 1.0, 0.0), axis=1, keepdims=True)
        take_lo = found * jnp.where(cnt >= cap, 1.0, 0.0)
        take_hi = found - take_lo
        return (jnp.where(take_lo > 0.0, pivot, lo), jnp.where(take_hi > 0.0, pivot, hi), found)

    shape = (n_req, 1, LANES)
    init = (jnp.full(shape, BRACKET_LO, F32), jnp.full(shape, BRACKET_HI, F32), jnp.ones(shape, F32))
    thr = lax.while_loop(lambda c: jnp.max(c[2]) > 0.0, search, init)[0]
    gt3 = jnp.where(a3 > thr, 1.0, 0.0)
    eq3 = jnp.where(a3 == thr, 1.0, 0.0)
    need3 = cap - jnp.sum(gt3, axis=1, keepdims=True)
    tri = tri_ref[...]
    c_iota = lax.broadcasted_iota(jnp.int32, (cap, seq), 0).astype(F32)
    real_seq = lax.broadcasted_iota(jnp.int32, (seq, LANES), 1) < N_EXPERTS
    group = max(1, min(N_EXPERTS, GATHER_ROWS // cap))
    for r in range(n_req):
        tok = slice(r * seq, (r + 1) * seq)
        gt, eq = gt3[r], eq3[r]
        sel = jnp.where(real_seq, gt + eq * jnp.where(_prefix_count(eq, tri) < need3[r], 1.0, 0.0), 0.0)
        slot = jnp.where(sel > 0.0, _prefix_count(sel, tri), -1.0)
        slot_ref[tok, :] = slot
        slot_t = slot.T
        h2 = h2_ref[tok, :]
        for g0 in range(0, N_EXPERTS, group):
            onehot = jnp.concatenate(
                [jnp.where(c_iota == slot_t[ex:ex + 1, :], 1.0, 0.0) for ex in range(g0, g0 + group)], axis=0)
            picked = _dot(onehot.astype(BF16), h2)
            for k in range(group):
                xs_ref[g0 + k, r * cap:(r + 1) * cap, :] = picked[k * cap:(k + 1) * cap].astype(BF16)


def _route(logits, h2, tri, n_b, seq):
    cap = EC_FACTOR * seq // N_EXPERTS
    n_req = max(1, min(n_b, ROUTE_ROWS // seq))
    tok = lambda w: pl.BlockSpec((n_req * seq, w), lambda b: (b, 0))
    return pl.pallas_call(
        functools.partial(_route_kernel, cap=cap, seq=seq),
        grid=(n_b // n_req,),
        in_specs=[tok(LANES), tok(D_MODEL), pl.BlockSpec((PREFIX_BLOCK, PREFIX_BLOCK), lambda b: (0, 0))],
        out_specs=[pl.BlockSpec((N_EXPERTS, n_req * cap, D_MODEL), lambda b: (0, b, 0)), tok(LANES), tok(LANES)],
        out_shape=[jax.ShapeDtypeStruct((N_EXPERTS, n_b * cap, D_MODEL), BF16),
                   jax.ShapeDtypeStruct((n_b * seq, LANES), F32),
                   jax.ShapeDtypeStruct((n_b * seq, LANES), F32)],
        compiler_params=_params("arbitrary"),
        name="route",
    )(logits, h2, tri)


EXPERT_RB = 256


def _expert_kernel(xa_ref, xb_ref, wg_ref, wu_ref, wd_ref, ya_ref, yb_ref, wg_s, wu_s, wd_s):
    wg_s[...] = wg_ref[...].astype(BF16)
    wu_s[...] = wu_ref[...].astype(BF16)
    wd_s[...] = wd_ref[...].astype(BF16)
    for x_ref, y_ref in ((xa_ref, ya_ref), (xb_ref, yb_ref)):
        for rb in range(x_ref.shape[0] // EXPERT_RB):
            rows = slice(rb * EXPERT_RB, (rb + 1) * EXPERT_RB)
            x = x_ref[rows, :]
            a = _dot(x, wg_s[...])
            u = _dot(x, wu_s[...])
            hid = (a * _sigmoid(a) * u).astype(BF16)
            y_ref[rows, :] = _dot(hid, wd_s[...]).astype(BF16)


def _experts(xs_a, xs_b, w_gate, w_up, w_down, layer):
    ma, mb = xs_a.shape[1], xs_b.shape[1]
    xspec = lambda m: pl.BlockSpec((None, m, D_MODEL), lambda e: (e, 0, 0))
    return pl.pallas_call(
        _expert_kernel,
        grid=(N_EXPERTS,),
        in_specs=[
            xspec(ma), xspec(mb),
            pl.BlockSpec((None, None, D_MODEL, EXPERT_FF), lambda e: (layer, e, 0, 0)),
            pl.BlockSpec((None, None, D_MODEL, EXPERT_FF), lambda e: (layer, e, 0, 0)),
            pl.BlockSpec((None, None, EXPERT_FF, D_MODEL), lambda e: (layer, e, 0, 0)),
        ],
        out_specs=[xspec(ma), xspec(mb)],
        out_shape=[jax.ShapeDtypeStruct(xs_a.shape, BF16), jax.ShapeDtypeStruct(xs_b.shape, BF16)],
        scratch_shapes=[pltpu.VMEM((D_MODEL, EXPERT_FF), BF16), pltpu.VMEM((D_MODEL, EXPERT_FF), BF16),
                        pltpu.VMEM((EXPERT_FF, D_MODEL), BF16)],
        compiler_params=_params("arbitrary"),
        name="experts",
    )(xs_a, xs_b, w_gate, w_up, w_down)


COMBINE_FUSED_COLS = 512
COMBINE_ROWS = 1024


def _combine_kernel(*refs, cap, fused, n_req):
    if fused:
        y_ref, slot_ref, aff_ref, x1_ref, mod_ref, ex_ref, o_ref = refs
    else:
        y_ref, slot_ref, aff_ref, x1_ref, mod_ref, o_ref = refs
    seq = x1_ref.shape[0]
    slot = slot_ref[...]
    aff = aff_ref[...]
    if fused:
        spread = ex_ref[...]
        a1 = aff.astype(BF16)
        r1 = aff - a1.astype(F32)
        a2 = r1.astype(BF16)
        a3 = (r1 - a2.astype(F32)).astype(BF16)
        slot_x = _dot(slot.astype(BF16), spread)
        aff_x = _dot(a1, spread) + _dot(a2, spread) + _dot(a3, spread)
        cols = N_EXPERTS * cap
        c_pat = (lax.broadcasted_iota(jnp.int32, (seq, cols), 1) % cap).astype(F32)
        w_hi, w_lo = _split(jnp.where(slot_x == c_pat, aff_x, 0.0))
        parts = []
        for r in range(n_req):
            rows = slice(r * (seq // n_req), (r + 1) * (seq // n_req))
            y2 = y_ref[:, r * cap:(r + 1) * cap, :].reshape(cols, D_MODEL)
            parts.append(_dot(w_hi[rows], y2) + _dot(w_lo[rows], y2))
        acc = jnp.concatenate(parts, axis=0) if n_req > 1 else parts[0]
    else:
        c_iota = lax.broadcasted_iota(jnp.int32, (seq, cap), 1).astype(F32)
        acc = jnp.zeros((seq, D_MODEL), F32)
        for ex in range(N_EXPERTS):
            onehot = jnp.where(slot[:, ex:ex + 1] == c_iota, 1.0, 0.0).astype(BF16)
            acc = acc + aff[:, ex:ex + 1] * _dot(onehot, y_ref[ex])
    o_ref[...] = x1_ref[...] + mod_ref[5:6, :] * acc


def _combine(y, slot, aff, x1, mod_l, n_b, seq, cond_row0):
    cap = EC_FACTOR * seq // N_EXPERTS
    fused = N_EXPERTS * cap <= COMBINE_FUSED_COLS
    if cond_row0 == 0:
        row = lambda b: 0
    else:
        row = lambda b: cond_row0 + b
    tm = min(seq, 512)
    n_t = seq // tm
    n_req = max(1, min(n_b, COMBINE_ROWS // seq)) if (fused and cond_row0 == 0) else 1
    tm *= n_req
    tok = lambda w: pl.BlockSpec((tm, w), lambda b, i: (b * n_t + i, 0))
    in_specs = [
        pl.BlockSpec((N_EXPERTS, n_req * cap, D_MODEL), lambda b, i: (0, b, 0)),
        tok(LANES), tok(LANES), tok(D_MODEL),
        pl.BlockSpec((None, N_MOD, D_MODEL), lambda b, i: (row(b), 0, 0)),
    ]
    args = [y, slot, aff, x1, mod_l]
    if fused:
        cols = N_EXPERTS * cap
        spread = (jnp.arange(LANES)[:, None] == (jnp.arange(cols) // cap)[None, :]).astype(BF16)
        in_specs.append(pl.BlockSpec((LANES, cols), lambda b, i: (0, 0)))
        args.append(spread)
    return pl.pallas_call(
        functools.partial(_combine_kernel, cap=cap, fused=fused, n_req=n_req),
        grid=(n_b // n_req, n_t),
        in_specs=in_specs,
        out_specs=tok(D_MODEL),
        out_shape=jax.ShapeDtypeStruct((n_b * seq, D_MODEL), F32),
        compiler_params=_params("arbitrary", "arbitrary"),
        name="combine_fused" if fused else "combine",
    )(*args)


def _rope_tables(seq):
    t = np.arange(seq)
    row = (t // GRID_W).astype(np.float64)
    col = (t % GRID_W).astype(np.float64)
    half = HEAD_DIM // 2
    inv = ROPE_THETA ** (-np.arange(0, half, 2, dtype=np.float64) / half)
    lane = np.arange(LANES)
    u = lane % HEAD_DIM
    pos = np.where((u // half)[None, :] == 0, row[:, None], col[:, None])
    ang = pos * inv[(u % half) % (half // 2)][None, :]
    first = ((u % half) < half // 2)[None, :]
    sin = np.sin(ang)
    tabs = (np.cos(ang), np.where(first, -sin, 0.0), np.where(first, 0.0, sin))
    return tuple(jnp.asarray(x.astype(np.float32)) for x in tabs)


def _block_diag(w):
    n, k, _ = w.shape
    eye = jnp.eye(n, dtype=w.dtype)
    return (eye[:, None, :, None] * w[:, :, None, :]).reshape(n * k, n * k)


def _split_weight(w):
    return jnp.stack(_split(w))


def _pad_rows(w, offset, total=LANES):
    return jnp.zeros((total, w.shape[1]), w.dtype).at[offset:offset + w.shape[0]].set(w)


def _pack_state(s):
    b = s.shape[0]
    st = s.reshape(b, 2, RWKV_PAIRS, 2, RWKV_HEAD, RWKV_HEAD).transpose(0, 1, 2, 3, 5, 4)
    eye = jnp.eye(2, dtype=s.dtype)
    out = st[:, :, :, :, :, None, :] * eye[None, None, None, :, None, :, None]
    return out.reshape(b, 2, RWKV_PAIRS, LANES, LANES)


def _unpack_state(s):
    return s.reshape(s.shape[0], 2, RWKV_HEADS, RWKV_HEAD, RWKV_HEAD)


def kernel(x_prompt, x_sample, cache_k, cache_v, state_rwkv, state_lru, c, c_ctx, w_ada, b_ada, norm1, norm2, w_in, w_out, q_norm, k_norm, rwkv_mu, rwkv_w0, rwkv_w_up, rwkv_a0, rwkv_a_up, rwkv_g_up, rwkv_k_k, rwkv_k_a, rwkv_r_k, rwkv_ln_w, rwkv_ln_b, lru_conv_w, lru_conv_b, lru_wa, lru_ba, lru_wx, lru_bx, lru_lambda, router, exp_w_gate, exp_w_up, exp_w_down):
    n_ctx, seq_ctx, _ = x_prompt.shape
    n_lat, seq_lat, _ = x_sample.shape
    past = cache_k.shape[2]
    assert n_lat + 1 <= COND_ROWS

    cond = jnp.zeros((COND_ROWS, D_MODEL), F32).at[0].set(c_ctx).at[1:1 + n_lat].set(c)
    mod = _ada(cond, w_ada, b_ada.reshape(DEPTH, 1, N_MOD * D_MODEL))
    mod = mod.reshape(DEPTH, COND_ROWS, N_MOD, D_MODEL)

    w_in_bf = w_in.astype(BF16)
    w_out_bf = w_out.astype(BF16)
    lane = jnp.arange(LANES)
    bd = (lane[:, None] // HEAD_DIM == lane[None, :] // HEAD_DIM).astype(BF16)
    pidx = jnp.arange(PREFIX_BLOCK)
    tri = (pidx[None, :] < pidx[:, None]).astype(BF16)
    rope_tabs = _rope_tables(seq_lat)

    paths = [
        dict(x=x_prompt.reshape(n_ctx * seq_ctx, D_MODEL), n_b=n_ctx, seq=seq_ctx, row0=0, rope=None),
        dict(x=x_sample.reshape(n_lat * seq_lat, D_MODEL), n_b=n_lat, seq=seq_lat, row0=1, rope=rope_tabs),
    ]
    new_k, new_v, new_sr, new_sl = [], [], [], []
    for l in range(DEPTH):
        mod_l = mod[l]
        qn = jnp.tile(q_norm[l], LANES // HEAD_DIM)[None, :]
        kn = jnp.tile(k_norm[l], LANES // HEAD_DIM)[None, :]
        prep_w = (
            rwkv_mu[l][None, :], rwkv_k_k[l][None, :], rwkv_k_a[l][None, :],
            rwkv_r_k[l].reshape(1, RWKV_WIDTH), rwkv_w0[l], rwkv_a0[l],
            jnp.stack([_split_weight(_pad_rows(rwkv_w_up[l, d], 0)) for d in range(2)]),
            jnp.stack([_split_weight(_pad_rows(rwkv_a_up[l, d], RWKV_DECAY_LORA)) for d in range(2)]),
            _split_weight(_pad_rows(rwkv_g_up[l], RWKV_DECAY_LORA + RWKV_AAA_LORA)),
        )
        lru_w = (
            lru_conv_w[l], lru_conv_b[l][None, :],
            jnp.stack([_block_diag(lru_wa[l, d]) for d in range(2)]).astype(BF16), lru_ba[l],
            jnp.stack([_block_diag(lru_wx[l, d]) for d in range(2)]).astype(BF16), lru_bx[l],
            lru_lambda[l],
        )
        router_pad = jnp.concatenate(_split(jnp.zeros((D_MODEL, LANES), F32).at[:, :N_EXPERTS].set(router[l])), axis=1)
        mids = []
        for pi, pth in enumerate(paths):
            n_b, seq, row0 = pth["n_b"], pth["seq"], pth["row0"]
            latent = pth["rope"] is not None
            outs = _inproj(pth["x"], mod_l, norm1[l][None, :], w_in_bf, l, qn, kn, bd, pth["rope"], seq, row0)
            if latent:
                q, k_n, k_att, v, p_rwkv, lru_x, lru_g = outs
                cache = (cache_k[:, l].reshape(n_b, past, KV_WIDTH), cache_v[:, l].reshape(n_b, past, KV_WIDTH))
                s0 = _pack_state(state_rwkv[:, l])
                h0 = state_lru[:, l]
            else:
                q, k_n, v, p_rwkv, lru_x, lru_g = outs
                k_att, cache = k_n, None
                s0 = None
                h0 = jnp.zeros((n_b, 2, LRU_WIDTH), F32)
                new_k.append(k_n.reshape(n_b, seq, ATT_KV_HEADS, HEAD_DIM))
                new_v.append(v.reshape(n_b, seq, ATT_KV_HEADS, HEAD_DIM))
            att = _attention(q, k_att, v, n_b, seq, cache)
            r, vv, na, w, kd, b, g, bonus = _rwkv_prep(p_rwkv.reshape(n_b, seq, RWKV_COLS), n_b, seq, prep_w, bd)
            W = RWKV_WIDTH
            yf, yb, s_fin = _rwkv_scan(r.reshape(n_b, seq, W), vv.reshape(n_b, seq, W), na.reshape(n_b, seq, W),
                                       w.reshape(2, n_b, seq, W), kd.reshape(2, n_b, seq, W),
                                       b.reshape(2, n_b, seq, W), s0, n_b, seq)
            lru_out, h_fin = _lru(lru_x, lru_g, h0, lru_w, n_b, seq)
            if not latent:
                new_sr.append(_unpack_state(s_fin))
                new_sl.append(h_fin)
            x1, h2, logits = _outproj(pth["x"], att, yf.reshape(n_b * seq, W), yb.reshape(n_b * seq, W), bonus, g,
                                      lru_out, mod_l, norm2[l][None, :], rwkv_ln_w[l][None, :],
                                      rwkv_ln_b[l][None, :], w_out_bf, l, router_pad, bd, seq, row0)
            xs, slot, aff = _route(logits, h2, tri, n_b, seq)
            mids.append((xs, slot, aff, x1))
        y_a, y_b = _experts(mids[0][0], mids[1][0], exp_w_gate, exp_w_up, exp_w_down, l)
        for pth, (xs, slot, aff, x1), y in zip(paths, mids, (y_a, y_b)):
            pth["x"] = _combine(y, slot, aff, x1, mod_l, pth["n_b"], pth["seq"], pth["row0"])

    y_prompt = paths[0]["x"].reshape(n_ctx, seq_ctx, D_MODEL)
    y_sample = paths[1]["x"].reshape(n_lat, seq_lat, D_MODEL)
    return (y_prompt, y_sample, jnp.stack(new_k, axis=1), jnp.stack(new_v, axis=1),
            jnp.stack(new_sr, axis=1), jnp.stack(new_sl, axis=1))
```

```python
import functools

import numpy as np
import jax
import jax.numpy as jnp
from jax import lax
from jax.experimental import pallas as pl
from jax.experimental.pallas import tpu as pltpu

F32 = jnp.float32
BF16 = jnp.bfloat16

D_MODEL = 1024
DEPTH = 2
GRID_W = 64
ATT_HEADS = 8
ATT_KV_HEADS = 2
HEAD_DIM = 64
ATT_WIDTH = ATT_HEADS * HEAD_DIM
KV_WIDTH = ATT_KV_HEADS * HEAD_DIM
ROPE_THETA = 10000.0
RWKV_HEADS = 4
RWKV_HEAD = 64
RWKV_WIDTH = RWKV_HEADS * RWKV_HEAD
RWKV_DECAY_LORA = 32
RWKV_AAA_LORA = 32
RWKV_GATE_LORA = 64
RWKV_COLS = 3 * RWKV_WIDTH + RWKV_DECAY_LORA + RWKV_AAA_LORA + RWKV_GATE_LORA
GN_EPS = 64e-5
LRU_BLOCKS = 4
LRU_BLOCK = 64
LRU_WIDTH = LRU_BLOCKS * LRU_BLOCK
LRU_CONV_W = 4
LRU_C = 8.0
IN_COLS = ATT_WIDTH + 2 * KV_WIDTH + RWKV_COLS + 2 * LRU_WIDTH
N_EXPERTS = 16
EC_FACTOR = 2
EXPERT_FF = 1024
RMS_EPS = 1e-6

LANES = 128
SUBLANES = 8
VMEM_LIMIT = 56 * 1024 * 1024
N_MOD = 6
COND_ROWS = 8
BRACKET_LO, BRACKET_HI = -1.0, 2.0


def _params(*sem):
    return pltpu.CompilerParams(dimension_semantics=sem, vmem_limit_bytes=VMEM_LIMIT)


def _dot(a, b, precision=None):
    return jnp.dot(a, b, preferred_element_type=F32, precision=precision)


def _sigmoid(x):
    return 1.0 / (1.0 + jnp.exp(-x))


def _softplus(x):
    return jnp.maximum(x, 0.0) + jnp.log1p(jnp.exp(-jnp.abs(x)))


def _split(x):
    hi = x.astype(BF16)
    return hi, (x - hi.astype(F32)).astype(BF16)


def _dot_split(x, w_ref):
    hi, lo = _split(x)
    return _dot(hi, w_ref[0]) + _dot(hi, w_ref[1]) + _dot(lo, w_ref[0])


def _seg_sum(x, ones_bd):
    return _dot(x.astype(BF16), ones_bd)


def _ada_kernel(c_ref, w_ref, b_ref, o_ref):
    c = c_ref[...]
    s_hi, s_lo = _split(c * _sigmoid(c))
    w_hi, w_lo = _split(w_ref[...])
    o_ref[...] = _dot(s_hi, w_hi) + _dot(s_hi, w_lo) + _dot(s_lo, w_hi) + b_ref[...]


def _ada(cond, w_ada, b_ada):
    n_l = w_ada.shape[0]
    tn = 1536
    n_t = N_MOD * D_MODEL // tn
    return pl.pallas_call(
        _ada_kernel,
        grid=(n_l, n_t),
        in_specs=[
            pl.BlockSpec((COND_ROWS, D_MODEL), lambda l, j: (0, 0)),
            pl.BlockSpec((None, D_MODEL, tn), lambda l, j: (l, 0, j)),
            pl.BlockSpec((None, 1, tn), lambda l, j: (l, 0, j)),
        ],
        out_specs=pl.BlockSpec((None, COND_ROWS, tn), lambda l, j: (l, 0, j)),
        out_shape=jax.ShapeDtypeStruct((n_l, COND_ROWS, N_MOD * D_MODEL), F32),
        compiler_params=_params("arbitrary", "arbitrary"),
        name="ada",
    )(cond, w_ada, b_ada)


def _head_rms(x, gain, ones_bd):
    ms = _seg_sum(x * x, ones_bd) * (1.0 / HEAD_DIM)
    return x * lax.rsqrt(ms + RMS_EPS) * gain


def _rope(x, cos, sin_up, sin_dn):
    return x * cos + pltpu.roll(x, LANES - 16, 1) * sin_up + pltpu.roll(x, 16, 1) * sin_dn


def _inproj_kernel(*refs, rope):
    if rope:
        (x_ref, mod_ref, n1_ref, w_ref, qn_ref, kn_ref, bd_ref, cos_ref, su_ref, sd_ref,
         q_ref, ko_ref, ka_ref, v_ref, pr_ref, lx_ref, lg_ref) = refs
    else:
        (x_ref, mod_ref, n1_ref, w_ref, qn_ref, kn_ref, bd_ref,
         q_ref, ko_ref, v_ref, pr_ref, lx_ref, lg_ref) = refs
    x = x_ref[...]
    rs = lax.rsqrt(jnp.mean(x * x, axis=-1, keepdims=True) + RMS_EPS)
    h = (x * rs * n1_ref[...]) * (1.0 + mod_ref[1:2, :]) + mod_ref[0:1, :]
    p = _dot(h.astype(BF16), w_ref[...])
    bd = bd_ref[...]
    for c in range(ATT_WIDTH // LANES):
        qc = _head_rms(p[:, c * LANES:(c + 1) * LANES], qn_ref[...], bd)
        if rope:
            qc = _rope(qc, cos_ref[...], su_ref[...], sd_ref[...])
        q_ref[:, c * LANES:(c + 1) * LANES] = (qc * (HEAD_DIM ** -0.5)).astype(BF16)
    o = ATT_WIDTH
    kc = _head_rms(p[:, o:o + KV_WIDTH], kn_ref[...], bd)
    ko_ref[...] = kc
    if rope:
        ka_ref[...] = _rope(kc, cos_ref[...], su_ref[...], sd_ref[...]).astype(BF16)
    o += KV_WIDTH
    v_ref[...] = p[:, o:o + KV_WIDTH]
    o += KV_WIDTH
    pr_ref[...] = p[:, o:o + RWKV_COLS]
    o += RWKV_COLS
    lx_ref[...] = p[:, o:o + LRU_WIDTH]
    o += LRU_WIDTH
    lg_ref[...] = p[:, o:o + LRU_WIDTH]


def _inproj(x, mod_l, norm1, w_in_bf, layer, qn, kn, bd, rope_tabs, seq, cond_row0):
    n_tok = x.shape[0]
    tm = 512 if seq >= 512 else seq * (512 // seq)
    per_seq = max(seq // tm, 1)
    n_t = n_tok // tm
    rope = rope_tabs is not None
    if cond_row0 == 0:
        row = lambda i: 0
    else:
        row = lambda i: cond_row0 + i // per_seq
    tok = lambda w: pl.BlockSpec((tm, w), lambda i: (i, 0))
    const = lambda shape: pl.BlockSpec(shape, lambda i: tuple(0 for _ in shape))
    in_specs = [
        tok(D_MODEL),
        pl.BlockSpec((None, N_MOD, D_MODEL), lambda i: (row(i), 0, 0)),
        const((1, D_MODEL)),
        pl.BlockSpec((None, D_MODEL, IN_COLS), lambda i: (layer, 0, 0)),
        const((1, LANES)), const((1, LANES)), const((LANES, LANES)),
    ]
    args = [x, mod_l, norm1, w_in_bf, qn, kn, bd]
    outs = [(ATT_WIDTH, BF16), (KV_WIDTH, F32)]
    if rope:
        in_specs += [pl.BlockSpec((tm, LANES), lambda i: (i % per_seq, 0))] * 3
        args += list(rope_tabs)
        outs.append((KV_WIDTH, BF16))
    outs += [(KV_WIDTH, F32), (RWKV_COLS, F32), (LRU_WIDTH, F32), (LRU_WIDTH, F32)]
    return pl.pallas_call(
        functools.partial(_inproj_kernel, rope=rope),
        grid=(n_t,),
        in_specs=in_specs,
        out_specs=[tok(w) for w, _ in outs],
        out_shape=[jax.ShapeDtypeStruct((n_tok, w), dt) for w, dt in outs],
        compiler_params=_params("arbitrary"),
        name="inproj_rope" if rope else "inproj",
    )(*args)


ATTN_ROWS = 1024


def _attn_kernel(*refs, has_cache, n_req):
    if has_cache:
        q_ref, k_ref, v_ref, ck_ref, cv_ref, o_ref, kn_s, ks_s, vn_s, vs_s = refs
    else:
        q_ref, k_ref, v_ref, o_ref, kn_s, ks_s, vn_s, vs_s = refs

    tq = q_ref.shape[0] // n_req
    seq = k_ref.shape[0] // n_req
    lo = lax.broadcasted_iota(jnp.int32, (tq, LANES), 1) < HEAD_DIM
    rep = ATT_HEADS // ATT_KV_HEADS
    for r in range(n_req):
        def stage_keys():
            k = k_ref[r * seq:(r + 1) * seq, :].astype(F32)
            v = v_ref[r * seq:(r + 1) * seq, :]
            if has_cache:
                k = jnp.concatenate([ck_ref[...], k], axis=0)
                v = jnp.concatenate([cv_ref[...], v], axis=0)
            kn_s[...] = k.astype(BF16)
            ks_s[...] = pltpu.roll(k, HEAD_DIM, 1).astype(BF16)
            vn_s[...] = v.astype(BF16)
            vs_s[...] = pltpu.roll(v, HEAD_DIM, 1).astype(BF16)

        if n_req == 1:
            pl.when(pl.program_id(1) == 0)(stage_keys)
        else:
            stage_keys()
        for c in range(ATT_WIDTH // LANES):
            qc = q_ref[r * tq:(r + 1) * tq, c * LANES:(c + 1) * LANES].astype(F32)
            halves = []
            for half in range(2):
                g = (2 * c + half) // rep
                qm = jnp.where(lo if half == 0 else jnp.logical_not(lo), qc, 0.0).astype(BF16)
                k_s, v_s = (kn_s, vn_s) if half == g else (ks_s, vs_s)
                s = lax.dot_general(qm, k_s[...], (((1,), (1,)), ((), ())), preferred_element_type=F32)
                m = jnp.max(s, axis=-1, keepdims=True)
                e = jnp.exp(s - m)
                l = jnp.sum(e, axis=-1, keepdims=True)
                halves.append(_dot(e.astype(BF16), v_s[...]) / l)
            o_ref[r * tq:(r + 1) * tq, c * LANES:(c + 1) * LANES] = (
                jnp.where(lo, halves[0], halves[1]).astype(BF16))


def _attention(q, k, v, n_b, seq, cache=None):
    tq = 256
    n_q = seq // tq
    has_cache = cache is not None
    past = cache[0].shape[1] if has_cache else 0
    n_req = max(1, min(n_b, ATTN_ROWS // seq)) if (n_q == 1 and not has_cache) else 1
    in_specs = [
        pl.BlockSpec((n_req * tq, ATT_WIDTH), lambda b, i: (b * n_q + i, 0)),
        pl.BlockSpec((n_req * seq, KV_WIDTH), lambda b, i: (b, 0)),
        pl.BlockSpec((n_req * seq, KV_WIDTH), lambda b, i: (b, 0)),
    ]
    args = [q, k, v]
    if has_cache:
        in_specs += [pl.BlockSpec((None, past, KV_WIDTH), lambda b, i: (b, 0, 0))] * 2
        args += list(cache)
    return pl.pallas_call(
        functools.partial(_attn_kernel, has_cache=has_cache, n_req=n_req),
        grid=(n_b // n_req, n_q),
        in_specs=in_specs,
        out_specs=pl.BlockSpec((n_req * tq, ATT_WIDTH), lambda b, i: (b * n_q + i, 0)),
        out_shape=jax.ShapeDtypeStruct((n_b * seq, ATT_WIDTH), BF16),
        scratch_shapes=[pltpu.VMEM((past + seq, KV_WIDTH), BF16)] * 4,
        compiler_params=_params("arbitrary", "arbitrary"),
        name="attn_cache" if has_cache else "attn",
    )(*args)


def _rwkv_prep_kernel(p_ref, mu_ref, kk_ref, ka_ref, rk_ref, w0_ref, a0_ref, wup_ref, aup_ref,
                      gup_ref, bd_ref,
                      r_ref, v_ref, na_ref, w_ref, k_ref, b_ref, g_ref, bonus_ref):
    tm = r_ref.shape[0]
    seq = p_ref.shape[0]
    i = pl.program_id(1)
    n_t = pl.num_programs(1)
    start = pl.multiple_of(i * tm, tm)
    cur = p_ref[pl.ds(start, tm), :]
    prev_base = pl.multiple_of(jnp.maximum(start - SUBLANES, 0), SUBLANES)
    next_base = pl.multiple_of(jnp.minimum(start + tm, seq - SUBLANES), SUBLANES)
    prev_row = p_ref[pl.ds(prev_base, SUBLANES), :][SUBLANES - 1:SUBLANES]
    next_row = p_ref[pl.ds(next_base, SUBLANES), :][0:1]
    prev_row = jnp.where(i > 0, prev_row, 0.0)
    next_row = jnp.where(i < n_t - 1, next_row, 0.0)
    row = lax.broadcasted_iota(jnp.int32, cur.shape, 0)
    prev = jnp.where(row == 0, prev_row, pltpu.roll(cur, 1, 0))
    nxt = jnp.where(row == tm - 1, next_row, pltpu.roll(cur, tm - 1, 0))
    ps = cur + mu_ref[...] * (0.5 * (prev + nxt) - cur)

    W = RWKV_WIDTH
    r = ps[:, :W]
    k = ps[:, W:2 * W]
    v = ps[:, 2 * W:3 * W]
    lora = ps[:, 3 * W:]
    r_ref[...] = r.astype(BF16)
    v_ref[...] = v.astype(BF16)
    g_ref[...] = _dot_split(_sigmoid(lora), gup_ref).astype(BF16)
    lora_t = jnp.tanh(lora)
    kk = k * kk_ref[...]
    kk_parts = []
    for c in range(W // LANES):
        kc = kk[:, c * LANES:(c + 1) * LANES]
        nrm = jnp.sqrt(_seg_sum(kc * kc, bd_ref[...]))
        kk_parts.append(kc / jnp.maximum(nrm, 1e-12))
    kk = jnp.concatenate(kk_parts, axis=-1)
    na_ref[...] = (-kk).astype(BF16)
    bonus = None
    for d in range(2):
        w_log = -_softplus(-(w0_ref[d:d + 1, :] + _dot_split(lora_t, wup_ref.at[d]))) - 0.5
        w_ref[d] = -jnp.exp(w_log)
        a_rate = _sigmoid(a0_ref[d:d + 1, :] + _dot_split(lora, aup_ref.at[d]))
        kd = k * (1.0 + (a_rate - 1.0) * ka_ref[...])
        k_ref[d] = kd.astype(BF16)
        b_ref[d] = (kk * a_rate).astype(BF16)
        rkr = r * kd * rk_ref[...]
        parts = [_seg_sum(rkr[:, c * LANES:(c + 1) * LANES], bd_ref[...]) for c in range(W // LANES)]
        bd_term = jnp.concatenate(parts, axis=-1) * v
        bonus = bd_term if bonus is None else bonus + bd_term
    bonus_ref[...] = bonus.astype(BF16)


def _rwkv_prep(p, n_b, seq, wts, bd):
    mu, k_k, k_a, r_k, w0, a0, wup, aup, gup = wts
    tm = min(seq, 512)
    n_t = seq // tm
    W = RWKV_WIDTH
    const = lambda shape: pl.BlockSpec(shape, lambda b, i: tuple(0 for _ in shape))
    tok = pl.BlockSpec((tm, W), lambda b, i: (b * n_t + i, 0))
    tok2 = pl.BlockSpec((2, tm, W), lambda b, i: (0, b * n_t + i, 0))
    two = jax.ShapeDtypeStruct((2, n_b * seq, W), F32)
    one_bf = jax.ShapeDtypeStruct((n_b * seq, W), BF16)
    two_bf = jax.ShapeDtypeStruct((2, n_b * seq, W), BF16)
    return pl.pallas_call(
        _rwkv_prep_kernel,
        grid=(n_b, n_t),
        in_specs=[
            pl.BlockSpec((None, seq, RWKV_COLS), lambda b, i: (b, 0, 0)),
            const((1, RWKV_COLS)), const((1, W)), const((1, W)), const((1, W)),
            const((2, W)), const((2, W)),
            const((2, 2, LANES, W)), const((2, 2, LANES, W)), const((2, LANES, W)),
            const((LANES, LANES)),
        ],
        out_specs=[tok, tok, tok, tok2, tok2, tok2, tok, tok],
        out_shape=[one_bf, one_bf, one_bf, two, two_bf, two_bf, one_bf, one_bf],
        compiler_params=_params("arbitrary", "arbitrary"),
        name="rwkv_prep",
    )(p, mu, k_k, k_a, r_k, w0, a0, wup, aup, gup, bd)


RWKV_BB = 4
RWKV_PAIRS = RWKV_HEADS * RWKV_HEAD // LANES
RWKV_CHUNK = 256
RWKV_SUB = 64
RWKV_STAGE_WIDTH = 64
INV_BASE_SHIFT = 3


def _rwkv_scan_kernel(*refs, has_s0, merge_y):
    (rf_ref, vf_ref, af_ref, wf_ref, kf_ref, bf_ref, rb_ref, vb_ref, ab_ref, wb_ref, kb_ref, bb_ref) = refs[:12]
    s0_ref = refs[12] if has_s0 else None
    if merge_y:
        y_ref, sfin_ref, st_s = refs[-3:]
        yf_ref = yb_ref = y_ref
    else:
        yf_ref, yb_ref, sfin_ref, st_s = refs[-4:]
    _rwkv_scan_body(rf_ref, vf_ref, af_ref, wf_ref, kf_ref, bf_ref, rb_ref, vb_ref, ab_ref, wb_ref, kb_ref, bb_ref,
                    s0_ref, yf_ref, yb_ref, sfin_ref, st_s, merge_y)


def _rwkv_scan_body(rf_ref, vf_ref, af_ref, wf_ref, kf_ref, bf_ref,
                    rb_ref, vb_ref, ab_ref, wb_ref, kb_ref, bb_ref, s0_ref,
                    yf_ref, yb_ref, sfin_ref, st_s, merge_y):
    j = pl.program_id(1)
    n_req, tc = rf_ref.shape[0], rf_ref.shape[1]
    C = RWKV_SUB

    @pl.when(j == 0)
    def _():
        st_s[...] = jnp.zeros(st_s.shape, F32) if s0_ref is None else s0_ref[...]
        if merge_y:
            yf_ref[...] = jnp.zeros(yf_ref.shape, F32)

    lane_c = lax.broadcasted_iota(jnp.int32, (C, LANES), 1)
    row_c = lax.broadcasted_iota(jnp.int32, (C, LANES), 0)
    lo = lane_c < C
    s_idx = lane_c % C
    eye2 = jnp.where(s_idx == row_c, 1.0, 0.0)
    blk_masks = [jnp.where((s_idx >> sh) == (row_c >> sh), 1.0, 0.0) for sh in range(INV_BASE_SHIFT, 7)]
    lane_f = lax.broadcasted_iota(jnp.int32, (LANES, LANES), 1)
    row_f = lax.broadcasted_iota(jnp.int32, (LANES, LANES), 0)
    eye_f = lane_f == row_f
    lo_f = lane_f < C
    bd_mask = (lane_f < C) == (row_f < C)

    def bd(m):
        return jnp.concatenate([jnp.where(lo, m, 0.0), jnp.where(lo, 0.0, m)], axis=0)

    def bd_swap(m):
        return jnp.concatenate([jnp.where(lo, 0.0, m), jnp.where(lo, m, 0.0)], axis=0)

    def bdot(x, y):
        return _dot(x.astype(BF16), y.astype(BF16))

    dirs = ((rf_ref, vf_ref, af_ref, wf_ref, kf_ref, bf_ref, yf_ref),
            (rb_ref, vb_ref, ab_ref, wb_ref, kb_ref, bb_ref, yb_ref))

    def running_sum(x, reverse):
        row8 = row_c % SUBLANES
        for sh in (1, 2, 4):
            if reverse:
                x = x + jnp.where(row8 < SUBLANES - sh, pltpu.roll(x, C - sh, 0), 0.0)
            else:
                x = x + jnp.where(row8 >= sh, pltpu.roll(x, sh, 0), 0.0)
        n_tiles = C // SUBLANES
        out = [None] * n_tiles
        carry = None
        for t in (range(n_tiles - 1, -1, -1) if reverse else range(n_tiles)):
            tile = x[t * SUBLANES:(t + 1) * SUBLANES]
            out[t] = tile if carry is None else tile + carry
            carry = out[t][0:1] if reverse else out[t][SUBLANES - 1:SUBLANES]
        return jnp.concatenate(out, axis=0)

    n_sub = max(1, min(tc // C, RWKV_STAGE_WIDTH // (2 * n_req * RWKV_PAIRS)))

    def sub_chunk(i, carry):
        chains = []
        for d in range(2):
            strict = jnp.where((s_idx < row_c) if d == 0 else (s_idx > row_c), 1.0, 0.0)
            incl = jnp.where((s_idx <= row_c) if d == 0 else (s_idx >= row_c), 1.0, 0.0)
            tri_mask = jnp.concatenate([strict, incl], axis=0)
            last = C - 1 if d == 0 else 0
            for k in range(n_sub):
                ii = i * n_sub + k
                base = pl.multiple_of(ii * C if d == 0 else tc - C - ii * C, C)
                for bi in range(n_req):
                    for pr in range(RWKV_PAIRS):
                        chains.append(dict(d=d, k=k, bi=bi, pr=pr, base=base, refs=dirs[d], tri=tri_mask,
                                           last=last, ls=slice(pr * LANES, (pr + 1) * LANES)))

        for c in chains:
            r_r, v_r, a_r, w_r, k_r, b_r, _ = c["refs"]
            r, v, a, lw, k, b = (ref[c["bi"], pl.ds(c["base"], C), c["ls"]].astype(F32)
                                 for ref in (r_r, v_r, a_r, w_r, k_r, b_r))
            c["G"] = running_sum(lw, c["d"] == 1)
            c["in"] = (r, v, a, lw, k, b)
        for c in chains:
            r, v, a, lw, k, b = c.pop("in")
            G = c.pop("G")
            g_inv = jnp.exp(-G)
            at = a * jnp.exp(G - lw)
            rt = r * jnp.exp(G)
            bt = b * g_inv
            kt = k * g_inv
            g_last = jnp.exp(G[c["last"]:c["last"] + 1, :])
            X = jnp.concatenate([at, rt], axis=0)
            c["X"] = X.astype(BF16)
            bt_b, kt_b = bt.astype(BF16), kt.astype(BF16)
            nt = (((1,), (1,)), ((), ()))
            c["P0"] = lax.dot_general(jnp.where(lo_f, X, 0.0).astype(BF16), jnp.concatenate([bt_b, kt_b], axis=0),
                                      nt, preferred_element_type=F32)
            c["P1"] = lax.dot_general(jnp.where(lo_f, 0.0, X).astype(BF16), jnp.concatenate([kt_b, bt_b], axis=0),
                                      nt, preferred_element_type=F32)
            c["v"] = v
            c["ygt"] = jnp.concatenate([bt * g_last, kt * g_last], axis=0).T.astype(BF16)
            c["g_col"] = jnp.broadcast_to(g_last, (LANES, LANES)).T
        for c in chains:
            P0 = c.pop("P0") * c["tri"]
            P1 = c.pop("P1") * c["tri"]
            ABRB = jnp.where(lo_f, P0, P1)
            c["AKRK"] = jnp.where(lo_f, P1, P0)
            c["AB"], c["RB"] = ABRB[:C], ABRB[C:]
        for c in chains:
            l8 = c["AB"] * blk_masks[0]
            c["T"] = eye2 + l8
            c["Lp"] = bdot(l8, bd(l8))
        for c in chains:
            R = bdot(jnp.concatenate([c["Lp"], c["T"]], axis=0), bd(c["Lp"]))
            c["T"] = c["T"] + R[C:]
            c["Lp"] = R[:C]
        for c in chains:
            c["T"] = c["T"] + bdot(c["T"], bd(c["Lp"]))
        for lvl in range(1, len(blk_masks)):
            for c in chains:
                c["Lp"] = bdot(c["AB"] * (blk_masks[lvl] - blk_masks[lvl - 1]), bd(c["T"]))
            for c in chains:
                c["T"] = c["T"] + bdot(c["T"], bd(c["Lp"]))
        for c in chains:
            c["VK"] = bdot(c["AKRK"], bd_swap(c["v"]))
        for k in range(n_sub):
            now = [c for c in chains if c["k"] == k]
            for c in now:
                c["S0"] = st_s[c["bi"], c["d"], c["pr"]]
                c["XS"] = _dot(c["X"], c["S0"].astype(BF16))
            for c in now:
                c["U"] = bdot(c["T"], bd(c["XS"][:C] + c["VK"][:C]))
            for c in now:
                y_r = c["refs"][6]
                y_new = c["XS"][C:] + c["VK"][C:] + bdot(c["RB"], bd(c["U"]))
                if merge_y:
                    y_new = y_new + y_r[c["bi"], pl.ds(c["base"], C), c["ls"]]
                y_r[c["bi"], pl.ds(c["base"], C), c["ls"]] = y_new
                uv = jnp.concatenate([c["U"], c["v"]], axis=0).astype(BF16)
                st_s[c["bi"], c["d"], c["pr"]] = jnp.where(bd_mask, c["g_col"] * c["S0"] + _dot(c["ygt"], uv), 0.0)
        return carry

    lax.fori_loop(0, tc // (C * n_sub), sub_chunk, 0)

    @pl.when(j == pl.num_programs(1) - 1)
    def _():
        for bi in range(n_req):
            for d in range(2):
                for pr in range(RWKV_PAIRS):
                    mt = st_s[bi, d, pr].T
                    sfin_ref[bi, d, pr] = jnp.where(row_f < C, mt, pltpu.roll(mt, C, 1))[:, :C]


def _rwkv_scan(r, v, na, lw, k, b, s0, n_b, seq):
    tc = min(seq, RWKV_CHUNK)
    n_c = seq // tc
    W = RWKV_WIDTH
    bb = min(RWKV_BB, n_b)
    blk = (bb, tc, W)
    fwd = pl.BlockSpec(blk, lambda g, j: (g, j, 0))
    bwd = pl.BlockSpec(blk, lambda g, j: (g, n_c - 1 - j, 0))
    fwd_d = pl.BlockSpec((None,) + blk, lambda g, j: (0, g, j, 0))
    bwd_d = pl.BlockSpec((None,) + blk, lambda g, j: (1, g, n_c - 1 - j, 0))
    st_blk = (bb, 2, RWKV_PAIRS, LANES, LANES)
    st_spec = pl.BlockSpec(st_blk, lambda g, j: (g, 0, 0, 0, 0))
    y_shape = jax.ShapeDtypeStruct((n_b, seq, W), F32)
    fin_blk = (bb, 2, RWKV_PAIRS, LANES, RWKV_HEAD)
    fin_spec = pl.BlockSpec(fin_blk, lambda g, j: (g, 0, 0, 0, 0))
    has_s0 = s0 is not None
    merge_y = n_c == 1
    n_y = 1 if merge_y else 2
    outs = pl.pallas_call(
        functools.partial(_rwkv_scan_kernel, has_s0=has_s0, merge_y=merge_y),
        grid=(n_b // bb, n_c),
        in_specs=[fwd, fwd, fwd, fwd_d, fwd_d, fwd_d, bwd, bwd, bwd, bwd_d, bwd_d, bwd_d] + [st_spec] * has_s0,
        out_specs=[fwd, bwd][:n_y] + [fin_spec],
        out_shape=[y_shape] * n_y + [jax.ShapeDtypeStruct((n_b,) + fin_blk[1:], F32)],
        scratch_shapes=[pltpu.VMEM(st_blk, F32)],
        compiler_params=_params("arbitrary", "arbitrary"),
        name="rwkv_scan_s0" if has_s0 else "rwkv_scan",
    )(*([r, v, na, lw, k, b, r, v, na, lw, k, b] + [s0] * has_s0))
    return tuple(outs[:n_y]), outs[n_y]


LRU_ROWS = 1024

def _lru_kernel(x_ref, g_ref, h0_ref, cw_ref, cb_ref, wa_ref, ba_ref, wx_ref, bx_ref, lam_ref,
                o_ref, hfin_ref, a_s, u_s, h_s, *, seq):
    rows = x_ref.shape[0]
    n_req = rows // seq
    x = x_ref[...]
    row = lax.broadcasted_iota(jnp.int32, x.shape, 0) % seq
    xm2 = jnp.where(row >= 2, pltpu.roll(x, 2, 0), 0.0)
    xm1 = jnp.where(row >= 1, pltpu.roll(x, 1, 0), 0.0)
    xp1 = jnp.where(row < seq - 1, pltpu.roll(x, rows - 1, 0), 0.0)
    xc = (xm2 * cw_ref[0:1, :] + xm1 * cw_ref[1:2, :] + x * cw_ref[2:3, :] + xp1 * cw_ref[3:4, :]
          + cb_ref[...])
    xb = xc.astype(BF16)
    for d in range(2):
        r_gate = _sigmoid(_dot(xb, wa_ref[d]) + ba_ref[d:d + 1, :])
        i_gate = _sigmoid(_dot(xb, wx_ref[d]) + bx_ref[d:d + 1, :])
        log_a = -LRU_C * r_gate * _softplus(-lam_ref[d:d + 1, :])
        a = jnp.exp(log_a)
        u = jnp.sqrt(1.0 - a * a) * (i_gate * xc)
        for sh in (1, 2, 4):
            if d == 0:
                a_n, u_n = pltpu.roll(a, sh, 0), pltpu.roll(u, sh, 0)
                m = (row % SUBLANES) >= sh
            else:
                a_n, u_n = pltpu.roll(a, rows - sh, 0), pltpu.roll(u, rows - sh, 0)
                m = (row % SUBLANES) < SUBLANES - sh
            u = jnp.where(m, a * u_n + u, u)
            a = jnp.where(m, a * a_n, a)
        a_s[d] = a
        u_s[d] = u

    n_t = seq // SUBLANES
    unroll = 4 // n_req if n_req <= 4 else 1

    def body(i, carry):
        carry = list(carry)
        for k in range(unroll):
            for r in range(n_req):
                hf, hb = carry[2 * r], carry[2 * r + 1]
                base = pl.multiple_of(r * seq + (i * unroll + k) * SUBLANES, SUBLANES)
                h8 = a_s[0, pl.ds(base, SUBLANES), :] * hf + u_s[0, pl.ds(base, SUBLANES), :]
                h_s[0, pl.ds(base, SUBLANES), :] = h8
                carry[2 * r] = h8[SUBLANES - 1:SUBLANES]
                base = pl.multiple_of((r + 1) * seq - SUBLANES - (i * unroll + k) * SUBLANES, SUBLANES)
                h8 = a_s[1, pl.ds(base, SUBLANES), :] * hb + u_s[1, pl.ds(base, SUBLANES), :]
                h_s[1, pl.ds(base, SUBLANES), :] = h8
                carry[2 * r + 1] = h8[0:1]
        return tuple(carry)

    init = tuple(h0_ref[r, d:d + 1, :] for r in range(n_req) for d in range(2))
    fin = lax.fori_loop(0, n_t // unroll, body, init)
    for r in range(n_req):
        hfin_ref[r] = jnp.concatenate([fin[2 * r], fin[2 * r + 1]], axis=0)
    g = g_ref[...]
    gelu = 0.5 * g * (1.0 + jnp.tanh(0.7978845608028654 * (g + 0.044715 * (g * g * g))))
    o_ref[...] = ((h_s[0] + h_s[1]) * gelu).astype(BF16)


def _lru(xb, gb, h0, wts, n_b, seq):
    cw, cb, wa, ba, wx, bx, lam = wts
    C = LRU_WIDTH
    const = lambda shape: pl.BlockSpec(shape, lambda b: tuple(0 for _ in shape))
    n_req = max(1, min(n_b, LRU_ROWS // seq))
    tok = pl.BlockSpec((n_req * seq, C), lambda b: (b, 0))
    st = pl.BlockSpec((n_req, 2, C), lambda b: (b, 0, 0))
    return pl.pallas_call(
        functools.partial(_lru_kernel, seq=seq),
        grid=(n_b // n_req,),
        in_specs=[tok, tok, st, const((LRU_CONV_W, C)), const((1, C)), const((2, C, C)), const((2, C)),
                  const((2, C, C)), const((2, C)), const((2, C))],
        out_specs=[tok, st],
        out_shape=[jax.ShapeDtypeStruct((n_b * seq, C), BF16), jax.ShapeDtypeStruct((n_b, 2, C), F32)],
        scratch_shapes=[pltpu.VMEM((2, n_req * seq, C), F32)] * 3,
        compiler_params=_params("arbitrary"),
        name="lru",
    )(xb, gb, h0, cw, cb, wa, ba, wx, bx, lam)


def _outproj_kernel(x_ref, att_ref, *refs, n_y):
    (bonus_ref, g_ref, lru_ref, mod_ref, n2_ref, lnw_ref, lnb_ref, w_ref, rt_ref, bd_ref,
     x1_ref, h2_ref, lg_ref) = refs[n_y:]
    y = refs[0][...] if n_y == 1 else refs[0][...] + refs[1][...]
    parts = []
    for c in range(RWKV_WIDTH // LANES):
        yc = y[:, c * LANES:(c + 1) * LANES]
        mean = _seg_sum(yc, bd_ref[...]) * (1.0 / RWKV_HEAD)
        dev = yc - mean
        var = _seg_sum(dev * dev, bd_ref[...]) * (1.0 / RWKV_HEAD)
        parts.append(dev * lax.rsqrt(var + GN_EPS))
    yn = jnp.concatenate(parts, axis=-1) * lnw_ref[...] + lnb_ref[...]
    rwkv = (yn + bonus_ref[...]) * g_ref[...]
    o1 = ATT_WIDTH
    o2 = o1 + RWKV_WIDTH
    mixed = (_dot(att_ref[...].astype(BF16), w_ref[:o1, :])
             + _dot(rwkv.astype(BF16), w_ref[o1:o2, :])
             + _dot(lru_ref[...].astype(BF16), w_ref[o2:, :]))
    x1 = x_ref[...] + mod_ref[2:3, :] * mixed
    x1_ref[...] = x1
    rs = lax.rsqrt(jnp.mean(x1 * x1, axis=-1, keepdims=True) + RMS_EPS)
    h2 = (x1 * rs * n2_ref[...]) * (1.0 + mod_ref[4:5, :]) + mod_ref[3:4, :]
    h2_hi = h2.astype(BF16)
    h2_lo = (h2 - h2_hi.astype(F32)).astype(BF16)
    h2_ref[...] = h2_hi
    both = _dot(h2_hi, rt_ref[...])
    lg_ref[...] = both[:, :LANES] + both[:, LANES:] + _dot(h2_lo, rt_ref[:, :LANES])


def _outproj(x, att, ys, bonus, g, lru, mod_l, norm2, lnw, lnb, w_out_bf, layer, router_pad, bd,
             seq, cond_row0):
    n_tok = x.shape[0]
    tm = 512 if seq >= 512 else seq * (512 // seq)
    per_seq = max(seq // tm, 1)
    n_t = n_tok // tm
    if cond_row0 == 0:
        row = lambda i: 0
    else:
        row = lambda i: cond_row0 + i // per_seq
    tok = lambda w: pl.BlockSpec((tm, w), lambda i: (i, 0))
    const = lambda shape: pl.BlockSpec(shape, lambda i: tuple(0 for _ in shape))
    W = RWKV_WIDTH
    return pl.pallas_call(
        functools.partial(_outproj_kernel, n_y=len(ys)),
        grid=(n_t,),
        in_specs=[
            tok(D_MODEL), tok(ATT_WIDTH)] + [tok(W)] * len(ys) + [tok(W), tok(W), tok(LRU_WIDTH),
            pl.BlockSpec((None, N_MOD, D_MODEL), lambda i: (row(i), 0, 0)),
            const((1, D_MODEL)), const((1, W)), const((1, W)),
            pl.BlockSpec((None, D_MODEL, D_MODEL), lambda i: (layer, 0, 0)),
            const((D_MODEL, 2 * LANES)), const((LANES, LANES)),
        ],
        out_specs=[tok(D_MODEL), tok(D_MODEL), tok(LANES)],
        out_shape=[jax.ShapeDtypeStruct((n_tok, D_MODEL), F32),
                   jax.ShapeDtypeStruct((n_tok, D_MODEL), BF16),
                   jax.ShapeDtypeStruct((n_tok, LANES), F32)],
        compiler_params=_params("arbitrary"),
        name="outproj",
    )(x, att, *ys, bonus, g, lru, mod_l, norm2, lnw, lnb, w_out_bf, router_pad, bd)


PREFIX_BLOCK = 256
GATHER_ROWS = 1024


def _prefix_count(mask_f, tri):
    seq = mask_f.shape[0]
    outs = []
    carry = jnp.zeros((1, LANES), F32)
    for blk in range(seq // PREFIX_BLOCK):
        m = mask_f[blk * PREFIX_BLOCK:(blk + 1) * PREFIX_BLOCK]
        outs.append(_dot(tri, m.astype(BF16)) + carry)
        carry = carry + jnp.sum(m, axis=0, keepdims=True)
    return jnp.concatenate(outs, axis=0) if len(outs) > 1 else outs[0]


ROUTE_ROWS = 1024


def _route_kernel(lg_ref, h2_ref, tri_ref, xs_ref, slot_ref, aff_ref, *, cap, seq):
    rows = lg_ref.shape[0]
    n_req = rows // seq
    lane = lax.broadcasted_iota(jnp.int32, (rows, LANES), 1)
    real = lane < N_EXPERTS
    lg = jnp.where(real, lg_ref[...], -jnp.inf)
    m = jnp.max(lg, axis=-1, keepdims=True)
    e = jnp.exp(lg - m)
    aff = e / jnp.sum(e, axis=-1, keepdims=True)
    aff_ref[...] = aff

    a3 = aff.reshape(n_req, seq, LANES)
    a_min = jnp.min(a3, axis=1, keepdims=True)
    a_max = jnp.max(a3, axis=1, keepdims=True)

    def search(carry):
        lo, hi, _ = carry
        mid = 0.5 * (jnp.maximum(lo, a_min) + jnp.minimum(hi, a_max))
        inside = jnp.where(a3 > lo, jnp.where(a3 < hi, 1.0, 0.0), 0.0)
        upper = inside * jnp.where(a3 >= mid, 1.0, 0.0)
        up = jnp.min(jnp.where(upper > 0.0, a3, BRACKET_HI), axis=1, keepdims=True)
        dn = jnp.max(jnp.where(inside - upper > 0.0, a3, BRACKET_LO), axis=1, keepdims=True)
        pivot = jnp.where(up < BRACKET_HI, up, dn)
        found = jnp.where(up < BRACKET_HI, 1.0, jnp.where(dn > BRACKET_LO, 1.0, 0.0))
        cnt = jnp.sum(jnp.where(a3 >= pivot, 1.0, 0.0), axis=1, keepdims=True)
        take_lo = found * jnp.where(cnt >= cap, 1.0, 0.0)
        take_hi = found - take_lo
        return (jnp.where(take_lo > 0.0, pivot, lo), jnp.where(take_hi > 0.0, pivot, hi), found)

    shape = (n_req, 1, LANES)
    init = (jnp.full(shape, BRACKET_LO, F32), jnp.full(shape, BRACKET_HI, F32), jnp.ones(shape, F32))
    thr = lax.while_loop(lambda c: jnp.max(c[2]) > 0.0, search, init)[0]
    gt3 = jnp.where(a3 > thr, 1.0, 0.0)
    eq3 = jnp.where(a3 == thr, 1.0, 0.0)
    need3 = cap - jnp.sum(gt3, axis=1, keepdims=True)
    tri = tri_ref[...]
    c_iota = lax.broadcasted_iota(jnp.int32, (cap, seq), 0).astype(F32)
    real_seq = lax.broadcasted_iota(jnp.int32, (seq, LANES), 1) < N_EXPERTS
    group = max(1, min(N_EXPERTS, GATHER_ROWS // cap))
    for r in range(n_req):
        tok = slice(r * seq, (r + 1) * seq)
        gt, eq = gt3[r], eq3[r]
        sel = jnp.where(real_seq, gt + eq * jnp.where(_prefix_count(eq, tri) < need3[r], 1.0, 0.0), 0.0)
        slot = jnp.where(sel > 0.0, _prefix_count(sel, tri), -1.0)
        slot_ref[tok, :] = slot
        slot_t = slot.T
        h2 = h2_ref[tok, :]
        for g0 in range(0, N_EXPERTS, group):
            onehot = jnp.concatenate(
                [jnp.where(c_iota == slot_t[ex:ex + 1, :], 1.0, 0.0) for ex in range(g0, g0 + group)], axis=0)
            picked = _dot(onehot.astype(BF16), h2)
            for k in range(group):
                xs_ref[g0 + k, r * cap:(r + 1) * cap, :] = picked[k * cap:(k + 1) * cap].astype(BF16)


def _route(logits, h2, tri, n_b, seq):
    cap = EC_FACTOR * seq // N_EXPERTS
    n_req = max(1, min(n_b, ROUTE_ROWS // seq))
    tok = lambda w: pl.BlockSpec((n_req * seq, w), lambda b: (b, 0))
    return pl.pallas_call(
        functools.partial(_route_kernel, cap=cap, seq=seq),
        grid=(n_b // n_req,),
        in_specs=[tok(LANES), tok(D_MODEL), pl.BlockSpec((PREFIX_BLOCK, PREFIX_BLOCK), lambda b: (0, 0))],
        out_specs=[pl.BlockSpec((N_EXPERTS, n_req * cap, D_MODEL), lambda b: (0, b, 0)), tok(LANES), tok(LANES)],
        out_shape=[jax.ShapeDtypeStruct((N_EXPERTS, n_b * cap, D_MODEL), BF16),
                   jax.ShapeDtypeStruct((n_b * seq, LANES), F32),
                   jax.ShapeDtypeStruct((n_b * seq, LANES), F32)],
        compiler_params=_params("arbitrary"),
        name="route",
    )(logits, h2, tri)


EXPERT_RB = 256


def _expert_kernel(xa_ref, xb_ref, wg_ref, wu_ref, wd_ref, ya_ref, yb_ref, wg_s, wu_s, wd_s):
    wg_s[...] = wg_ref[...].astype(BF16)
    wu_s[...] = wu_ref[...].astype(BF16)
    wd_s[...] = wd_ref[...].astype(BF16)
    for x_ref, y_ref in ((xa_ref, ya_ref), (xb_ref, yb_ref)):
        for rb in range(x_ref.shape[0] // EXPERT_RB):
            rows = slice(rb * EXPERT_RB, (rb + 1) * EXPERT_RB)
            x = x_ref[rows, :]
            a = _dot(x, wg_s[...])
            u = _dot(x, wu_s[...])
            hid = (a * _sigmoid(a) * u).astype(BF16)
            y_ref[rows, :] = _dot(hid, wd_s[...]).astype(BF16)


def _experts(xs_a, xs_b, w_gate, w_up, w_down, layer):
    ma, mb = xs_a.shape[1], xs_b.shape[1]
    xspec = lambda m: pl.BlockSpec((None, m, D_MODEL), lambda e: (e, 0, 0))
    return pl.pallas_call(
        _expert_kernel,
        grid=(N_EXPERTS,),
        in_specs=[
            xspec(ma), xspec(mb),
            pl.BlockSpec((None, None, D_MODEL, EXPERT_FF), lambda e: (layer, e, 0, 0)),
            pl.BlockSpec((None, None, D_MODEL, EXPERT_FF), lambda e: (layer, e, 0, 0)),
            pl.BlockSpec((None, None, EXPERT_FF, D_MODEL), lambda e: (layer, e, 0, 0)),
        ],
        out_specs=[xspec(ma), xspec(mb)],
        out_shape=[jax.ShapeDtypeStruct(xs_a.shape, BF16), jax.ShapeDtypeStruct(xs_b.shape, BF16)],
        scratch_shapes=[pltpu.VMEM((D_MODEL, EXPERT_FF), BF16), pltpu.VMEM((D_MODEL, EXPERT_FF), BF16),
                        pltpu.VMEM((EXPERT_FF, D_MODEL), BF16)],
        compiler_params=_params("arbitrary"),
        name="experts",
    )(xs_a, xs_b, w_gate, w_up, w_down)


COMBINE_FUSED_COLS = 512
COMBINE_ROWS = 1024


def _combine_kernel(*refs, cap, fused, n_req):
    if fused:
        y_ref, slot_ref, aff_ref, x1_ref, mod_ref, ex_ref, o_ref = refs
    else:
        y_ref, slot_ref, aff_ref, x1_ref, mod_ref, o_ref = refs
    seq = x1_ref.shape[0]
    slot = slot_ref[...]
    aff = aff_ref[...]
    if fused:
        spread = ex_ref[...]
        a1 = aff.astype(BF16)
        r1 = aff - a1.astype(F32)
        a2 = r1.astype(BF16)
        a3 = (r1 - a2.astype(F32)).astype(BF16)
        slot_x = _dot(slot.astype(BF16), spread)
        aff_x = _dot(a1, spread) + _dot(a2, spread) + _dot(a3, spread)
        cols = N_EXPERTS * cap
        c_pat = (lax.broadcasted_iota(jnp.int32, (seq, cols), 1) % cap).astype(F32)
        w_hi, w_lo = _split(jnp.where(slot_x == c_pat, aff_x, 0.0))
        parts = []
        for r in range(n_req):
            rows = slice(r * (seq // n_req), (r + 1) * (seq // n_req))
            y2 = y_ref[:, r * cap:(r + 1) * cap, :].reshape(cols, D_MODEL)
            parts.append(_dot(w_hi[rows], y2) + _dot(w_lo[rows], y2))
        acc = jnp.concatenate(parts, axis=0) if n_req > 1 else parts[0]
    else:
        c_iota = lax.broadcasted_iota(jnp.int32, (seq, cap), 1).astype(F32)
        acc = jnp.zeros((seq, D_MODEL), F32)
        for ex in range(N_EXPERTS):
            onehot = jnp.where(slot[:, ex:ex + 1] == c_iota, 1.0, 0.0).astype(BF16)
            acc = acc + aff[:, ex:ex + 1] * _dot(onehot, y_ref[ex])
    o_ref[...] = x1_ref[...] + mod_ref[5:6, :] * acc


def _combine(y, slot, aff, x1, mod_l, n_b, seq, cond_row0):
    cap = EC_FACTOR * seq // N_EXPERTS
    fused = N_EXPERTS * cap <= COMBINE_FUSED_COLS
    if cond_row0 == 0:
        row = lambda b: 0
    else:
        row = lambda b: cond_row0 + b
    tm = min(seq, 512)
    n_t = seq // tm
    n_req = max(1, min(n_b, COMBINE_ROWS // seq)) if (fused and cond_row0 == 0) else 1
    tm *= n_req
    tok = lambda w: pl.BlockSpec((tm, w), lambda b, i: (b * n_t + i, 0))
    in_specs = [
        pl.BlockSpec((N_EXPERTS, n_req * cap, D_MODEL), lambda b, i: (0, b, 0)),
        tok(LANES), tok(LANES), tok(D_MODEL),
        pl.BlockSpec((None, N_MOD, D_MODEL), lambda b, i: (row(b), 0, 0)),
    ]
    args = [y, slot, aff, x1, mod_l]
    if fused:
        cols = N_EXPERTS * cap
        spread = (jnp.arange(LANES)[:, None] == (jnp.arange(cols) // cap)[None, :]).astype(BF16)
        in_specs.append(pl.BlockSpec((LANES, cols), lambda b, i: (0, 0)))
        args.append(spread)
    return pl.pallas_call(
        functools.partial(_combine_kernel, cap=cap, fused=fused, n_req=n_req),
        grid=(n_b // n_req, n_t),
        in_specs=in_specs,
        out_specs=tok(D_MODEL),
        out_shape=jax.ShapeDtypeStruct((n_b * seq, D_MODEL), F32),
        compiler_params=_params("arbitrary", "arbitrary"),
        name="combine_fused" if fused else "combine",
    )(*args)


def _rope_tables(seq):
    t = np.arange(seq)
    row = (t // GRID_W).astype(np.float64)
    col = (t % GRID_W).astype(np.float64)
    half = HEAD_DIM // 2
    inv = ROPE_THETA ** (-np.arange(0, half, 2, dtype=np.float64) / half)
    lane = np.arange(LANES)
    u = lane % HEAD_DIM
    pos = np.where((u // half)[None, :] == 0, row[:, None], col[:, None])
    ang = pos * inv[(u % half) % (half // 2)][None, :]
    first = ((u % half) < half // 2)[None, :]
    sin = np.sin(ang)
    tabs = (np.cos(ang), np.where(first, -sin, 0.0), np.where(first, 0.0, sin))
    return tuple(jnp.asarray(x.astype(np.float32)) for x in tabs)


def _block_diag(w):
    n, k, _ = w.shape
    eye = jnp.eye(n, dtype=w.dtype)
    return (eye[:, None, :, None] * w[:, :, None, :]).reshape(n * k, n * k)


def _split_weight(w):
    return jnp.stack(_split(w))


def _pad_rows(w, offset, total=LANES):
    return jnp.zeros((total, w.shape[1]), w.dtype).at[offset:offset + w.shape[0]].set(w)


def _pack_state(s):
    b = s.shape[0]
    st = s.reshape(b, 2, RWKV_PAIRS, 2, RWKV_HEAD, RWKV_HEAD).transpose(0, 1, 2, 3, 5, 4)
    eye = jnp.eye(2, dtype=s.dtype)
    out = st[:, :, :, :, :, None, :] * eye[None, None, None, :, None, :, None]
    return out.reshape(b, 2, RWKV_PAIRS, LANES, LANES)


def _unpack_state(s):
    return s.reshape(s.shape[0], 2, RWKV_HEADS, RWKV_HEAD, RWKV_HEAD)


def kernel(x_prompt, x_sample, cache_k, cache_v, state_rwkv, state_lru, c, c_ctx, w_ada, b_ada, norm1, norm2, w_in, w_out, q_norm, k_norm, rwkv_mu, rwkv_w0, rwkv_w_up, rwkv_a0, rwkv_a_up, rwkv_g_up, rwkv_k_k, rwkv_k_a, rwkv_r_k, rwkv_ln_w, rwkv_ln_b, lru_conv_w, lru_conv_b, lru_wa, lru_ba, lru_wx, lru_bx, lru_lambda, router, exp_w_gate, exp_w_up, exp_w_down):
    n_ctx, seq_ctx, _ = x_prompt.shape
    n_lat, seq_lat, _ = x_sample.shape
    past = cache_k.shape[2]
    assert n_lat + 1 <= COND_ROWS

    cond = jnp.zeros((COND_ROWS, D_MODEL), F32).at[0].set(c_ctx).at[1:1 + n_lat].set(c)
    mod = _ada(cond, w_ada, b_ada.reshape(DEPTH, 1, N_MOD * D_MODEL))
    mod = mod.reshape(DEPTH, COND_ROWS, N_MOD, D_MODEL)

    w_in_bf = w_in.astype(BF16)
    w_out_bf = w_out.astype(BF16)
    lane = jnp.arange(LANES)
    bd = (lane[:, None] // HEAD_DIM == lane[None, :] // HEAD_DIM).astype(BF16)
    pidx = jnp.arange(PREFIX_BLOCK)
    tri = (pidx[None, :] < pidx[:, None]).astype(BF16)
    rope_tabs = _rope_tables(seq_lat)

    paths = [
        dict(x=x_prompt.reshape(n_ctx * seq_ctx, D_MODEL), n_b=n_ctx, seq=seq_ctx, row0=0, rope=None),
        dict(x=x_sample.reshape(n_lat * seq_lat, D_MODEL), n_b=n_lat, seq=seq_lat, row0=1, rope=rope_tabs),
    ]
    new_k, new_v, new_sr, new_sl = [], [], [], []
    for l in range(DEPTH):
        mod_l = mod[l]
        qn = jnp.tile(q_norm[l], LANES // HEAD_DIM)[None, :]
        kn = jnp.tile(k_norm[l], LANES // HEAD_DIM)[None, :]
        prep_w = (
            rwkv_mu[l][None, :], rwkv_k_k[l][None, :], rwkv_k_a[l][None, :],
            rwkv_r_k[l].reshape(1, RWKV_WIDTH), rwkv_w0[l], rwkv_a0[l],
            jnp.stack([_split_weight(_pad_rows(rwkv_w_up[l, d], 0)) for d in range(2)]),
            jnp.stack([_split_weight(_pad_rows(rwkv_a_up[l, d], RWKV_DECAY_LORA)) for d in range(2)]),
            _split_weight(_pad_rows(rwkv_g_up[l], RWKV_DECAY_LORA + RWKV_AAA_LORA)),
        )
        lru_w = (
            lru_conv_w[l], lru_conv_b[l][None, :],
            jnp.stack([_block_diag(lru_wa[l, d]) for d in range(2)]).astype(BF16), lru_ba[l],
            jnp.stack([_block_diag(lru_wx[l, d]) for d in range(2)]).astype(BF16), lru_bx[l],
            lru_lambda[l],
        )
        router_pad = jnp.concatenate(_split(jnp.zeros((D_MODEL, LANES), F32).at[:, :N_EXPERTS].set(router[l])), axis=1)
        mids = []
        for pi, pth in enumerate(paths):
            n_b, seq, row0 = pth["n_b"], pth["seq"], pth["row0"]
            latent = pth["rope"] is not None
            outs = _inproj(pth["x"], mod_l, norm1[l][None, :], w_in_bf, l, qn, kn, bd, pth["rope"], seq, row0)
            if latent:
                q, k_n, k_att, v, p_rwkv, lru_x, lru_g = outs
                cache = (cache_k[:, l].reshape(n_b, past, KV_WIDTH), cache_v[:, l].reshape(n_b, past, KV_WIDTH))
                s0 = _pack_state(state_rwkv[:, l])
                h0 = state_lru[:, l]
            else:
                q, k_n, v, p_rwkv, lru_x, lru_g = outs
                k_att, cache = k_n, None
                s0 = None
                h0 = jnp.zeros((n_b, 2, LRU_WIDTH), F32)
                new_k.append(k_n.reshape(n_b, seq, ATT_KV_HEADS, HEAD_DIM))
                new_v.append(v.reshape(n_b, seq, ATT_KV_HEADS, HEAD_DIM))
            att = _attention(q, k_att, v, n_b, seq, cache)
            r, vv, na, w, kd, b, g, bonus = _rwkv_prep(p_rwkv.reshape(n_b, seq, RWKV_COLS), n_b, seq, prep_w, bd)
            W = RWKV_WIDTH
            ys, s_fin = _rwkv_scan(r.reshape(n_b, seq, W), vv.reshape(n_b, seq, W), na.reshape(n_b, seq, W),
                                   w.reshape(2, n_b, seq, W), kd.reshape(2, n_b, seq, W),
                                   b.reshape(2, n_b, seq, W), s0, n_b, seq)
            ys = [y.reshape(n_b * seq, W) for y in ys]
            lru_out, h_fin = _lru(lru_x, lru_g, h0, lru_w, n_b, seq)
            if not latent:
                new_sr.append(_unpack_state(s_fin))
                new_sl.append(h_fin)
            x1, h2, logits = _outproj(pth["x"], att, ys, bonus, g,
                                      lru_out, mod_l, norm2[l][None, :], rwkv_ln_w[l][None, :],
                                      rwkv_ln_b[l][None, :], w_out_bf, l, router_pad, bd, seq, row0)
            xs, slot, aff = _route(logits, h2, tri, n_b, seq)
            mids.append((xs, slot, aff, x1))
        y_a, y_b = _experts(mids[0][0], mids[1][0], exp_w_gate, exp_w_up, exp_w_down, l)
        for pth, (xs, slot, aff, x1), y in zip(paths, mids, (y_a, y_b)):
            pth["x"] = _combine(y, slot, aff, x1, mod_l, pth["n_b"], pth["seq"], pth["row0"])

    y_prompt = paths[0]["x"].reshape(n_ctx, seq_ctx, D_MODEL)
    y_sample = paths[1]["x"].reshape(n_lat, seq_lat, D_MODEL)
    return (y_prompt, y_sample, jnp.stack(new_k, axis=1), jnp.stack(new_v, axis=1),
            jnp.stack(new_sr, axis=1), jnp.stack(new_sl, axis=1))
```

```python
import functools
import math

import numpy as np
import jax
import jax.numpy as jnp
from jax import lax
from jax.experimental import pallas as pl
from jax.experimental.pallas import tpu as pltpu

F32 = jnp.float32
BF16 = jnp.bfloat16

D_MODEL = 1024
DEPTH = 2
GRID_W = 64
ATT_HEADS = 8
ATT_KV_HEADS = 2
HEAD_DIM = 64
ATT_WIDTH = ATT_HEADS * HEAD_DIM
KV_WIDTH = ATT_KV_HEADS * HEAD_DIM
ROPE_THETA = 10000.0
RWKV_HEADS = 4
RWKV_HEAD = 64
RWKV_WIDTH = RWKV_HEADS * RWKV_HEAD
RWKV_DECAY_LORA = 32
RWKV_AAA_LORA = 32
RWKV_GATE_LORA = 64
RWKV_COLS = 3 * RWKV_WIDTH + RWKV_DECAY_LORA + RWKV_AAA_LORA + RWKV_GATE_LORA
GN_EPS = 64e-5
LRU_BLOCKS = 4
LRU_BLOCK = 64
LRU_WIDTH = LRU_BLOCKS * LRU_BLOCK
LRU_CONV_W = 4
LRU_C = 8.0
IN_COLS = ATT_WIDTH + 2 * KV_WIDTH + RWKV_COLS + 2 * LRU_WIDTH
N_EXPERTS = 16
EC_FACTOR = 2
EXPERT_FF = 1024
RMS_EPS = 1e-6

LANES = 128
SUBLANES = 8
VMEM_LIMIT = 56 * 1024 * 1024
N_MOD = 6
COND_ROWS = 8
BRACKET_LO, BRACKET_HI = -1.0, 2.0


def _params(*sem):
    return pltpu.CompilerParams(dimension_semantics=sem, vmem_limit_bytes=VMEM_LIMIT)


def _dot(a, b, precision=None):
    return jnp.dot(a, b, preferred_element_type=F32, precision=precision)


def _sigmoid(x):
    return 1.0 / (1.0 + jnp.exp(-x))


def _softplus(x):
    return jnp.maximum(x, 0.0) + jnp.log1p(jnp.exp(-jnp.abs(x)))


def _split(x):
    hi = x.astype(BF16)
    return hi, (x - hi.astype(F32)).astype(BF16)


def _dot_split(x, w_ref):
    hi, lo = _split(x)
    return _dot(hi, w_ref[0]) + _dot(hi, w_ref[1]) + _dot(lo, w_ref[0])


def _seg_sum(x, ones_bd):
    return _dot(x.astype(BF16), ones_bd)


def _ada_kernel(c_ref, w_ref, b_ref, o_ref):
    c = c_ref[...]
    s_hi, s_lo = _split(c * _sigmoid(c))
    w_hi, w_lo = _split(w_ref[...])
    o_ref[...] = _dot(s_hi, w_hi) + _dot(s_hi, w_lo) + _dot(s_lo, w_hi) + b_ref[...]


def _ada(cond, w_ada, b_ada):
    n_l = w_ada.shape[0]
    tn = 1536
    n_t = N_MOD * D_MODEL // tn
    return pl.pallas_call(
        _ada_kernel,
        grid=(n_l, n_t),
        in_specs=[
            pl.BlockSpec((COND_ROWS, D_MODEL), lambda l, j: (0, 0)),
            pl.BlockSpec((None, D_MODEL, tn), lambda l, j: (l, 0, j)),
            pl.BlockSpec((None, 1, tn), lambda l, j: (l, 0, j)),
        ],
        out_specs=pl.BlockSpec((None, COND_ROWS, tn), lambda l, j: (l, 0, j)),
        out_shape=jax.ShapeDtypeStruct((n_l, COND_ROWS, N_MOD * D_MODEL), F32),
        compiler_params=_params("arbitrary", "arbitrary"),
        name="ada",
    )(cond, w_ada, b_ada)


def _head_rms(x, gain, ones_bd):
    ms = _seg_sum(x * x, ones_bd) * (1.0 / HEAD_DIM)
    return x * lax.rsqrt(ms + RMS_EPS) * gain


def _rope(x, cos, sin_up, sin_dn):
    return x * cos + pltpu.roll(x, LANES - 16, 1) * sin_up + pltpu.roll(x, 16, 1) * sin_dn


def _inproj_kernel(*refs, rope):
    if rope:
        (x_ref, mod_ref, n1_ref, w_ref, qn_ref, kn_ref, bd_ref, cos_ref, su_ref, sd_ref,
         q_ref, ko_ref, ka_ref, v_ref, pr_ref, lx_ref, lg_ref) = refs
    else:
        (x_ref, mod_ref, n1_ref, w_ref, qn_ref, kn_ref, bd_ref,
         q_ref, ko_ref, v_ref, pr_ref, lx_ref, lg_ref) = refs
    x = x_ref[...]
    rs = lax.rsqrt(jnp.mean(x * x, axis=-1, keepdims=True) + RMS_EPS)
    h = (x * rs * n1_ref[...]) * (1.0 + mod_ref[1:2, :]) + mod_ref[0:1, :]
    p = _dot(h.astype(BF16), w_ref[...])
    bd = bd_ref[...]
    for c in range(ATT_WIDTH // LANES):
        qc = _head_rms(p[:, c * LANES:(c + 1) * LANES], qn_ref[...], bd)
        if rope:
            qc = _rope(qc, cos_ref[...], su_ref[...], sd_ref[...])
        q_ref[:, c * LANES:(c + 1) * LANES] = (qc * (HEAD_DIM ** -0.5)).astype(BF16)
    o = ATT_WIDTH
    kc = _head_rms(p[:, o:o + KV_WIDTH], kn_ref[...], bd)
    ko_ref[...] = kc
    if rope:
        ka_ref[...] = _rope(kc, cos_ref[...], su_ref[...], sd_ref[...]).astype(BF16)
    o += KV_WIDTH
    v_ref[...] = p[:, o:o + KV_WIDTH]
    o += KV_WIDTH
    pr_ref[...] = p[:, o:o + RWKV_COLS]
    o += RWKV_COLS
    lx_ref[...] = p[:, o:o + LRU_WIDTH]
    o += LRU_WIDTH
    lg_ref[...] = p[:, o:o + LRU_WIDTH]


def _inproj(x, mod_l, norm1, w_in_bf, layer, qn, kn, bd, rope_tabs, seq, cond_row0):
    n_tok = x.shape[0]
    tm = 512 if seq >= 512 else seq * (512 // seq)
    per_seq = max(seq // tm, 1)
    n_t = n_tok // tm
    rope = rope_tabs is not None
    if cond_row0 == 0:
        row = lambda i: 0
    else:
        row = lambda i: cond_row0 + i // per_seq
    tok = lambda w: pl.BlockSpec((tm, w), lambda i: (i, 0))
    const = lambda shape: pl.BlockSpec(shape, lambda i: tuple(0 for _ in shape))
    in_specs = [
        tok(D_MODEL),
        pl.BlockSpec((None, N_MOD, D_MODEL), lambda i: (row(i), 0, 0)),
        const((1, D_MODEL)),
        pl.BlockSpec((None, D_MODEL, IN_COLS), lambda i: (layer, 0, 0)),
        const((1, LANES)), const((1, LANES)), const((LANES, LANES)),
    ]
    args = [x, mod_l, norm1, w_in_bf, qn, kn, bd]
    outs = [(ATT_WIDTH, BF16), (KV_WIDTH, F32)]
    if rope:
        in_specs += [pl.BlockSpec((tm, LANES), lambda i: (i % per_seq, 0))] * 3
        args += list(rope_tabs)
        outs.append((KV_WIDTH, BF16))
    outs += [(KV_WIDTH, F32), (RWKV_COLS, F32), (LRU_WIDTH, F32), (LRU_WIDTH, F32)]
    return pl.pallas_call(
        functools.partial(_inproj_kernel, rope=rope),
        grid=(n_t,),
        in_specs=in_specs,
        out_specs=[tok(w) for w, _ in outs],
        out_shape=[jax.ShapeDtypeStruct((n_tok, w), dt) for w, dt in outs],
        compiler_params=_params("arbitrary"),
        name="inproj_rope" if rope else "inproj",
    )(*args)


ATTN_ROWS = 1024


def _attn_kernel(*refs, has_cache, n_req):
    if has_cache:
        q_ref, k_ref, v_ref, ck_ref, cv_ref, o_ref, kn_s, ks_s, vn_s, vs_s = refs
    else:
        q_ref, k_ref, v_ref, o_ref, kn_s, ks_s, vn_s, vs_s = refs

    tq = q_ref.shape[0] // n_req
    seq = k_ref.shape[0] // n_req
    lo = lax.broadcasted_iota(jnp.int32, (tq, LANES), 1) < HEAD_DIM
    rep = ATT_HEADS // ATT_KV_HEADS
    for r in range(n_req):
        def stage_keys():
            k = k_ref[r * seq:(r + 1) * seq, :].astype(F32)
            v = v_ref[r * seq:(r + 1) * seq, :]
            if has_cache:
                k = jnp.concatenate([ck_ref[...], k], axis=0)
                v = jnp.concatenate([cv_ref[...], v], axis=0)
            kn_s[...] = k.astype(BF16)
            ks_s[...] = pltpu.roll(k, HEAD_DIM, 1).astype(BF16)
            vn_s[...] = v.astype(BF16)
            vs_s[...] = pltpu.roll(v, HEAD_DIM, 1).astype(BF16)

        if n_req == 1:
            pl.when(pl.program_id(1) == 0)(stage_keys)
        else:
            stage_keys()
        for c in range(ATT_WIDTH // LANES):
            qc = q_ref[r * tq:(r + 1) * tq, c * LANES:(c + 1) * LANES].astype(F32)
            halves = []
            for half in range(2):
                g = (2 * c + half) // rep
                qm = jnp.where(lo if half == 0 else jnp.logical_not(lo), qc, 0.0).astype(BF16)
                k_s, v_s = (kn_s, vn_s) if half == g else (ks_s, vs_s)
                s = lax.dot_general(qm, k_s[...], (((1,), (1,)), ((), ())), preferred_element_type=F32)
                m = jnp.max(s, axis=-1, keepdims=True)
                e = jnp.exp(s - m)
                l = jnp.sum(e, axis=-1, keepdims=True)
                halves.append(_dot(e.astype(BF16), v_s[...]) / l)
            o_ref[r * tq:(r + 1) * tq, c * LANES:(c + 1) * LANES] = (
                jnp.where(lo, halves[0], halves[1]).astype(BF16))


def _attention(q, k, v, n_b, seq, cache=None):
    tq = 256
    n_q = seq // tq
    has_cache = cache is not None
    past = cache[0].shape[1] if has_cache else 0
    n_req = max(1, min(n_b, ATTN_ROWS // seq)) if (n_q == 1 and not has_cache) else 1
    in_specs = [
        pl.BlockSpec((n_req * tq, ATT_WIDTH), lambda b, i: (b * n_q + i, 0)),
        pl.BlockSpec((n_req * seq, KV_WIDTH), lambda b, i: (b, 0)),
        pl.BlockSpec((n_req * seq, KV_WIDTH), lambda b, i: (b, 0)),
    ]
    args = [q, k, v]
    if has_cache:
        in_specs += [pl.BlockSpec((None, past, KV_WIDTH), lambda b, i: (b, 0, 0))] * 2
        args += list(cache)
    return pl.pallas_call(
        functools.partial(_attn_kernel, has_cache=has_cache, n_req=n_req),
        grid=(n_b // n_req, n_q),
        in_specs=in_specs,
        out_specs=pl.BlockSpec((n_req * tq, ATT_WIDTH), lambda b, i: (b * n_q + i, 0)),
        out_shape=jax.ShapeDtypeStruct((n_b * seq, ATT_WIDTH), BF16),
        scratch_shapes=[pltpu.VMEM((past + seq, KV_WIDTH), BF16)] * 4,
        compiler_params=_params("arbitrary", "arbitrary"),
        name="attn_cache" if has_cache else "attn",
    )(*args)


def _rwkv_prep_kernel(p_ref, mu_ref, kk_ref, ka_ref, rk_ref, w0_ref, a0_ref, wup_ref, aup_ref,
                      gup_ref, bd_ref,
                      r_ref, v_ref, na_ref, w_ref, k_ref, b_ref, g_ref, bonus_ref):
    tm = r_ref.shape[0]
    seq = p_ref.shape[0]
    i = pl.program_id(1)
    n_t = pl.num_programs(1)
    start = pl.multiple_of(i * tm, tm)
    cur = p_ref[pl.ds(start, tm), :]
    prev_base = pl.multiple_of(jnp.maximum(start - SUBLANES, 0), SUBLANES)
    next_base = pl.multiple_of(jnp.minimum(start + tm, seq - SUBLANES), SUBLANES)
    prev_row = p_ref[pl.ds(prev_base, SUBLANES), :][SUBLANES - 1:SUBLANES]
    next_row = p_ref[pl.ds(next_base, SUBLANES), :][0:1]
    prev_row = jnp.where(i > 0, prev_row, 0.0)
    next_row = jnp.where(i < n_t - 1, next_row, 0.0)
    row = lax.broadcasted_iota(jnp.int32, cur.shape, 0)
    prev = jnp.where(row == 0, prev_row, pltpu.roll(cur, 1, 0))
    nxt = jnp.where(row == tm - 1, next_row, pltpu.roll(cur, tm - 1, 0))
    ps = cur + mu_ref[...] * (0.5 * (prev + nxt) - cur)

    W = RWKV_WIDTH
    r = ps[:, :W]
    k = ps[:, W:2 * W]
    v = ps[:, 2 * W:3 * W]
    lora = ps[:, 3 * W:]
    r_ref[...] = r.astype(BF16)
    v_ref[...] = v.astype(BF16)
    g_ref[...] = _dot_split(_sigmoid(lora), gup_ref).astype(BF16)
    lora_t = jnp.tanh(lora)
    kk = k * kk_ref[...]
    kk_parts = []
    for c in range(W // LANES):
        kc = kk[:, c * LANES:(c + 1) * LANES]
        nrm = jnp.sqrt(_seg_sum(kc * kc, bd_ref[...]))
        kk_parts.append(kc / jnp.maximum(nrm, 1e-12))
    kk = jnp.concatenate(kk_parts, axis=-1)
    na_ref[...] = (-kk).astype(BF16)
    bonus = None
    for d in range(2):
        z = w0_ref[d:d + 1, :] + _dot_split(lora_t, wup_ref.at[d])
        w_ref[d] = -math.exp(-0.5) * _sigmoid(z)
        a_rate = _sigmoid(a0_ref[d:d + 1, :] + _dot_split(lora, aup_ref.at[d]))
        kd = k * (1.0 + (a_rate - 1.0) * ka_ref[...])
        k_ref[d] = kd.astype(BF16)
        b_ref[d] = (kk * a_rate).astype(BF16)
        rkr = r * kd * rk_ref[...]
        parts = [_seg_sum(rkr[:, c * LANES:(c + 1) * LANES], bd_ref[...]) for c in range(W // LANES)]
        bd_term = jnp.concatenate(parts, axis=-1) * v
        bonus = bd_term if bonus is None else bonus + bd_term
    bonus_ref[...] = bonus.astype(BF16)


def _rwkv_prep(p, n_b, seq, wts, bd):
    mu, k_k, k_a, r_k, w0, a0, wup, aup, gup = wts
    tm = min(seq, 512)
    n_t = seq // tm
    W = RWKV_WIDTH
    const = lambda shape: pl.BlockSpec(shape, lambda b, i: tuple(0 for _ in shape))
    tok = pl.BlockSpec((tm, W), lambda b, i: (b * n_t + i, 0))
    tok2 = pl.BlockSpec((2, tm, W), lambda b, i: (0, b * n_t + i, 0))
    two = jax.ShapeDtypeStruct((2, n_b * seq, W), F32)
    one_bf = jax.ShapeDtypeStruct((n_b * seq, W), BF16)
    two_bf = jax.ShapeDtypeStruct((2, n_b * seq, W), BF16)
    return pl.pallas_call(
        _rwkv_prep_kernel,
        grid=(n_b, n_t),
        in_specs=[
            pl.BlockSpec((None, seq, RWKV_COLS), lambda b, i: (b, 0, 0)),
            const((1, RWKV_COLS)), const((1, W)), const((1, W)), const((1, W)),
            const((2, W)), const((2, W)),
            const((2, 2, LANES, W)), const((2, 2, LANES, W)), const((2, LANES, W)),
            const((LANES, LANES)),
        ],
        out_specs=[tok, tok, tok, tok2, tok2, tok2, tok, tok],
        out_shape=[one_bf, one_bf, one_bf, two, two_bf, two_bf, one_bf, one_bf],
        compiler_params=_params("arbitrary", "arbitrary"),
        name="rwkv_prep",
    )(p, mu, k_k, k_a, r_k, w0, a0, wup, aup, gup, bd)


RWKV_BB = 4
RWKV_PAIRS = RWKV_HEADS * RWKV_HEAD // LANES
RWKV_CHUNK = 256
RWKV_SUB = 64
RWKV_STAGE_WIDTH = 64
INV_BASE_SHIFT = 3


def _rwkv_scan_kernel(*refs, has_s0, merge_y):
    (rf_ref, vf_ref, af_ref, wf_ref, kf_ref, bf_ref, rb_ref, vb_ref, ab_ref, wb_ref, kb_ref, bb_ref) = refs[:12]
    s0_ref = refs[12] if has_s0 else None
    if merge_y:
        y_ref, sfin_ref, st_s = refs[-3:]
        yf_ref = yb_ref = y_ref
    else:
        yf_ref, yb_ref, sfin_ref, st_s = refs[-4:]
    _rwkv_scan_body(rf_ref, vf_ref, af_ref, wf_ref, kf_ref, bf_ref, rb_ref, vb_ref, ab_ref, wb_ref, kb_ref, bb_ref,
                    s0_ref, yf_ref, yb_ref, sfin_ref, st_s, merge_y)


def _rwkv_scan_body(rf_ref, vf_ref, af_ref, wf_ref, kf_ref, bf_ref,
                    rb_ref, vb_ref, ab_ref, wb_ref, kb_ref, bb_ref, s0_ref,
                    yf_ref, yb_ref, sfin_ref, st_s, merge_y):
    j = pl.program_id(1)
    n_req, tc = rf_ref.shape[0], rf_ref.shape[1]
    C = RWKV_SUB

    @pl.when(j == 0)
    def _():
        st_s[...] = jnp.zeros(st_s.shape, F32) if s0_ref is None else s0_ref[...]
        if merge_y:
            yf_ref[...] = jnp.zeros(yf_ref.shape, F32)

    lane_c = lax.broadcasted_iota(jnp.int32, (C, LANES), 1)
    row_c = lax.broadcasted_iota(jnp.int32, (C, LANES), 0)
    lo = lane_c < C
    s_idx = lane_c % C
    eye2 = jnp.where(s_idx == row_c, 1.0, 0.0)
    blk_masks = [jnp.where((s_idx >> sh) == (row_c >> sh), 1.0, 0.0) for sh in range(INV_BASE_SHIFT, 7)]
    lane_f = lax.broadcasted_iota(jnp.int32, (LANES, LANES), 1)
    row_f = lax.broadcasted_iota(jnp.int32, (LANES, LANES), 0)
    eye_f = lane_f == row_f
    lo_f = lane_f < C
    bd_mask = (lane_f < C) == (row_f < C)

    def bd(m):
        return jnp.concatenate([jnp.where(lo, m, 0.0), jnp.where(lo, 0.0, m)], axis=0)

    def bd_swap(m):
        return jnp.concatenate([jnp.where(lo, 0.0, m), jnp.where(lo, m, 0.0)], axis=0)

    def bdot(x, y):
        return _dot(x.astype(BF16), y.astype(BF16))

    dirs = ((rf_ref, vf_ref, af_ref, wf_ref, kf_ref, bf_ref, yf_ref),
            (rb_ref, vb_ref, ab_ref, wb_ref, kb_ref, bb_ref, yb_ref))

    def running_sum(x, reverse):
        row8 = row_c % SUBLANES
        for sh in (1, 2, 4):
            if reverse:
                x = x + jnp.where(row8 < SUBLANES - sh, pltpu.roll(x, C - sh, 0), 0.0)
            else:
                x = x + jnp.where(row8 >= sh, pltpu.roll(x, sh, 0), 0.0)
        n_tiles = C // SUBLANES
        out = [None] * n_tiles
        carry = None
        for t in (range(n_tiles - 1, -1, -1) if reverse else range(n_tiles)):
            tile = x[t * SUBLANES:(t + 1) * SUBLANES]
            out[t] = tile if carry is None else tile + carry
            carry = out[t][0:1] if reverse else out[t][SUBLANES - 1:SUBLANES]
        return jnp.concatenate(out, axis=0)

    n_sub = max(1, min(tc // C, RWKV_STAGE_WIDTH // (2 * n_req * RWKV_PAIRS)))

    def sub_chunk(i, carry):
        chains = []
        for d in range(2):
            strict = jnp.where((s_idx < row_c) if d == 0 else (s_idx > row_c), 1.0, 0.0)
            incl = jnp.where((s_idx <= row_c) if d == 0 else (s_idx >= row_c), 1.0, 0.0)
            tri_mask = jnp.concatenate([strict, incl], axis=0)
            last = C - 1 if d == 0 else 0
            for k in range(n_sub):
                ii = i * n_sub + k
                base = pl.multiple_of(ii * C if d == 0 else tc - C - ii * C, C)
                for bi in range(n_req):
                    for pr in range(RWKV_PAIRS):
                        chains.append(dict(d=d, k=k, bi=bi, pr=pr, base=base, refs=dirs[d], tri=tri_mask,
                                           last=last, ls=slice(pr * LANES, (pr + 1) * LANES)))

        for c in chains:
            r_r, v_r, a_r, w_r, k_r, b_r, _ = c["refs"]
            r, v, a, lw, k, b = (ref[c["bi"], pl.ds(c["base"], C), c["ls"]].astype(F32)
                                 for ref in (r_r, v_r, a_r, w_r, k_r, b_r))
            c["G"] = running_sum(lw, c["d"] == 1)
            c["in"] = (r, v, a, lw, k, b)
        for c in chains:
            r, v, a, lw, k, b = c.pop("in")
            G = c.pop("G")
            g_inv = jnp.exp(-G)
            at = a * jnp.exp(G - lw)
            rt = r * jnp.exp(G)
            bt = b * g_inv
            kt = k * g_inv
            g_last = jnp.exp(G[c["last"]:c["last"] + 1, :])
            X = jnp.concatenate([at, rt], axis=0)
            c["X"] = X.astype(BF16)
            bt_b, kt_b = bt.astype(BF16), kt.astype(BF16)
            nt = (((1,), (1,)), ((), ()))
            c["P0"] = lax.dot_general(jnp.where(lo_f, X, 0.0).astype(BF16), jnp.concatenate([bt_b, kt_b], axis=0),
                                      nt, preferred_element_type=F32)
            c["P1"] = lax.dot_general(jnp.where(lo_f, 0.0, X).astype(BF16), jnp.concatenate([kt_b, bt_b], axis=0),
                                      nt, preferred_element_type=F32)
            c["v"] = v
            c["ygt"] = jnp.concatenate([bt * g_last, kt * g_last], axis=0).T.astype(BF16)
            c["g_col"] = jnp.broadcast_to(g_last, (LANES, LANES)).T
        for c in chains:
            P0 = c.pop("P0") * c["tri"]
            P1 = c.pop("P1") * c["tri"]
            ABRB = jnp.where(lo_f, P0, P1)
            c["AKRK"] = jnp.where(lo_f, P1, P0)
            c["AB"], c["RB"] = ABRB[:C], ABRB[C:]
        for c in chains:
            l8 = c["AB"] * blk_masks[0]
            c["T"] = eye2 + l8
            c["Lp"] = bdot(l8, bd(l8))
        for c in chains:
            R = bdot(jnp.concatenate([c["Lp"], c["T"]], axis=0), bd(c["Lp"]))
            c["T"] = c["T"] + R[C:]
            c["Lp"] = R[:C]
        for c in chains:
            c["T"] = c["T"] + bdot(c["T"], bd(c["Lp"]))
        for lvl in range(1, len(blk_masks)):
            for c in chains:
                c["Lp"] = bdot(c["AB"] * (blk_masks[lvl] - blk_masks[lvl - 1]), bd(c["T"]))
            for c in chains:
                c["T"] = c["T"] + bdot(c["T"], bd(c["Lp"]))
        for c in chains:
            c["VK"] = bdot(c["AKRK"], bd_swap(c["v"]))
        for k in range(n_sub):
            now = [c for c in chains if c["k"] == k]
            for c in now:
                c["S0"] = st_s[c["bi"], c["d"], c["pr"]]
                c["XS"] = _dot(c["X"], c["S0"].astype(BF16))
            for c in now:
                c["U"] = bdot(c["T"], bd(c["XS"][:C] + c["VK"][:C]))
            for c in now:
                y_r = c["refs"][6]
                y_new = c["XS"][C:] + c["VK"][C:] + bdot(c["RB"], bd(c["U"]))
                if merge_y:
                    y_new = y_new + y_r[c["bi"], pl.ds(c["base"], C), c["ls"]]
                y_r[c["bi"], pl.ds(c["base"], C), c["ls"]] = y_new
                uv = jnp.concatenate([c["U"], c["v"]], axis=0).astype(BF16)
                st_s[c["bi"], c["d"], c["pr"]] = jnp.where(bd_mask, c["g_col"] * c["S0"] + _dot(c["ygt"], uv), 0.0)
        return carry

    lax.fori_loop(0, tc // (C * n_sub), sub_chunk, 0)

    @pl.when(j == pl.num_programs(1) - 1)
    def _():
        for bi in range(n_req):
            for d in range(2):
                for pr in range(RWKV_PAIRS):
                    mt = st_s[bi, d, pr].T
                    sfin_ref[bi, d, pr] = jnp.where(row_f < C, mt, pltpu.roll(mt, C, 1))[:, :C]


def _rwkv_scan(r, v, na, lw, k, b, s0, n_b, seq):
    tc = min(seq, RWKV_CHUNK)
    n_c = seq // tc
    W = RWKV_WIDTH
    bb = min(RWKV_BB, n_b)
    blk = (bb, tc, W)
    fwd = pl.BlockSpec(blk, lambda g, j: (g, j, 0))
    bwd = pl.BlockSpec(blk, lambda g, j: (g, n_c - 1 - j, 0))
    fwd_d = pl.BlockSpec((None,) + blk, lambda g, j: (0, g, j, 0))
    bwd_d = pl.BlockSpec((None,) + blk, lambda g, j: (1, g, n_c - 1 - j, 0))
    st_blk = (bb, 2, RWKV_PAIRS, LANES, LANES)
    st_spec = pl.BlockSpec(st_blk, lambda g, j: (g, 0, 0, 0, 0))
    y_shape = jax.ShapeDtypeStruct((n_b, seq, W), F32)
    fin_blk = (bb, 2, RWKV_PAIRS, LANES, RWKV_HEAD)
    fin_spec = pl.BlockSpec(fin_blk, lambda g, j: (g, 0, 0, 0, 0))
    has_s0 = s0 is not None
    merge_y = n_c == 1
    n_y = 1 if merge_y else 2
    outs = pl.pallas_call(
        functools.partial(_rwkv_scan_kernel, has_s0=has_s0, merge_y=merge_y),
        grid=(n_b // bb, n_c),
        in_specs=[fwd, fwd, fwd, fwd_d, fwd_d, fwd_d, bwd, bwd, bwd, bwd_d, bwd_d, bwd_d] + [st_spec] * has_s0,
        out_specs=[fwd, bwd][:n_y] + [fin_spec],
        out_shape=[y_shape] * n_y + [jax.ShapeDtypeStruct((n_b,) + fin_blk[1:], F32)],
        scratch_shapes=[pltpu.VMEM(st_blk, F32)],
        compiler_params=_params("arbitrary", "arbitrary"),
        name="rwkv_scan_s0" if has_s0 else "rwkv_scan",
    )(*([r, v, na, lw, k, b, r, v, na, lw, k, b] + [s0] * has_s0))
    return tuple(outs[:n_y]), outs[n_y]


LRU_ROWS = 1024

def _lru_kernel(x_ref, g_ref, h0_ref, cw_ref, cb_ref, wa_ref, ba_ref, wx_ref, bx_ref, lam_ref,
                o_ref, hfin_ref, a_s, u_s, h_s, *, seq):
    rows = x_ref.shape[0]
    n_req = rows // seq
    x = x_ref[...]
    row = lax.broadcasted_iota(jnp.int32, x.shape, 0) % seq
    xm2 = jnp.where(row >= 2, pltpu.roll(x, 2, 0), 0.0)
    xm1 = jnp.where(row >= 1, pltpu.roll(x, 1, 0), 0.0)
    xp1 = jnp.where(row < seq - 1, pltpu.roll(x, rows - 1, 0), 0.0)
    xc = (xm2 * cw_ref[0:1, :] + xm1 * cw_ref[1:2, :] + x * cw_ref[2:3, :] + xp1 * cw_ref[3:4, :]
          + cb_ref[...])
    xb = xc.astype(BF16)
    for d in range(2):
        r_gate = _sigmoid(_dot(xb, wa_ref[d]) + ba_ref[d:d + 1, :])
        i_gate = _sigmoid(_dot(xb, wx_ref[d]) + bx_ref[d:d + 1, :])
        log_a = -LRU_C * r_gate * _softplus(-lam_ref[d:d + 1, :])
        a = jnp.exp(log_a)
        u = jnp.sqrt(1.0 - a * a) * (i_gate * xc)
        for sh in (1, 2, 4):
            if d == 0:
                a_n, u_n = pltpu.roll(a, sh, 0), pltpu.roll(u, sh, 0)
                m = (row % SUBLANES) >= sh
            else:
                a_n, u_n = pltpu.roll(a, rows - sh, 0), pltpu.roll(u, rows - sh, 0)
                m = (row % SUBLANES) < SUBLANES - sh
            u = jnp.where(m, a * u_n + u, u)
            a = jnp.where(m, a * a_n, a)
        a_s[d] = a
        u_s[d] = u

    n_t = seq // SUBLANES
    unroll = 4 // n_req if n_req <= 4 else 1

    def body(i, carry):
        carry = list(carry)
        for k in range(unroll):
            for r in range(n_req):
                hf, hb = carry[2 * r], carry[2 * r + 1]
                base = pl.multiple_of(r * seq + (i * unroll + k) * SUBLANES, SUBLANES)
                h8 = a_s[0, pl.ds(base, SUBLANES), :] * hf + u_s[0, pl.ds(base, SUBLANES), :]
                h_s[0, pl.ds(base, SUBLANES), :] = h8
                carry[2 * r] = h8[SUBLANES - 1:SUBLANES]
                base = pl.multiple_of((r + 1) * seq - SUBLANES - (i * unroll + k) * SUBLANES, SUBLANES)
                h8 = a_s[1, pl.ds(base, SUBLANES), :] * hb + u_s[1, pl.ds(base, SUBLANES), :]
                h_s[1, pl.ds(base, SUBLANES), :] = h8
                carry[2 * r + 1] = h8[0:1]
        return tuple(carry)

    init = tuple(h0_ref[r, d:d + 1, :] for r in range(n_req) for d in range(2))
    fin = lax.fori_loop(0, n_t // unroll, body, init)
    for r in range(n_req):
        hfin_ref[r] = jnp.concatenate([fin[2 * r], fin[2 * r + 1]], axis=0)
    g = g_ref[...]
    gelu = 0.5 * g * (1.0 + jnp.tanh(0.7978845608028654 * (g + 0.044715 * (g * g * g))))
    o_ref[...] = ((h_s[0] + h_s[1]) * gelu).astype(BF16)


def _lru(xb, gb, h0, wts, n_b, seq):
    cw, cb, wa, ba, wx, bx, lam = wts
    C = LRU_WIDTH
    const = lambda shape: pl.BlockSpec(shape, lambda b: tuple(0 for _ in shape))
    n_req = max(1, min(n_b, LRU_ROWS // seq))
    tok = pl.BlockSpec((n_req * seq, C), lambda b: (b, 0))
    st = pl.BlockSpec((n_req, 2, C), lambda b: (b, 0, 0))
    return pl.pallas_call(
        functools.partial(_lru_kernel, seq=seq),
        grid=(n_b // n_req,),
        in_specs=[tok, tok, st, const((LRU_CONV_W, C)), const((1, C)), const((2, C, C)), const((2, C)),
                  const((2, C, C)), const((2, C)), const((2, C))],
        out_specs=[tok, st],
        out_shape=[jax.ShapeDtypeStruct((n_b * seq, C), BF16), jax.ShapeDtypeStruct((n_b, 2, C), F32)],
        scratch_shapes=[pltpu.VMEM((2, n_req * seq, C), F32)] * 3,
        compiler_params=_params("arbitrary"),
        name="lru",
    )(xb, gb, h0, cw, cb, wa, ba, wx, bx, lam)


def _outproj_kernel(x_ref, att_ref, *refs, n_y):
    (bonus_ref, g_ref, lru_ref, mod_ref, n2_ref, lnw_ref, lnb_ref, w_ref, rt_ref, bd_ref,
     x1_ref, h2_ref, lg_ref) = refs[n_y:]
    y = refs[0][...] if n_y == 1 else refs[0][...] + refs[1][...]
    parts = []
    for c in range(RWKV_WIDTH // LANES):
        yc = y[:, c * LANES:(c + 1) * LANES]
        mean = _seg_sum(yc, bd_ref[...]) * (1.0 / RWKV_HEAD)
        dev = yc - mean
        var = _seg_sum(dev * dev, bd_ref[...]) * (1.0 / RWKV_HEAD)
        parts.append(dev * lax.rsqrt(var + GN_EPS))
    yn = jnp.concatenate(parts, axis=-1) * lnw_ref[...] + lnb_ref[...]
    rwkv = (yn + bonus_ref[...]) * g_ref[...]
    o1 = ATT_WIDTH
    o2 = o1 + RWKV_WIDTH
    mixed = (_dot(att_ref[...].astype(BF16), w_ref[:o1, :])
             + _dot(rwkv.astype(BF16), w_ref[o1:o2, :])
             + _dot(lru_ref[...].astype(BF16), w_ref[o2:, :]))
    x1 = x_ref[...] + mod_ref[2:3, :] * mixed
    x1_ref[...] = x1
    rs = lax.rsqrt(jnp.mean(x1 * x1, axis=-1, keepdims=True) + RMS_EPS)
    h2 = (x1 * rs * n2_ref[...]) * (1.0 + mod_ref[4:5, :]) + mod_ref[3:4, :]
    h2_hi = h2.astype(BF16)
    h2_lo = (h2 - h2_hi.astype(F32)).astype(BF16)
    h2_ref[...] = h2_hi
    both = _dot(h2_hi, rt_ref[...])
    lg_ref[...] = both[:, :LANES] + both[:, LANES:] + _dot(h2_lo, rt_ref[:, :LANES])


def _outproj(x, att, ys, bonus, g, lru, mod_l, norm2, lnw, lnb, w_out_bf, layer, router_pad, bd,
             seq, cond_row0):
    n_tok = x.shape[0]
    tm = 512 if seq >= 512 else seq * (512 // seq)
    per_seq = max(seq // tm, 1)
    n_t = n_tok // tm
    if cond_row0 == 0:
        row = lambda i: 0
    else:
        row = lambda i: cond_row0 + i // per_seq
    tok = lambda w: pl.BlockSpec((tm, w), lambda i: (i, 0))
    const = lambda shape: pl.BlockSpec(shape, lambda i: tuple(0 for _ in shape))
    W = RWKV_WIDTH
    return pl.pallas_call(
        functools.partial(_outproj_kernel, n_y=len(ys)),
        grid=(n_t,),
        in_specs=[
            tok(D_MODEL), tok(ATT_WIDTH)] + [tok(W)] * len(ys) + [tok(W), tok(W), tok(LRU_WIDTH),
            pl.BlockSpec((None, N_MOD, D_MODEL), lambda i: (row(i), 0, 0)),
            const((1, D_MODEL)), const((1, W)), const((1, W)),
            pl.BlockSpec((None, D_MODEL, D_MODEL), lambda i: (layer, 0, 0)),
            const((D_MODEL, 2 * LANES)), const((LANES, LANES)),
        ],
        out_specs=[tok(D_MODEL), tok(D_MODEL), tok(LANES)],
        out_shape=[jax.ShapeDtypeStruct((n_tok, D_MODEL), F32),
                   jax.ShapeDtypeStruct((n_tok, D_MODEL), BF16),
                   jax.ShapeDtypeStruct((n_tok, LANES), F32)],
        compiler_params=_params("arbitrary"),
        name="outproj",
    )(x, att, *ys, bonus, g, lru, mod_l, norm2, lnw, lnb, w_out_bf, router_pad, bd)


PREFIX_BLOCK = 256
GATHER_ROWS = 1024


def _prefix_count(mask_f, tri):
    seq = mask_f.shape[0]
    outs = []
    carry = jnp.zeros((1, LANES), F32)
    for blk in range(seq // PREFIX_BLOCK):
        m = mask_f[blk * PREFIX_BLOCK:(blk + 1) * PREFIX_BLOCK]
        outs.append(_dot(tri, m.astype(BF16)) + carry)
        carry = carry + jnp.sum(m, axis=0, keepdims=True)
    return jnp.concatenate(outs, axis=0) if len(outs) > 1 else outs[0]


ROUTE_ROWS = 1024


def _route_kernel(lg_ref, h2_ref, tri_ref, xs_ref, slot_ref, aff_ref, *, cap, seq):
    rows = lg_ref.shape[0]
    n_req = rows // seq
    lane = lax.broadcasted_iota(jnp.int32, (rows, LANES), 1)
    real = lane < N_EXPERTS
    lg = jnp.where(real, lg_ref[...], -jnp.inf)
    m = jnp.max(lg, axis=-1, keepdims=True)
    e = jnp.exp(lg - m)
    aff = e / jnp.sum(e, axis=-1, keepdims=True)
    aff_ref[...] = aff

    a3 = aff.reshape(n_req, seq, LANES)
    a_min = jnp.min(a3, axis=1, keepdims=True)
    a_max = jnp.max(a3, axis=1, keepdims=True)

    def search(carry):
        lo, hi, _ = carry
        mid = 0.5 * (jnp.maximum(lo, a_min) + jnp.minimum(hi, a_max))
        inside = jnp.where(a3 > lo, jnp.where(a3 < hi, 1.0, 0.0), 0.0)
        upper = inside * jnp.where(a3 >= mid, 1.0, 0.0)
        up = jnp.min(jnp.where(upper > 0.0, a3, BRACKET_HI), axis=1, keepdims=True)
        dn = jnp.max(jnp.where(inside - upper > 0.0, a3, BRACKET_LO), axis=1, keepdims=True)
        pivot = jnp.where(up < BRACKET_HI, up, dn)
        found = jnp.where(up < BRACKET_HI, 1.0, jnp.where(dn > BRACKET_LO, 1.0, 0.0))
        cnt = jnp.sum(jnp.where(a3 >= pivot, 1.0, 0.0), axis=1, keepdims=True)
        take_lo = found * jnp.where(cnt >= cap, 1.0, 0.0)
        take_hi = found - take_lo
        return (jnp.where(take_lo > 0.0, pivot, lo), jnp.where(take_hi > 0.0, pivot, hi), found)

    shape = (n_req, 1, LANES)
    init = (jnp.full(shape, BRACKET_LO, F32), jnp.full(shape, BRACKET_HI, F32), jnp.ones(shape, F32))
    thr = lax.while_loop(lambda c: jnp.max(c[2]) > 0.0, search, init)[0]
    gt3 = jnp.where(a3 > thr, 1.0, 0.0)
    eq3 = jnp.where(a3 == thr, 1.0, 0.0)
    need3 = cap - jnp.sum(gt3, axis=1, keepdims=True)
    tri = tri_ref[...]
    c_iota = lax.broadcasted_iota(jnp.int32, (cap, seq), 0).astype(F32)
    real_seq = lax.broadcasted_iota(jnp.int32, (seq, LANES), 1) < N_EXPERTS
    group = max(1, min(N_EXPERTS, GATHER_ROWS // cap))
    for r in range(n_req):
        tok = slice(r * seq, (r + 1) * seq)
        gt, eq = gt3[r], eq3[r]
        sel = jnp.where(real_seq, gt + eq * jnp.where(_prefix_count(eq, tri) < need3[r], 1.0, 0.0), 0.0)
        slot = jnp.where(sel > 0.0, _prefix_count(sel, tri), -1.0)
        slot_ref[tok, :] = slot
        slot_t = slot.T
        h2 = h2_ref[tok, :]
        for g0 in range(0, N_EXPERTS, group):
            onehot = jnp.concatenate(
                [jnp.where(c_iota == slot_t[ex:ex + 1, :], 1.0, 0.0) for ex in range(g0, g0 + group)], axis=0)
            picked = _dot(onehot.astype(BF16), h2)
            for k in range(group):
                xs_ref[g0 + k, r * cap:(r + 1) * cap, :] = picked[k * cap:(k + 1) * cap].astype(BF16)


def _route(logits, h2, tri, n_b, seq):
    cap = EC_FACTOR * seq // N_EXPERTS
    n_req = max(1, min(n_b, ROUTE_ROWS // seq))
    tok = lambda w: pl.BlockSpec((n_req * seq, w), lambda b: (b, 0))
    return pl.pallas_call(
        functools.partial(_route_kernel, cap=cap, seq=seq),
        grid=(n_b // n_req,),
        in_specs=[tok(LANES), tok(D_MODEL), pl.BlockSpec((PREFIX_BLOCK, PREFIX_BLOCK), lambda b: (0, 0))],
        out_specs=[pl.BlockSpec((N_EXPERTS, n_req * cap, D_MODEL), lambda b: (0, b, 0)), tok(LANES), tok(LANES)],
        out_shape=[jax.ShapeDtypeStruct((N_EXPERTS, n_b * cap, D_MODEL), BF16),
                   jax.ShapeDtypeStruct((n_b * seq, LANES), F32),
                   jax.ShapeDtypeStruct((n_b * seq, LANES), F32)],
        compiler_params=_params("arbitrary"),
        name="route",
    )(logits, h2, tri)


EXPERT_RB = 256


def _expert_kernel(xa_ref, xb_ref, wg_ref, wu_ref, wd_ref, ya_ref, yb_ref, wg_s, wu_s, wd_s):
    wg_s[...] = wg_ref[...].astype(BF16)
    wu_s[...] = wu_ref[...].astype(BF16)
    wd_s[...] = wd_ref[...].astype(BF16)
    for x_ref, y_ref in ((xa_ref, ya_ref), (xb_ref, yb_ref)):
        for rb in range(x_ref.shape[0] // EXPERT_RB):
            rows = slice(rb * EXPERT_RB, (rb + 1) * EXPERT_RB)
            x = x_ref[rows, :]
            a = _dot(x, wg_s[...])
            u = _dot(x, wu_s[...])
            hid = (a * _sigmoid(a) * u).astype(BF16)
            y_ref[rows, :] = _dot(hid, wd_s[...]).astype(BF16)


def _experts(xs_a, xs_b, w_gate, w_up, w_down, layer):
    ma, mb = xs_a.shape[1], xs_b.shape[1]
    xspec = lambda m: pl.BlockSpec((None, m, D_MODEL), lambda e: (e, 0, 0))
    return pl.pallas_call(
        _expert_kernel,
        grid=(N_EXPERTS,),
        in_specs=[
            xspec(ma), xspec(mb),
            pl.BlockSpec((None, None, D_MODEL, EXPERT_FF), lambda e: (layer, e, 0, 0)),
            pl.BlockSpec((None, None, D_MODEL, EXPERT_FF), lambda e: (layer, e, 0, 0)),
            pl.BlockSpec((None, None, EXPERT_FF, D_MODEL), lambda e: (layer, e, 0, 0)),
        ],
        out_specs=[xspec(ma), xspec(mb)],
        out_shape=[jax.ShapeDtypeStruct(xs_a.shape, BF16), jax.ShapeDtypeStruct(xs_b.shape, BF16)],
        scratch_shapes=[pltpu.VMEM((D_MODEL, EXPERT_FF), BF16), pltpu.VMEM((D_MODEL, EXPERT_FF), BF16),
                        pltpu.VMEM((EXPERT_FF, D_MODEL), BF16)],
        compiler_params=_params("arbitrary"),
        name="experts",
    )(xs_a, xs_b, w_gate, w_up, w_down)


COMBINE_FUSED_COLS = 512
COMBINE_ROWS = 1024


def _combine_kernel(*refs, cap, fused, n_req):
    if fused:
        y_ref, slot_ref, aff_ref, x1_ref, mod_ref, ex_ref, o_ref = refs
    else:
        y_ref, slot_ref, aff_ref, x1_ref, mod_ref, o_ref = refs
    seq = x1_ref.shape[0]
    slot = slot_ref[...]
    aff = aff_ref[...]
    if fused:
        spread = ex_ref[...]
        a1 = aff.astype(BF16)
        r1 = aff - a1.astype(F32)
        a2 = r1.astype(BF16)
        a3 = (r1 - a2.astype(F32)).astype(BF16)
        slot_x = _dot(slot.astype(BF16), spread)
        aff_x = _dot(a1, spread) + _dot(a2, spread) + _dot(a3, spread)
        cols = N_EXPERTS * cap
        c_pat = (lax.broadcasted_iota(jnp.int32, (seq, cols), 1) % cap).astype(F32)
        w_hi, w_lo = _split(jnp.where(slot_x == c_pat, aff_x, 0.0))
        parts = []
        for r in range(n_req):
            rows = slice(r * (seq // n_req), (r + 1) * (seq // n_req))
            y2 = y_ref[:, r * cap:(r + 1) * cap, :].reshape(cols, D_MODEL)
            parts.append(_dot(w_hi[rows], y2) + _dot(w_lo[rows], y2))
        acc = jnp.concatenate(parts, axis=0) if n_req > 1 else parts[0]
    else:
        c_iota = lax.broadcasted_iota(jnp.int32, (seq, cap), 1).astype(F32)
        acc = jnp.zeros((seq, D_MODEL), F32)
        for ex in range(N_EXPERTS):
            onehot = jnp.where(slot[:, ex:ex + 1] == c_iota, 1.0, 0.0).astype(BF16)
            acc = acc + aff[:, ex:ex + 1] * _dot(onehot, y_ref[ex])
    o_ref[...] = x1_ref[...] + mod_ref[5:6, :] * acc


def _combine(y, slot, aff, x1, mod_l, n_b, seq, cond_row0):
    cap = EC_FACTOR * seq // N_EXPERTS
    fused = N_EXPERTS * cap <= COMBINE_FUSED_COLS
    if cond_row0 == 0:
        row = lambda b: 0
    else:
        row = lambda b: cond_row0 + b
    tm = min(seq, 512)
    n_t = seq // tm
    n_req = max(1, min(n_b, COMBINE_ROWS // seq)) if (fused and cond_row0 == 0) else 1
    tm *= n_req
    tok = lambda w: pl.BlockSpec((tm, w), lambda b, i: (b * n_t + i, 0))
    in_specs = [
        pl.BlockSpec((N_EXPERTS, n_req * cap, D_MODEL), lambda b, i: (0, b, 0)),
        tok(LANES), tok(LANES), tok(D_MODEL),
        pl.BlockSpec((None, N_MOD, D_MODEL), lambda b, i: (row(b), 0, 0)),
    ]
    args = [y, slot, aff, x1, mod_l]
    if fused:
        cols = N_EXPERTS * cap
        spread = (jnp.arange(LANES)[:, None] == (jnp.arange(cols) // cap)[None, :]).astype(BF16)
        in_specs.append(pl.BlockSpec((LANES, cols), lambda b, i: (0, 0)))
        args.append(spread)
    return pl.pallas_call(
        functools.partial(_combine_kernel, cap=cap, fused=fused, n_req=n_req),
        grid=(n_b // n_req, n_t),
        in_specs=in_specs,
        out_specs=tok(D_MODEL),
        out_shape=jax.ShapeDtypeStruct((n_b * seq, D_MODEL), F32),
        compiler_params=_params("arbitrary", "arbitrary"),
        name="combine_fused" if fused else "combine",
    )(*args)


def _rope_tables(seq):
    t = np.arange(seq)
    row = (t // GRID_W).astype(np.float64)
    col = (t % GRID_W).astype(np.float64)
    half = HEAD_DIM // 2
    inv = ROPE_THETA ** (-np.arange(0, half, 2, dtype=np.float64) / half)
    lane = np.arange(LANES)
    u = lane % HEAD_DIM
    pos = np.where((u // half)[None, :] == 0, row[:, None], col[:, None])
    ang = pos * inv[(u % half) % (half // 2)][None, :]
    first = ((u % half) < half // 2)[None, :]
    sin = np.sin(ang)
    tabs = (np.cos(ang), np.where(first, -sin, 0.0), np.where(first, 0.0, sin))
    return tuple(jnp.asarray(x.astype(np.float32)) for x in tabs)


def _block_diag(w):
    n, k, _ = w.shape
    eye = jnp.eye(n, dtype=w.dtype)
    return (eye[:, None, :, None] * w[:, :, None, :]).reshape(n * k, n * k)


def _split_weight(w):
    return jnp.stack(_split(w))


def _pad_rows(w, offset, total=LANES):
    return jnp.zeros((total, w.shape[1]), w.dtype).at[offset:offset + w.shape[0]].set(w)


def _pack_state(s):
    b = s.shape[0]
    st = s.reshape(b, 2, RWKV_PAIRS, 2, RWKV_HEAD, RWKV_HEAD).transpose(0, 1, 2, 3, 5, 4)
    eye = jnp.eye(2, dtype=s.dtype)
    out = st[:, :, :, :, :, None, :] * eye[None, None, None, :, None, :, None]
    return out.reshape(b, 2, RWKV_PAIRS, LANES, LANES)


def _unpack_state(s):
    return s.reshape(s.shape[0], 2, RWKV_HEADS, RWKV_HEAD, RWKV_HEAD)


def kernel(x_prompt, x_sample, cache_k, cache_v, state_rwkv, state_lru, c, c_ctx, w_ada, b_ada, norm1, norm2, w_in, w_out, q_norm, k_norm, rwkv_mu, rwkv_w0, rwkv_w_up, rwkv_a0, rwkv_a_up, rwkv_g_up, rwkv_k_k, rwkv_k_a, rwkv_r_k, rwkv_ln_w, rwkv_ln_b, lru_conv_w, lru_conv_b, lru_wa, lru_ba, lru_wx, lru_bx, lru_lambda, router, exp_w_gate, exp_w_up, exp_w_down):
    n_ctx, seq_ctx, _ = x_prompt.shape
    n_lat, seq_lat, _ = x_sample.shape
    past = cache_k.shape[2]
    assert n_lat + 1 <= COND_ROWS

    cond = jnp.zeros((COND_ROWS, D_MODEL), F32).at[0].set(c_ctx).at[1:1 + n_lat].set(c)
    mod = _ada(cond, w_ada, b_ada.reshape(DEPTH, 1, N_MOD * D_MODEL))
    mod = mod.reshape(DEPTH, COND_ROWS, N_MOD, D_MODEL)

    w_in_bf = w_in.astype(BF16)
    w_out_bf = w_out.astype(BF16)
    lane = jnp.arange(LANES)
    bd = (lane[:, None] // HEAD_DIM == lane[None, :] // HEAD_DIM).astype(BF16)
    pidx = jnp.arange(PREFIX_BLOCK)
    tri = (pidx[None, :] < pidx[:, None]).astype(BF16)
    rope_tabs = _rope_tables(seq_lat)

    paths = [
        dict(x=x_prompt.reshape(n_ctx * seq_ctx, D_MODEL), n_b=n_ctx, seq=seq_ctx, row0=0, rope=None),
        dict(x=x_sample.reshape(n_lat * seq_lat, D_MODEL), n_b=n_lat, seq=seq_lat, row0=1, rope=rope_tabs),
    ]
    new_k, new_v, new_sr, new_sl = [], [], [], []
    for l in range(DEPTH):
        mod_l = mod[l]
        qn = jnp.tile(q_norm[l], LANES // HEAD_DIM)[None, :]
        kn = jnp.tile(k_norm[l], LANES // HEAD_DIM)[None, :]
        prep_w = (
            rwkv_mu[l][None, :], rwkv_k_k[l][None, :], rwkv_k_a[l][None, :],
            rwkv_r_k[l].reshape(1, RWKV_WIDTH), rwkv_w0[l], rwkv_a0[l],
            jnp.stack([_split_weight(_pad_rows(rwkv_w_up[l, d], 0)) for d in range(2)]),
            jnp.stack([_split_weight(_pad_rows(rwkv_a_up[l, d], RWKV_DECAY_LORA)) for d in range(2)]),
            _split_weight(_pad_rows(rwkv_g_up[l], RWKV_DECAY_LORA + RWKV_AAA_LORA)),
        )
        lru_w = (
            lru_conv_w[l], lru_conv_b[l][None, :],
            jnp.stack([_block_diag(lru_wa[l, d]) for d in range(2)]).astype(BF16), lru_ba[l],
            jnp.stack([_block_diag(lru_wx[l, d]) for d in range(2)]).astype(BF16), lru_bx[l],
            lru_lambda[l],
        )
        router_pad = jnp.concatenate(_split(jnp.zeros((D_MODEL, LANES), F32).at[:, :N_EXPERTS].set(router[l])), axis=1)
        mids = []
        for pi, pth in enumerate(paths):
            n_b, seq, row0 = pth["n_b"], pth["seq"], pth["row0"]
            latent = pth["rope"] is not None
            outs = _inproj(pth["x"], mod_l, norm1[l][None, :], w_in_bf, l, qn, kn, bd, pth["rope"], seq, row0)
            if latent:
                q, k_n, k_att, v, p_rwkv, lru_x, lru_g = outs
                cache = (cache_k[:, l].reshape(n_b, past, KV_WIDTH), cache_v[:, l].reshape(n_b, past, KV_WIDTH))
                s0 = _pack_state(state_rwkv[:, l])
                h0 = state_lru[:, l]
            else:
                q, k_n, v, p_rwkv, lru_x, lru_g = outs
                k_att, cache = k_n, None
                s0 = None
                h0 = jnp.zeros((n_b, 2, LRU_WIDTH), F32)
                new_k.append(k_n.reshape(n_b, seq, ATT_KV_HEADS, HEAD_DIM))
                new_v.append(v.reshape(n_b, seq, ATT_KV_HEADS, HEAD_DIM))
            att = _attention(q, k_att, v, n_b, seq, cache)
            r, vv, na, w, kd, b, g, bonus = _rwkv_prep(p_rwkv.reshape(n_b, seq, RWKV_COLS), n_b, seq, prep_w, bd)
            W = RWKV_WIDTH
            ys, s_fin = _rwkv_scan(r.reshape(n_b, seq, W), vv.reshape(n_b, seq, W), na.reshape(n_b, seq, W),
                                   w.reshape(2, n_b, seq, W), kd.reshape(2, n_b, seq, W),
                                   b.reshape(2, n_b, seq, W), s0, n_b, seq)
            ys = [y.reshape(n_b * seq, W) for y in ys]
            lru_out, h_fin = _lru(lru_x, lru_g, h0, lru_w, n_b, seq)
            if not latent:
                new_sr.append(_unpack_state(s_fin))
                new_sl.append(h_fin)
            x1, h2, logits = _outproj(pth["x"], att, ys, bonus, g,
                                      lru_out, mod_l, norm2[l][None, :], rwkv_ln_w[l][None, :],
                                      rwkv_ln_b[l][None, :], w_out_bf, l, router_pad, bd, seq, row0)
            xs, slot, aff = _route(logits, h2, tri, n_b, seq)
            mids.append((xs, slot, aff, x1))
        y_a, y_b = _experts(mids[0][0], mids[1][0], exp_w_gate, exp_w_up, exp_w_down, l)
        for pth, (xs, slot, aff, x1), y in zip(paths, mids, (y_a, y_b)):
            pth["x"] = _combine(y, slot, aff, x1, mod_l, pth["n_b"], pth["seq"], pth["row0"])

    y_prompt = paths[0]["x"].reshape(n_ctx, seq_ctx, D_MODEL)
    y_sample = paths[1]["x"].reshape(n_lat, seq_lat, D_MODEL)
    return (y_prompt, y_sample, jnp.stack(new_k, axis=1), jnp.stack(new_v, axis=1),
            jnp.stack(new_sr, axis=1), jnp.stack(new_sl, axis=1))
```

```python
import functools
import math

import numpy as np
import jax
import jax.numpy as jnp
from jax import lax
from jax.experimental import pallas as pl
from jax.experimental.pallas import tpu as pltpu

F32 = jnp.float32
BF16 = jnp.bfloat16

D_MODEL = 1024
DEPTH = 2
GRID_W = 64
ATT_HEADS = 8
ATT_KV_HEADS = 2
HEAD_DIM = 64
ATT_WIDTH = ATT_HEADS * HEAD_DIM
KV_WIDTH = ATT_KV_HEADS * HEAD_DIM
ROPE_THETA = 10000.0
RWKV_HEADS = 4
RWKV_HEAD = 64
RWKV_WIDTH = RWKV_HEADS * RWKV_HEAD
RWKV_DECAY_LORA = 32
RWKV_AAA_LORA = 32
RWKV_GATE_LORA = 64
RWKV_COLS = 3 * RWKV_WIDTH + RWKV_DECAY_LORA + RWKV_AAA_LORA + RWKV_GATE_LORA
GN_EPS = 64e-5
LRU_BLOCKS = 4
LRU_BLOCK = 64
LRU_WIDTH = LRU_BLOCKS * LRU_BLOCK
LRU_CONV_W = 4
LRU_C = 8.0
IN_COLS = ATT_WIDTH + 2 * KV_WIDTH + RWKV_COLS + 2 * LRU_WIDTH
N_EXPERTS = 16
EC_FACTOR = 2
EXPERT_FF = 1024
RMS_EPS = 1e-6

LANES = 128
SUBLANES = 8
VMEM_LIMIT = 56 * 1024 * 1024
N_MOD = 6
COND_ROWS = 8
BRACKET_LO, BRACKET_HI = -1.0, 2.0


def _params(*sem):
    return pltpu.CompilerParams(dimension_semantics=sem, vmem_limit_bytes=VMEM_LIMIT)


def _dot(a, b, precision=None):
    return jnp.dot(a, b, preferred_element_type=F32, precision=precision)


def _sigmoid(x):
    return 1.0 / (1.0 + jnp.exp(-x))


def _softplus(x):
    return jnp.maximum(x, 0.0) + jnp.log1p(jnp.exp(-jnp.abs(x)))


def _split(x):
    hi = x.astype(BF16)
    return hi, (x - hi.astype(F32)).astype(BF16)


def _dot_split(x, w_ref):
    hi, lo = _split(x)
    return _dot(hi, w_ref[0]) + _dot(hi, w_ref[1]) + _dot(lo, w_ref[0])


def _seg_sum(x, ones_bd):
    return _dot(x.astype(BF16), ones_bd)


def _ada_kernel(c_ref, w_ref, b_ref, o_ref):
    c = c_ref[...]
    s_hi, s_lo = _split(c * _sigmoid(c))
    w_hi, w_lo = _split(w_ref[...])
    o_ref[...] = _dot(s_hi, w_hi) + _dot(s_hi, w_lo) + _dot(s_lo, w_hi) + b_ref[...]


def _ada(cond, w_ada, b_ada):
    n_l = w_ada.shape[0]
    tn = 1536
    n_t = N_MOD * D_MODEL // tn
    return pl.pallas_call(
        _ada_kernel,
        grid=(n_l, n_t),
        in_specs=[
            pl.BlockSpec((COND_ROWS, D_MODEL), lambda l, j: (0, 0)),
            pl.BlockSpec((None, D_MODEL, tn), lambda l, j: (l, 0, j)),
            pl.BlockSpec((None, 1, tn), lambda l, j: (l, 0, j)),
        ],
        out_specs=pl.BlockSpec((None, COND_ROWS, tn), lambda l, j: (l, 0, j)),
        out_shape=jax.ShapeDtypeStruct((n_l, COND_ROWS, N_MOD * D_MODEL), F32),
        compiler_params=_params("arbitrary", "arbitrary"),
        name="ada",
    )(cond, w_ada, b_ada)


def _head_rms(x, gain, ones_bd):
    ms = _seg_sum(x * x, ones_bd) * (1.0 / HEAD_DIM)
    return x * lax.rsqrt(ms + RMS_EPS) * gain


def _rope(x, cos, sin_up, sin_dn):
    return x * cos + pltpu.roll(x, LANES - 16, 1) * sin_up + pltpu.roll(x, 16, 1) * sin_dn


def _inproj_kernel(*refs, rope):
    if rope:
        (x_ref, mod_ref, n1_ref, w_ref, qn_ref, kn_ref, bd_ref, cos_ref, su_ref, sd_ref,
         q_ref, ko_ref, ka_ref, v_ref, pr_ref, lx_ref, lg_ref) = refs
    else:
        (x_ref, mod_ref, n1_ref, w_ref, qn_ref, kn_ref, bd_ref,
         q_ref, ko_ref, v_ref, pr_ref, lx_ref, lg_ref) = refs
    x = x_ref[...]
    rs = lax.rsqrt(jnp.mean(x * x, axis=-1, keepdims=True) + RMS_EPS)
    h = (x * rs * n1_ref[...]) * (1.0 + mod_ref[1:2, :]) + mod_ref[0:1, :]
    p = _dot(h.astype(BF16), w_ref[...])
    bd = bd_ref[...]
    for c in range(ATT_WIDTH // LANES):
        qc = _head_rms(p[:, c * LANES:(c + 1) * LANES], qn_ref[...], bd)
        if rope:
            qc = _rope(qc, cos_ref[...], su_ref[...], sd_ref[...])
        q_ref[:, c * LANES:(c + 1) * LANES] = (qc * (HEAD_DIM ** -0.5)).astype(BF16)
    o = ATT_WIDTH
    kc = _head_rms(p[:, o:o + KV_WIDTH], kn_ref[...], bd)
    ko_ref[...] = kc
    if rope:
        ka_ref[...] = _rope(kc, cos_ref[...], su_ref[...], sd_ref[...]).astype(BF16)
    o += KV_WIDTH
    v_ref[...] = p[:, o:o + KV_WIDTH]
    o += KV_WIDTH
    pr_ref[...] = p[:, o:o + RWKV_COLS]
    o += RWKV_COLS
    lx_ref[...] = p[:, o:o + LRU_WIDTH]
    o += LRU_WIDTH
    lg_ref[...] = p[:, o:o + LRU_WIDTH]


def _inproj(x, mod_l, norm1, w_in_bf, layer, qn, kn, bd, rope_tabs, seq, cond_row0):
    n_tok = x.shape[0]
    tm = 512 if seq >= 512 else seq * (512 // seq)
    per_seq = max(seq // tm, 1)
    n_t = n_tok // tm
    rope = rope_tabs is not None
    if cond_row0 == 0:
        row = lambda i: 0
    else:
        row = lambda i: cond_row0 + i // per_seq
    tok = lambda w: pl.BlockSpec((tm, w), lambda i: (i, 0))
    const = lambda shape: pl.BlockSpec(shape, lambda i: tuple(0 for _ in shape))
    in_specs = [
        tok(D_MODEL),
        pl.BlockSpec((None, N_MOD, D_MODEL), lambda i: (row(i), 0, 0)),
        const((1, D_MODEL)),
        pl.BlockSpec((None, D_MODEL, IN_COLS), lambda i: (layer, 0, 0)),
        const((1, LANES)), const((1, LANES)), const((LANES, LANES)),
    ]
    args = [x, mod_l, norm1, w_in_bf, qn, kn, bd]
    outs = [(ATT_WIDTH, BF16), (KV_WIDTH, F32)]
    if rope:
        in_specs += [pl.BlockSpec((tm, LANES), lambda i: (i % per_seq, 0))] * 3
        args += list(rope_tabs)
        outs.append((KV_WIDTH, BF16))
    outs += [(KV_WIDTH, F32), (RWKV_COLS, F32), (LRU_WIDTH, F32), (LRU_WIDTH, F32)]
    return pl.pallas_call(
        functools.partial(_inproj_kernel, rope=rope),
        grid=(n_t,),
        in_specs=in_specs,
        out_specs=[tok(w) for w, _ in outs],
        out_shape=[jax.ShapeDtypeStruct((n_tok, w), dt) for w, dt in outs],
        compiler_params=_params("arbitrary"),
        name="inproj_rope" if rope else "inproj",
    )(*args)


ATTN_ROWS = 2048


def _attn_kernel(*refs, has_cache, n_req):
    if has_cache:
        q_ref, k_ref, v_ref, ck_ref, cv_ref, o_ref, kn_s, ks_s, vn_s, vs_s = refs
    else:
        q_ref, k_ref, v_ref, o_ref, kn_s, ks_s, vn_s, vs_s = refs

    tq = q_ref.shape[0] // n_req
    seq = k_ref.shape[0] // n_req
    lo = lax.broadcasted_iota(jnp.int32, (tq, LANES), 1) < HEAD_DIM
    rep = ATT_HEADS // ATT_KV_HEADS
    for r in range(n_req):
        def stage_keys():
            k = k_ref[r * seq:(r + 1) * seq, :].astype(F32)
            v = v_ref[r * seq:(r + 1) * seq, :]
            if has_cache:
                k = jnp.concatenate([ck_ref[...], k], axis=0)
                v = jnp.concatenate([cv_ref[...], v], axis=0)
            kn_s[...] = k.astype(BF16)
            ks_s[...] = pltpu.roll(k, HEAD_DIM, 1).astype(BF16)
            vn_s[...] = v.astype(BF16)
            vs_s[...] = pltpu.roll(v, HEAD_DIM, 1).astype(BF16)

        if n_req == 1:
            pl.when(pl.program_id(1) == 0)(stage_keys)
        else:
            stage_keys()
        for c in range(ATT_WIDTH // LANES):
            qc = q_ref[r * tq:(r + 1) * tq, c * LANES:(c + 1) * LANES].astype(F32)
            halves = []
            for half in range(2):
                g = (2 * c + half) // rep
                qm = jnp.where(lo if half == 0 else jnp.logical_not(lo), qc, 0.0).astype(BF16)
                k_s, v_s = (kn_s, vn_s) if half == g else (ks_s, vs_s)
                s = lax.dot_general(qm, k_s[...], (((1,), (1,)), ((), ())), preferred_element_type=F32)
                m = jnp.max(s, axis=-1, keepdims=True)
                e = jnp.exp(s - m)
                l = jnp.sum(e, axis=-1, keepdims=True)
                halves.append(_dot(e.astype(BF16), v_s[...]) / l)
            o_ref[r * tq:(r + 1) * tq, c * LANES:(c + 1) * LANES] = (
                jnp.where(lo, halves[0], halves[1]).astype(BF16))


def _attention(q, k, v, n_b, seq, cache=None):
    tq = 256
    n_q = seq // tq
    has_cache = cache is not None
    past = cache[0].shape[1] if has_cache else 0
    n_req = max(1, min(n_b, ATTN_ROWS // seq)) if (n_q == 1 and not has_cache) else 1
    in_specs = [
        pl.BlockSpec((n_req * tq, ATT_WIDTH), lambda b, i: (b * n_q + i, 0)),
        pl.BlockSpec((n_req * seq, KV_WIDTH), lambda b, i: (b, 0)),
        pl.BlockSpec((n_req * seq, KV_WIDTH), lambda b, i: (b, 0)),
    ]
    args = [q, k, v]
    if has_cache:
        in_specs += [pl.BlockSpec((None, past, KV_WIDTH), lambda b, i: (b, 0, 0))] * 2
        args += list(cache)
    return pl.pallas_call(
        functools.partial(_attn_kernel, has_cache=has_cache, n_req=n_req),
        grid=(n_b // n_req, n_q),
        in_specs=in_specs,
        out_specs=pl.BlockSpec((n_req * tq, ATT_WIDTH), lambda b, i: (b * n_q + i, 0)),
        out_shape=jax.ShapeDtypeStruct((n_b * seq, ATT_WIDTH), BF16),
        scratch_shapes=[pltpu.VMEM((past + seq, KV_WIDTH), BF16)] * 4,
        compiler_params=_params("arbitrary", "arbitrary"),
        name="attn_cache" if has_cache else "attn",
    )(*args)


def _rwkv_prep_kernel(p_ref, mu_ref, kk_ref, ka_ref, rk_ref, w0_ref, a0_ref, wup_ref, aup_ref,
                      gup_ref, bd_ref,
                      r_ref, v_ref, na_ref, w_ref, k_ref, b_ref, g_ref, bonus_ref):
    tm = r_ref.shape[0]
    seq = p_ref.shape[0]
    i = pl.program_id(1)
    n_t = pl.num_programs(1)
    start = pl.multiple_of(i * tm, tm)
    cur = p_ref[pl.ds(start, tm), :]
    prev_base = pl.multiple_of(jnp.maximum(start - SUBLANES, 0), SUBLANES)
    next_base = pl.multiple_of(jnp.minimum(start + tm, seq - SUBLANES), SUBLANES)
    prev_row = p_ref[pl.ds(prev_base, SUBLANES), :][SUBLANES - 1:SUBLANES]
    next_row = p_ref[pl.ds(next_base, SUBLANES), :][0:1]
    prev_row = jnp.where(i > 0, prev_row, 0.0)
    next_row = jnp.where(i < n_t - 1, next_row, 0.0)
    row = lax.broadcasted_iota(jnp.int32, cur.shape, 0)
    prev = jnp.where(row == 0, prev_row, pltpu.roll(cur, 1, 0))
    nxt = jnp.where(row == tm - 1, next_row, pltpu.roll(cur, tm - 1, 0))
    ps = cur + mu_ref[...] * (0.5 * (prev + nxt) - cur)

    W = RWKV_WIDTH
    r = ps[:, :W]
    k = ps[:, W:2 * W]
    v = ps[:, 2 * W:3 * W]
    lora = ps[:, 3 * W:]
    r_ref[...] = r.astype(BF16)
    v_ref[...] = v.astype(BF16)
    g_ref[...] = _dot_split(_sigmoid(lora), gup_ref).astype(BF16)
    lora_t = jnp.tanh(lora)
    kk = k * kk_ref[...]
    kk_parts = []
    for c in range(W // LANES):
        kc = kk[:, c * LANES:(c + 1) * LANES]
        nrm = jnp.sqrt(_seg_sum(kc * kc, bd_ref[...]))
        kk_parts.append(kc / jnp.maximum(nrm, 1e-12))
    kk = jnp.concatenate(kk_parts, axis=-1)
    na_ref[...] = (-kk).astype(BF16)
    bonus = None
    for d in range(2):
        z = w0_ref[d:d + 1, :] + _dot_split(lora_t, wup_ref.at[d])
        w_ref[d] = -math.exp(-0.5) * _sigmoid(z)
        a_rate = _sigmoid(a0_ref[d:d + 1, :] + _dot_split(lora, aup_ref.at[d]))
        kd = k * (1.0 + (a_rate - 1.0) * ka_ref[...])
        k_ref[d] = kd.astype(BF16)
        b_ref[d] = (kk * a_rate).astype(BF16)
        rkr = r * kd * rk_ref[...]
        parts = [_seg_sum(rkr[:, c * LANES:(c + 1) * LANES], bd_ref[...]) for c in range(W // LANES)]
        bd_term = jnp.concatenate(parts, axis=-1) * v
        bonus = bd_term if bonus is None else bonus + bd_term
    bonus_ref[...] = bonus.astype(BF16)


def _rwkv_prep(p, n_b, seq, wts, bd):
    mu, k_k, k_a, r_k, w0, a0, wup, aup, gup = wts
    tm = min(seq, 512)
    n_t = seq // tm
    W = RWKV_WIDTH
    const = lambda shape: pl.BlockSpec(shape, lambda b, i: tuple(0 for _ in shape))
    tok = pl.BlockSpec((tm, W), lambda b, i: (b * n_t + i, 0))
    tok2 = pl.BlockSpec((2, tm, W), lambda b, i: (0, b * n_t + i, 0))
    two = jax.ShapeDtypeStruct((2, n_b * seq, W), F32)
    one_bf = jax.ShapeDtypeStruct((n_b * seq, W), BF16)
    two_bf = jax.ShapeDtypeStruct((2, n_b * seq, W), BF16)
    return pl.pallas_call(
        _rwkv_prep_kernel,
        grid=(n_b, n_t),
        in_specs=[
            pl.BlockSpec((None, seq, RWKV_COLS), lambda b, i: (b, 0, 0)),
            const((1, RWKV_COLS)), const((1, W)), const((1, W)), const((1, W)),
            const((2, W)), const((2, W)),
            const((2, 2, LANES, W)), const((2, 2, LANES, W)), const((2, LANES, W)),
            const((LANES, LANES)),
        ],
        out_specs=[tok, tok, tok, tok2, tok2, tok2, tok, tok],
        out_shape=[one_bf, one_bf, one_bf, two, two_bf, two_bf, one_bf, one_bf],
        compiler_params=_params("arbitrary", "arbitrary"),
        name="rwkv_prep",
    )(p, mu, k_k, k_a, r_k, w0, a0, wup, aup, gup, bd)


RWKV_BB = 4
RWKV_PAIRS = RWKV_HEADS * RWKV_HEAD // LANES
RWKV_CHUNK = 256
RWKV_SUB = 64
RWKV_STAGE_WIDTH = 64
INV_BASE_SHIFT = 3


def _rwkv_scan_kernel(*refs, has_s0, merge_y):
    (rf_ref, vf_ref, af_ref, wf_ref, kf_ref, bf_ref, rb_ref, vb_ref, ab_ref, wb_ref, kb_ref, bb_ref) = refs[:12]
    s0_ref = refs[12] if has_s0 else None
    if merge_y:
        y_ref, sfin_ref, st_s = refs[-3:]
        yf_ref = yb_ref = y_ref
    else:
        yf_ref, yb_ref, sfin_ref, st_s = refs[-4:]
    _rwkv_scan_body(rf_ref, vf_ref, af_ref, wf_ref, kf_ref, bf_ref, rb_ref, vb_ref, ab_ref, wb_ref, kb_ref, bb_ref,
                    s0_ref, yf_ref, yb_ref, sfin_ref, st_s, merge_y)


def _rwkv_scan_body(rf_ref, vf_ref, af_ref, wf_ref, kf_ref, bf_ref,
                    rb_ref, vb_ref, ab_ref, wb_ref, kb_ref, bb_ref, s0_ref,
                    yf_ref, yb_ref, sfin_ref, st_s, merge_y):
    j = pl.program_id(1)
    n_req, tc = rf_ref.shape[0], rf_ref.shape[1]
    C = RWKV_SUB

    @pl.when(j == 0)
    def _():
        st_s[...] = jnp.zeros(st_s.shape, F32) if s0_ref is None else s0_ref[...]
        if merge_y:
            yf_ref[...] = jnp.zeros(yf_ref.shape, F32)

    lane_c = lax.broadcasted_iota(jnp.int32, (C, LANES), 1)
    row_c = lax.broadcasted_iota(jnp.int32, (C, LANES), 0)
    lo = lane_c < C
    s_idx = lane_c % C
    eye2 = jnp.where(s_idx == row_c, 1.0, 0.0)
    blk_masks = [jnp.where((s_idx >> sh) == (row_c >> sh), 1.0, 0.0) for sh in range(INV_BASE_SHIFT, 7)]
    lane_f = lax.broadcasted_iota(jnp.int32, (LANES, LANES), 1)
    row_f = lax.broadcasted_iota(jnp.int32, (LANES, LANES), 0)
    eye_f = lane_f == row_f
    lo_f = lane_f < C
    bd_mask = (lane_f < C) == (row_f < C)

    def bd(m):
        return jnp.concatenate([jnp.where(lo, m, 0.0), jnp.where(lo, 0.0, m)], axis=0)

    def bd_swap(m):
        return jnp.concatenate([jnp.where(lo, 0.0, m), jnp.where(lo, m, 0.0)], axis=0)

    def bdot(x, y):
        return _dot(x.astype(BF16), y.astype(BF16))

    dirs = ((rf_ref, vf_ref, af_ref, wf_ref, kf_ref, bf_ref, yf_ref),
            (rb_ref, vb_ref, ab_ref, wb_ref, kb_ref, bb_ref, yb_ref))

    def running_sum(x, reverse):
        row8 = row_c % SUBLANES
        for sh in (1, 2, 4):
            if reverse:
                x = x + jnp.where(row8 < SUBLANES - sh, pltpu.roll(x, C - sh, 0), 0.0)
            else:
                x = x + jnp.where(row8 >= sh, pltpu.roll(x, sh, 0), 0.0)
        n_tiles = C // SUBLANES
        out = [None] * n_tiles
        carry = None
        for t in (range(n_tiles - 1, -1, -1) if reverse else range(n_tiles)):
            tile = x[t * SUBLANES:(t + 1) * SUBLANES]
            out[t] = tile if carry is None else tile + carry
            carry = out[t][0:1] if reverse else out[t][SUBLANES - 1:SUBLANES]
        return jnp.concatenate(out, axis=0)

    n_sub = max(1, min(tc // C, RWKV_STAGE_WIDTH // (2 * n_req * RWKV_PAIRS)))

    def sub_chunk(i, carry):
        chains = []
        for d in range(2):
            strict = jnp.where((s_idx < row_c) if d == 0 else (s_idx > row_c), 1.0, 0.0)
            incl = jnp.where((s_idx <= row_c) if d == 0 else (s_idx >= row_c), 1.0, 0.0)
            tri_mask = jnp.concatenate([strict, incl], axis=0)
            last = C - 1 if d == 0 else 0
            for k in range(n_sub):
                ii = i * n_sub + k
                base = pl.multiple_of(ii * C if d == 0 else tc - C - ii * C, C)
                for bi in range(n_req):
                    for pr in range(RWKV_PAIRS):
                        chains.append(dict(d=d, k=k, bi=bi, pr=pr, base=base, refs=dirs[d], tri=tri_mask,
                                           last=last, ls=slice(pr * LANES, (pr + 1) * LANES)))

        for c in chains:
            r_r, v_r, a_r, w_r, k_r, b_r, _ = c["refs"]
            r, v, a, lw, k, b = (ref[c["bi"], pl.ds(c["base"], C), c["ls"]].astype(F32)
                                 for ref in (r_r, v_r, a_r, w_r, k_r, b_r))
            c["G"] = running_sum(lw, c["d"] == 1)
            c["in"] = (r, v, a, lw, k, b)
        for c in chains:
            r, v, a, lw, k, b = c.pop("in")
            G = c.pop("G")
            g_inv = jnp.exp(-G)
            at = a * jnp.exp(G - lw)
            rt = r * jnp.exp(G)
            bt = b * g_inv
            kt = k * g_inv
            g_last = jnp.exp(G[c["last"]:c["last"] + 1, :])
            X = jnp.concatenate([at, rt], axis=0)
            c["X"] = X.astype(BF16)
            bt_b, kt_b = bt.astype(BF16), kt.astype(BF16)
            nt = (((1,), (1,)), ((), ()))
            c["P0"] = lax.dot_general(jnp.where(lo_f, X, 0.0).astype(BF16), jnp.concatenate([bt_b, kt_b], axis=0),
                                      nt, preferred_element_type=F32)
            c["P1"] = lax.dot_general(jnp.where(lo_f, 0.0, X).astype(BF16), jnp.concatenate([kt_b, bt_b], axis=0),
                                      nt, preferred_element_type=F32)
            c["v"] = v
            c["ygt"] = jnp.concatenate([bt * g_last, kt * g_last], axis=0).T.astype(BF16)
            c["g_col"] = jnp.broadcast_to(g_last, (LANES, LANES)).T
        for c in chains:
            P0 = c.pop("P0") * c["tri"]
            P1 = c.pop("P1") * c["tri"]
            ABRB = jnp.where(lo_f, P0, P1)
            c["AKRK"] = jnp.where(lo_f, P1, P0)
            c["AB"], c["RB"] = ABRB[:C], ABRB[C:]
        for c in chains:
            l8 = c["AB"] * blk_masks[0]
            c["T"] = eye2 + l8
            c["Lp"] = bdot(l8, bd(l8))
        for c in chains:
            R = bdot(jnp.concatenate([c["Lp"], c["T"]], axis=0), bd(c["Lp"]))
            c["T"] = c["T"] + R[C:]
            c["Lp"] = R[:C]
        for c in chains:
            c["T"] = c["T"] + bdot(c["T"], bd(c["Lp"]))
        for lvl in range(1, len(blk_masks)):
            for c in chains:
                c["Lp"] = bdot(c["AB"] * (blk_masks[lvl] - blk_masks[lvl - 1]), bd(c["T"]))
            for c in chains:
                c["T"] = c["T"] + bdot(c["T"], bd(c["Lp"]))
        for c in chains:
            c["VK"] = bdot(c["AKRK"], bd_swap(c["v"]))
        for k in range(n_sub):
            now = [c for c in chains if c["k"] == k]
            for c in now:
                c["S0"] = st_s[c["bi"], c["d"], c["pr"]]
                c["XS"] = _dot(c["X"], c["S0"].astype(BF16))
            for c in now:
                c["U"] = bdot(c["T"], bd(c["XS"][:C] + c["VK"][:C]))
            for c in now:
                y_r = c["refs"][6]
                y_new = c["XS"][C:] + c["VK"][C:] + bdot(c["RB"], bd(c["U"]))
                if merge_y:
                    y_new = y_new + y_r[c["bi"], pl.ds(c["base"], C), c["ls"]]
                y_r[c["bi"], pl.ds(c["base"], C), c["ls"]] = y_new
                uv = jnp.concatenate([c["U"], c["v"]], axis=0).astype(BF16)
                st_s[c["bi"], c["d"], c["pr"]] = jnp.where(bd_mask, c["g_col"] * c["S0"] + _dot(c["ygt"], uv), 0.0)
        return carry

    lax.fori_loop(0, tc // (C * n_sub), sub_chunk, 0)

    @pl.when(j == pl.num_programs(1) - 1)
    def _():
        for bi in range(n_req):
            for d in range(2):
                for pr in range(RWKV_PAIRS):
                    mt = st_s[bi, d, pr].T
                    sfin_ref[bi, d, pr] = jnp.where(row_f < C, mt, pltpu.roll(mt, C, 1))[:, :C]


def _rwkv_scan(r, v, na, lw, k, b, s0, n_b, seq):
    tc = min(seq, RWKV_CHUNK)
    n_c = seq // tc
    W = RWKV_WIDTH
    bb = min(RWKV_BB, n_b)
    blk = (bb, tc, W)
    fwd = pl.BlockSpec(blk, lambda g, j: (g, j, 0))
    bwd = pl.BlockSpec(blk, lambda g, j: (g, n_c - 1 - j, 0))
    fwd_d = pl.BlockSpec((None,) + blk, lambda g, j: (0, g, j, 0))
    bwd_d = pl.BlockSpec((None,) + blk, lambda g, j: (1, g, n_c - 1 - j, 0))
    st_blk = (bb, 2, RWKV_PAIRS, LANES, LANES)
    st_spec = pl.BlockSpec(st_blk, lambda g, j: (g, 0, 0, 0, 0))
    y_shape = jax.ShapeDtypeStruct((n_b, seq, W), F32)
    fin_blk = (bb, 2, RWKV_PAIRS, LANES, RWKV_HEAD)
    fin_spec = pl.BlockSpec(fin_blk, lambda g, j: (g, 0, 0, 0, 0))
    has_s0 = s0 is not None
    merge_y = n_c == 1
    n_y = 1 if merge_y else 2
    outs = pl.pallas_call(
        functools.partial(_rwkv_scan_kernel, has_s0=has_s0, merge_y=merge_y),
        grid=(n_b // bb, n_c),
        in_specs=[fwd, fwd, fwd, fwd_d, fwd_d, fwd_d, bwd, bwd, bwd, bwd_d, bwd_d, bwd_d] + [st_spec] * has_s0,
        out_specs=[fwd, bwd][:n_y] + [fin_spec],
        out_shape=[y_shape] * n_y + [jax.ShapeDtypeStruct((n_b,) + fin_blk[1:], F32)],
        scratch_shapes=[pltpu.VMEM(st_blk, F32)],
        compiler_params=_params("arbitrary", "arbitrary"),
        name="rwkv_scan_s0" if has_s0 else "rwkv_scan",
    )(*([r, v, na, lw, k, b, r, v, na, lw, k, b] + [s0] * has_s0))
    return tuple(outs[:n_y]), outs[n_y]


LRU_ROWS = 2048

def _lru_kernel(x_ref, g_ref, h0_ref, cw_ref, cb_ref, wa_ref, ba_ref, wx_ref, bx_ref, lam_ref,
                o_ref, hfin_ref, a_s, u_s, h_s, *, seq):
    rows = x_ref.shape[0]
    n_req = rows // seq
    x = x_ref[...]
    row = lax.broadcasted_iota(jnp.int32, x.shape, 0) % seq
    xm2 = jnp.where(row >= 2, pltpu.roll(x, 2, 0), 0.0)
    xm1 = jnp.where(row >= 1, pltpu.roll(x, 1, 0), 0.0)
    xp1 = jnp.where(row < seq - 1, pltpu.roll(x, rows - 1, 0), 0.0)
    xc = (xm2 * cw_ref[0:1, :] + xm1 * cw_ref[1:2, :] + x * cw_ref[2:3, :] + xp1 * cw_ref[3:4, :]
          + cb_ref[...])
    xb = xc.astype(BF16)
    for d in range(2):
        r_gate = _sigmoid(_dot(xb, wa_ref[d]) + ba_ref[d:d + 1, :])
        i_gate = _sigmoid(_dot(xb, wx_ref[d]) + bx_ref[d:d + 1, :])
        log_a = -LRU_C * r_gate * _softplus(-lam_ref[d:d + 1, :])
        a = jnp.exp(log_a)
        u = jnp.sqrt(1.0 - a * a) * (i_gate * xc)
        for sh in (1, 2, 4):
            if d == 0:
                a_n, u_n = pltpu.roll(a, sh, 0), pltpu.roll(u, sh, 0)
                m = (row % SUBLANES) >= sh
            else:
                a_n, u_n = pltpu.roll(a, rows - sh, 0), pltpu.roll(u, rows - sh, 0)
                m = (row % SUBLANES) < SUBLANES - sh
            u = jnp.where(m, a * u_n + u, u)
            a = jnp.where(m, a * a_n, a)
        a_s[d] = a
        u_s[d] = u

    n_t = seq // SUBLANES
    unroll = 4 // n_req if n_req <= 4 else 1

    def body(i, carry):
        carry = list(carry)
        for k in range(unroll):
            for r in range(n_req):
                hf, hb = carry[2 * r], carry[2 * r + 1]
                base = pl.multiple_of(r * seq + (i * unroll + k) * SUBLANES, SUBLANES)
                h8 = a_s[0, pl.ds(base, SUBLANES), :] * hf + u_s[0, pl.ds(base, SUBLANES), :]
                h_s[0, pl.ds(base, SUBLANES), :] = h8
                carry[2 * r] = h8[SUBLANES - 1:SUBLANES]
                base = pl.multiple_of((r + 1) * seq - SUBLANES - (i * unroll + k) * SUBLANES, SUBLANES)
                h8 = a_s[1, pl.ds(base, SUBLANES), :] * hb + u_s[1, pl.ds(base, SUBLANES), :]
                h_s[1, pl.ds(base, SUBLANES), :] = h8
                carry[2 * r + 1] = h8[0:1]
        return tuple(carry)

    init = tuple(h0_ref[r, d:d + 1, :] for r in range(n_req) for d in range(2))
    fin = lax.fori_loop(0, n_t // unroll, body, init)
    for r in range(n_req):
        hfin_ref[r] = jnp.concatenate([fin[2 * r], fin[2 * r + 1]], axis=0)
    g = g_ref[...]
    gelu = 0.5 * g * (1.0 + jnp.tanh(0.7978845608028654 * (g + 0.044715 * (g * g * g))))
    o_ref[...] = ((h_s[0] + h_s[1]) * gelu).astype(BF16)


def _lru(xb, gb, h0, wts, n_b, seq):
    cw, cb, wa, ba, wx, bx, lam = wts
    C = LRU_WIDTH
    const = lambda shape: pl.BlockSpec(shape, lambda b: tuple(0 for _ in shape))
    n_req = max(1, min(n_b, LRU_ROWS // seq))
    tok = pl.BlockSpec((n_req * seq, C), lambda b: (b, 0))
    st = pl.BlockSpec((n_req, 2, C), lambda b: (b, 0, 0))
    return pl.pallas_call(
        functools.partial(_lru_kernel, seq=seq),
        grid=(n_b // n_req,),
        in_specs=[tok, tok, st, const((LRU_CONV_W, C)), const((1, C)), const((2, C, C)), const((2, C)),
                  const((2, C, C)), const((2, C)), const((2, C))],
        out_specs=[tok, st],
        out_shape=[jax.ShapeDtypeStruct((n_b * seq, C), BF16), jax.ShapeDtypeStruct((n_b, 2, C), F32)],
        scratch_shapes=[pltpu.VMEM((2, n_req * seq, C), F32)] * 3,
        compiler_params=_params("arbitrary"),
        name="lru",
    )(xb, gb, h0, cw, cb, wa, ba, wx, bx, lam)


def _outproj_kernel(x_ref, att_ref, *refs, n_y):
    (bonus_ref, g_ref, lru_ref, mod_ref, n2_ref, lnw_ref, lnb_ref, w_ref, rt_ref, bd_ref,
     x1_ref, h2_ref, lg_ref) = refs[n_y:]
    y = refs[0][...] if n_y == 1 else refs[0][...] + refs[1][...]
    parts = []
    for c in range(RWKV_WIDTH // LANES):
        yc = y[:, c * LANES:(c + 1) * LANES]
        mean = _seg_sum(yc, bd_ref[...]) * (1.0 / RWKV_HEAD)
        dev = yc - mean
        var = _seg_sum(dev * dev, bd_ref[...]) * (1.0 / RWKV_HEAD)
        parts.append(dev * lax.rsqrt(var + GN_EPS))
    yn = jnp.concatenate(parts, axis=-1) * lnw_ref[...] + lnb_ref[...]
    rwkv = (yn + bonus_ref[...]) * g_ref[...]
    o1 = ATT_WIDTH
    o2 = o1 + RWKV_WIDTH
    mixed = (_dot(att_ref[...].astype(BF16), w_ref[:o1, :])
             + _dot(rwkv.astype(BF16), w_ref[o1:o2, :])
             + _dot(lru_ref[...].astype(BF16), w_ref[o2:, :]))
    x1 = x_ref[...] + mod_ref[2:3, :] * mixed
    x1_ref[...] = x1
    rs = lax.rsqrt(jnp.mean(x1 * x1, axis=-1, keepdims=True) + RMS_EPS)
    h2 = (x1 * rs * n2_ref[...]) * (1.0 + mod_ref[4:5, :]) + mod_ref[3:4, :]
    h2_hi = h2.astype(BF16)
    h2_lo = (h2 - h2_hi.astype(F32)).astype(BF16)
    h2_ref[...] = h2_hi
    both = _dot(h2_hi, rt_ref[...])
    lg_ref[...] = both[:, :LANES] + both[:, LANES:] + _dot(h2_lo, rt_ref[:, :LANES])


def _outproj(x, att, ys, bonus, g, lru, mod_l, norm2, lnw, lnb, w_out_bf, layer, router_pad, bd,
             seq, cond_row0):
    n_tok = x.shape[0]
    tm = 512 if seq >= 512 else seq * (512 // seq)
    per_seq = max(seq // tm, 1)
    n_t = n_tok // tm
    if cond_row0 == 0:
        row = lambda i: 0
    else:
        row = lambda i: cond_row0 + i // per_seq
    tok = lambda w: pl.BlockSpec((tm, w), lambda i: (i, 0))
    const = lambda shape: pl.BlockSpec(shape, lambda i: tuple(0 for _ in shape))
    W = RWKV_WIDTH
    return pl.pallas_call(
        functools.partial(_outproj_kernel, n_y=len(ys)),
        grid=(n_t,),
        in_specs=[
            tok(D_MODEL), tok(ATT_WIDTH)] + [tok(W)] * len(ys) + [tok(W), tok(W), tok(LRU_WIDTH),
            pl.BlockSpec((None, N_MOD, D_MODEL), lambda i: (row(i), 0, 0)),
            const((1, D_MODEL)), const((1, W)), const((1, W)),
            pl.BlockSpec((None, D_MODEL, D_MODEL), lambda i: (layer, 0, 0)),
            const((D_MODEL, 2 * LANES)), const((LANES, LANES)),
        ],
        out_specs=[tok(D_MODEL), tok(D_MODEL), tok(LANES)],
        out_shape=[jax.ShapeDtypeStruct((n_tok, D_MODEL), F32),
                   jax.ShapeDtypeStruct((n_tok, D_MODEL), BF16),
                   jax.ShapeDtypeStruct((n_tok, LANES), F32)],
        compiler_params=_params("arbitrary"),
        name="outproj",
    )(x, att, *ys, bonus, g, lru, mod_l, norm2, lnw, lnb, w_out_bf, router_pad, bd)


PREFIX_BLOCK = 256
GATHER_ROWS = 1024


def _prefix_count(mask_f, tri):
    seq = mask_f.shape[0]
    outs = []
    carry = jnp.zeros((1, LANES), F32)
    for blk in range(seq // PREFIX_BLOCK):
        m = mask_f[blk * PREFIX_BLOCK:(blk + 1) * PREFIX_BLOCK]
        outs.append(_dot(tri, m.astype(BF16)) + carry)
        carry = carry + jnp.sum(m, axis=0, keepdims=True)
    return jnp.concatenate(outs, axis=0) if len(outs) > 1 else outs[0]


ROUTE_ROWS = 2048


def _route_kernel(lg_ref, h2_ref, tri_ref, xs_ref, slot_ref, aff_ref, *, cap, seq):
    rows = lg_ref.shape[0]
    n_req = rows // seq
    lane = lax.broadcasted_iota(jnp.int32, (rows, LANES), 1)
    real = lane < N_EXPERTS
    lg = jnp.where(real, lg_ref[...], -jnp.inf)
    m = jnp.max(lg, axis=-1, keepdims=True)
    e = jnp.exp(lg - m)
    aff = e / jnp.sum(e, axis=-1, keepdims=True)
    aff_ref[...] = aff

    a3 = aff.reshape(n_req, seq, LANES)
    a_min = jnp.min(a3, axis=1, keepdims=True)
    a_max = jnp.max(a3, axis=1, keepdims=True)

    def search(carry):
        lo, hi, _ = carry
        mid = 0.5 * (jnp.maximum(lo, a_min) + jnp.minimum(hi, a_max))
        inside = jnp.where(a3 > lo, jnp.where(a3 < hi, 1.0, 0.0), 0.0)
        upper = inside * jnp.where(a3 >= mid, 1.0, 0.0)
        up = jnp.min(jnp.where(upper > 0.0, a3, BRACKET_HI), axis=1, keepdims=True)
        dn = jnp.max(jnp.where(inside - upper > 0.0, a3, BRACKET_LO), axis=1, keepdims=True)
        pivot = jnp.where(up < BRACKET_HI, up, dn)
        found = jnp.where(up < BRACKET_HI, 1.0, jnp.where(dn > BRACKET_LO, 1.0, 0.0))
        cnt = jnp.sum(jnp.where(a3 >= pivot, 1.0, 0.0), axis=1, keepdims=True)
        take_lo = found * jnp.where(cnt >= cap, 1.0, 0.0)
        take_hi = found - take_lo
        return (jnp.where(take_lo > 0.0, pivot, lo), jnp.where(take_hi > 0.0, pivot, hi), found)

    shape = (n_req, 1, LANES)
    init = (jnp.full(shape, BRACKET_LO, F32), jnp.full(shape, BRACKET_HI, F32), jnp.ones(shape, F32))
    thr = lax.while_loop(lambda c: jnp.max(c[2]) > 0.0, search, init)[0]
    gt3 = jnp.where(a3 > thr, 1.0, 0.0)
    eq3 = jnp.where(a3 == thr, 1.0, 0.0)
    need3 = cap - jnp.sum(gt3, axis=1, keepdims=True)
    tri = tri_ref[...]
    c_iota = lax.broadcasted_iota(jnp.int32, (cap, seq), 0).astype(F32)
    real_seq = lax.broadcasted_iota(jnp.int32, (seq, LANES), 1) < N_EXPERTS
    group = max(1, min(N_EXPERTS, GATHER_ROWS // cap))
    for r in range(n_req):
        tok = slice(r * seq, (r + 1) * seq)
        gt, eq = gt3[r], eq3[r]
        sel = jnp.where(real_seq, gt + eq * jnp.where(_prefix_count(eq, tri) < need3[r], 1.0, 0.0), 0.0)
        slot = jnp.where(sel > 0.0, _prefix_count(sel, tri), -1.0)
        slot_ref[tok, :] = slot
        slot_t = slot.T
        h2 = h2_ref[tok, :]
        for g0 in range(0, N_EXPERTS, group):
            onehot = jnp.concatenate(
                [jnp.where(c_iota == slot_t[ex:ex + 1, :], 1.0, 0.0) for ex in range(g0, g0 + group)], axis=0)
            picked = _dot(onehot.astype(BF16), h2)
            for k in range(group):
                xs_ref[g0 + k, r * cap:(r + 1) * cap, :] = picked[k * cap:(k + 1) * cap].astype(BF16)


def _route(logits, h2, tri, n_b, seq):
    cap = EC_FACTOR * seq // N_EXPERTS
    n_req = max(1, min(n_b, ROUTE_ROWS // seq))
    tok = lambda w: pl.BlockSpec((n_req * seq, w), lambda b: (b, 0))
    return pl.pallas_call(
        functools.partial(_route_kernel, cap=cap, seq=seq),
        grid=(n_b // n_req,),
        in_specs=[tok(LANES), tok(D_MODEL), pl.BlockSpec((PREFIX_BLOCK, PREFIX_BLOCK), lambda b: (0, 0))],
        out_specs=[pl.BlockSpec((N_EXPERTS, n_req * cap, D_MODEL), lambda b: (0, b, 0)), tok(LANES), tok(LANES)],
        out_shape=[jax.ShapeDtypeStruct((N_EXPERTS, n_b * cap, D_MODEL), BF16),
                   jax.ShapeDtypeStruct((n_b * seq, LANES), F32),
                   jax.ShapeDtypeStruct((n_b * seq, LANES), F32)],
        compiler_params=_params("arbitrary"),
        name="route",
    )(logits, h2, tri)


EXPERT_RB = 256


def _expert_kernel(xa_ref, xb_ref, wg_ref, wu_ref, wd_ref, ya_ref, yb_ref, wg_s, wu_s, wd_s):
    wg_s[...] = wg_ref[...].astype(BF16)
    wu_s[...] = wu_ref[...].astype(BF16)
    wd_s[...] = wd_ref[...].astype(BF16)
    for x_ref, y_ref in ((xa_ref, ya_ref), (xb_ref, yb_ref)):
        for rb in range(x_ref.shape[0] // EXPERT_RB):
            rows = slice(rb * EXPERT_RB, (rb + 1) * EXPERT_RB)
            x = x_ref[rows, :]
            a = _dot(x, wg_s[...])
            u = _dot(x, wu_s[...])
            hid = (a * _sigmoid(a) * u).astype(BF16)
            y_ref[rows, :] = _dot(hid, wd_s[...]).astype(BF16)


def _experts(xs_a, xs_b, w_gate, w_up, w_down, layer):
    ma, mb = xs_a.shape[1], xs_b.shape[1]
    xspec = lambda m: pl.BlockSpec((None, m, D_MODEL), lambda e: (e, 0, 0))
    return pl.pallas_call(
        _expert_kernel,
        grid=(N_EXPERTS,),
        in_specs=[
            xspec(ma), xspec(mb),
            pl.BlockSpec((None, None, D_MODEL, EXPERT_FF), lambda e: (layer, e, 0, 0)),
            pl.BlockSpec((None, None, D_MODEL, EXPERT_FF), lambda e: (layer, e, 0, 0)),
            pl.BlockSpec((None, None, EXPERT_FF, D_MODEL), lambda e: (layer, e, 0, 0)),
        ],
        out_specs=[xspec(ma), xspec(mb)],
        out_shape=[jax.ShapeDtypeStruct(xs_a.shape, BF16), jax.ShapeDtypeStruct(xs_b.shape, BF16)],
        scratch_shapes=[pltpu.VMEM((D_MODEL, EXPERT_FF), BF16), pltpu.VMEM((D_MODEL, EXPERT_FF), BF16),
                        pltpu.VMEM((EXPERT_FF, D_MODEL), BF16)],
        compiler_params=_params("arbitrary"),
        name="experts",
    )(xs_a, xs_b, w_gate, w_up, w_down)


COMBINE_FUSED_COLS = 512
COMBINE_ROWS = 1024


def _combine_kernel(*refs, cap, fused, n_req):
    if fused:
        y_ref, slot_ref, aff_ref, x1_ref, mod_ref, ex_ref, o_ref = refs
    else:
        y_ref, slot_ref, aff_ref, x1_ref, mod_ref, o_ref = refs
    seq = x1_ref.shape[0]
    slot = slot_ref[...]
    aff = aff_ref[...]
    if fused:
        spread = ex_ref[...]
        a1 = aff.astype(BF16)
        r1 = aff - a1.astype(F32)
        a2 = r1.astype(BF16)
        a3 = (r1 - a2.astype(F32)).astype(BF16)
        slot_x = _dot(slot.astype(BF16), spread)
        aff_x = _dot(a1, spread) + _dot(a2, spread) + _dot(a3, spread)
        cols = N_EXPERTS * cap
        c_pat = (lax.broadcasted_iota(jnp.int32, (seq, cols), 1) % cap).astype(F32)
        w_hi, w_lo = _split(jnp.where(slot_x == c_pat, aff_x, 0.0))
        parts = []
        for r in range(n_req):
            rows = slice(r * (seq // n_req), (r + 1) * (seq // n_req))
            y2 = y_ref[:, r * cap:(r + 1) * cap, :].reshape(cols, D_MODEL)
            parts.append(_dot(w_hi[rows], y2) + _dot(w_lo[rows], y2))
        acc = jnp.concatenate(parts, axis=0) if n_req > 1 else parts[0]
    else:
        c_iota = lax.broadcasted_iota(jnp.int32, (seq, cap), 1).astype(F32)
        acc = jnp.zeros((seq, D_MODEL), F32)
        for ex in range(N_EXPERTS):
            onehot = jnp.where(slot[:, ex:ex + 1] == c_iota, 1.0, 0.0).astype(BF16)
            acc = acc + aff[:, ex:ex + 1] * _dot(onehot, y_ref[ex])
    o_ref[...] = x1_ref[...] + mod_ref[5:6, :] * acc


def _combine(y, slot, aff, x1, mod_l, n_b, seq, cond_row0):
    cap = EC_FACTOR * seq // N_EXPERTS
    fused = N_EXPERTS * cap <= COMBINE_FUSED_COLS
    if cond_row0 == 0:
        row = lambda b: 0
    else:
        row = lambda b: cond_row0 + b
    tm = min(seq, 512)
    n_t = seq // tm
    n_req = max(1, min(n_b, COMBINE_ROWS // seq)) if (fused and cond_row0 == 0) else 1
    tm *= n_req
    tok = lambda w: pl.BlockSpec((tm, w), lambda b, i: (b * n_t + i, 0))
    in_specs = [
        pl.BlockSpec((N_EXPERTS, n_req * cap, D_MODEL), lambda b, i: (0, b, 0)),
        tok(LANES), tok(LANES), tok(D_MODEL),
        pl.BlockSpec((None, N_MOD, D_MODEL), lambda b, i: (row(b), 0, 0)),
    ]
    args = [y, slot, aff, x1, mod_l]
    if fused:
        cols = N_EXPERTS * cap
        spread = (jnp.arange(LANES)[:, None] == (jnp.arange(cols) // cap)[None, :]).astype(BF16)
        in_specs.append(pl.BlockSpec((LANES, cols), lambda b, i: (0, 0)))
        args.append(spread)
    return pl.pallas_call(
        functools.partial(_combine_kernel, cap=cap, fused=fused, n_req=n_req),
        grid=(n_b // n_req, n_t),
        in_specs=in_specs,
        out_specs=tok(D_MODEL),
        out_shape=jax.ShapeDtypeStruct((n_b * seq, D_MODEL), F32),
        compiler_params=_params("arbitrary", "arbitrary"),
        name="combine_fused" if fused else "combine",
    )(*args)


def _rope_tables(seq):
    t = np.arange(seq)
    row = (t // GRID_W).astype(np.float64)
    col = (t % GRID_W).astype(np.float64)
    half = HEAD_DIM // 2
    inv = ROPE_THETA ** (-np.arange(0, half, 2, dtype=np.float64) / half)
    lane = np.arange(LANES)
    u = lane % HEAD_DIM
    pos = np.where((u // half)[None, :] == 0, row[:, None], col[:, None])
    ang = pos * inv[(u % half) % (half // 2)][None, :]
    first = ((u % half) < half // 2)[None, :]
    sin = np.sin(ang)
    tabs = (np.cos(ang), np.where(first, -sin, 0.0), np.where(first, 0.0, sin))
    return tuple(jnp.asarray(x.astype(np.float32)) for x in tabs)


def _block_diag(w):
    n, k, _ = w.shape
    eye = jnp.eye(n, dtype=w.dtype)
    return (eye[:, None, :, None] * w[:, :, None, :]).reshape(n * k, n * k)


def _split_weight(w):
    return jnp.stack(_split(w))


def _pad_rows(w, offset, total=LANES):
    return jnp.zeros((total, w.shape[1]), w.dtype).at[offset:offset + w.shape[0]].set(w)


def _pack_state(s):
    b = s.shape[0]
    st = s.reshape(b, 2, RWKV_PAIRS, 2, RWKV_HEAD, RWKV_HEAD).transpose(0, 1, 2, 3, 5, 4)
    eye = jnp.eye(2, dtype=s.dtype)
    out = st[:, :, :, :, :, None, :] * eye[None, None, None, :, None, :, None]
    return out.reshape(b, 2, RWKV_PAIRS, LANES, LANES)


def _unpack_state(s):
    return s.reshape(s.shape[0], 2, RWKV_HEADS, RWKV_HEAD, RWKV_HEAD)


def kernel(x_prompt, x_sample, cache_k, cache_v, state_rwkv, state_lru, c, c_ctx, w_ada, b_ada, norm1, norm2, w_in, w_out, q_norm, k_norm, rwkv_mu, rwkv_w0, rwkv_w_up, rwkv_a0, rwkv_a_up, rwkv_g_up, rwkv_k_k, rwkv_k_a, rwkv_r_k, rwkv_ln_w, rwkv_ln_b, lru_conv_w, lru_conv_b, lru_wa, lru_ba, lru_wx, lru_bx, lru_lambda, router, exp_w_gate, exp_w_up, exp_w_down):
    n_ctx, seq_ctx, _ = x_prompt.shape
    n_lat, seq_lat, _ = x_sample.shape
    past = cache_k.shape[2]
    assert n_lat + 1 <= COND_ROWS

    cond = jnp.zeros((COND_ROWS, D_MODEL), F32).at[0].set(c_ctx).at[1:1 + n_lat].set(c)
    mod = _ada(cond, w_ada, b_ada.reshape(DEPTH, 1, N_MOD * D_MODEL))
    mod = mod.reshape(DEPTH, COND_ROWS, N_MOD, D_MODEL)

    w_in_bf = w_in.astype(BF16)
    w_out_bf = w_out.astype(BF16)
    lane = jnp.arange(LANES)
    bd = (lane[:, None] // HEAD_DIM == lane[None, :] // HEAD_DIM).astype(BF16)
    pidx = jnp.arange(PREFIX_BLOCK)
    tri = (pidx[None, :] < pidx[:, None]).astype(BF16)
    rope_tabs = _rope_tables(seq_lat)

    paths = [
        dict(x=x_prompt.reshape(n_ctx * seq_ctx, D_MODEL), n_b=n_ctx, seq=seq_ctx, row0=0, rope=None),
        dict(x=x_sample.reshape(n_lat * seq_lat, D_MODEL), n_b=n_lat, seq=seq_lat, row0=1, rope=rope_tabs),
    ]
    new_k, new_v, new_sr, new_sl = [], [], [], []
    for l in range(DEPTH):
        mod_l = mod[l]
        qn = jnp.tile(q_norm[l], LANES // HEAD_DIM)[None, :]
        kn = jnp.tile(k_norm[l], LANES // HEAD_DIM)[None, :]
        prep_w = (
            rwkv_mu[l][None, :], rwkv_k_k[l][None, :], rwkv_k_a[l][None, :],
            rwkv_r_k[l].reshape(1, RWKV_WIDTH), rwkv_w0[l], rwkv_a0[l],
            jnp.stack([_split_weight(_pad_rows(rwkv_w_up[l, d], 0)) for d in range(2)]),
            jnp.stack([_split_weight(_pad_rows(rwkv_a_up[l, d], RWKV_DECAY_LORA)) for d in range(2)]),
            _split_weight(_pad_rows(rwkv_g_up[l], RWKV_DECAY_LORA + RWKV_AAA_LORA)),
        )
        lru_w = (
            lru_conv_w[l], lru_conv_b[l][None, :],
            jnp.stack([_block_diag(lru_wa[l, d]) for d in range(2)]).astype(BF16), lru_ba[l],
            jnp.stack([_block_diag(lru_wx[l, d]) for d in range(2)]).astype(BF16), lru_bx[l],
            lru_lambda[l],
        )
        router_pad = jnp.concatenate(_split(jnp.zeros((D_MODEL, LANES), F32).at[:, :N_EXPERTS].set(router[l])), axis=1)
        mids = []
        for pi, pth in enumerate(paths):
            n_b, seq, row0 = pth["n_b"], pth["seq"], pth["row0"]
            latent = pth["rope"] is not None
            outs = _inproj(pth["x"], mod_l, norm1[l][None, :], w_in_bf, l, qn, kn, bd, pth["rope"], seq, row0)
            if latent:
                q, k_n, k_att, v, p_rwkv, lru_x, lru_g = outs
                cache = (cache_k[:, l].reshape(n_b, past, KV_WIDTH), cache_v[:, l].reshape(n_b, past, KV_WIDTH))
                s0 = _pack_state(state_rwkv[:, l])
                h0 = state_lru[:, l]
            else:
                q, k_n, v, p_rwkv, lru_x, lru_g = outs
                k_att, cache = k_n, None
                s0 = None
                h0 = jnp.zeros((n_b, 2, LRU_WIDTH), F32)
                new_k.append(k_n.reshape(n_b, seq, ATT_KV_HEADS, HEAD_DIM))
                new_v.append(v.reshape(n_b, seq, ATT_KV_HEADS, HEAD_DIM))
            att = _attention(q, k_att, v, n_b, seq, cache)
            r, vv, na, w, kd, b, g, bonus = _rwkv_prep(p_rwkv.reshape(n_b, seq, RWKV_COLS), n_b, seq, prep_w, bd)
            W = RWKV_WIDTH
            ys, s_fin = _rwkv_scan(r.reshape(n_b, seq, W), vv.reshape(n_b, seq, W), na.reshape(n_b, seq, W),
                                   w.reshape(2, n_b, seq, W), kd.reshape(2, n_b, seq, W),
                                   b.reshape(2, n_b, seq, W), s0, n_b, seq)
            ys = [y.reshape(n_b * seq, W) for y in ys]
            lru_out, h_fin = _lru(lru_x, lru_g, h0, lru_w, n_b, seq)
            if not latent:
                new_sr.append(_unpack_state(s_fin))
                new_sl.append(h_fin)
            x1, h2, logits = _outproj(pth["x"], att, ys, bonus, g,
                                      lru_out, mod_l, norm2[l][None, :], rwkv_ln_w[l][None, :],
                                      rwkv_ln_b[l][None, :], w_out_bf, l, router_pad, bd, seq, row0)
            xs, slot, aff = _route(logits, h2, tri, n_b, seq)
            mids.append((xs, slot, aff, x1))
        y_a, y_b = _experts(mids[0][0], mids[1][0], exp_w_gate, exp_w_up, exp_w_down, l)
        for pth, (xs, slot, aff, x1), y in zip(paths, mids, (y_a, y_b)):
            pth["x"] = _combine(y, slot, aff, x1, mod_l, pth["n_b"], pth["seq"], pth["row0"])

    y_prompt = paths[0]["x"].reshape(n_ctx, seq_ctx, D_MODEL)
    y_sample = paths[1]["x"].reshape(n_lat, seq_lat, D_MODEL)
    return (y_prompt, y_sample, jnp.stack(new_k, axis=1), jnp.stack(new_v, axis=1),
            jnp.stack(new_sr, axis=1), jnp.stack(new_sl, axis=1))
```
